```python
import math
import jax, jax.numpy as jnp
from jax import lax
import numpy as np

D_MODEL = 1024
BATCH = 16
SEQ = 4096
DEPTH = 1
DEC_BATCH = 8
DEC_SEQ = 2048
PAST_LEN = 128

N_META = 16
HEAD_DIM = 64
BLOCK = 128
DA_HEADS = 4
DA_VDIM = 2 * HEAD_DIM
DA_WIDTH = DA_HEADS * DA_VDIM
SW_HEADS = 8
SW_KV_HEADS = 2
SW_GROUP = SW_HEADS // SW_KV_HEADS
SW_WIDTH = SW_HEADS * HEAD_DIM
WINDOW = 128
MIX_WIDTH = DA_WIDTH + SW_WIDTH
DA_Q = DA_HEADS * 2 * HEAD_DIM
DA_K = DA_HEADS * 2 * HEAD_DIM
DA_V = DA_HEADS * DA_VDIM
SW_Q = SW_HEADS * HEAD_DIM
SW_K = SW_KV_HEADS * HEAD_DIM
SW_V = SW_KV_HEADS * HEAD_DIM
IN_WIDTH = DA_Q + DA_K + DA_V + SW_Q + SW_K + SW_V
N_GROUPS = 4
EXPERTS_PER_GROUP = 8
N_EXPERTS = N_GROUPS * EXPERTS_PER_GROUP
TOP_K = 2
D_EXPERT = 256
MOE_CHUNK = 512
EPS = 1e-6
SUBLN_EPS = 1e-5
NEG_INF = -1e30

kernel_name = "hymba_diffattn_swa_hmoe_encoder"


def rmsnorm(x, g, eps=EPS):
    xf = x.astype(jnp.float32)
    y = xf * lax.rsqrt(jnp.mean(xf * xf, axis=-1, keepdims=True) + eps)
    return (y * g.astype(jnp.float32)).astype(x.dtype)


def alibi_slopes(n):
    return 2.0 ** (-8.0 * jnp.arange(1, n + 1, dtype=jnp.float32) / n)


def alibi_distance(t, s):
    both = (t[:, None] >= N_META) & (s[None, :] >= N_META)
    return jnp.where(both, jnp.abs(t[:, None] - s[None, :]), 0).astype(jnp.float32)


def diff_attention(q, k, v, lam, lam_init, subln_g):
    B, L = q.shape[0], q.shape[1]
    nb = -(-L // BLOCK)
    Lp = nb * BLOCK
    scale = 1.0 / math.sqrt(HEAD_DIM)
    qp = jnp.pad(q, ((0, 0), (0, Lp - L), (0, 0), (0, 0), (0, 0)))
    qb = qp.reshape(B, nb, BLOCK, DA_HEADS, 2, HEAD_DIM).transpose(1, 0, 3, 4, 2, 5)
    kt = k.transpose(0, 2, 3, 1, 4)
    vt = v.transpose(0, 2, 1, 3)
    slopes = alibi_slopes(DA_HEADS)
    s_pos = jnp.arange(L)

    def one_block(args):
        qi, i = args
        t_pos = i * BLOCK + jnp.arange(BLOCK)
        bias = -slopes[:, None, None] * alibi_distance(t_pos, s_pos)[None]
        sc = jnp.einsum('bhmqd,bhmkd->bhmqk', qi, kt).astype(jnp.float32) * scale + bias[None, :, None]
        p = jax.nn.softmax(sc, axis=-1)
        w = p[:, :, 0] - lam * p[:, :, 1]
        return jnp.einsum('bhqk,bhkd->bhqd', w.astype(vt.dtype), vt)

    o = lax.map(one_block, (qb, jnp.arange(nb)))
    o = o.transpose(1, 0, 3, 2, 4).reshape(B, Lp, DA_HEADS, DA_VDIM)[:, :L]
    o = rmsnorm(o, subln_g, SUBLN_EPS) * (1.0 - lam_init)
    return o.reshape(B, L, DA_WIDTH)


def window_attention(q, k, v, sink):
    B, L = q.shape[0], q.shape[1]
    nb = -(-L // BLOCK)
    Lp = nb * BLOCK
    scale = 1.0 / math.sqrt(HEAD_DIM)
    qp = jnp.pad(q, ((0, 0), (0, Lp - L), (0, 0), (0, 0)))
    qb = qp.reshape(B, nb, BLOCK, SW_KV_HEADS, SW_GROUP, HEAD_DIM).transpose(1, 0, 3, 4, 2, 5)
    pad_cfg = ((0, 0), (BLOCK, Lp - L + BLOCK), (0, 0), (0, 0))
    kp = jnp.pad(k, pad_cfg).transpose(0, 2, 1, 3)
    vp = jnp.pad(v, pad_cfg).transpose(0, 2, 1, 3)
    km = k[:, :N_META].transpose(0, 2, 1, 3)
    vm = v[:, :N_META].transpose(0, 2, 1, 3)
    slopes = alibi_slopes(SW_HEADS).reshape(SW_KV_HEADS, SW_GROUP)
    sink_f = sink.astype(jnp.float32).reshape(SW_KV_HEADS, SW_GROUP)

    def one_block(args):
        qi, i = args
        start = i * BLOCK
        kb = lax.dynamic_slice_in_dim(kp, start, 3 * BLOCK, axis=2)
        vb = lax.dynamic_slice_in_dim(vp, start, 3 * BLOCK, axis=2)
        t = start + jnp.arange(BLOCK)
        s = start - BLOCK + jnp.arange(3 * BLOCK)
        diff = jnp.abs(t[:, None] - s[None, :])
        valid = (diff <= WINDOW) & (s[None, :] >= N_META) & (s[None, :] < L)
        dist = jnp.where(t[:, None] >= N_META, diff, 0).astype(jnp.float32)
        band = jnp.einsum('bkgqd,bksd->bkgqs', qi, kb).astype(jnp.float32) * scale
        band = jnp.where(valid, band - slopes[:, :, None, None] * dist, NEG_INF)
        meta = jnp.einsum('bkgqd,bkmd->bkgqm', qi, km).astype(jnp.float32) * scale
        sink_l = jnp.broadcast_to(sink_f[None, :, :, None, None], band.shape[:-1] + (1,))
        p = jax.nn.softmax(jnp.concatenate([meta, band, sink_l], axis=-1), axis=-1)
        pm = p[..., :N_META].astype(vm.dtype)
        pb = p[..., N_META:N_META + 3 * BLOCK].astype(vb.dtype)
        return (jnp.einsum('bkgqm,bkmd->bkgqd', pm, vm)
                + jnp.einsum('bkgqs,bksd->bkgqd', pb, vb))

    o = lax.map(one_block, (qb, jnp.arange(nb)))
    o = o.transpose(1, 0, 4, 2, 3, 5).reshape(B, Lp, SW_WIDTH)[:, :L]
    return o


def hier_moe(x, w_gr, b_gr, w_er, b_er, w_gate, w_up, w_down):
    B, L, D = x.shape
    xf = x.reshape(-1, D)
    N = xf.shape[0]
    g_prob = jax.nn.softmax((xf @ w_gr + b_gr).astype(jnp.float32), axis=-1)
    g_w, g_idx = lax.top_k(g_prob, 1)
    e_logits = (xf @ w_er + b_er).astype(jnp.float32).reshape(N, N_GROUPS, EXPERTS_PER_GROUP)
    e_logits = jnp.take_along_axis(e_logits, g_idx[:, :, None], axis=1)[:, 0]
    e_prob = jax.nn.softmax(e_logits, axis=-1)
    top_w, top_i = lax.top_k(e_prob, TOP_K)
    top_w = top_w / jnp.sum(top_w, axis=-1, keepdims=True) * g_w
    within = jnp.sum(jax.nn.one_hot(top_i, EXPERTS_PER_GROUP, dtype=jnp.float32) * top_w[..., None], axis=1)
    gate = (jax.nn.one_hot(g_idx[:, 0], N_GROUPS, dtype=jnp.float32)[:, :, None]
            * within[:, None, :]).reshape(N, N_EXPERTS)
    pad = (-N) % MOE_CHUNK
    xc = jnp.pad(xf, ((0, pad), (0, 0))).reshape(-1, MOE_CHUNK, D)
    gc = jnp.pad(gate, ((0, pad), (0, 0))).reshape(-1, MOE_CHUNK, N_EXPERTS)

    def one_chunk(args):
        xi, gi = args
        h = jax.nn.silu(jnp.einsum('cd,edf->cef', xi, w_gate)) * jnp.einsum('cd,edf->cef', xi, w_up)
        h = h * gi[:, :, None].astype(h.dtype)
        return jnp.einsum('cef,efd->cd', h, w_down)

    y = lax.map(one_chunk, (xc, gc)).reshape(-1, D)[:N]
    return y.reshape(B, L, D)


def encoder_forward(x, meta, norm1_g, w_in, lam_q1, lam_k1, lam_q2, lam_k2, subln_g, sink, w_out,
                    norm2_g, w_gr, b_gr, w_er, b_er, w_gate, w_up, w_down, final_g):
    B = x.shape[0]
    h = jnp.concatenate([jnp.broadcast_to(meta[None].astype(x.dtype), (B, N_META, D_MODEL)), x], axis=1)
    L = h.shape[1]
    splits = list(np.cumsum([DA_Q, DA_K, DA_V, SW_Q, SW_K]))
    for l in range(DEPTH):
        lam_init = 0.8 - 0.6 * math.exp(-0.3 * l)
        a = rmsnorm(h, norm1_g[l])
        proj = a @ w_in[l]
        qd, kd, vd, qs, ks, vs = jnp.split(proj, [int(c) for c in splits], axis=-1)
        qd = qd.reshape(B, L, DA_HEADS, 2, HEAD_DIM)
        kd = kd.reshape(B, L, DA_HEADS, 2, HEAD_DIM)
        vd = vd.reshape(B, L, DA_HEADS, DA_VDIM)
        lam = (jnp.exp(jnp.sum(lam_q1[l].astype(jnp.float32) * lam_k1[l].astype(jnp.float32)))
               - jnp.exp(jnp.sum(lam_q2[l].astype(jnp.float32) * lam_k2[l].astype(jnp.float32)))
               + lam_init)
        od = diff_attention(qd, kd, vd, lam, lam_init, subln_g[l])
        os_ = window_attention(qs.reshape(B, L, SW_HEADS, HEAD_DIM),
                               ks.reshape(B, L, SW_KV_HEADS, HEAD_DIM),
                               vs.reshape(B, L, SW_KV_HEADS, HEAD_DIM), sink[l])
        h = h + jnp.concatenate([od, os_], axis=-1) @ w_out[l]
        h = h + hier_moe(rmsnorm(h, norm2_g[l]), w_gr[l], b_gr[l], w_er[l], b_er[l],
                         w_gate[l], w_up[l], w_down[l])
    h = rmsnorm(h, final_g)
    return h[:, N_META:]


def setup_inputs(seed: int = 0) -> dict:
    key = jax.random.key(seed)
    ks = jax.random.split(key, 24)
    f32 = jnp.float32
    nrm = lambda k, shape, s: jax.random.normal(k, shape, f32) * s
    return {
        "x_prompt": nrm(ks[0], (BATCH, SEQ, D_MODEL), 1.0),
        "x_sample": nrm(ks[1], (DEC_BATCH, DEC_SEQ, D_MODEL), 1.0),
        "meta": nrm(ks[2], (N_META, D_MODEL), 1.0),
        "norm1_g": 1.0 + nrm(ks[3], (DEPTH, D_MODEL), 0.01),
        "w_in": nrm(ks[4], (DEPTH, D_MODEL, IN_WIDTH), D_MODEL ** -0.5),
        "lam_q1": nrm(ks[5], (DEPTH, HEAD_DIM), 0.1),
        "lam_k1": nrm(ks[6], (DEPTH, HEAD_DIM), 0.1),
        "lam_q2": nrm(ks[7], (DEPTH, HEAD_DIM), 0.1),
        "lam_k2": nrm(ks[8], (DEPTH, HEAD_DIM), 0.1),
        "subln_g": 1.0 + nrm(ks[9], (DEPTH, DA_VDIM), 0.01),
        "sink": nrm(ks[10], (DEPTH, SW_HEADS), 0.5),
        "w_out": nrm(ks[11], (DEPTH, MIX_WIDTH, D_MODEL), MIX_WIDTH ** -0.5),
        "norm2_g": 1.0 + nrm(ks[12], (DEPTH, D_MODEL), 0.01),
        "w_gr": nrm(ks[13], (DEPTH, D_MODEL, N_GROUPS), D_MODEL ** -0.5),
        "b_gr": nrm(ks[14], (DEPTH, N_GROUPS), 0.01),
        "w_er": nrm(ks[15], (DEPTH, D_MODEL, N_EXPERTS), D_MODEL ** -0.5),
        "b_er": nrm(ks[16], (DEPTH, N_EXPERTS), 0.01),
        "w_gate": nrm(ks[17], (DEPTH, N_EXPERTS, D_MODEL, D_EXPERT), D_MODEL ** -0.5),
        "w_up": nrm(ks[18], (DEPTH, N_EXPERTS, D_MODEL, D_EXPERT), D_MODEL ** -0.5),
        "w_down": nrm(ks[19], (DEPTH, N_EXPERTS, D_EXPERT, D_MODEL), D_EXPERT ** -0.5),
        "final_g": 1.0 + nrm(ks[20], (D_MODEL,), 0.01),
    }


def reference(x_prompt, x_sample, meta, norm1_g, w_in, lam_q1, lam_k1, lam_q2, lam_k2, subln_g, sink,
              w_out, norm2_g, w_gr, b_gr, w_er, b_er, w_gate, w_up, w_down, final_g):
    y_prompt = encoder_forward(x_prompt, meta, norm1_g, w_in, lam_q1, lam_k1, lam_q2, lam_k2, subln_g,
                               sink, w_out, norm2_g, w_gr, b_gr, w_er, b_er, w_gate, w_up, w_down, final_g)
    y_sample = encoder_forward(x_sample, meta, norm1_g, w_in, lam_q1, lam_k1, lam_q2, lam_k2, subln_g,
                               sink, w_out, norm2_g, w_gr, b_gr, w_er, b_er, w_gate, w_up, w_down, final_g)
    return (y_prompt, y_sample)
```

```python
import functools
import math

import jax
import jax.numpy as jnp
from jax import lax
from jax.experimental import pallas as pl
from jax.experimental.pallas import tpu as pltpu

F32 = jnp.float32
BF16 = jnp.bfloat16

N_META = 16
HEAD_DIM = 64
DA_HEADS = 4
SW_HEADS = 8
SW_KV_HEADS = 2
SW_GROUP = SW_HEADS // SW_KV_HEADS
WINDOW = 128
N_GROUPS = 4
EXPERTS_PER_GROUP = 8
N_EXPERTS = N_GROUPS * EXPERTS_PER_GROUP
EPS = 1e-6
SUBLN_EPS = 1e-5
NEG_INF = -1e30
LAM_INIT = 0.8 - 0.6 * math.exp(-0.3 * 0)
LANES = 128
VMEM_LIMIT = 48 * 1024 * 1024

COL_QD, COL_KD, COL_VD, COL_QS, COL_KS, COL_VS, N_COLBLK = 0, 4, 8, 12, 16, 18, 20
CONTRACT_LAST = (((1,), (1,)), ((), ()))


def _params(sem):
    return pltpu.CompilerParams(dimension_semantics=sem, vmem_limit_bytes=VMEM_LIMIT)


def _norm_proj_kernel(x_ref, g_ref, w_ref, o_ref):
    x = x_ref[...]
    ms = jnp.mean(x * x, axis=-1, keepdims=True)
    y = (x * lax.rsqrt(ms + EPS) * g_ref[...]).astype(BF16)
    o_ref[...] = jnp.dot(y, w_ref[...], preferred_element_type=F32).astype(o_ref.dtype)


def _norm_proj(x2d, g, w_ext, tm):
    n, d = x2d.shape
    wcols = w_ext.shape[1]
    return pl.pallas_call(
        _norm_proj_kernel,
        grid=(n // tm,),
        in_specs=[
            pl.BlockSpec((tm, d), lambda i: (i, 0)),
            pl.BlockSpec((1, d), lambda i: (0, 0)),
            pl.BlockSpec((d, wcols), lambda i: (0, 0)),
        ],
        out_specs=pl.BlockSpec((tm, wcols), lambda i: (i, 0)),
        out_shape=jax.ShapeDtypeStruct((n, wcols), BF16),
        compiler_params=_params(("parallel",)),
        name="norm_proj",
    )(x2d, g, w_ext)


def _diff_attn_kernel(q_ref, k_ref, v_ref, km_ref, vm_ref, lq1_ref, lk1_ref, lq2_ref, lk2_ref, sg_ref,
                      o_ref, *, bq, bk, seq):
    h = pl.program_id(1)
    i = pl.program_id(2)
    scale = 1.0 / math.sqrt(HEAD_DIM)
    slope = jnp.exp2(jnp.full((1, 1), -8.0 / DA_HEADS, F32) * (h + 1).astype(F32))
    q = q_ref[...]
    lane = lax.broadcasted_iota(jnp.int32, (1, LANES), 1)
    qs = (jnp.where(lane < HEAD_DIM, q, jnp.zeros_like(q)), jnp.where(lane >= HEAD_DIM, q, jnp.zeros_like(q)))

    km = km_ref[...]
    vm = vm_ref[...]
    ms, ls, accs = [], [], []
    for c in range(2):
        s = lax.dot_general(qs[c], km, CONTRACT_LAST, preferred_element_type=F32) * scale
        m = jnp.max(s, axis=-1, keepdims=True)
        p = jnp.exp(s - m)
        ms.append(m)
        ls.append(jnp.sum(p, axis=-1, keepdims=True))
        accs.append(jnp.dot(p.astype(BF16), vm, preferred_element_type=F32))

    qpos = i * bq + lax.broadcasted_iota(jnp.int32, (bq, 1), 0)

    def body(kb, carry):
        ms, ls, accs = carry
        k = k_ref[pl.ds(pl.multiple_of(kb * bk, bk), bk), :]
        v = v_ref[pl.ds(pl.multiple_of(kb * bk, bk), bk), :]
        kpos = kb * bk + lax.broadcasted_iota(jnp.int32, (1, bk), 1)
        bias = -slope * jnp.abs(qpos - kpos).astype(F32)
        new_ms, new_ls, new_accs = [], [], []
        for c in range(2):
            s = lax.dot_general(qs[c], k, CONTRACT_LAST, preferred_element_type=F32) * scale + bias
            m_new = jnp.maximum(ms[c], jnp.max(s, axis=-1, keepdims=True))
            alpha = jnp.exp(ms[c] - m_new)
            p = jnp.exp(s - m_new)
            new_ms.append(m_new)
            new_ls.append(alpha * ls[c] + jnp.sum(p, axis=-1, keepdims=True))
            new_accs.append(alpha * accs[c] + jnp.dot(p.astype(BF16), v, preferred_element_type=F32))
        return tuple(new_ms), tuple(new_ls), tuple(new_accs)

    ms, ls, accs = lax.fori_loop(0, seq // bk, body, (tuple(ms), tuple(ls), tuple(accs)))

    lam = (jnp.exp(jnp.sum(lq1_ref[...] * lk1_ref[...], axis=-1, keepdims=True))
           - jnp.exp(jnp.sum(lq2_ref[...] * lk2_ref[...], axis=-1, keepdims=True)) + LAM_INIT)
    o = accs[0] / ls[0] - lam * (accs[1] / ls[1])
    o = o * lax.rsqrt(jnp.mean(o * o, axis=-1, keepdims=True) + SUBLN_EPS) * sg_ref[...]
    o_ref[...] = (o * (1.0 - LAM_INIT)).astype(o_ref.dtype)


def _diff_attn(proj, proj_meta, lam_vecs, subln_g, bq, bk):
    b, s, _ = proj.shape
    vec = pl.BlockSpec((1, HEAD_DIM), lambda bi, h, i: (0, 0))
    return pl.pallas_call(
        functools.partial(_diff_attn_kernel, bq=bq, bk=bk, seq=s),
        grid=(b, DA_HEADS, s // bq),
        in_specs=[
            pl.BlockSpec((None, bq, LANES), lambda bi, h, i: (bi, i, COL_QD + h)),
            pl.BlockSpec((None, s, LANES), lambda bi, h, i: (bi, 0, COL_KD + h)),
            pl.BlockSpec((None, s, LANES), lambda bi, h, i: (bi, 0, COL_VD + h)),
            pl.BlockSpec((N_META, LANES), lambda bi, h, i: (0, COL_KD + h)),
            pl.BlockSpec((N_META, LANES), lambda bi, h, i: (0, COL_VD + h)),
            vec, vec, vec, vec,
            pl.BlockSpec((1, LANES), lambda bi, h, i: (0, 0)),
        ],
        out_specs=pl.BlockSpec((None, bq, LANES), lambda bi, h, i: (bi, i, h)),
        out_shape=jax.ShapeDtypeStruct((b, s, DA_HEADS * LANES), BF16),
        compiler_params=_params(("parallel", "parallel", "arbitrary")),
        name="diff_attn",
    )(proj, proj, proj, proj_meta, proj_meta, *lam_vecs, subln_g)


def _win_attn_kernel(q_ref, kp_ref, kc_ref, kn_ref, vp_ref, vc_ref, vn_ref, km_ref, vm_ref, sink_ref, o_ref,
                     *, seq):
    g = pl.program_id(1)
    c = pl.program_id(2)
    blk = WINDOW
    scale = 1.0 / math.sqrt(HEAD_DIM)
    lane = lax.broadcasted_iota(jnp.int32, (1, LANES), 1)
    qpos = c * blk + lax.broadcasted_iota(jnp.int32, (blk, 1), 0)
    km = km_ref[...]
    vm = vm_ref[...]
    kblocks = (kp_ref[...], kc_ref[...], kn_ref[...])
    vblocks = (vp_ref[...], vc_ref[...], vn_ref[...])
    dists, valids = [], []
    for p in range(3):
        kpos = (c + p - 1) * blk + lax.broadcasted_iota(jnp.int32, (1, blk), 1)
        diff = jnp.abs(qpos - kpos)
        dists.append(diff.astype(F32))
        valids.append((diff <= WINDOW) & (kpos >= 0) & (kpos < seq))

    results = []
    for hh in range(SW_GROUP):
        head = g * SW_GROUP + hh
        slope = jnp.exp2(jnp.full((1, 1), -8.0 / SW_HEADS, F32) * (head + 1).astype(F32))
        sink = jnp.full((1, 1), 1.0, F32) * sink_ref[0, head]
        qb = q_ref[:, (hh // 2) * LANES:(hh // 2 + 1) * LANES]
        if hh % 2 == 0:
            qh = jnp.where(lane < HEAD_DIM, qb, jnp.zeros_like(qb))
        else:
            qh = jnp.where(lane >= HEAD_DIM, qb, jnp.zeros_like(qb))
        s_meta = lax.dot_general(qh, km, CONTRACT_LAST, preferred_element_type=F32) * scale
        m = jnp.maximum(jnp.max(s_meta, axis=-1, keepdims=True), sink)
        s_band = []
        for p in range(3):
            s = lax.dot_general(qh, kblocks[p], CONTRACT_LAST, preferred_element_type=F32) * scale
            s = jnp.where(valids[p], s - slope * dists[p], NEG_INF)
            s_band.append(s)
            m = jnp.maximum(m, jnp.max(s, axis=-1, keepdims=True))
        e_meta = jnp.exp(s_meta - m)
        denom = jnp.sum(e_meta, axis=-1, keepdims=True) + jnp.exp(sink - m)
        acc = jnp.dot(e_meta.astype(BF16), vm, preferred_element_type=F32)
        for p in range(3):
            e = jnp.exp(s_band[p] - m)
            denom = denom + jnp.sum(e, axis=-1, keepdims=True)
            acc = acc + jnp.dot(e.astype(BF16), vblocks[p], preferred_element_type=F32)
        results.append(acc / denom)
    for lb in range(SW_GROUP // 2):
        o_ref[:, lb * LANES:(lb + 1) * LANES] = jnp.where(
            lane < HEAD_DIM, results[2 * lb], results[2 * lb + 1]).astype(o_ref.dtype)


def _win_attn(proj, proj_meta, sink):
    b, s, _ = proj.shape
    nblk = s // WINDOW
    gw = SW_GROUP * HEAD_DIM

    def kv_spec(col, shift):
        return pl.BlockSpec(
            (None, WINDOW, LANES),
            lambda bi, g, c: (bi, jnp.clip(c + shift, 0, nblk - 1), col + g))

    return pl.pallas_call(
        functools.partial(_win_attn_kernel, seq=s),
        grid=(b, SW_KV_HEADS, nblk),
        in_specs=[
            pl.BlockSpec((None, WINDOW, gw), lambda bi, g, c: (bi, c, COL_QS * LANES // gw + g)),
            kv_spec(COL_KS, -1), kv_spec(COL_KS, 0), kv_spec(COL_KS, 1),
            kv_spec(COL_VS, -1), kv_spec(COL_VS, 0), kv_spec(COL_VS, 1),
            pl.BlockSpec((N_META, LANES), lambda bi, g, c: (0, COL_KS + g)),
            pl.BlockSpec((N_META, LANES), lambda bi, g, c: (0, COL_VS + g)),
            pl.BlockSpec(memory_space=pltpu.SMEM),
        ],
        out_specs=pl.BlockSpec((None, WINDOW, gw), lambda bi, g, c: (bi, c, g)),
        out_shape=jax.ShapeDtypeStruct((b, s, SW_HEADS * HEAD_DIM), BF16),
        compiler_params=_params(("parallel", "parallel", "arbitrary")),
        name="win_attn",
    )(proj, proj, proj, proj, proj, proj, proj, proj_meta, proj_meta, sink)


def _out_router_kernel(od_ref, os_ref, x_ref, wod_ref, wos_ref, g2_ref, wr_hi_ref, wr_lo_ref, br_ref,
                       h_ref, a_ref, gate_ref):
    h = (x_ref[...]
         + jnp.dot(od_ref[...], wod_ref[...], preferred_element_type=F32)
         + jnp.dot(os_ref[...], wos_ref[...], preferred_element_type=F32))
    h_ref[...] = h
    a = h * lax.rsqrt(jnp.mean(h * h, axis=-1, keepdims=True) + EPS) * g2_ref[...]
    a_hi = a.astype(BF16)
    a_ref[...] = a_hi
    a_lo = (a - a_hi.astype(F32)).astype(BF16)
    logits = (jnp.dot(a_hi, wr_hi_ref[...], preferred_element_type=F32)
              + jnp.dot(a_lo, wr_hi_ref[...], preferred_element_type=F32)
              + jnp.dot(a_hi, wr_lo_ref[...], preferred_element_type=F32)) + br_ref[...]
    tm = logits.shape[0]
    lane = lax.broadcasted_iota(jnp.int32, (tm, LANES), 1)
    big = jnp.int32(LANES)

    def first_argmax(mask):
        vals = jnp.where(mask, logits, NEG_INF)
        mx = jnp.max(vals, axis=-1, keepdims=True)
        idx = jnp.min(jnp.where(mask & (vals == mx), lane, big), axis=-1, keepdims=True)
        return mx, idx

    gmask = (lane >= N_EXPERTS) & (lane < N_EXPERTS + N_GROUPS)
    gmax, gidx = first_argmax(gmask)
    gsum = jnp.sum(jnp.where(gmask, jnp.exp(logits - gmax), 0.0), axis=-1, keepdims=True)
    g_w = 1.0 / gsum
    in_group = (lane // EXPERTS_PER_GROUP) == (gidx - N_EXPERTS)
    m1, i1 = first_argmax(in_group)
    m2, i2 = first_argmax(in_group & (lane != i1))
    r = jnp.exp(m2 - m1)
    w1 = g_w / (1.0 + r)
    w2 = g_w * r / (1.0 + r)
    gate_ref[...] = jnp.where(lane == i1, w1, 0.0) + jnp.where(lane == i2, w2, 0.0)


def _out_router(od, os_, x2d, wod, wos, g2, wr_hi, wr_lo, br, tm):
    n, d = x2d.shape
    half = od.shape[1]
    row = lambda i: (i, 0)
    const = lambda i: (0, 0)
    return pl.pallas_call(
        _out_router_kernel,
        grid=(n // tm,),
        in_specs=[
            pl.BlockSpec((tm, half), row),
            pl.BlockSpec((tm, half), row),
            pl.BlockSpec((tm, d), row),
            pl.BlockSpec((half, d), const),
            pl.BlockSpec((half, d), const),
            pl.BlockSpec((1, d), const),
            pl.BlockSpec((d, LANES), const),
            pl.BlockSpec((d, LANES), const),
            pl.BlockSpec((1, LANES), const),
        ],
        out_specs=[
            pl.BlockSpec((tm, d), row),
            pl.BlockSpec((tm, d), row),
            pl.BlockSpec((tm, LANES), row),
        ],
        out_shape=[
            jax.ShapeDtypeStruct((n, d), F32),
            jax.ShapeDtypeStruct((n, d), BF16),
            jax.ShapeDtypeStruct((n, LANES), F32),
        ],
        compiler_params=_params(("parallel",)),
        name="out_router",
    )(od, os_, x2d, wod, wos, g2, wr_hi, wr_lo, br)


def _moe_kernel(a_ref, gate_ref, h_ref, wg_ref, wu_ref, wd_ref, fg_ref, o_ref, acc_ref):
    e = pl.program_id(1)

    @pl.when(e == 0)
    def _():
        acc_ref[...] = jnp.zeros_like(acc_ref)

    a = a_ref[...]
    lane = lax.broadcasted_iota(jnp.int32, gate_ref.shape, 1)
    ge = jnp.sum(jnp.where(lane == e, gate_ref[...], 0.0), axis=-1, keepdims=True)
    hg = jnp.dot(a, wg_ref[...], preferred_element_type=F32)
    hu = jnp.dot(a, wu_ref[...], preferred_element_type=F32)
    hid = (hg * jax.nn.sigmoid(hg)) * hu * ge
    acc_ref[...] += jnp.dot(hid.astype(BF16), wd_ref[...], preferred_element_type=F32)

    @pl.when(e == pl.num_programs(1) - 1)
    def _():
        y = h_ref[...] + acc_ref[...]
        o_ref[...] = y * lax.rsqrt(jnp.mean(y * y, axis=-1, keepdims=True) + EPS) * fg_ref[...]


def _moe(a2, gate, h, wg, wu, wd, fg, tm):
    n, d = h.shape
    ne, _, de = wg.shape
    row = lambda i, e: (i, 0)
    return pl.pallas_call(
        _moe_kernel,
        grid=(n // tm, ne),
        in_specs=[
            pl.BlockSpec((tm, d), row),
            pl.BlockSpec((tm, LANES), row),
            pl.BlockSpec((tm, d), row),
            pl.BlockSpec((None, d, de), lambda i, e: (e, 0, 0)),
            pl.BlockSpec((None, d, de), lambda i, e: (e, 0, 0)),
            pl.BlockSpec((None, de, d), lambda i, e: (e, 0, 0)),
            pl.BlockSpec((1, d), lambda i, e: (0, 0)),
        ],
        out_specs=pl.BlockSpec((tm, d), row),
        out_shape=jax.ShapeDtypeStruct((n, d), F32),
        scratch_shapes=[pltpu.VMEM((tm, d), F32)],
        compiler_params=_params(("parallel", "arbitrary")),
        name="moe",
    )(a2, gate, h, wg, wu, wd, fg)


def _dup_heads(w, n_heads):
    d = w.shape[0]
    w = w.reshape(d, n_heads, 1, HEAD_DIM)
    return jnp.broadcast_to(w, (d, n_heads, 2, HEAD_DIM)).reshape(d, n_heads * 2 * HEAD_DIM)


def _encoder(x, proj_meta, wts):
    b, s, d = x.shape
    x2d = x.reshape(b * s, d)
    proj = _norm_proj(x2d, wts["g1"], wts["w_ext"], tm=512).reshape(b, s, -1)
    od = _diff_attn(proj, proj_meta, wts["lam_vecs"], wts["subln_g"], bq=256, bk=256)
    os_ = _win_attn(proj, proj_meta, wts["sink"])
    h, a2, gate = _out_router(od.reshape(b * s, -1), os_.reshape(b * s, -1), x2d, wts["wod"], wts["wos"],
                              wts["g2"], wts["wr_hi"], wts["wr_lo"], wts["br"], tm=512)
    y = _moe(a2, gate, h, wts["wg"], wts["wu"], wts["wd"], wts["fg"], tm=1024)
    return y.reshape(b, s, d)


def kernel(x_prompt, x_sample, meta, norm1_g, w_in, lam_q1, lam_k1, lam_q2, lam_k2, subln_g, sink, w_out,
           norm2_g, w_gr, b_gr, w_er, b_er, w_gate, w_up, w_down, final_g):
    d = x_prompt.shape[-1]
    w = w_in[0]
    c_kd = 2 * DA_HEADS * HEAD_DIM
    c_vd = 2 * c_kd
    c_qs = c_vd + DA_HEADS * 2 * HEAD_DIM
    c_ks = c_qs + SW_HEADS * HEAD_DIM
    c_vs = c_ks + SW_KV_HEADS * HEAD_DIM
    w_ext = jnp.concatenate(
        [w[:, :c_ks], _dup_heads(w[:, c_ks:c_vs], SW_KV_HEADS), _dup_heads(w[:, c_vs:], SW_KV_HEADS)],
        axis=1).astype(BF16)
    w_router = jnp.concatenate([w_er[0], w_gr[0]], axis=1)
    w_router = jnp.pad(w_router, ((0, 0), (0, LANES - w_router.shape[1])))
    wr_hi = w_router.astype(BF16)
    wr_lo = (w_router - wr_hi.astype(F32)).astype(BF16)
    br = jnp.pad(jnp.concatenate([b_er[0], b_gr[0]]), (0, LANES - N_EXPERTS - N_GROUPS)).reshape(1, LANES)
    wo = w_out[0].astype(BF16)
    half = DA_HEADS * 2 * HEAD_DIM
    wts = dict(
        g1=norm1_g[0].reshape(1, d), w_ext=w_ext,
        lam_vecs=(lam_q1[0].reshape(1, -1), lam_k1[0].reshape(1, -1),
                  lam_q2[0].reshape(1, -1), lam_k2[0].reshape(1, -1)),
        subln_g=subln_g[0].reshape(1, -1), sink=sink[0].reshape(1, -1),
        wod=wo[:half], wos=wo[half:], g2=norm2_g[0].reshape(1, d),
        wr_hi=wr_hi, wr_lo=wr_lo, br=br,
        wg=w_gate[0].astype(BF16), wu=w_up[0].astype(BF16), wd=w_down[0].astype(BF16),
        fg=final_g.reshape(1, d),
    )
    proj_meta = _norm_proj(meta, wts["g1"], w_ext, tm=N_META)
    return _encoder(x_prompt, proj_meta, wts), _encoder(x_sample, proj_meta, wts)
```

```python
import functools
import math

import jax
import jax.numpy as jnp
from jax import lax
from jax.experimental import pallas as pl
from jax.experimental.pallas import tpu as pltpu

F32 = jnp.float32
BF16 = jnp.bfloat16

N_META = 16
HEAD_DIM = 64
DA_HEADS = 4
SW_HEADS = 8
SW_KV_HEADS = 2
SW_GROUP = SW_HEADS // SW_KV_HEADS
WINDOW = 128
N_GROUPS = 4
EXPERTS_PER_GROUP = 8
N_EXPERTS = N_GROUPS * EXPERTS_PER_GROUP
EPS = 1e-6
SUBLN_EPS = 1e-5
NEG_INF = -1e30
LAM_INIT = 0.8 - 0.6 * math.exp(-0.3 * 0)
LANES = 128
VMEM_LIMIT = 48 * 1024 * 1024

COL_QD, COL_KD, COL_VD, COL_QS, COL_KS, COL_VS, N_COLBLK = 0, 4, 8, 12, 16, 18, 20
CONTRACT_LAST = (((1,), (1,)), ((), ()))


def _params(sem):
    return pltpu.CompilerParams(dimension_semantics=sem, vmem_limit_bytes=VMEM_LIMIT)


def _norm_proj_kernel(x_ref, g_ref, w_ref, o_ref):
    x = x_ref[...]
    ms = jnp.mean(x * x, axis=-1, keepdims=True)
    y = (x * lax.rsqrt(ms + EPS) * g_ref[...]).astype(BF16)
    o_ref[...] = jnp.dot(y, w_ref[...], preferred_element_type=F32).astype(o_ref.dtype)


def _norm_proj(x2d, g, w_ext, tm):
    n, d = x2d.shape
    wcols = w_ext.shape[1]
    return pl.pallas_call(
        _norm_proj_kernel,
        grid=(n // tm,),
        in_specs=[
            pl.BlockSpec((tm, d), lambda i: (i, 0)),
            pl.BlockSpec((1, d), lambda i: (0, 0)),
            pl.BlockSpec((d, wcols), lambda i: (0, 0)),
        ],
        out_specs=pl.BlockSpec((tm, wcols), lambda i: (i, 0)),
        out_shape=jax.ShapeDtypeStruct((n, wcols), BF16),
        compiler_params=_params(("parallel",)),
        name="norm_proj",
    )(x2d, g, w_ext)


POS_SPLIT = 64
FEAT_ONE_A, FEAT_ONE_B, FEAT_HI, FEAT_LO = 0, 1, 2, 3


def _key_pos_features(seq):
    j = jnp.arange(seq, dtype=jnp.int32)[:, None]
    lane = jnp.arange(LANES, dtype=jnp.int32)[None, :]
    hi = (j // POS_SPLIT * POS_SPLIT).astype(F32)
    lo = (j % POS_SPLIT).astype(F32)
    feat = jnp.where(lane <= FEAT_ONE_B, 1.0, jnp.where(lane == FEAT_HI, hi, jnp.where(lane == FEAT_LO, lo, 0.0)))
    return feat.astype(BF16)


def _diff_attn_kernel(q_ref, k_ref, v_ref, kf_ref, km_ref, vm_ref, lq1_ref, lk1_ref, lq2_ref, lk2_ref, sg_ref,
                      o_ref, qe_ref, acc_ref, m_ref, s0_ref, s1_ref, *, bq, bk, seq):
    h = pl.program_id(1)
    i = pl.program_id(2)
    scale = 1.0 / math.sqrt(HEAD_DIM)
    slope = jnp.exp2(jnp.full((1, 1), -8.0 / DA_HEADS, F32) * (h + 1).astype(F32))
    lane = lax.broadcasted_iota(jnp.int32, (1, LANES), 1)
    q = q_ref[...] * jnp.asarray(scale, BF16)
    zero = jnp.zeros_like(q)
    qmaps = (jnp.where(lane < HEAD_DIM, q, zero), jnp.where(lane >= HEAD_DIM, q, zero))
    qpos = i * bq + lax.broadcasted_iota(jnp.int32, (bq, 1), 0)
    q_hi = (qpos // POS_SPLIT * POS_SPLIT).astype(F32)
    q_lo = (qpos % POS_SPLIT).astype(F32)
    feat = jnp.where(lane == FEAT_ONE_A, -slope * q_hi,
                     jnp.where(lane == FEAT_ONE_B, -slope * q_lo, jnp.where(lane <= FEAT_LO, slope, 0.0)))
    feats = (feat.astype(BF16), (-feat).astype(BF16))
    for side in range(2):
        for c in range(2):
            qe_ref[side, c * bq:(c + 1) * bq, :LANES] = qmaps[c]
            qe_ref[side, c * bq:(c + 1) * bq, LANES:] = feats[side]

    ones_blk = jnp.broadcast_to(jnp.where(lane == 0, 1.0, 0.0).astype(BF16), (bk, LANES))

    def attend(s, v_ext):
        m_old = m_ref[...]
        m_new = jnp.maximum(m_old, jnp.max(s, axis=-1, keepdims=True))
        alpha = jnp.exp(m_old - m_new)
        p = jnp.exp(s - jnp.tile(m_new, (1, s.shape[1] // LANES))).astype(BF16)
        acc_ref[...] = jnp.tile(alpha, (1, 2)) * acc_ref[...] + jnp.dot(p, v_ext, preferred_element_type=F32)
        m_ref[...] = m_new

    m_ref[...] = jnp.full(m_ref.shape, NEG_INF, F32)
    acc_ref[...] = jnp.zeros(acc_ref.shape, F32)

    nblk = seq // bk
    diag = (i * bq) // bk

    def key_block(pos):
        t = pos - 1
        return jnp.where(pos == 0, diag, t + jnp.where(t >= diag, 1, 0))

    def k_ext_at(kb):
        start = pl.multiple_of(kb * bk, bk)
        return jnp.concatenate([k_ref[pl.ds(start, bk), :], kf_ref[pl.ds(start, bk), :]], axis=1)

    def scores_into(dst_ref, pos):
        kb = key_block(pos)
        side = jnp.where(kb > diag, 1, 0)
        dst_ref[...] = lax.dot_general(qe_ref[side], k_ext_at(kb), CONTRACT_LAST, preferred_element_type=F32)

    def consume(src_ref, pos):
        start = pl.multiple_of(key_block(pos) * bk, bk)
        attend(src_ref[...], jnp.concatenate([v_ref[pl.ds(start, bk), :], ones_blk], axis=1))

    k_diag = k_ext_at(diag)
    s0_ref[...] = jnp.minimum(
        lax.dot_general(qe_ref[0], k_diag, CONTRACT_LAST, preferred_element_type=F32),
        lax.dot_general(qe_ref[1], k_diag, CONTRACT_LAST, preferred_element_type=F32))

    s = lax.dot_general(qe_ref[0, :, :LANES], km_ref[...], CONTRACT_LAST, preferred_element_type=F32)
    s = jnp.where(lane < N_META, s, NEG_INF)
    attend(s, jnp.concatenate([vm_ref[...], ones_blk[:LANES]], axis=1))

    if nblk == 1:
        consume(s0_ref, 0)
    else:
        def body(u, carry):
            scores_into(s1_ref, 2 * u + 1)
            consume(s0_ref, 2 * u)
            scores_into(s0_ref, 2 * u + 2)
            consume(s1_ref, 2 * u + 1)
            return carry

        lax.fori_loop(0, (nblk - 2) // 2, body, 0)
        scores_into(s1_ref, nblk - 1)
        consume(s0_ref, nblk - 2)
        consume(s1_ref, nblk - 1)

    lam = (jnp.exp(jnp.sum(lq1_ref[...] * lk1_ref[...], axis=-1, keepdims=True))
           - jnp.exp(jnp.sum(lq2_ref[...] * lk2_ref[...], axis=-1, keepdims=True)) + LAM_INIT)
    o1 = acc_ref[:bq, :LANES] / acc_ref[:bq, LANES:LANES + 1]
    o2 = acc_ref[bq:, :LANES] / acc_ref[bq:, LANES:LANES + 1]
    o = o1 - lam * o2
    o = o * lax.rsqrt(jnp.mean(o * o, axis=-1, keepdims=True) + SUBLN_EPS) * sg_ref[...]
    o_ref[...] = (o * (1.0 - LAM_INIT)).astype(o_ref.dtype)


def _diff_attn(proj, proj_meta, lam_vecs, subln_g, bq, bk):
    b, s, _ = proj.shape
    assert bk % bq == 0 and s % bk == 0 and (s // bk == 1 or (s // bk) % 2 == 0)
    vec = pl.BlockSpec((1, HEAD_DIM), lambda bi, h, i: (0, 0))
    return pl.pallas_call(
        functools.partial(_diff_attn_kernel, bq=bq, bk=bk, seq=s),
        grid=(b, DA_HEADS, s // bq),
        in_specs=[
            pl.BlockSpec((None, bq, LANES), lambda bi, h, i: (bi, i, COL_QD + h)),
            pl.BlockSpec((None, s, LANES), lambda bi, h, i: (bi, 0, COL_KD + h)),
            pl.BlockSpec((None, s, LANES), lambda bi, h, i: (bi, 0, COL_VD + h)),
            pl.BlockSpec((s, LANES), lambda bi, h, i: (0, 0)),
            pl.BlockSpec((LANES, LANES), lambda bi, h, i: (0, COL_KD + h)),
            pl.BlockSpec((LANES, LANES), lambda bi, h, i: (0, COL_VD + h)),
            vec, vec, vec, vec,
            pl.BlockSpec((1, LANES), lambda bi, h, i: (0, 0)),
        ],
        out_specs=pl.BlockSpec((None, bq, LANES), lambda bi, h, i: (bi, i, h)),
        out_shape=jax.ShapeDtypeStruct((b, s, DA_HEADS * LANES), BF16),
        scratch_shapes=[
            pltpu.VMEM((2, 2 * bq, 2 * LANES), BF16),
            pltpu.VMEM((2 * bq, 2 * LANES), F32),
            pltpu.VMEM((2 * bq, LANES), F32),
            pltpu.VMEM((2 * bq, bk), F32),
            pltpu.VMEM((2 * bq, bk), F32),
        ],
        compiler_params=_params(("parallel", "parallel", "arbitrary")),
        name="diff_attn",
    )(proj, proj, proj, _key_pos_features(s), proj_meta, proj_meta, *lam_vecs, subln_g)


def _win_attn_kernel(q_ref, kp_ref, kc_ref, kn_ref, vp_ref, vc_ref, vn_ref, km_ref, vm_ref, sink_ref, o_ref,
                     *, seq):
    g = pl.program_id(1)
    c = pl.program_id(2)
    blk = WINDOW
    scale = 1.0 / math.sqrt(HEAD_DIM)
    lane = lax.broadcasted_iota(jnp.int32, (1, LANES), 1)
    qpos = c * blk + lax.broadcasted_iota(jnp.int32, (blk, 1), 0)
    km = km_ref[...]
    vm = vm_ref[...]
    kblocks = (kp_ref[...], kc_ref[...], kn_ref[...])
    vblocks = (vp_ref[...], vc_ref[...], vn_ref[...])
    dists, valids = [], []
    for p in range(3):
        kpos = (c + p - 1) * blk + lax.broadcasted_iota(jnp.int32, (1, blk), 1)
        diff = jnp.abs(qpos - kpos)
        dists.append(diff.astype(F32))
        valids.append((diff <= WINDOW) & (kpos >= 0) & (kpos < seq))

    results = []
    for hh in range(SW_GROUP):
        head = g * SW_GROUP + hh
        slope = jnp.exp2(jnp.full((1, 1), -8.0 / SW_HEADS, F32) * (head + 1).astype(F32))
        sink = jnp.full((1, 1), 1.0, F32) * sink_ref[0, head]
        qb = q_ref[:, (hh // 2) * LANES:(hh // 2 + 1) * LANES]
        if hh % 2 == 0:
            qh = jnp.where(lane < HEAD_DIM, qb, jnp.zeros_like(qb))
        else:
            qh = jnp.where(lane >= HEAD_DIM, qb, jnp.zeros_like(qb))
        s_meta = lax.dot_general(qh, km, CONTRACT_LAST, preferred_element_type=F32) * scale
        m = jnp.maximum(jnp.max(s_meta, axis=-1, keepdims=True), sink)
        s_band = []
        for p in range(3):
            s = lax.dot_general(qh, kblocks[p], CONTRACT_LAST, preferred_element_type=F32) * scale
            s = jnp.where(valids[p], s - slope * dists[p], NEG_INF)
            s_band.append(s)
            m = jnp.maximum(m, jnp.max(s, axis=-1, keepdims=True))
        e_meta = jnp.exp(s_meta - m)
        denom = jnp.sum(e_meta, axis=-1, keepdims=True) + jnp.exp(sink - m)
        acc = jnp.dot(e_meta.astype(BF16), vm, preferred_element_type=F32)
        for p in range(3):
            e = jnp.exp(s_band[p] - m)
            denom = denom + jnp.sum(e, axis=-1, keepdims=True)
            acc = acc + jnp.dot(e.astype(BF16), vblocks[p], preferred_element_type=F32)
        results.append(acc / denom)
    for lb in range(SW_GROUP // 2):
        o_ref[:, lb * LANES:(lb + 1) * LANES] = jnp.where(
            lane < HEAD_DIM, results[2 * lb], results[2 * lb + 1]).astype(o_ref.dtype)


def _win_attn(proj, proj_meta, sink):
    b, s, _ = proj.shape
    nblk = s // WINDOW
    gw = SW_GROUP * HEAD_DIM

    def kv_spec(col, shift):
        return pl.BlockSpec(
            (None, WINDOW, LANES),
            lambda bi, g, c: (bi, jnp.clip(c + shift, 0, nblk - 1), col + g))

    return pl.pallas_call(
        functools.partial(_win_attn_kernel, seq=s),
        grid=(b, SW_KV_HEADS, nblk),
        in_specs=[
            pl.BlockSpec((None, WINDOW, gw), lambda bi, g, c: (bi, c, COL_QS * LANES // gw + g)),
            kv_spec(COL_KS, -1), kv_spec(COL_KS, 0), kv_spec(COL_KS, 1),
            kv_spec(COL_VS, -1), kv_spec(COL_VS, 0), kv_spec(COL_VS, 1),
            pl.BlockSpec((N_META, LANES), lambda bi, g, c: (0, COL_KS + g)),
            pl.BlockSpec((N_META, LANES), lambda bi, g, c: (0, COL_VS + g)),
            pl.BlockSpec(memory_space=pltpu.SMEM),
        ],
        out_specs=pl.BlockSpec((None, WINDOW, gw), lambda bi, g, c: (bi, c, g)),
        out_shape=jax.ShapeDtypeStruct((b, s, SW_HEADS * HEAD_DIM), BF16),
        compiler_params=_params(("parallel", "parallel", "arbitrary")),
        name="win_attn",
    )(proj, proj, proj, proj, proj, proj, proj, proj_meta, proj_meta, sink)


def _out_router_kernel(od_ref, os_ref, x_ref, wod_ref, wos_ref, g2_ref, wr_hi_ref, wr_lo_ref, br_ref,
                       h_ref, a_ref, gate_ref):
    h = (x_ref[...]
         + jnp.dot(od_ref[...], wod_ref[...], preferred_element_type=F32)
         + jnp.dot(os_ref[...], wos_ref[...], preferred_element_type=F32))
    h_ref[...] = h
    a = h * lax.rsqrt(jnp.mean(h * h, axis=-1, keepdims=True) + EPS) * g2_ref[...]
    a_hi = a.astype(BF16)
    a_ref[...] = a_hi
    a_lo = (a - a_hi.astype(F32)).astype(BF16)
    logits = (jnp.dot(a_hi, wr_hi_ref[...], preferred_element_type=F32)
              + jnp.dot(a_lo, wr_hi_ref[...], preferred_element_type=F32)
              + jnp.dot(a_hi, wr_lo_ref[...], preferred_element_type=F32)) + br_ref[...]
    tm = logits.shape[0]
    lane = lax.broadcasted_iota(jnp.int32, (tm, LANES), 1)
    big = jnp.int32(LANES)

    def first_argmax(mask):
        vals = jnp.where(mask, logits, NEG_INF)
        mx = jnp.max(vals, axis=-1, keepdims=True)
        idx = jnp.min(jnp.where(mask & (vals == mx), lane, big), axis=-1, keepdims=True)
        return mx, idx

    gmask = (lane >= N_EXPERTS) & (lane < N_EXPERTS + N_GROUPS)
    gmax, gidx = first_argmax(gmask)
    gsum = jnp.sum(jnp.where(gmask, jnp.exp(logits - gmax), 0.0), axis=-1, keepdims=True)
    g_w = 1.0 / gsum
    in_group = (lane // EXPERTS_PER_GROUP) == (gidx - N_EXPERTS)
    m1, i1 = first_argmax(in_group)
    m2, i2 = first_argmax(in_group & (lane != i1))
    r = jnp.exp(m2 - m1)
    w1 = g_w / (1.0 + r)
    w2 = g_w * r / (1.0 + r)
    gate_ref[...] = jnp.where(lane == i1, w1, 0.0) + jnp.where(lane == i2, w2, 0.0)


def _out_router(od, os_, x2d, wod, wos, g2, wr_hi, wr_lo, br, tm):
    n, d = x2d.shape
    half = od.shape[1]
    row = lambda i: (i, 0)
    const = lambda i: (0, 0)
    return pl.pallas_call(
        _out_router_kernel,
        grid=(n // tm,),
        in_specs=[
            pl.BlockSpec((tm, half), row),
            pl.BlockSpec((tm, half), row),
            pl.BlockSpec((tm, d), row),
            pl.BlockSpec((half, d), const),
            pl.BlockSpec((half, d), const),
            pl.BlockSpec((1, d), const),
            pl.BlockSpec((d, LANES), const),
            pl.BlockSpec((d, LANES), const),
            pl.BlockSpec((1, LANES), const),
        ],
        out_specs=[
            pl.BlockSpec((tm, d), row),
            pl.BlockSpec((tm, d), row),
            pl.BlockSpec((tm, LANES), row),
        ],
        out_shape=[
            jax.ShapeDtypeStruct((n, d), F32),
            jax.ShapeDtypeStruct((n, d), BF16),
            jax.ShapeDtypeStruct((n, LANES), F32),
        ],
        compiler_params=_params(("parallel",)),
        name="out_router",
    )(od, os_, x2d, wod, wos, g2, wr_hi, wr_lo, br)


def _moe_kernel(a_ref, gate_ref, h_ref, wg_ref, wu_ref, wd_ref, fg_ref, o_ref, acc_ref):
    e = pl.program_id(1)

    @pl.when(e == 0)
    def _():
        acc_ref[...] = jnp.zeros_like(acc_ref)

    a = a_ref[...]
    lane = lax.broadcasted_iota(jnp.int32, gate_ref.shape, 1)
    ge = jnp.sum(jnp.where(lane == e, gate_ref[...], 0.0), axis=-1, keepdims=True)
    hg = jnp.dot(a, wg_ref[...], preferred_element_type=F32)
    hu = jnp.dot(a, wu_ref[...], preferred_element_type=F32)
    hid = (hg * jax.nn.sigmoid(hg)) * hu * ge
    acc_ref[...] += jnp.dot(hid.astype(BF16), wd_ref[...], preferred_element_type=F32)

    @pl.when(e == pl.num_programs(1) - 1)
    def _():
        y = h_ref[...] + acc_ref[...]
        o_ref[...] = y * lax.rsqrt(jnp.mean(y * y, axis=-1, keepdims=True) + EPS) * fg_ref[...]


def _moe(a2, gate, h, wg, wu, wd, fg, tm):
    n, d = h.shape
    ne, _, de = wg.shape
    row = lambda i, e: (i, 0)
    return pl.pallas_call(
        _moe_kernel,
        grid=(n // tm, ne),
        in_specs=[
            pl.BlockSpec((tm, d), row),
            pl.BlockSpec((tm, LANES), row),
            pl.BlockSpec((tm, d), row),
            pl.BlockSpec((None, d, de), lambda i, e: (e, 0, 0)),
            pl.BlockSpec((None, d, de), lambda i, e: (e, 0, 0)),
            pl.BlockSpec((None, de, d), lambda i, e: (e, 0, 0)),
            pl.BlockSpec((1, d), lambda i, e: (0, 0)),
        ],
        out_specs=pl.BlockSpec((tm, d), row),
        out_shape=jax.ShapeDtypeStruct((n, d), F32),
        scratch_shapes=[pltpu.VMEM((tm, d), F32)],
        compiler_params=_params(("parallel", "arbitrary")),
        name="moe",
    )(a2, gate, h, wg, wu, wd, fg)


def _dup_heads(w, n_heads):
    d = w.shape[0]
    w = w.reshape(d, n_heads, 1, HEAD_DIM)
    return jnp.broadcast_to(w, (d, n_heads, 2, HEAD_DIM)).reshape(d, n_heads * 2 * HEAD_DIM)


def _encoder(x, proj_meta, wts):
    b, s, d = x.shape
    x2d = x.reshape(b * s, d)
    proj = _norm_proj(x2d, wts["g1"], wts["w_ext"], tm=512).reshape(b, s, -1)
    od = _diff_attn(proj, proj_meta, wts["lam_vecs"], wts["subln_g"], bq=256, bk=min(512, s))
    os_ = _win_attn(proj, proj_meta, wts["sink"])
    h, a2, gate = _out_router(od.reshape(b * s, -1), os_.reshape(b * s, -1), x2d, wts["wod"], wts["wos"],
                              wts["g2"], wts["wr_hi"], wts["wr_lo"], wts["br"], tm=512)
    y = _moe(a2, gate, h, wts["wg"], wts["wu"], wts["wd"], wts["fg"], tm=1024)
    return y.reshape(b, s, d)


def kernel(x_prompt, x_sample, meta, norm1_g, w_in, lam_q1, lam_k1, lam_q2, lam_k2, subln_g, sink, w_out,
           norm2_g, w_gr, b_gr, w_er, b_er, w_gate, w_up, w_down, final_g):
    d = x_prompt.shape[-1]
    w = w_in[0]
    c_kd = 2 * DA_HEADS * HEAD_DIM
    c_vd = 2 * c_kd
    c_qs = c_vd + DA_HEADS * 2 * HEAD_DIM
    c_ks = c_qs + SW_HEADS * HEAD_DIM
    c_vs = c_ks + SW_KV_HEADS * HEAD_DIM
    w_ext = jnp.concatenate(
        [w[:, :c_ks], _dup_heads(w[:, c_ks:c_vs], SW_KV_HEADS), _dup_heads(w[:, c_vs:], SW_KV_HEADS)],
        axis=1).astype(BF16)
    w_router = jnp.concatenate([w_er[0], w_gr[0]], axis=1)
    w_router = jnp.pad(w_router, ((0, 0), (0, LANES - w_router.shape[1])))
    wr_hi = w_router.astype(BF16)
    wr_lo = (w_router - wr_hi.astype(F32)).astype(BF16)
    br = jnp.pad(jnp.concatenate([b_er[0], b_gr[0]]), (0, LANES - N_EXPERTS - N_GROUPS)).reshape(1, LANES)
    wo = w_out[0].astype(BF16)
    half = DA_HEADS * 2 * HEAD_DIM
    wts = dict(
        g1=norm1_g[0].reshape(1, d), w_ext=w_ext,
        lam_vecs=(lam_q1[0].reshape(1, -1), lam_k1[0].reshape(1, -1),
                  lam_q2[0].reshape(1, -1), lam_k2[0].reshape(1, -1)),
        subln_g=subln_g[0].reshape(1, -1), sink=sink[0].reshape(1, -1),
        wod=wo[:half], wos=wo[half:], g2=norm2_g[0].reshape(1, d),
        wr_hi=wr_hi, wr_lo=wr_lo, br=br,
        wg=w_gate[0].astype(BF16), wu=w_up[0].astype(BF16), wd=w_down[0].astype(BF16),
        fg=final_g.reshape(1, d),
    )
    proj_meta = _norm_proj(meta, wts["g1"], w_ext, tm=N_META)
    proj_meta = jnp.pad(proj_meta, ((0, LANES - N_META), (0, 0)))
    return _encoder(x_prompt, proj_meta, wts), _encoder(x_sample, proj_meta, wts)
```

```python
import functools
import math

import jax
import jax.numpy as jnp
import numpy as np
from jax import lax
from jax.experimental import pallas as pl
from jax.experimental.pallas import tpu as pltpu
from jax.experimental.pallas import tpu_sc as plsc

F32 = jnp.float32
BF16 = jnp.bfloat16

N_META = 16
HEAD_DIM = 64
DA_HEADS = 4
SW_HEADS = 8
SW_KV_HEADS = 2
SW_GROUP = SW_HEADS // SW_KV_HEADS
WINDOW = 128
N_GROUPS = 4
EXPERTS_PER_GROUP = 8
N_EXPERTS = N_GROUPS * EXPERTS_PER_GROUP
EPS = 1e-6
SUBLN_EPS = 1e-5
NEG_INF = -1e30
LAM_INIT = 0.8 - 0.6 * math.exp(-0.3 * 0)
LANES = 128
VMEM_LIMIT = 48 * 1024 * 1024

COL_QD, COL_KD, COL_VD, COL_QS, COL_KS, COL_VS, N_COLBLK = 0, 4, 8, 12, 16, 18, 20
CONTRACT_LAST = (((1,), (1,)), ((), ()))


def _params(sem):
    return pltpu.CompilerParams(dimension_semantics=sem, vmem_limit_bytes=VMEM_LIMIT)


def _norm_proj_kernel(x_ref, g_ref, w_ref, o_ref):
    x = x_ref[...]
    ms = jnp.mean(x * x, axis=-1, keepdims=True)
    y = (x * lax.rsqrt(ms + EPS) * g_ref[...]).astype(BF16)
    o_ref[...] = jnp.dot(y, w_ref[...], preferred_element_type=F32).astype(o_ref.dtype)


def _norm_proj(x2d, g, w_ext, tm):
    n, d = x2d.shape
    wcols = w_ext.shape[1]
    return pl.pallas_call(
        _norm_proj_kernel,
        grid=(n // tm,),
        in_specs=[
            pl.BlockSpec((tm, d), lambda i: (i, 0)),
            pl.BlockSpec((1, d), lambda i: (0, 0)),
            pl.BlockSpec((d, wcols), lambda i: (0, 0)),
        ],
        out_specs=pl.BlockSpec((tm, wcols), lambda i: (i, 0)),
        out_shape=jax.ShapeDtypeStruct((n, wcols), BF16),
        compiler_params=_params(("parallel",)),
        name="norm_proj",
    )(x2d, g, w_ext)


POS_SPLIT = 64
FEAT_ONE_A, FEAT_ONE_B, FEAT_HI, FEAT_LO = 0, 1, 2, 3


def _key_pos_features(seq):
    j = jnp.arange(seq, dtype=jnp.int32)[:, None]
    lane = jnp.arange(LANES, dtype=jnp.int32)[None, :]
    hi = (j // POS_SPLIT * POS_SPLIT).astype(F32)
    lo = (j % POS_SPLIT).astype(F32)
    feat = jnp.where(lane <= FEAT_ONE_B, 1.0, jnp.where(lane == FEAT_HI, hi, jnp.where(lane == FEAT_LO, lo, 0.0)))
    return feat.astype(BF16)


def _diff_attn_kernel(q_ref, k_ref, v_ref, kf_ref, km_ref, vm_ref, lq1_ref, lk1_ref, lq2_ref, lk2_ref, sg_ref,
                      o_ref, qe_ref, acc_ref, m_ref, s0_ref, s1_ref, *, bq, bk, seq):
    h = pl.program_id(1)
    i = pl.program_id(2)
    scale = 1.0 / math.sqrt(HEAD_DIM)
    slope = jnp.exp2(jnp.full((1, 1), -8.0 / DA_HEADS, F32) * (h + 1).astype(F32))
    lane = lax.broadcasted_iota(jnp.int32, (1, LANES), 1)
    q = q_ref[...] * jnp.asarray(scale, BF16)
    zero = jnp.zeros_like(q)
    qmaps = (jnp.where(lane < HEAD_DIM, q, zero), jnp.where(lane >= HEAD_DIM, q, zero))
    qpos = i * bq + lax.broadcasted_iota(jnp.int32, (bq, 1), 0)
    q_hi = (qpos // POS_SPLIT * POS_SPLIT).astype(F32)
    q_lo = (qpos % POS_SPLIT).astype(F32)
    feat = jnp.where(lane == FEAT_ONE_A, -slope * q_hi,
                     jnp.where(lane == FEAT_ONE_B, -slope * q_lo, jnp.where(lane <= FEAT_LO, slope, 0.0)))
    feats = (feat.astype(BF16), (-feat).astype(BF16))
    for side in range(2):
        for c in range(2):
            qe_ref[side, c * bq:(c + 1) * bq, :LANES] = qmaps[c]
            qe_ref[side, c * bq:(c + 1) * bq, LANES:] = feats[side]

    ones_blk = jnp.broadcast_to(jnp.where(lane == 0, 1.0, 0.0).astype(BF16), (bk, LANES))

    def attend(s, v_ext):
        m_old = m_ref[...]
        m_new = jnp.maximum(m_old, jnp.max(s, axis=-1, keepdims=True))
        alpha = jnp.exp(m_old - m_new)
        p = jnp.exp(s - jnp.tile(m_new, (1, s.shape[1] // LANES))).astype(BF16)
        acc_ref[...] = jnp.tile(alpha, (1, 2)) * acc_ref[...] + jnp.dot(p, v_ext, preferred_element_type=F32)
        m_ref[...] = m_new

    m_ref[...] = jnp.full(m_ref.shape, NEG_INF, F32)
    acc_ref[...] = jnp.zeros(acc_ref.shape, F32)

    nblk = seq // bk
    diag = (i * bq) // bk

    def key_block(pos):
        t = pos - 1
        return jnp.where(pos == 0, diag, t + jnp.where(t >= diag, 1, 0))

    def k_ext_at(kb):
        start = pl.multiple_of(kb * bk, bk)
        return jnp.concatenate([k_ref[pl.ds(start, bk), :], kf_ref[pl.ds(start, bk), :]], axis=1)

    def scores_into(dst_ref, pos):
        kb = key_block(pos)
        side = jnp.where(kb > diag, 1, 0)
        dst_ref[...] = lax.dot_general(qe_ref[side], k_ext_at(kb), CONTRACT_LAST, preferred_element_type=F32)

    def consume(src_ref, pos):
        start = pl.multiple_of(key_block(pos) * bk, bk)
        attend(src_ref[...], jnp.concatenate([v_ref[pl.ds(start, bk), :], ones_blk], axis=1))

    k_diag = k_ext_at(diag)
    s0_ref[...] = jnp.minimum(
        lax.dot_general(qe_ref[0], k_diag, CONTRACT_LAST, preferred_element_type=F32),
        lax.dot_general(qe_ref[1], k_diag, CONTRACT_LAST, preferred_element_type=F32))

    s = lax.dot_general(qe_ref[0, :, :LANES], km_ref[...], CONTRACT_LAST, preferred_element_type=F32)
    s = jnp.where(lane < N_META, s, NEG_INF)
    attend(s, jnp.concatenate([vm_ref[...], ones_blk[:LANES]], axis=1))

    if nblk == 1:
        consume(s0_ref, 0)
    else:
        def body(u, carry):
            scores_into(s1_ref, 2 * u + 1)
            consume(s0_ref, 2 * u)
            scores_into(s0_ref, 2 * u + 2)
            consume(s1_ref, 2 * u + 1)
            return carry

        lax.fori_loop(0, (nblk - 2) // 2, body, 0)
        scores_into(s1_ref, nblk - 1)
        consume(s0_ref, nblk - 2)
        consume(s1_ref, nblk - 1)

    lam = (jnp.exp(jnp.sum(lq1_ref[...] * lk1_ref[...], axis=-1, keepdims=True))
           - jnp.exp(jnp.sum(lq2_ref[...] * lk2_ref[...], axis=-1, keepdims=True)) + LAM_INIT)
    o1 = acc_ref[:bq, :LANES] / acc_ref[:bq, LANES:LANES + 1]
    o2 = acc_ref[bq:, :LANES] / acc_ref[bq:, LANES:LANES + 1]
    o = o1 - lam * o2
    o = o * lax.rsqrt(jnp.mean(o * o, axis=-1, keepdims=True) + SUBLN_EPS) * sg_ref[...]
    o_ref[...] = (o * (1.0 - LAM_INIT)).astype(o_ref.dtype)


def _diff_attn(proj, proj_meta, lam_vecs, subln_g, bq, bk):
    b, s, _ = proj.shape
    assert bk % bq == 0 and s % bk == 0 and (s // bk == 1 or (s // bk) % 2 == 0)
    vec = pl.BlockSpec((1, HEAD_DIM), lambda bi, h, i: (0, 0))
    return pl.pallas_call(
        functools.partial(_diff_attn_kernel, bq=bq, bk=bk, seq=s),
        grid=(b, DA_HEADS, s // bq),
        in_specs=[
            pl.BlockSpec((None, bq, LANES), lambda bi, h, i: (bi, i, COL_QD + h)),
            pl.BlockSpec((None, s, LANES), lambda bi, h, i: (bi, 0, COL_KD + h)),
            pl.BlockSpec((None, s, LANES), lambda bi, h, i: (bi, 0, COL_VD + h)),
            pl.BlockSpec((s, LANES), lambda bi, h, i: (0, 0)),
            pl.BlockSpec((LANES, LANES), lambda bi, h, i: (0, COL_KD + h)),
            pl.BlockSpec((LANES, LANES), lambda bi, h, i: (0, COL_VD + h)),
            vec, vec, vec, vec,
            pl.BlockSpec((1, LANES), lambda bi, h, i: (0, 0)),
        ],
        out_specs=pl.BlockSpec((None, bq, LANES), lambda bi, h, i: (bi, i, h)),
        out_shape=jax.ShapeDtypeStruct((b, s, DA_HEADS * LANES), BF16),
        scratch_shapes=[
            pltpu.VMEM((2, 2 * bq, 2 * LANES), BF16),
            pltpu.VMEM((2 * bq, 2 * LANES), F32),
            pltpu.VMEM((2 * bq, LANES), F32),
            pltpu.VMEM((2 * bq, bk), F32),
            pltpu.VMEM((2 * bq, bk), F32),
        ],
        compiler_params=_params(("parallel", "parallel", "arbitrary")),
        name="diff_attn",
    )(proj, proj, proj, _key_pos_features(s), proj_meta, proj_meta, *lam_vecs, subln_g)


WIN_KEYS = 4 * WINDOW


def _win_tables():
    r = np.arange(WINDOW)
    qf = np.zeros((SW_KV_HEADS, SW_GROUP * WINDOW, LANES), np.float32)
    for head in range(SW_HEADS):
        slope = 2.0 ** (-8.0 * (head + 1) / SW_HEADS)
        i_rel = WINDOW + r
        hi, lo = i_rel // POS_SPLIT * POS_SPLIT, i_rel % POS_SPLIT
        rows = qf[head // SW_GROUP, (head % SW_GROUP) * WINDOW:(head % SW_GROUP + 1) * WINDOW]
        rows[:, 0], rows[:, 1], rows[:, 2], rows[:, 3] = -slope * hi, -slope * lo, slope, slope
        rows[:, 4:8] = -rows[:, 0:4]
    kf = np.zeros((4, WINDOW, LANES), np.float32)
    for n, (blk, right) in enumerate(((0, 0), (1, 0), (1, 1), (2, 1))):
        j_rel = blk * WINDOW + r
        o = 4 * right
        kf[n, :, o], kf[n, :, o + 1] = 1.0, 1.0
        kf[n, :, o + 2], kf[n, :, o + 3] = j_rel // POS_SPLIT * POS_SPLIT, j_rel % POS_SPLIT
    mask = np.zeros((2, WINDOW, WINDOW), np.float32)
    mask[0] = np.where(r[None, :] >= r[:, None], 0.0, NEG_INF)
    mask[1] = np.where(r[None, :] <= r[:, None], 0.0, NEG_INF)
    return jnp.asarray(qf, BF16), jnp.asarray(kf, BF16), jnp.asarray(mask, F32)


def _win_attn_kernel(q_ref, kp_ref, kc_ref, kn_ref, vp_ref, vc_ref, vn_ref, km_ref, vm_ref, qf_ref, kf_ref,
                     mask_ref, sink_ref, o_ref, *, nblk):
    c = pl.program_id(1)
    lane = lax.broadcasted_iota(jnp.int32, (1, LANES), 1)
    scale = jnp.asarray(1.0 / math.sqrt(HEAD_DIM), BF16)
    edge_p = jnp.where(c == 0, NEG_INF, 0.0)
    edge_n = jnp.where(c == nblk - 1, NEG_INF, 0.0)
    mask_meta = jnp.where(lane < N_META, 0.0, NEG_INF)
    mask_prev = jnp.tile(mask_ref[0] + edge_p, (SW_GROUP, 1))
    mask_next = jnp.tile(mask_ref[1] + edge_n, (SW_GROUP, 1))
    row = lax.broadcasted_iota(jnp.int32, (SW_GROUP * WINDOW, 1), 0)
    ones_blk = jnp.broadcast_to(jnp.where(lane == 0, 1.0, 0.0).astype(BF16), (WIN_KEYS, LANES))
    zeros_blk = jnp.zeros((WINDOW, LANES), BF16)
    gw = SW_GROUP * HEAD_DIM
    for g in range(SW_KV_HEADS):
        kv = slice(g * LANES, (g + 1) * LANES)
        rows = []
        for hh in range(SW_GROUP):
            col = g * gw + (hh // 2) * LANES
            qb = q_ref[:, col:col + LANES] * scale
            keep = (lane < HEAD_DIM) if hh % 2 == 0 else (lane >= HEAD_DIM)
            rows.append(jnp.where(keep, qb, jnp.zeros_like(qb)))
        q_ext = jnp.concatenate([jnp.concatenate(rows, axis=0), qf_ref[g]], axis=1)
        k_all = jnp.concatenate([
            jnp.concatenate([km_ref[:, kv], zeros_blk], axis=1),
            jnp.concatenate([kp_ref[:, kv], kf_ref[0]], axis=1),
            jnp.concatenate([kc_ref[:, kv], kf_ref[1]], axis=1),
            jnp.concatenate([kc_ref[:, kv], kf_ref[2]], axis=1),
            jnp.concatenate([kn_ref[:, kv], kf_ref[3]], axis=1)], axis=0)
        s = lax.dot_general(q_ext, k_all, CONTRACT_LAST, preferred_element_type=F32)
        s_meta = s[:, :LANES] + mask_meta
        s_prev = s[:, LANES:2 * LANES] + mask_prev
        s_cur = jnp.minimum(s[:, 2 * LANES:3 * LANES], s[:, 3 * LANES:4 * LANES])
        s_next = s[:, 4 * LANES:] + mask_next
        sink = jnp.zeros((SW_GROUP * WINDOW, 1), F32)
        for hh in range(SW_GROUP):
            sink = jnp.where(row // WINDOW == hh, sink_ref[0, g * SW_GROUP + hh], sink)
        parts = (s_meta, s_prev, s_cur, s_next)
        m = jnp.maximum(jnp.maximum(s_meta, s_prev), jnp.maximum(s_cur, s_next))
        m = jnp.maximum(jnp.max(m, axis=-1, keepdims=True), sink)
        p = jnp.concatenate([jnp.exp(x - m) for x in parts], axis=1).astype(BF16)
        v_all = jnp.concatenate(
            [jnp.concatenate([vm_ref[:, kv], vp_ref[:, kv], vc_ref[:, kv], vn_ref[:, kv]], axis=0), ones_blk],
            axis=1)
        acc = jnp.dot(p, v_all, preferred_element_type=F32)
        denom = acc[:, LANES:LANES + 1] + jnp.exp(sink - m)
        o = acc[:, :LANES] / denom
        for lb in range(SW_GROUP // 2):
            even = o[(2 * lb) * WINDOW:(2 * lb + 1) * WINDOW]
            odd = o[(2 * lb + 1) * WINDOW:(2 * lb + 2) * WINDOW]
            col = g * gw + lb * LANES
            o_ref[:, col:col + LANES] = jnp.where(lane < HEAD_DIM, even, odd).astype(o_ref.dtype)


def _win_attn(proj, proj_meta, sink):
    b, s, _ = proj.shape
    nblk = s // WINDOW
    qw = SW_HEADS * HEAD_DIM
    kvw = SW_KV_HEADS * LANES
    qf, kf, mask = _win_tables()

    def kv_spec(col, shift):
        return pl.BlockSpec(
            (None, WINDOW, kvw),
            lambda bi, c: (bi, jnp.clip(c + shift, 0, nblk - 1), col * LANES // kvw))

    return pl.pallas_call(
        functools.partial(_win_attn_kernel, nblk=nblk),
        grid=(b, nblk),
        in_specs=[
            pl.BlockSpec((None, WINDOW, qw), lambda bi, c: (bi, c, COL_QS * LANES // qw)),
            kv_spec(COL_KS, -1), kv_spec(COL_KS, 0), kv_spec(COL_KS, 1),
            kv_spec(COL_VS, -1), kv_spec(COL_VS, 0), kv_spec(COL_VS, 1),
            pl.BlockSpec((LANES, kvw), lambda bi, c: (0, COL_KS * LANES // kvw)),
            pl.BlockSpec((LANES, kvw), lambda bi, c: (0, COL_VS * LANES // kvw)),
            pl.BlockSpec((SW_KV_HEADS, SW_GROUP * WINDOW, LANES), lambda bi, c: (0, 0, 0)),
            pl.BlockSpec((4, WINDOW, LANES), lambda bi, c: (0, 0, 0)),
            pl.BlockSpec((2, WINDOW, WINDOW), lambda bi, c: (0, 0, 0)),
            pl.BlockSpec(memory_space=pltpu.SMEM),
        ],
        out_specs=pl.BlockSpec((None, WINDOW, qw), lambda bi, c: (bi, c, 0)),
        out_shape=jax.ShapeDtypeStruct((b, s, qw), BF16),
        compiler_params=_params(("parallel", "arbitrary")),
        name="win_attn",
    )(proj, proj, proj, proj, proj, proj, proj, proj_meta, proj_meta, qf, kf, mask, sink)


ROUTE_E1, ROUTE_E2, ROUTE_POS1, ROUTE_POS2, ROUTE_W1, ROUTE_W2 = range(6)


def _pack_bf16_pairs(x):
    k = x.shape[1] // 2
    bits = lax.bitcast_convert_type(x.astype(BF16).astype(F32), jnp.uint32)
    return lax.bitcast_convert_type(bits[:, :k] | (bits[:, k:] >> 16), jnp.int32)


def _unpack_bf16_pairs(w):
    bits = lax.bitcast_convert_type(w, jnp.uint32)
    hi = lax.bitcast_convert_type(bits & jnp.uint32(0xFFFF0000), F32)
    lo = lax.bitcast_convert_type(bits << 16, F32)
    return jnp.concatenate([hi, lo], axis=1)


def _out_router_kernel(od_ref, os_ref, x_ref, wod_ref, wos_ref, g2_ref, wr_hi_ref, wr_lo_ref, br_ref,
                       h_ref, a_ref, route_ref, cnt_ref, base_ref):
    @pl.when(pl.program_id(0) == 0)
    def _():
        base_ref[...] = jnp.zeros_like(base_ref)

    h = (x_ref[...]
         + jnp.dot(od_ref[...], wod_ref[...], preferred_element_type=F32)
         + jnp.dot(os_ref[...], wos_ref[...], preferred_element_type=F32))
    h_ref[...] = h
    a = h * lax.rsqrt(jnp.mean(h * h, axis=-1, keepdims=True) + EPS) * g2_ref[...]
    a_hi = a.astype(BF16)
    a_ref[...] = _pack_bf16_pairs(a)
    a_lo = (a - a_hi.astype(F32)).astype(BF16)
    logits = (jnp.dot(a_hi, wr_hi_ref[...], preferred_element_type=F32)
              + jnp.dot(a_lo, wr_hi_ref[...], preferred_element_type=F32)
              + jnp.dot(a_hi, wr_lo_ref[...], preferred_element_type=F32)) + br_ref[...]
    tm = logits.shape[0]
    lane = lax.broadcasted_iota(jnp.int32, (tm, LANES), 1)
    big = jnp.int32(LANES)

    def first_argmax(mask):
        vals = jnp.where(mask, logits, NEG_INF)
        mx = jnp.max(vals, axis=-1, keepdims=True)
        idx = jnp.min(jnp.where(mask & (vals == mx), lane, big), axis=-1, keepdims=True)
        return mx, idx

    gmask = (lane >= N_EXPERTS) & (lane < N_EXPERTS + N_GROUPS)
    gmax, gidx = first_argmax(gmask)
    gsum = jnp.sum(jnp.where(gmask, jnp.exp(logits - gmax), 0.0), axis=-1, keepdims=True)
    g_w = 1.0 / gsum
    in_group = (lane // EXPERTS_PER_GROUP) == (gidx - N_EXPERTS)
    m1, i1 = first_argmax(in_group)
    m2, i2 = first_argmax(in_group & (lane != i1))
    r = jnp.exp(m2 - m1)
    w1 = g_w / (1.0 + r)
    w2 = g_w * r / (1.0 + r)

    rix = lax.broadcasted_iota(jnp.int32, (tm, tm), 0)
    cix = lax.broadcasted_iota(jnp.int32, (tm, tm), 1)
    lower = jnp.where(rix > cix, 1.0, 0.0).astype(BF16)
    oh1 = jnp.where(lane == i1, 1.0, 0.0)
    oh2 = jnp.where(lane == i2, 1.0, 0.0)
    base1 = base_ref[...]
    pos1 = jnp.sum(oh1 * (base1 + jnp.dot(lower, oh1.astype(BF16), preferred_element_type=F32)),
                   axis=-1, keepdims=True)
    base2 = base1 + jnp.sum(oh1, axis=0, keepdims=True)
    pos2 = jnp.sum(oh2 * (base2 + jnp.dot(lower, oh2.astype(BF16), preferred_element_type=F32)),
                   axis=-1, keepdims=True)
    total = base2 + jnp.sum(oh2, axis=0, keepdims=True)
    base_ref[...] = total
    cnt_ref[...] = total

    fields = (i1.astype(F32), i2.astype(F32), pos1, pos2, w1, w2)
    route = jnp.zeros((tm, LANES), F32)
    for n, val in enumerate(fields):
        route = jnp.where(lane == n, val, route)
    route_ref[...] = route


def _out_router(od, os_, x2d, wod, wos, g2, wr_hi, wr_lo, br, tm):
    n, d = x2d.shape
    half = od.shape[1]
    row = lambda i: (i, 0)
    const = lambda i: (0, 0)
    return pl.pallas_call(
        _out_router_kernel,
        grid=(n // tm,),
        in_specs=[
            pl.BlockSpec((tm, half), row),
            pl.BlockSpec((tm, half), row),
            pl.BlockSpec((tm, d), row),
            pl.BlockSpec((half, d), const),
            pl.BlockSpec((half, d), const),
            pl.BlockSpec((1, d), const),
            pl.BlockSpec((d, LANES), const),
            pl.BlockSpec((d, LANES), const),
            pl.BlockSpec((1, LANES), const),
        ],
        out_specs=[
            pl.BlockSpec((tm, d), row),
            pl.BlockSpec((tm, d // 2), row),
            pl.BlockSpec((tm, LANES), row),
            pl.BlockSpec((1, LANES), const),
        ],
        out_shape=[
            jax.ShapeDtypeStruct((n, d), F32),
            jax.ShapeDtypeStruct((n, d // 2), jnp.int32),
            jax.ShapeDtypeStruct((n, LANES), F32),
            jax.ShapeDtypeStruct((1, LANES), F32),
        ],
        scratch_shapes=[pltpu.VMEM((1, LANES), F32)],
        compiler_params=_params(("arbitrary",)),
        name="out_router",
    )(od, os_, x2d, wod, wos, g2, wr_hi, wr_lo, br)


SC_CORES, SC_SUBCORES = 2, 16
SC_WORKERS = SC_CORES * SC_SUBCORES
SC_CHUNK = 128


def _sc_mesh():
    return plsc.VectorSubcoreMesh(core_axis_name="c", subcore_axis_name="s",
                                  num_cores=SC_CORES, num_subcores=SC_SUBCORES)


def _sc_scatter_rows(x, idx1, idx2, n_out):
    n, d = x.shape
    assert n % (SC_WORKERS * SC_CHUNK) == 0
    per_w = n // SC_WORKERS

    @functools.partial(
        pl.kernel, mesh=_sc_mesh(), out_type=jax.ShapeDtypeStruct((n_out, d), x.dtype),
        scratch_types=[pltpu.VMEM((SC_CHUNK,), jnp.int32), pltpu.VMEM((SC_CHUNK,), jnp.int32),
                       pltpu.VMEM((SC_CHUNK, d), x.dtype), pltpu.SemaphoreType.DMA],
        name="sc_scatter_rows")
    def scatter(x_hbm, i1_hbm, i2_hbm, o_hbm, i1_v, i2_v, rows_v, sem):
        wid = lax.axis_index("s") * SC_CORES + lax.axis_index("c")

        @pl.loop(0, per_w // SC_CHUNK)
        def _(j):
            base = wid * per_w + j * SC_CHUNK
            pltpu.sync_copy(i1_hbm.at[pl.ds(base, SC_CHUNK)], i1_v)
            pltpu.sync_copy(i2_hbm.at[pl.ds(base, SC_CHUNK)], i2_v)
            pltpu.sync_copy(x_hbm.at[pl.ds(base, SC_CHUNK)], rows_v)
            pltpu.async_copy(rows_v, o_hbm.at[i1_v], sem).wait()
            pltpu.async_copy(rows_v, o_hbm.at[i2_v], sem).wait()

    return scatter(x, idx1, idx2)


def _sc_gather_rows(table, idx1, idx2):
    n = idx1.shape[0]
    d = table.shape[1]
    assert n % (SC_WORKERS * SC_CHUNK) == 0
    per_w = n // SC_WORKERS
    out = jax.ShapeDtypeStruct((n, d), table.dtype)

    @functools.partial(
        pl.kernel, mesh=_sc_mesh(), out_type=(out, out),
        scratch_types=[pltpu.VMEM((SC_CHUNK,), jnp.int32), pltpu.VMEM((SC_CHUNK, d), table.dtype),
                       pltpu.SemaphoreType.DMA],
        name="sc_gather_rows")
    def gather(t_hbm, i1_hbm, i2_hbm, o1_hbm, o2_hbm, i_v, rows_v, sem):
        wid = lax.axis_index("s") * SC_CORES + lax.axis_index("c")

        @pl.loop(0, per_w // SC_CHUNK)
        def _(j):
            base = wid * per_w + j * SC_CHUNK
            for i_hbm, o_hbm in ((i1_hbm, o1_hbm), (i2_hbm, o2_hbm)):
                pltpu.sync_copy(i_hbm.at[pl.ds(base, SC_CHUNK)], i_v)
                pltpu.async_copy(t_hbm.at[i_v], rows_v, sem).wait()
                pltpu.sync_copy(rows_v, o_hbm.at[pl.ds(base, SC_CHUNK)])

    return gather(table, idx1, idx2)


def _routing_tables(route, cnt, tm, n_tiles_max):
    e1 = route[:, ROUTE_E1].astype(jnp.int32)
    e2 = route[:, ROUTE_E2].astype(jnp.int32)
    counts = cnt[0, :N_EXPERTS].astype(jnp.int32)
    padded = (counts + tm - 1) // tm * tm
    ends = jnp.cumsum(padded)
    starts = ends - padded
    dest1 = starts[e1] + route[:, ROUTE_POS1].astype(jnp.int32)
    dest2 = starts[e2] + route[:, ROUTE_POS2].astype(jnp.int32)
    tile_start = jnp.arange(n_tiles_max, dtype=jnp.int32) * tm
    tile_expert = jnp.minimum(jnp.searchsorted(ends, tile_start, side="right"), N_EXPERTS - 1).astype(jnp.int32)
    n_tiles = (ends[-1] // tm).astype(jnp.int32).reshape(1)
    return dest1, dest2, tile_expert, n_tiles


def _moe_tiles_kernel(te_ref, nt_ref, xs_ref, wg_ref, wu_ref, wd_ref, ys_ref):
    del te_ref

    @pl.when(pl.program_id(0) < nt_ref[0])
    def _():
        x = _unpack_bf16_pairs(xs_ref[...]).astype(BF16)
        hg = jnp.dot(x, wg_ref[...], preferred_element_type=F32)
        hu = jnp.dot(x, wu_ref[...], preferred_element_type=F32)
        hid = (hg * jax.nn.sigmoid(hg) * hu).astype(BF16)
        ys_ref[...] = _pack_bf16_pairs(jnp.dot(hid, wd_ref[...], preferred_element_type=F32))


def _moe_tiles(xs, tile_expert, n_tiles, wg, wu, wd, tm):
    r, dh = xs.shape
    ne, d, de = wg.shape
    row = lambda t, te, nt: (t, 0)
    return pl.pallas_call(
        _moe_tiles_kernel,
        grid_spec=pltpu.PrefetchScalarGridSpec(
            num_scalar_prefetch=2,
            grid=(r // tm,),
            in_specs=[
                pl.BlockSpec((tm, dh), row),
                pl.BlockSpec((None, d, de), lambda t, te, nt: (te[t], 0, 0)),
                pl.BlockSpec((None, d, de), lambda t, te, nt: (te[t], 0, 0)),
                pl.BlockSpec((None, de, d), lambda t, te, nt: (te[t], 0, 0)),
            ],
            out_specs=pl.BlockSpec((tm, dh), row),
        ),
        out_shape=jax.ShapeDtypeStruct((r, dh), jnp.int32),
        compiler_params=_params(("arbitrary",)),
        name="moe_tiles",
    )(tile_expert, n_tiles, xs, wg, wu, wd)


def _combine_kernel(h_ref, y1_ref, y2_ref, route_ref, fg_ref, o_ref):
    lane = lax.broadcasted_iota(jnp.int32, route_ref.shape, 1)
    route = route_ref[...]
    w1 = jnp.sum(jnp.where(lane == ROUTE_W1, route, 0.0), axis=-1, keepdims=True)
    w2 = jnp.sum(jnp.where(lane == ROUTE_W2, route, 0.0), axis=-1, keepdims=True)
    y = h_ref[...] + w1 * _unpack_bf16_pairs(y1_ref[...]) + w2 * _unpack_bf16_pairs(y2_ref[...])
    o_ref[...] = y * lax.rsqrt(jnp.mean(y * y, axis=-1, keepdims=True) + EPS) * fg_ref[...]


def _combine(h, y1, y2, route, fg, tm):
    n, d = h.shape
    row = lambda i: (i, 0)
    return pl.pallas_call(
        _combine_kernel,
        grid=(n // tm,),
        in_specs=[
            pl.BlockSpec((tm, d), row),
            pl.BlockSpec((tm, d // 2), row),
            pl.BlockSpec((tm, d // 2), row),
            pl.BlockSpec((tm, LANES), row),
            pl.BlockSpec((1, d), lambda i: (0, 0)),
        ],
        out_specs=pl.BlockSpec((tm, d), row),
        out_shape=jax.ShapeDtypeStruct((n, d), F32),
        compiler_params=_params(("parallel",)),
        name="moe_combine",
    )(h, y1, y2, route, fg)


def _moe(a2p, route, cnt, h, wg, wu, wd, fg, tm):
    n = h.shape[0]
    n_tiles_max = (2 * n) // tm + N_EXPERTS
    dest1, dest2, tile_expert, n_tiles = _routing_tables(route, cnt, tm, n_tiles_max)
    xs = _sc_scatter_rows(a2p, dest1, dest2, n_tiles_max * tm)
    ys = _moe_tiles(xs, tile_expert, n_tiles, wg, wu, wd, tm)
    y1, y2 = _sc_gather_rows(ys, dest1, dest2)
    return _combine(h, y1, y2, route, fg, tm=min(512, n))


def _dup_heads(w, n_heads):
    d = w.shape[0]
    w = w.reshape(d, n_heads, 1, HEAD_DIM)
    return jnp.broadcast_to(w, (d, n_heads, 2, HEAD_DIM)).reshape(d, n_heads * 2 * HEAD_DIM)


def _encoder(x, proj_meta, wts):
    b, s, d = x.shape
    x2d = x.reshape(b * s, d)
    proj = _norm_proj(x2d, wts["g1"], wts["w_ext"], tm=512).reshape(b, s, -1)
    od = _diff_attn(proj, proj_meta, wts["lam_vecs"], wts["subln_g"], bq=256, bk=min(512, s))
    os_ = _win_attn(proj, proj_meta, wts["sink"])
    h, a2p, route, cnt = _out_router(od.reshape(b * s, -1), os_.reshape(b * s, -1), x2d, wts["wod"], wts["wos"],
                                     wts["g2"], wts["wr_hi"], wts["wr_lo"], wts["br"], tm=512)
    y = _moe(a2p, route, cnt, h, wts["wg"], wts["wu"], wts["wd"], wts["fg"], tm=512)
    return y.reshape(b, s, d)


def kernel(x_prompt, x_sample, meta, norm1_g, w_in, lam_q1, lam_k1, lam_q2, lam_k2, subln_g, sink, w_out,
           norm2_g, w_gr, b_gr, w_er, b_er, w_gate, w_up, w_down, final_g):
    d = x_prompt.shape[-1]
    w = w_in[0]
    c_kd = 2 * DA_HEADS * HEAD_DIM
    c_vd = 2 * c_kd
    c_qs = c_vd + DA_HEADS * 2 * HEAD_DIM
    c_ks = c_qs + SW_HEADS * HEAD_DIM
    c_vs = c_ks + SW_KV_HEADS * HEAD_DIM
    w_ext = jnp.concatenate(
        [w[:, :c_ks], _dup_heads(w[:, c_ks:c_vs], SW_KV_HEADS), _dup_heads(w[:, c_vs:], SW_KV_HEADS)],
        axis=1).astype(BF16)
    w_router = jnp.concatenate([w_er[0], w_gr[0]], axis=1)
    w_router = jnp.pad(w_router, ((0, 0), (0, LANES - w_router.shape[1])))
    wr_hi = w_router.astype(BF16)
    wr_lo = (w_router - wr_hi.astype(F32)).astype(BF16)
    br = jnp.pad(jnp.concatenate([b_er[0], b_gr[0]]), (0, LANES - N_EXPERTS - N_GROUPS)).reshape(1, LANES)
    wo = w_out[0].astype(BF16)
    half = DA_HEADS * 2 * HEAD_DIM
    wts = dict(
        g1=norm1_g[0].reshape(1, d), w_ext=w_ext,
        lam_vecs=(lam_q1[0].reshape(1, -1), lam_k1[0].reshape(1, -1),
                  lam_q2[0].reshape(1, -1), lam_k2[0].reshape(1, -1)),
        subln_g=subln_g[0].reshape(1, -1), sink=sink[0].reshape(1, -1),
        wod=wo[:half], wos=wo[half:], g2=norm2_g[0].reshape(1, d),
        wr_hi=wr_hi, wr_lo=wr_lo, br=br,
        wg=w_gate[0].astype(BF16), wu=w_up[0].astype(BF16), wd=w_down[0].astype(BF16),
        fg=final_g.reshape(1, d),
    )
    proj_meta = _norm_proj(meta, wts["g1"], w_ext, tm=N_META)
    proj_meta = jnp.pad(proj_meta, ((0, LANES - N_META), (0, 0)))
    return _encoder(x_prompt, proj_meta, wts), _encoder(x_sample, proj_meta, wts)
```

```python
import functools
import math

import jax
import jax.numpy as jnp
import numpy as np
from jax import lax
from jax.experimental import pallas as pl
from jax.experimental.pallas import tpu as pltpu
from jax.experimental.pallas import tpu_sc as plsc

F32 = jnp.float32
BF16 = jnp.bfloat16

N_META = 16
HEAD_DIM = 64
DA_HEADS = 4
SW_HEADS = 8
SW_KV_HEADS = 2
SW_GROUP = SW_HEADS // SW_KV_HEADS
WINDOW = 128
N_GROUPS = 4
EXPERTS_PER_GROUP = 8
N_EXPERTS = N_GROUPS * EXPERTS_PER_GROUP
EPS = 1e-6
SUBLN_EPS = 1e-5
NEG_INF = -1e30
LAM_INIT = 0.8 - 0.6 * math.exp(-0.3 * 0)
LANES = 128
VMEM_LIMIT = 48 * 1024 * 1024

COL_QD, COL_KD, COL_VD, COL_QS, COL_KS, COL_VS, N_COLBLK = 0, 4, 8, 12, 16, 18, 20
CONTRACT_LAST = (((1,), (1,)), ((), ()))


def _params(sem):
    return pltpu.CompilerParams(dimension_semantics=sem, vmem_limit_bytes=VMEM_LIMIT)


def _norm_proj_kernel(x_ref, g_ref, w_ref, o_ref):
    x = x_ref[...]
    ms = jnp.mean(x * x, axis=-1, keepdims=True)
    y = (x * lax.rsqrt(ms + EPS) * g_ref[...]).astype(BF16)
    o_ref[...] = jnp.dot(y, w_ref[...], preferred_element_type=F32).astype(o_ref.dtype)


def _norm_proj(x2d, g, w_ext, tm):
    n, d = x2d.shape
    wcols = w_ext.shape[1]
    return pl.pallas_call(
        _norm_proj_kernel,
        grid=(n // tm,),
        in_specs=[
            pl.BlockSpec((tm, d), lambda i: (i, 0)),
            pl.BlockSpec((1, d), lambda i: (0, 0)),
            pl.BlockSpec((d, wcols), lambda i: (0, 0)),
        ],
        out_specs=pl.BlockSpec((tm, wcols), lambda i: (i, 0)),
        out_shape=jax.ShapeDtypeStruct((n, wcols), BF16),
        compiler_params=_params(("parallel",)),
        name="norm_proj",
    )(x2d, g, w_ext)


POS_SPLIT = 64
FEAT_ONE_A, FEAT_ONE_B, FEAT_HI, FEAT_LO = 0, 1, 2, 3


def _key_pos_features(seq):
    j = jnp.arange(seq, dtype=jnp.int32)[:, None]
    lane = jnp.arange(LANES, dtype=jnp.int32)[None, :]
    hi = (j // POS_SPLIT * POS_SPLIT).astype(F32)
    lo = (j % POS_SPLIT).astype(F32)
    feat = jnp.where(lane <= FEAT_ONE_B, 1.0, jnp.where(lane == FEAT_HI, hi, jnp.where(lane == FEAT_LO, lo, 0.0)))
    return feat.astype(BF16)


def _diff_attn_kernel(q_ref, k_ref, v_ref, kf_ref, km_ref, vm_ref, lq1_ref, lk1_ref, lq2_ref, lk2_ref, sg_ref,
                      o_ref, qe_ref, acc_ref, m_ref, s0_ref, s1_ref, *, bq, bk, seq):
    h = pl.program_id(1)
    i = pl.program_id(2)
    scale = 1.0 / math.sqrt(HEAD_DIM)
    slope = jnp.exp2(jnp.full((1, 1), -8.0 / DA_HEADS, F32) * (h + 1).astype(F32))
    lane = lax.broadcasted_iota(jnp.int32, (1, LANES), 1)
    q = q_ref[...] * jnp.asarray(scale, BF16)
    zero = jnp.zeros_like(q)
    qmaps = (jnp.where(lane < HEAD_DIM, q, zero), jnp.where(lane >= HEAD_DIM, q, zero))
    qpos = i * bq + lax.broadcasted_iota(jnp.int32, (bq, 1), 0)
    q_hi = (qpos // POS_SPLIT * POS_SPLIT).astype(F32)
    q_lo = (qpos % POS_SPLIT).astype(F32)
    feat = jnp.where(lane == FEAT_ONE_A, -slope * q_hi,
                     jnp.where(lane == FEAT_ONE_B, -slope * q_lo, jnp.where(lane <= FEAT_LO, slope, 0.0)))
    feats = (feat.astype(BF16), (-feat).astype(BF16))
    for side in range(2):
        for c in range(2):
            qe_ref[side, c * bq:(c + 1) * bq, :LANES] = qmaps[c]
            qe_ref[side, c * bq:(c + 1) * bq, LANES:] = feats[side]

    ones_blk = jnp.broadcast_to(jnp.where(lane == 0, 1.0, 0.0).astype(BF16), (bk, LANES))

    def attend(s, v_ext):
        m_old = m_ref[...]
        m_new = jnp.maximum(m_old, jnp.max(s, axis=-1, keepdims=True))
        alpha = jnp.exp(m_old - m_new)
        p = jnp.exp(s - jnp.tile(m_new, (1, s.shape[1] // LANES))).astype(BF16)
        acc_ref[...] = jnp.tile(alpha, (1, 2)) * acc_ref[...] + jnp.dot(p, v_ext, preferred_element_type=F32)
        m_ref[...] = m_new

    m_ref[...] = jnp.full(m_ref.shape, NEG_INF, F32)
    acc_ref[...] = jnp.zeros(acc_ref.shape, F32)

    nblk = seq // bk
    diag = (i * bq) // bk

    def key_block(pos):
        t = pos - 1
        return jnp.where(pos == 0, diag, t + jnp.where(t >= diag, 1, 0))

    def k_ext_at(kb):
        start = pl.multiple_of(kb * bk, bk)
        return jnp.concatenate([k_ref[pl.ds(start, bk), :], kf_ref[pl.ds(start, bk), :]], axis=1)

    def scores_into(dst_ref, pos):
        kb = key_block(pos)
        side = jnp.where(kb > diag, 1, 0)
        dst_ref[...] = lax.dot_general(qe_ref[side], k_ext_at(kb), CONTRACT_LAST, preferred_element_type=F32)

    def consume(src_ref, pos):
        start = pl.multiple_of(key_block(pos) * bk, bk)
        attend(src_ref[...], jnp.concatenate([v_ref[pl.ds(start, bk), :], ones_blk], axis=1))

    k_diag = k_ext_at(diag)
    s0_ref[...] = jnp.minimum(
        lax.dot_general(qe_ref[0], k_diag, CONTRACT_LAST, preferred_element_type=F32),
        lax.dot_general(qe_ref[1], k_diag, CONTRACT_LAST, preferred_element_type=F32))

    s = lax.dot_general(qe_ref[0, :, :LANES], km_ref[...], CONTRACT_LAST, preferred_element_type=F32)
    s = jnp.where(lane < N_META, s, NEG_INF)
    attend(s, jnp.concatenate([vm_ref[...], ones_blk[:LANES]], axis=1))

    if nblk == 1:
        consume(s0_ref, 0)
    else:
        def body(u, carry):
            scores_into(s1_ref, 2 * u + 1)
            consume(s0_ref, 2 * u)
            scores_into(s0_ref, 2 * u + 2)
            consume(s1_ref, 2 * u + 1)
            return carry

        lax.fori_loop(0, (nblk - 2) // 2, body, 0)
        scores_into(s1_ref, nblk - 1)
        consume(s0_ref, nblk - 2)
        consume(s1_ref, nblk - 1)

    lam = (jnp.exp(jnp.sum(lq1_ref[...] * lk1_ref[...], axis=-1, keepdims=True))
           - jnp.exp(jnp.sum(lq2_ref[...] * lk2_ref[...], axis=-1, keepdims=True)) + LAM_INIT)
    o1 = acc_ref[:bq, :LANES] / acc_ref[:bq, LANES:LANES + 1]
    o2 = acc_ref[bq:, :LANES] / acc_ref[bq:, LANES:LANES + 1]
    o = o1 - lam * o2
    o = o * lax.rsqrt(jnp.mean(o * o, axis=-1, keepdims=True) + SUBLN_EPS) * sg_ref[...]
    o_ref[...] = (o * (1.0 - LAM_INIT)).astype(o_ref.dtype)


def _diff_attn(proj, proj_meta, lam_vecs, subln_g, bq, bk):
    b, s, _ = proj.shape
    assert bk % bq == 0 and s % bk == 0 and (s // bk == 1 or (s // bk) % 2 == 0)
    vec = pl.BlockSpec((1, HEAD_DIM), lambda bi, h, i: (0, 0))
    return pl.pallas_call(
        functools.partial(_diff_attn_kernel, bq=bq, bk=bk, seq=s),
        grid=(b, DA_HEADS, s // bq),
        in_specs=[
            pl.BlockSpec((None, bq, LANES), lambda bi, h, i: (bi, i, COL_QD + h)),
            pl.BlockSpec((None, s, LANES), lambda bi, h, i: (bi, 0, COL_KD + h)),
            pl.BlockSpec((None, s, LANES), lambda bi, h, i: (bi, 0, COL_VD + h)),
            pl.BlockSpec((s, LANES), lambda bi, h, i: (0, 0)),
            pl.BlockSpec((LANES, LANES), lambda bi, h, i: (0, COL_KD + h)),
            pl.BlockSpec((LANES, LANES), lambda bi, h, i: (0, COL_VD + h)),
            vec, vec, vec, vec,
            pl.BlockSpec((1, LANES), lambda bi, h, i: (0, 0)),
        ],
        out_specs=pl.BlockSpec((None, bq, LANES), lambda bi, h, i: (bi, i, h)),
        out_shape=jax.ShapeDtypeStruct((b, s, DA_HEADS * LANES), BF16),
        scratch_shapes=[
            pltpu.VMEM((2, 2 * bq, 2 * LANES), BF16),
            pltpu.VMEM((2 * bq, 2 * LANES), F32),
            pltpu.VMEM((2 * bq, LANES), F32),
            pltpu.VMEM((2 * bq, bk), F32),
            pltpu.VMEM((2 * bq, bk), F32),
        ],
        compiler_params=_params(("parallel", "parallel", "arbitrary")),
        name="diff_attn",
    )(proj, proj, proj, _key_pos_features(s), proj_meta, proj_meta, *lam_vecs, subln_g)


WIN_KEYS = 4 * WINDOW


def _win_tables():
    r = np.arange(WINDOW)
    qf = np.zeros((SW_KV_HEADS, SW_GROUP * WINDOW, LANES), np.float32)
    for head in range(SW_HEADS):
        slope = 2.0 ** (-8.0 * (head + 1) / SW_HEADS)
        i_rel = WINDOW + r
        hi, lo = i_rel // POS_SPLIT * POS_SPLIT, i_rel % POS_SPLIT
        rows = qf[head // SW_GROUP, (head % SW_GROUP) * WINDOW:(head % SW_GROUP + 1) * WINDOW]
        rows[:, 0], rows[:, 1], rows[:, 2], rows[:, 3] = -slope * hi, -slope * lo, slope, slope
        rows[:, 4:8] = -rows[:, 0:4]
    kf = np.zeros((4, WINDOW, LANES), np.float32)
    for n, (blk, right) in enumerate(((0, 0), (1, 0), (1, 1), (2, 1))):
        j_rel = blk * WINDOW + r
        o = 4 * right
        kf[n, :, o], kf[n, :, o + 1] = 1.0, 1.0
        kf[n, :, o + 2], kf[n, :, o + 3] = j_rel // POS_SPLIT * POS_SPLIT, j_rel % POS_SPLIT
    mask = np.zeros((2, WINDOW, WINDOW), np.float32)
    mask[0] = np.where(r[None, :] >= r[:, None], 0.0, NEG_INF)
    mask[1] = np.where(r[None, :] <= r[:, None], 0.0, NEG_INF)
    return jnp.asarray(qf, BF16), jnp.asarray(kf, BF16), jnp.asarray(mask, F32)


def _win_attn_kernel(q_ref, kp_ref, kc_ref, kn_ref, vp_ref, vc_ref, vn_ref, km_ref, vm_ref, qf_ref, kf_ref,
                     mask_ref, sink_ref, o_ref, *, nblk):
    c = pl.program_id(1)
    lane = lax.broadcasted_iota(jnp.int32, (1, LANES), 1)
    scale = jnp.asarray(1.0 / math.sqrt(HEAD_DIM), BF16)
    edge_p = jnp.where(c == 0, NEG_INF, 0.0)
    edge_n = jnp.where(c == nblk - 1, NEG_INF, 0.0)
    mask_meta = jnp.where(lane < N_META, 0.0, NEG_INF)
    mask_prev = jnp.tile(mask_ref[0] + edge_p, (SW_GROUP, 1))
    mask_next = jnp.tile(mask_ref[1] + edge_n, (SW_GROUP, 1))
    row = lax.broadcasted_iota(jnp.int32, (SW_GROUP * WINDOW, 1), 0)
    ones_blk = jnp.broadcast_to(jnp.where(lane == 0, 1.0, 0.0).astype(BF16), (WIN_KEYS, LANES))
    zeros_blk = jnp.zeros((WINDOW, LANES), BF16)
    gw = SW_GROUP * HEAD_DIM
    for g in range(SW_KV_HEADS):
        kv = slice(g * LANES, (g + 1) * LANES)
        rows = []
        for hh in range(SW_GROUP):
            col = g * gw + (hh // 2) * LANES
            qb = q_ref[:, col:col + LANES] * scale
            keep = (lane < HEAD_DIM) if hh % 2 == 0 else (lane >= HEAD_DIM)
            rows.append(jnp.where(keep, qb, jnp.zeros_like(qb)))
        q_ext = jnp.concatenate([jnp.concatenate(rows, axis=0), qf_ref[g]], axis=1)
        k_all = jnp.concatenate([
            jnp.concatenate([km_ref[:, kv], zeros_blk], axis=1),
            jnp.concatenate([kp_ref[:, kv], kf_ref[0]], axis=1),
            jnp.concatenate([kc_ref[:, kv], kf_ref[1]], axis=1),
            jnp.concatenate([kc_ref[:, kv], kf_ref[2]], axis=1),
            jnp.concatenate([kn_ref[:, kv], kf_ref[3]], axis=1)], axis=0)
        s = lax.dot_general(q_ext, k_all, CONTRACT_LAST, preferred_element_type=F32)
        s_meta = s[:, :LANES] + mask_meta
        s_prev = s[:, LANES:2 * LANES] + mask_prev
        s_cur = jnp.minimum(s[:, 2 * LANES:3 * LANES], s[:, 3 * LANES:4 * LANES])
        s_next = s[:, 4 * LANES:] + mask_next
        sink = jnp.zeros((SW_GROUP * WINDOW, 1), F32)
        for hh in range(SW_GROUP):
            sink = jnp.where(row // WINDOW == hh, sink_ref[0, g * SW_GROUP + hh], sink)
        parts = (s_meta, s_prev, s_cur, s_next)
        m = jnp.maximum(jnp.maximum(s_meta, s_prev), jnp.maximum(s_cur, s_next))
        m = jnp.maximum(jnp.max(m, axis=-1, keepdims=True), sink)
        p = jnp.concatenate([jnp.exp(x - m) for x in parts], axis=1).astype(BF16)
        v_all = jnp.concatenate(
            [jnp.concatenate([vm_ref[:, kv], vp_ref[:, kv], vc_ref[:, kv], vn_ref[:, kv]], axis=0), ones_blk],
            axis=1)
        acc = jnp.dot(p, v_all, preferred_element_type=F32)
        denom = acc[:, LANES:LANES + 1] + jnp.exp(sink - m)
        o = acc[:, :LANES] / denom
        for lb in range(SW_GROUP // 2):
            even = o[(2 * lb) * WINDOW:(2 * lb + 1) * WINDOW]
            odd = o[(2 * lb + 1) * WINDOW:(2 * lb + 2) * WINDOW]
            col = g * gw + lb * LANES
            o_ref[:, col:col + LANES] = jnp.where(lane < HEAD_DIM, even, odd).astype(o_ref.dtype)


def _win_attn(proj, proj_meta, sink):
    b, s, _ = proj.shape
    nblk = s // WINDOW
    qw = SW_HEADS * HEAD_DIM
    kvw = SW_KV_HEADS * LANES
    qf, kf, mask = _win_tables()

    def kv_spec(col, shift):
        return pl.BlockSpec(
            (None, WINDOW, kvw),
            lambda bi, c: (bi, jnp.clip(c + shift, 0, nblk - 1), col * LANES // kvw))

    return pl.pallas_call(
        functools.partial(_win_attn_kernel, nblk=nblk),
        grid=(b, nblk),
        in_specs=[
            pl.BlockSpec((None, WINDOW, qw), lambda bi, c: (bi, c, COL_QS * LANES // qw)),
            kv_spec(COL_KS, -1), kv_spec(COL_KS, 0), kv_spec(COL_KS, 1),
            kv_spec(COL_VS, -1), kv_spec(COL_VS, 0), kv_spec(COL_VS, 1),
            pl.BlockSpec((LANES, kvw), lambda bi, c: (0, COL_KS * LANES // kvw)),
            pl.BlockSpec((LANES, kvw), lambda bi, c: (0, COL_VS * LANES // kvw)),
            pl.BlockSpec((SW_KV_HEADS, SW_GROUP * WINDOW, LANES), lambda bi, c: (0, 0, 0)),
            pl.BlockSpec((4, WINDOW, LANES), lambda bi, c: (0, 0, 0)),
            pl.BlockSpec((2, WINDOW, WINDOW), lambda bi, c: (0, 0, 0)),
            pl.BlockSpec(memory_space=pltpu.SMEM),
        ],
        out_specs=pl.BlockSpec((None, WINDOW, qw), lambda bi, c: (bi, c, 0)),
        out_shape=jax.ShapeDtypeStruct((b, s, qw), BF16),
        compiler_params=_params(("parallel", "arbitrary")),
        name="win_attn",
    )(proj, proj, proj, proj, proj, proj, proj, proj_meta, proj_meta, qf, kf, mask, sink)


ROUTE_E1, ROUTE_E2, ROUTE_POS1, ROUTE_POS2, ROUTE_W1, ROUTE_W2 = range(6)


def _pack_bf16_pairs(x):
    k = x.shape[1] // 2
    bits = lax.bitcast_convert_type(x.astype(BF16).astype(F32), jnp.uint32)
    return lax.bitcast_convert_type(bits[:, :k] | (bits[:, k:] >> 16), jnp.int32)


def _unpack_bf16_pairs(w):
    bits = lax.bitcast_convert_type(w, jnp.uint32)
    hi = lax.bitcast_convert_type(bits & jnp.uint32(0xFFFF0000), F32)
    lo = lax.bitcast_convert_type(bits << 16, F32)
    return jnp.concatenate([hi, lo], axis=1)


def _out_router_kernel(od_ref, os_ref, x_ref, wod_ref, wos_ref, g2_ref, wr_hi_ref, wr_lo_ref, br_ref,
                       h_ref, a_ref, route_ref, cnt_ref, base_ref):
    @pl.when(pl.program_id(0) == 0)
    def _():
        base_ref[...] = jnp.zeros_like(base_ref)

    h = (x_ref[...]
         + jnp.dot(od_ref[...], wod_ref[...], preferred_element_type=F32)
         + jnp.dot(os_ref[...], wos_ref[...], preferred_element_type=F32))
    h_ref[...] = h
    a = h * lax.rsqrt(jnp.mean(h * h, axis=-1, keepdims=True) + EPS) * g2_ref[...]
    a_hi = a.astype(BF16)
    a_ref[...] = _pack_bf16_pairs(a)
    a_lo = (a - a_hi.astype(F32)).astype(BF16)
    logits = (jnp.dot(a_hi, wr_hi_ref[...], preferred_element_type=F32)
              + jnp.dot(a_lo, wr_hi_ref[...], preferred_element_type=F32)
              + jnp.dot(a_hi, wr_lo_ref[...], preferred_element_type=F32)) + br_ref[...]
    tm = logits.shape[0]
    lane = lax.broadcasted_iota(jnp.int32, (tm, LANES), 1)
    big = jnp.int32(LANES)

    def first_argmax(mask):
        vals = jnp.where(mask, logits, NEG_INF)
        mx = jnp.max(vals, axis=-1, keepdims=True)
        idx = jnp.min(jnp.where(mask & (vals == mx), lane, big), axis=-1, keepdims=True)
        return mx, idx

    gmask = (lane >= N_EXPERTS) & (lane < N_EXPERTS + N_GROUPS)
    gmax, gidx = first_argmax(gmask)
    gsum = jnp.sum(jnp.where(gmask, jnp.exp(logits - gmax), 0.0), axis=-1, keepdims=True)
    g_w = 1.0 / gsum
    in_group = (lane // EXPERTS_PER_GROUP) == (gidx - N_EXPERTS)
    m1, i1 = first_argmax(in_group)
    m2, i2 = first_argmax(in_group & (lane != i1))
    r = jnp.exp(m2 - m1)
    w1 = g_w / (1.0 + r)
    w2 = g_w * r / (1.0 + r)

    rix = lax.broadcasted_iota(jnp.int32, (tm, tm), 0)
    cix = lax.broadcasted_iota(jnp.int32, (tm, tm), 1)
    lower = jnp.where(rix > cix, 1.0, 0.0).astype(BF16)
    oh1 = jnp.where(lane == i1, 1.0, 0.0)
    oh2 = jnp.where(lane == i2, 1.0, 0.0)
    base1 = base_ref[...]
    pos1 = jnp.sum(oh1 * (base1 + jnp.dot(lower, oh1.astype(BF16), preferred_element_type=F32)),
                   axis=-1, keepdims=True)
    base2 = base1 + jnp.sum(oh1, axis=0, keepdims=True)
    pos2 = jnp.sum(oh2 * (base2 + jnp.dot(lower, oh2.astype(BF16), preferred_element_type=F32)),
                   axis=-1, keepdims=True)
    total = base2 + jnp.sum(oh2, axis=0, keepdims=True)
    base_ref[...] = total
    cnt_ref[...] = total

    fields = (i1.astype(F32), i2.astype(F32), pos1, pos2, w1, w2)
    route = jnp.zeros((tm, LANES), F32)
    for n, val in enumerate(fields):
        route = jnp.where(lane == n, val, route)
    route_ref[...] = route


def _out_router(od, os_, x2d, wod, wos, g2, wr_hi, wr_lo, br, tm):
    n, d = x2d.shape
    half = od.shape[1]
    row = lambda i: (i, 0)
    const = lambda i: (0, 0)
    return pl.pallas_call(
        _out_router_kernel,
        grid=(n // tm,),
        in_specs=[
            pl.BlockSpec((tm, half), row),
            pl.BlockSpec((tm, half), row),
            pl.BlockSpec((tm, d), row),
            pl.BlockSpec((half, d), const),
            pl.BlockSpec((half, d), const),
            pl.BlockSpec((1, d), const),
            pl.BlockSpec((d, LANES), const),
            pl.BlockSpec((d, LANES), const),
            pl.BlockSpec((1, LANES), const),
        ],
        out_specs=[
            pl.BlockSpec((tm, d), row),
            pl.BlockSpec((tm, d // 2), row),
            pl.BlockSpec((tm, LANES), row),
            pl.BlockSpec((1, LANES), const),
        ],
        out_shape=[
            jax.ShapeDtypeStruct((n, d), F32),
            jax.ShapeDtypeStruct((n, d // 2), jnp.int32),
            jax.ShapeDtypeStruct((n, LANES), F32),
            jax.ShapeDtypeStruct((1, LANES), F32),
        ],
        scratch_shapes=[pltpu.VMEM((1, LANES), F32)],
        compiler_params=_params(("arbitrary",)),
        name="out_router",
    )(od, os_, x2d, wod, wos, g2, wr_hi, wr_lo, br)


SC_CORES, SC_SUBCORES = 2, 16
SC_WORKERS = SC_CORES * SC_SUBCORES
SC_CHUNK = 128


def _sc_mesh():
    return plsc.VectorSubcoreMesh(core_axis_name="c", subcore_axis_name="s",
                                  num_cores=SC_CORES, num_subcores=SC_SUBCORES)


def _sc_scatter_rows(x, idx1, idx2, n_out):
    n, d = x.shape
    assert n % (SC_WORKERS * SC_CHUNK) == 0
    per_w = n // SC_WORKERS

    @functools.partial(
        pl.kernel, mesh=_sc_mesh(), out_type=jax.ShapeDtypeStruct((n_out, d), x.dtype),
        scratch_types=[pltpu.VMEM((SC_CHUNK,), jnp.int32), pltpu.VMEM((SC_CHUNK,), jnp.int32),
                       pltpu.VMEM((SC_CHUNK, d), x.dtype), pltpu.SemaphoreType.DMA],
        name="sc_scatter_rows")
    def scatter(x_hbm, i1_hbm, i2_hbm, o_hbm, i1_v, i2_v, rows_v, sem):
        wid = lax.axis_index("s") * SC_CORES + lax.axis_index("c")

        @pl.loop(0, per_w // SC_CHUNK)
        def _(j):
            base = wid * per_w + j * SC_CHUNK
            pltpu.sync_copy(i1_hbm.at[pl.ds(base, SC_CHUNK)], i1_v)
            pltpu.sync_copy(i2_hbm.at[pl.ds(base, SC_CHUNK)], i2_v)
            pltpu.sync_copy(x_hbm.at[pl.ds(base, SC_CHUNK)], rows_v)
            pltpu.async_copy(rows_v, o_hbm.at[i1_v], sem).wait()
            pltpu.async_copy(rows_v, o_hbm.at[i2_v], sem).wait()

    return scatter(x, idx1, idx2)


def _sc_gather_rows(table, idx1, idx2):
    n = idx1.shape[0]
    d = table.shape[1]
    assert n % (SC_WORKERS * SC_CHUNK) == 0
    per_w = n // SC_WORKERS
    out = jax.ShapeDtypeStruct((n, d), table.dtype)

    @functools.partial(
        pl.kernel, mesh=_sc_mesh(), out_type=(out, out),
        scratch_types=[pltpu.VMEM((SC_CHUNK,), jnp.int32), pltpu.VMEM((SC_CHUNK, d), table.dtype),
                       pltpu.SemaphoreType.DMA],
        name="sc_gather_rows")
    def gather(t_hbm, i1_hbm, i2_hbm, o1_hbm, o2_hbm, i_v, rows_v, sem):
        wid = lax.axis_index("s") * SC_CORES + lax.axis_index("c")

        @pl.loop(0, per_w // SC_CHUNK)
        def _(j):
            base = wid * per_w + j * SC_CHUNK
            for i_hbm, o_hbm in ((i1_hbm, o1_hbm), (i2_hbm, o2_hbm)):
                pltpu.sync_copy(i_hbm.at[pl.ds(base, SC_CHUNK)], i_v)
                pltpu.async_copy(t_hbm.at[i_v], rows_v, sem).wait()
                pltpu.sync_copy(rows_v, o_hbm.at[pl.ds(base, SC_CHUNK)])

    return gather(table, idx1, idx2)


def _dest_kernel(route_ref, starts_ref, d1_ref, d2_ref):
    tm = route_ref.shape[0]
    route = route_ref[...]
    lane = lax.broadcasted_iota(jnp.int32, (tm, LANES), 1)
    row = lax.broadcasted_iota(jnp.int32, (tm, LANES), 0)

    def field(n):
        return jnp.sum(jnp.where(lane == n, route, 0.0), axis=-1, keepdims=True)

    for e_lane, pos_lane, out_ref in ((ROUTE_E1, ROUTE_POS1, d1_ref), (ROUTE_E2, ROUTE_POS2, d2_ref)):
        expert = field(e_lane).astype(jnp.int32)
        start = jnp.sum(jnp.where(lane == expert, starts_ref[...], 0.0), axis=-1, keepdims=True)
        dest = start + field(pos_lane)
        spread = jnp.where(lane == row % LANES, dest, 0.0)
        out_ref[...] = jnp.sum(spread.reshape(tm // LANES, LANES, LANES), axis=1).astype(jnp.int32)


def _routing_tables(route, cnt, tm, n_tiles_max):
    n = route.shape[0]
    counts = cnt[0, :N_EXPERTS].astype(jnp.int32)
    padded = (counts + tm - 1) // tm * tm
    ends = jnp.cumsum(padded)
    starts = jnp.pad((ends - padded).astype(F32), (0, LANES - N_EXPERTS)).reshape(1, LANES)
    tr = min(1024, n)
    lane_dense = jax.ShapeDtypeStruct((n // LANES, LANES), jnp.int32)
    dest1, dest2 = pl.pallas_call(
        _dest_kernel,
        grid=(n // tr,),
        in_specs=[pl.BlockSpec((tr, LANES), lambda i: (i, 0)), pl.BlockSpec((1, LANES), lambda i: (0, 0))],
        out_specs=[pl.BlockSpec((tr // LANES, LANES), lambda i: (i, 0))] * 2,
        out_shape=[lane_dense, lane_dense],
        compiler_params=_params(("parallel",)),
        name="route_dest",
    )(route, starts)
    tile_start = jnp.arange(n_tiles_max, dtype=jnp.int32) * tm
    tile_expert = jnp.sum((ends[None, :] <= tile_start[:, None]).astype(jnp.int32), axis=1)
    tile_expert = jnp.minimum(tile_expert, N_EXPERTS - 1)
    n_tiles = (ends[-1] // tm).astype(jnp.int32).reshape(1)
    return dest1.reshape(n), dest2.reshape(n), tile_expert, n_tiles


def _moe_tiles_kernel(te_ref, nt_ref, xs_ref, wg_ref, wu_ref, wd_ref, ys_ref):
    del te_ref

    @pl.when(pl.program_id(0) < nt_ref[0])
    def _():
        x = _unpack_bf16_pairs(xs_ref[...]).astype(BF16)
        hg = jnp.dot(x, wg_ref[...], preferred_element_type=F32)
        hu = jnp.dot(x, wu_ref[...], preferred_element_type=F32)
        hid = (hg * jax.nn.sigmoid(hg) * hu).astype(BF16)
        ys_ref[...] = _pack_bf16_pairs(jnp.dot(hid, wd_ref[...], preferred_element_type=F32))


def _moe_tiles(xs, tile_expert, n_tiles, wg, wu, wd, tm):
    r, dh = xs.shape
    ne, d, de = wg.shape
    row = lambda t, te, nt: (t, 0)
    return pl.pallas_call(
        _moe_tiles_kernel,
        grid_spec=pltpu.PrefetchScalarGridSpec(
            num_scalar_prefetch=2,
            grid=(r // tm,),
            in_specs=[
                pl.BlockSpec((tm, dh), row),
                pl.BlockSpec((None, d, de), lambda t, te, nt: (te[t], 0, 0)),
                pl.BlockSpec((None, d, de), lambda t, te, nt: (te[t], 0, 0)),
                pl.BlockSpec((None, de, d), lambda t, te, nt: (te[t], 0, 0)),
            ],
            out_specs=pl.BlockSpec((tm, dh), row),
        ),
        out_shape=jax.ShapeDtypeStruct((r, dh), jnp.int32),
        compiler_params=_params(("arbitrary",)),
        name="moe_tiles",
    )(tile_expert, n_tiles, xs, wg, wu, wd)


def _combine_kernel(h_ref, y1_ref, y2_ref, route_ref, fg_ref, o_ref):
    lane = lax.broadcasted_iota(jnp.int32, route_ref.shape, 1)
    route = route_ref[...]
    w1 = jnp.sum(jnp.where(lane == ROUTE_W1, route, 0.0), axis=-1, keepdims=True)
    w2 = jnp.sum(jnp.where(lane == ROUTE_W2, route, 0.0), axis=-1, keepdims=True)
    y = h_ref[...] + w1 * _unpack_bf16_pairs(y1_ref[...]) + w2 * _unpack_bf16_pairs(y2_ref[...])
    o_ref[...] = y * lax.rsqrt(jnp.mean(y * y, axis=-1, keepdims=True) + EPS) * fg_ref[...]


def _combine(h, y1, y2, route, fg, tm):
    n, d = h.shape
    row = lambda i: (i, 0)
    return pl.pallas_call(
        _combine_kernel,
        grid=(n // tm,),
        in_specs=[
            pl.BlockSpec((tm, d), row),
            pl.BlockSpec((tm, d // 2), row),
            pl.BlockSpec((tm, d // 2), row),
            pl.BlockSpec((tm, LANES), row),
            pl.BlockSpec((1, d), lambda i: (0, 0)),
        ],
        out_specs=pl.BlockSpec((tm, d), row),
        out_shape=jax.ShapeDtypeStruct((n, d), F32),
        compiler_params=_params(("parallel",)),
        name="moe_combine",
    )(h, y1, y2, route, fg)


def _moe(a2p, route, cnt, h, wg, wu, wd, fg, tm):
    n = h.shape[0]
    n_tiles_max = (2 * n) // tm + N_EXPERTS
    dest1, dest2, tile_expert, n_tiles = _routing_tables(route, cnt, tm, n_tiles_max)
    xs = _sc_scatter_rows(a2p, dest1, dest2, n_tiles_max * tm)
    ys = _moe_tiles(xs, tile_expert, n_tiles, wg, wu, wd, tm)
    y1, y2 = _sc_gather_rows(ys, dest1, dest2)
    return _combine(h, y1, y2, route, fg, tm=min(512, n))


def _dup_heads(w, n_heads):
    d = w.shape[0]
    w = w.reshape(d, n_heads, 1, HEAD_DIM)
    return jnp.broadcast_to(w, (d, n_heads, 2, HEAD_DIM)).reshape(d, n_heads * 2 * HEAD_DIM)


def _encoder(x, proj_meta, wts):
    b, s, d = x.shape
    x2d = x.reshape(b * s, d)
    proj = _norm_proj(x2d, wts["g1"], wts["w_ext"], tm=512).reshape(b, s, -1)
    od = _diff_attn(proj, proj_meta, wts["lam_vecs"], wts["subln_g"], bq=512, bk=min(1024, s))
    os_ = _win_attn(proj, proj_meta, wts["sink"])
    h, a2p, route, cnt = _out_router(od.reshape(b * s, -1), os_.reshape(b * s, -1), x2d, wts["wod"], wts["wos"],
                                     wts["g2"], wts["wr_hi"], wts["wr_lo"], wts["br"], tm=512)
    y = _moe(a2p, route, cnt, h, wts["wg"], wts["wu"], wts["wd"], wts["fg"], tm=512)
    return y.reshape(b, s, d)


def kernel(x_prompt, x_sample, meta, norm1_g, w_in, lam_q1, lam_k1, lam_q2, lam_k2, subln_g, sink, w_out,
           norm2_g, w_gr, b_gr, w_er, b_er, w_gate, w_up, w_down, final_g):
    d = x_prompt.shape[-1]
    w = w_in[0]
    c_kd = 2 * DA_HEADS * HEAD_DIM
    c_vd = 2 * c_kd
    c_qs = c_vd + DA_HEADS * 2 * HEAD_DIM
    c_ks = c_qs + SW_HEADS * HEAD_DIM
    c_vs = c_ks + SW_KV_HEADS * HEAD_DIM
    w_ext = jnp.concatenate(
        [w[:, :c_ks], _dup_heads(w[:, c_ks:c_vs], SW_KV_HEADS), _dup_heads(w[:, c_vs:], SW_KV_HEADS)],
        axis=1).astype(BF16)
    w_router = jnp.concatenate([w_er[0], w_gr[0]], axis=1)
    w_router = jnp.pad(w_router, ((0, 0), (0, LANES - w_router.shape[1])))
    wr_hi = w_router.astype(BF16)
    wr_lo = (w_router - wr_hi.astype(F32)).astype(BF16)
    br = jnp.pad(jnp.concatenate([b_er[0], b_gr[0]]), (0, LANES - N_EXPERTS - N_GROUPS)).reshape(1, LANES)
    wo = w_out[0].astype(BF16)
    half = DA_HEADS * 2 * HEAD_DIM
    wts = dict(
        g1=norm1_g[0].reshape(1, d), w_ext=w_ext,
        lam_vecs=(lam_q1[0].reshape(1, -1), lam_k1[0].reshape(1, -1),
                  lam_q2[0].reshape(1, -1), lam_k2[0].reshape(1, -1)),
        subln_g=subln_g[0].reshape(1, -1), sink=sink[0].reshape(1, -1),
        wod=wo[:half], wos=wo[half:], g2=norm2_g[0].reshape(1, d),
        wr_hi=wr_hi, wr_lo=wr_lo, br=br,
        wg=w_gate[0].astype(BF16), wu=w_up[0].astype(BF16), wd=w_down[0].astype(BF16),
        fg=final_g.reshape(1, d),
    )
    proj_meta = _norm_proj(meta, wts["g1"], w_ext, tm=N_META)
    proj_meta = jnp.pad(proj_meta, ((0, LANES - N_META), (0, 0)))
    return _encoder(x_prompt, proj_meta, wts), _encoder(x_sample, proj_meta, wts)
```

```python
import functools
import math

import jax
import jax.numpy as jnp
import numpy as np
from jax import lax
from jax.experimental import pallas as pl
from jax.experimental.pallas import tpu as pltpu
from jax.experimental.pallas import tpu_sc as plsc

F32 = jnp.float32
BF16 = jnp.bfloat16

N_META = 16
HEAD_DIM = 64
DA_HEADS = 4
SW_HEADS = 8
SW_KV_HEADS = 2
SW_GROUP = SW_HEADS // SW_KV_HEADS
WINDOW = 128
N_GROUPS = 4
EXPERTS_PER_GROUP = 8
N_EXPERTS = N_GROUPS * EXPERTS_PER_GROUP
EPS = 1e-6
SUBLN_EPS = 1e-5
NEG_INF = -1e30
LAM_INIT = 0.8 - 0.6 * math.exp(-0.3 * 0)
LANES = 128
VMEM_LIMIT = 48 * 1024 * 1024

COL_QD, COL_KD, COL_VD, COL_QS, COL_KS, COL_VS, N_COLBLK = 0, 4, 8, 12, 16, 18, 20
CONTRACT_LAST = (((1,), (1,)), ((), ()))


def _params(sem):
    return pltpu.CompilerParams(dimension_semantics=sem, vmem_limit_bytes=VMEM_LIMIT)


def _norm_proj_kernel(x_ref, g_ref, w_ref, o_ref):
    x = x_ref[...]
    ms = jnp.mean(x * x, axis=-1, keepdims=True)
    y = (x * lax.rsqrt(ms + EPS) * g_ref[...]).astype(BF16)
    o_ref[...] = jnp.dot(y, w_ref[...], preferred_element_type=F32).astype(o_ref.dtype)


def _norm_proj(x2d, g, w_ext, tm):
    n, d = x2d.shape
    wcols = w_ext.shape[1]
    return pl.pallas_call(
        _norm_proj_kernel,
        grid=(n // tm,),
        in_specs=[
            pl.BlockSpec((tm, d), lambda i: (i, 0)),
            pl.BlockSpec((1, d), lambda i: (0, 0)),
            pl.BlockSpec((d, wcols), lambda i: (0, 0)),
        ],
        out_specs=pl.BlockSpec((tm, wcols), lambda i: (i, 0)),
        out_shape=jax.ShapeDtypeStruct((n, wcols), BF16),
        compiler_params=_params(("parallel",)),
        name="norm_proj",
    )(x2d, g, w_ext)


POS_SPLIT = 64
FEAT_ONE_A, FEAT_ONE_B, FEAT_HI, FEAT_LO = 0, 1, 2, 3


def _key_pos_features(seq):
    j = jnp.arange(seq, dtype=jnp.int32)[:, None]
    lane = jnp.arange(LANES, dtype=jnp.int32)[None, :]
    hi = (j // POS_SPLIT * POS_SPLIT).astype(F32)
    lo = (j % POS_SPLIT).astype(F32)
    feat = jnp.where(lane <= FEAT_ONE_B, 1.0, jnp.where(lane == FEAT_HI, hi, jnp.where(lane == FEAT_LO, lo, 0.0)))
    return feat.astype(BF16)


def _diff_attn_head(h, q_ref, k_ref, v_ref, kf_ref, km_ref, vm_ref, lam, sg_ref,
                    o_ref, qe_ref, acc_ref, m_ref, s0_ref, s1_ref, *, bq, bk, seq):
    i = pl.program_id(2)
    scale = 1.0 / math.sqrt(HEAD_DIM)
    slope = jnp.exp2(jnp.full((1, 1), -8.0 / DA_HEADS, F32) * (h + 1).astype(F32))
    lane = lax.broadcasted_iota(jnp.int32, (1, LANES), 1)
    q = q_ref[...] * jnp.asarray(scale, BF16)
    zero = jnp.zeros_like(q)
    qmaps = (jnp.where(lane < HEAD_DIM, q, zero), jnp.where(lane >= HEAD_DIM, q, zero))
    qpos = i * bq + lax.broadcasted_iota(jnp.int32, (bq, 1), 0)
    q_hi = (qpos // POS_SPLIT * POS_SPLIT).astype(F32)
    q_lo = (qpos % POS_SPLIT).astype(F32)
    feat = jnp.where(lane == FEAT_ONE_A, -slope * q_hi,
                     jnp.where(lane == FEAT_ONE_B, -slope * q_lo, jnp.where(lane <= FEAT_LO, slope, 0.0)))
    feats = (feat.astype(BF16), (-feat).astype(BF16))
    for side in range(2):
        for c in range(2):
            qe_ref[side, c * bq:(c + 1) * bq, :LANES] = qmaps[c]
            qe_ref[side, c * bq:(c + 1) * bq, LANES:] = feats[side]

    ones_blk = jnp.broadcast_to(jnp.where(lane == 0, 1.0, 0.0).astype(BF16), (bk, LANES))

    def attend(s, v_ext):
        m_old = m_ref[...]
        m_new = jnp.maximum(m_old, jnp.max(s, axis=-1, keepdims=True))
        alpha = jnp.exp(m_old - m_new)
        p = jnp.exp(s - jnp.tile(m_new, (1, s.shape[1] // LANES))).astype(BF16)
        acc_ref[...] = jnp.tile(alpha, (1, 2)) * acc_ref[...] + jnp.dot(p, v_ext, preferred_element_type=F32)
        m_ref[...] = m_new

    m_ref[...] = jnp.full(m_ref.shape, NEG_INF, F32)
    acc_ref[...] = jnp.zeros(acc_ref.shape, F32)

    nblk = seq // bk
    diag = (i * bq) // bk

    def key_block(pos):
        t = pos - 1
        return jnp.where(pos == 0, diag, t + jnp.where(t >= diag, 1, 0))

    def k_ext_at(kb):
        start = pl.multiple_of(kb * bk, bk)
        return jnp.concatenate([k_ref[pl.ds(start, bk), :], kf_ref[pl.ds(start, bk), :]], axis=1)

    def scores_into(dst_ref, pos):
        kb = key_block(pos)
        side = jnp.where(kb > diag, 1, 0)
        dst_ref[...] = lax.dot_general(qe_ref[side], k_ext_at(kb), CONTRACT_LAST, preferred_element_type=F32)

    def consume(src_ref, pos):
        start = pl.multiple_of(key_block(pos) * bk, bk)
        attend(src_ref[...], jnp.concatenate([v_ref[pl.ds(start, bk), :], ones_blk], axis=1))

    kpos = diag * bk + lax.broadcasted_iota(jnp.int32, (1, bk), 1)
    bias = -slope * jnp.abs(qpos - kpos).astype(F32)
    s0_ref[...] = lax.dot_general(qe_ref[0, :, :LANES], k_ref[pl.ds(pl.multiple_of(diag * bk, bk), bk), :],
                                  CONTRACT_LAST, preferred_element_type=F32) + jnp.tile(bias, (2, 1))

    s = lax.dot_general(qe_ref[0, :, :LANES], km_ref[...], CONTRACT_LAST, preferred_element_type=F32)
    s = jnp.where(lane < N_META, s, NEG_INF)
    attend(s, jnp.concatenate([vm_ref[...], ones_blk[:LANES]], axis=1))

    if nblk == 1:
        consume(s0_ref, 0)
    else:
        def body(u, carry):
            scores_into(s1_ref, 2 * u + 1)
            consume(s0_ref, 2 * u)
            scores_into(s0_ref, 2 * u + 2)
            consume(s1_ref, 2 * u + 1)
            return carry

        lax.fori_loop(0, (nblk - 2) // 2, body, 0)
        scores_into(s1_ref, nblk - 1)
        consume(s0_ref, nblk - 2)
        consume(s1_ref, nblk - 1)

    o1 = acc_ref[:bq, :LANES] / acc_ref[:bq, LANES:LANES + 1]
    o2 = acc_ref[bq:, :LANES] / acc_ref[bq:, LANES:LANES + 1]
    o = o1 - lam * o2
    o = o * lax.rsqrt(jnp.mean(o * o, axis=-1, keepdims=True) + SUBLN_EPS) * sg_ref[...]
    o_ref[...] = (o * (1.0 - LAM_INIT)).astype(o_ref.dtype)


HEADS_PER_STEP = 4


def _diff_attn_kernel(q_ref, k_ref, v_ref, kf_ref, km_ref, vm_ref, lq1_ref, lk1_ref, lq2_ref, lk2_ref, sg_ref,
                      o_ref, qe_ref, acc_ref, m_ref, s0_ref, s1_ref, **kw):
    lam = (jnp.exp(jnp.sum(lq1_ref[...] * lk1_ref[...], axis=-1, keepdims=True))
           - jnp.exp(jnp.sum(lq2_ref[...] * lk2_ref[...], axis=-1, keepdims=True)) + LAM_INIT)
    for hh in range(HEADS_PER_STEP):
        cols = slice(hh * LANES, (hh + 1) * LANES)
        _diff_attn_head(pl.program_id(1) * HEADS_PER_STEP + hh,
                        q_ref.at[:, cols], k_ref.at[:, cols], v_ref.at[:, cols], kf_ref,
                        km_ref.at[:, cols], vm_ref.at[:, cols], lam, sg_ref, o_ref.at[:, cols],
                        qe_ref.at[hh], acc_ref.at[hh], m_ref.at[hh], s0_ref.at[hh], s1_ref.at[hh], **kw)


def _diff_attn(proj, proj_meta, lam_vecs, subln_g, bq, bk):
    b, s, _ = proj.shape
    assert bk % bq == 0 and s % bk == 0 and (s // bk == 1 or (s // bk) % 2 == 0)
    hp = HEADS_PER_STEP
    w = hp * LANES
    vec = pl.BlockSpec((1, HEAD_DIM), lambda bi, h, i: (0, 0))
    return pl.pallas_call(
        functools.partial(_diff_attn_kernel, bq=bq, bk=bk, seq=s),
        grid=(b, DA_HEADS // hp, s // bq),
        in_specs=[
            pl.BlockSpec((None, bq, w), lambda bi, h, i: (bi, i, COL_QD // hp + h)),
            pl.BlockSpec((None, s, w), lambda bi, h, i: (bi, 0, COL_KD // hp + h)),
            pl.BlockSpec((None, s, w), lambda bi, h, i: (bi, 0, COL_VD // hp + h)),
            pl.BlockSpec((s, LANES), lambda bi, h, i: (0, 0)),
            pl.BlockSpec((LANES, w), lambda bi, h, i: (0, COL_KD // hp + h)),
            pl.BlockSpec((LANES, w), lambda bi, h, i: (0, COL_VD // hp + h)),
            vec, vec, vec, vec,
            pl.BlockSpec((1, LANES), lambda bi, h, i: (0, 0)),
        ],
        out_specs=pl.BlockSpec((None, bq, w), lambda bi, h, i: (bi, i, h)),
        out_shape=jax.ShapeDtypeStruct((b, s, DA_HEADS * LANES), BF16),
        scratch_shapes=[
            pltpu.VMEM((hp, 2, 2 * bq, 2 * LANES), BF16),
            pltpu.VMEM((hp, 2 * bq, 2 * LANES), F32),
            pltpu.VMEM((hp, 2 * bq, LANES), F32),
            pltpu.VMEM((hp, 2 * bq, bk), F32),
            pltpu.VMEM((hp, 2 * bq, bk), F32),
        ],
        compiler_params=_params(("parallel", "parallel", "arbitrary")),
        name="diff_attn",
    )(proj, proj, proj, _key_pos_features(s), proj_meta, proj_meta, *lam_vecs, subln_g)


WIN_KEYS = 4 * WINDOW


def _win_tables():
    r = np.arange(WINDOW)
    qf = np.zeros((SW_KV_HEADS, SW_GROUP * WINDOW, LANES), np.float32)
    for head in range(SW_HEADS):
        slope = 2.0 ** (-8.0 * (head + 1) / SW_HEADS)
        i_rel = WINDOW + r
        hi, lo = i_rel // POS_SPLIT * POS_SPLIT, i_rel % POS_SPLIT
        rows = qf[head // SW_GROUP, (head % SW_GROUP) * WINDOW:(head % SW_GROUP + 1) * WINDOW]
        rows[:, 0], rows[:, 1], rows[:, 2], rows[:, 3] = -slope * hi, -slope * lo, slope, slope
        rows[:, 4:8] = -rows[:, 0:4]
    kf = np.zeros((4, WINDOW, LANES), np.float32)
    for n, (blk, right) in enumerate(((0, 0), (1, 0), (1, 1), (2, 1))):
        j_rel = blk * WINDOW + r
        o = 4 * right
        kf[n, :, o], kf[n, :, o + 1] = 1.0, 1.0
        kf[n, :, o + 2], kf[n, :, o + 3] = j_rel // POS_SPLIT * POS_SPLIT, j_rel % POS_SPLIT
    mask = np.zeros((2, WINDOW, WINDOW), np.float32)
    mask[0] = np.where(r[None, :] >= r[:, None], 0.0, NEG_INF)
    mask[1] = np.where(r[None, :] <= r[:, None], 0.0, NEG_INF)
    return jnp.asarray(qf, BF16), jnp.asarray(kf, BF16), jnp.asarray(mask, F32)


def _win_attn_kernel(q_ref, kp_ref, kc_ref, kn_ref, vp_ref, vc_ref, vn_ref, km_ref, vm_ref, qf_ref, kf_ref,
                     mask_ref, sink_ref, o_ref, *, nstep):
    c = pl.program_id(1)
    lane = lax.broadcasted_iota(jnp.int32, (1, LANES), 1)
    scale = jnp.asarray(1.0 / math.sqrt(HEAD_DIM), BF16)
    mask_meta = jnp.where(lane < N_META, 0.0, NEG_INF)
    row = lax.broadcasted_iota(jnp.int32, (SW_GROUP * WINDOW, 1), 0)
    ones_blk = jnp.broadcast_to(jnp.where(lane == 0, 1.0, 0.0).astype(BF16), (WIN_KEYS, LANES))
    zeros_blk = jnp.zeros((WINDOW, LANES), BF16)
    gw = SW_GROUP * HEAD_DIM
    kblk = (kp_ref, kc_ref.at[:WINDOW], kc_ref.at[WINDOW:], kn_ref)
    vblk = (vp_ref, vc_ref.at[:WINDOW], vc_ref.at[WINDOW:], vn_ref)
    for j in range(2):
        rq = slice(j * WINDOW, (j + 1) * WINDOW)
        edge_p = jnp.where(c == 0, NEG_INF, 0.0) if j == 0 else 0.0
        edge_n = jnp.where(c == nstep - 1, NEG_INF, 0.0) if j == 1 else 0.0
        mask_prev = jnp.tile(mask_ref[0] + edge_p, (SW_GROUP, 1))
        mask_next = jnp.tile(mask_ref[1] + edge_n, (SW_GROUP, 1))
        for g in range(SW_KV_HEADS):
            kv = slice(g * LANES, (g + 1) * LANES)
            rows = []
            for hh in range(SW_GROUP):
                col = g * gw + (hh // 2) * LANES
                qb = q_ref[rq, col:col + LANES] * scale
                keep = (lane < HEAD_DIM) if hh % 2 == 0 else (lane >= HEAD_DIM)
                rows.append(jnp.where(keep, qb, jnp.zeros_like(qb)))
            q_ext = jnp.concatenate([jnp.concatenate(rows, axis=0), qf_ref[g]], axis=1)
            k_all = jnp.concatenate([
                jnp.concatenate([km_ref[:, kv], zeros_blk], axis=1),
                jnp.concatenate([kblk[j][:, kv], kf_ref[0]], axis=1),
                jnp.concatenate([kblk[j + 1][:, kv], kf_ref[1]], axis=1),
                jnp.concatenate([kblk[j + 1][:, kv], kf_ref[2]], axis=1),
                jnp.concatenate([kblk[j + 2][:, kv], kf_ref[3]], axis=1)], axis=0)
            s = lax.dot_general(q_ext, k_all, CONTRACT_LAST, preferred_element_type=F32)
            s_meta = s[:, :LANES] + mask_meta
            s_prev = s[:, LANES:2 * LANES] + mask_prev
            s_cur = jnp.minimum(s[:, 2 * LANES:3 * LANES], s[:, 3 * LANES:4 * LANES])
            s_next = s[:, 4 * LANES:] + mask_next
            sink = jnp.zeros((SW_GROUP * WINDOW, 1), F32)
            for hh in range(SW_GROUP):
                sink = jnp.where(row // WINDOW == hh, sink_ref[0, g * SW_GROUP + hh], sink)
            parts = (s_meta, s_prev, s_cur, s_next)
            m = jnp.maximum(jnp.maximum(s_meta, s_prev), jnp.maximum(s_cur, s_next))
            m = jnp.maximum(jnp.max(m, axis=-1, keepdims=True), sink)
            p = jnp.concatenate([jnp.exp(x - m) for x in parts], axis=1).astype(BF16)
            v_all = jnp.concatenate(
                [jnp.concatenate([vm_ref[:, kv], vblk[j][:, kv], vblk[j + 1][:, kv], vblk[j + 2][:, kv]], axis=0),
                 ones_blk], axis=1)
            acc = jnp.dot(p, v_all, preferred_element_type=F32)
            denom = acc[:, LANES:LANES + 1] + jnp.exp(sink - m)
            o = acc[:, :LANES] / denom
            for lb in range(SW_GROUP // 2):
                even = o[(2 * lb) * WINDOW:(2 * lb + 1) * WINDOW]
                odd = o[(2 * lb + 1) * WINDOW:(2 * lb + 2) * WINDOW]
                col = g * gw + lb * LANES
                o_ref[rq, col:col + LANES] = jnp.where(lane < HEAD_DIM, even, odd).astype(o_ref.dtype)


def _win_attn(proj, proj_meta, sink):
    b, s, _ = proj.shape
    nblk = s // WINDOW
    nstep = nblk // 2
    qw = SW_HEADS * HEAD_DIM
    kvw = SW_KV_HEADS * LANES
    qf, kf, mask = _win_tables()

    def edge_spec(col, shift):
        return pl.BlockSpec(
            (None, WINDOW, kvw),
            lambda bi, c: (bi, jnp.clip(2 * c + shift, 0, nblk - 1), col * LANES // kvw))

    def pair_spec(col):
        return pl.BlockSpec((None, 2 * WINDOW, kvw), lambda bi, c: (bi, c, col * LANES // kvw))

    return pl.pallas_call(
        functools.partial(_win_attn_kernel, nstep=nstep),
        grid=(b, nstep),
        in_specs=[
            pl.BlockSpec((None, 2 * WINDOW, qw), lambda bi, c: (bi, c, COL_QS * LANES // qw)),
            edge_spec(COL_KS, -1), pair_spec(COL_KS), edge_spec(COL_KS, 2),
            edge_spec(COL_VS, -1), pair_spec(COL_VS), edge_spec(COL_VS, 2),
            pl.BlockSpec((LANES, kvw), lambda bi, c: (0, COL_KS * LANES // kvw)),
            pl.BlockSpec((LANES, kvw), lambda bi, c: (0, COL_VS * LANES // kvw)),
            pl.BlockSpec((SW_KV_HEADS, SW_GROUP * WINDOW, LANES), lambda bi, c: (0, 0, 0)),
            pl.BlockSpec((4, WINDOW, LANES), lambda bi, c: (0, 0, 0)),
            pl.BlockSpec((2, WINDOW, WINDOW), lambda bi, c: (0, 0, 0)),
            pl.BlockSpec(memory_space=pltpu.SMEM),
        ],
        out_specs=pl.BlockSpec((None, 2 * WINDOW, qw), lambda bi, c: (bi, c, 0)),
        out_shape=jax.ShapeDtypeStruct((b, s, qw), BF16),
        compiler_params=_params(("parallel", "arbitrary")),
        name="win_attn",
    )(proj, proj, proj, proj, proj, proj, proj, proj_meta, proj_meta, qf, kf, mask, sink)


ROUTE_E1, ROUTE_E2, ROUTE_POS1, ROUTE_POS2, ROUTE_W1, ROUTE_W2 = range(6)


def _pack_bf16_pairs(x):
    k = x.shape[1] // 2
    bits = lax.bitcast_convert_type(x.astype(BF16).astype(F32), jnp.uint32)
    return lax.bitcast_convert_type(bits[:, :k] | (bits[:, k:] >> 16), jnp.int32)


def _unpack_bf16_pairs(w):
    bits = lax.bitcast_convert_type(w, jnp.uint32)
    hi = lax.bitcast_convert_type(bits & jnp.uint32(0xFFFF0000), F32)
    lo = lax.bitcast_convert_type(bits << 16, F32)
    return jnp.concatenate([hi, lo], axis=1)


def _out_router_kernel(od_ref, os_ref, x_ref, wod_ref, wos_ref, g2_ref, wr_hi_ref, wr_lo_ref, br_ref,
                       h_ref, a_ref, route_ref, cnt_ref, base_ref):
    @pl.when(pl.program_id(0) == 0)
    def _():
        base_ref[...] = jnp.zeros_like(base_ref)

    h = (x_ref[...]
         + jnp.dot(od_ref[...], wod_ref[...], preferred_element_type=F32)
         + jnp.dot(os_ref[...], wos_ref[...], preferred_element_type=F32))
    h_ref[...] = h
    a = h * lax.rsqrt(jnp.mean(h * h, axis=-1, keepdims=True) + EPS) * g2_ref[...]
    a_hi = a.astype(BF16)
    a_ref[...] = _pack_bf16_pairs(a)
    a_lo = (a - a_hi.astype(F32)).astype(BF16)
    logits = (jnp.dot(a_hi, wr_hi_ref[...], preferred_element_type=F32)
              + jnp.dot(a_lo, wr_hi_ref[...], preferred_element_type=F32)
              + jnp.dot(a_hi, wr_lo_ref[...], preferred_element_type=F32)) + br_ref[...]
    tm = logits.shape[0]
    lane = lax.broadcasted_iota(jnp.int32, (tm, LANES), 1)
    big = jnp.int32(LANES)

    def first_argmax(mask):
        vals = jnp.where(mask, logits, NEG_INF)
        mx = jnp.max(vals, axis=-1, keepdims=True)
        idx = jnp.min(jnp.where(mask & (vals == mx), lane, big), axis=-1, keepdims=True)
        return mx, idx

    gmask = (lane >= N_EXPERTS) & (lane < N_EXPERTS + N_GROUPS)
    gmax, gidx = first_argmax(gmask)
    gsum = jnp.sum(jnp.where(gmask, jnp.exp(logits - gmax), 0.0), axis=-1, keepdims=True)
    g_w = 1.0 / gsum
    in_group = (lane // EXPERTS_PER_GROUP) == (gidx - N_EXPERTS)
    m1, i1 = first_argmax(in_group)
    m2, i2 = first_argmax(in_group & (lane != i1))
    r = jnp.exp(m2 - m1)
    w1 = g_w / (1.0 + r)
    w2 = g_w * r / (1.0 + r)

    rix = lax.broadcasted_iota(jnp.int32, (tm, tm), 0)
    cix = lax.broadcasted_iota(jnp.int32, (tm, tm), 1)
    lower = jnp.where(rix > cix, 1.0, 0.0).astype(BF16)
    oh1 = jnp.where(lane == i1, 1.0, 0.0)
    oh2 = jnp.where(lane == i2, 1.0, 0.0)
    base1 = base_ref[...]
    pos1 = jnp.sum(oh1 * (base1 + jnp.dot(lower, oh1.astype(BF16), preferred_element_type=F32)),
                   axis=-1, keepdims=True)
    base2 = base1 + jnp.sum(oh1, axis=0, keepdims=True)
    pos2 = jnp.sum(oh2 * (base2 + jnp.dot(lower, oh2.astype(BF16), preferred_element_type=F32)),
                   axis=-1, keepdims=True)
    total = base2 + jnp.sum(oh2, axis=0, keepdims=True)
    base_ref[...] = total
    cnt_ref[...] = total

    fields = (i1.astype(F32), i2.astype(F32), pos1, pos2, w1, w2)
    route = jnp.zeros((tm, LANES), F32)
    for n, val in enumerate(fields):
        route = jnp.where(lane == n, val, route)
    route_ref[...] = route


def _out_router(od, os_, x2d, wod, wos, g2, wr_hi, wr_lo, br, tm):
    n, d = x2d.shape
    half = od.shape[1]
    row = lambda i: (i, 0)
    const = lambda i: (0, 0)
    return pl.pallas_call(
        _out_router_kernel,
        grid=(n // tm,),
        in_specs=[
            pl.BlockSpec((tm, half), row),
            pl.BlockSpec((tm, half), row),
            pl.BlockSpec((tm, d), row),
            pl.BlockSpec((half, d), const),
            pl.BlockSpec((half, d), const),
            pl.BlockSpec((1, d), const),
            pl.BlockSpec((d, LANES), const),
            pl.BlockSpec((d, LANES), const),
            pl.BlockSpec((1, LANES), const),
        ],
        out_specs=[
            pl.BlockSpec((tm, d), row),
            pl.BlockSpec((tm, d // 2), row),
            pl.BlockSpec((tm, LANES), row),
            pl.BlockSpec((1, LANES), const),
        ],
        out_shape=[
            jax.ShapeDtypeStruct((n, d), F32),
            jax.ShapeDtypeStruct((n, d // 2), jnp.int32),
            jax.ShapeDtypeStruct((n, LANES), F32),
            jax.ShapeDtypeStruct((1, LANES), F32),
        ],
        scratch_shapes=[pltpu.VMEM((1, LANES), F32)],
        compiler_params=_params(("arbitrary",)),
        name="out_router",
    )(od, os_, x2d, wod, wos, g2, wr_hi, wr_lo, br)


SC_CORES, SC_SUBCORES = 2, 16
SC_WORKERS = SC_CORES * SC_SUBCORES
SC_CHUNK = 128


def _sc_mesh():
    return plsc.VectorSubcoreMesh(core_axis_name="c", subcore_axis_name="s",
                                  num_cores=SC_CORES, num_subcores=SC_SUBCORES)


def _sc_scatter_rows(x, idx1, idx2, n_out):
    n, d = x.shape
    assert n % (SC_WORKERS * SC_CHUNK) == 0
    per_w = n // SC_WORKERS

    @functools.partial(
        pl.kernel, mesh=_sc_mesh(), out_type=jax.ShapeDtypeStruct((n_out, d), x.dtype),
        scratch_types=[pltpu.VMEM((SC_CHUNK,), jnp.int32), pltpu.VMEM((SC_CHUNK,), jnp.int32),
                       pltpu.VMEM((SC_CHUNK, d), x.dtype), pltpu.SemaphoreType.DMA],
        name="sc_scatter_rows")
    def scatter(x_hbm, i1_hbm, i2_hbm, o_hbm, i1_v, i2_v, rows_v, sem):
        wid = lax.axis_index("s") * SC_CORES + lax.axis_index("c")

        @pl.loop(0, per_w // SC_CHUNK)
        def _(j):
            base = wid * per_w + j * SC_CHUNK
            pltpu.sync_copy(i1_hbm.at[pl.ds(base, SC_CHUNK)], i1_v)
            pltpu.sync_copy(i2_hbm.at[pl.ds(base, SC_CHUNK)], i2_v)
            pltpu.sync_copy(x_hbm.at[pl.ds(base, SC_CHUNK)], rows_v)
            pltpu.async_copy(rows_v, o_hbm.at[i1_v], sem).wait()
            pltpu.async_copy(rows_v, o_hbm.at[i2_v], sem).wait()

    return scatter(x, idx1, idx2)


def _sc_gather_rows(table, idx1, idx2):
    n = idx1.shape[0]
    d = table.shape[1]
    assert n % (SC_WORKERS * SC_CHUNK) == 0
    per_w = n // SC_WORKERS
    out = jax.ShapeDtypeStruct((n, d), table.dtype)

    @functools.partial(
        pl.kernel, mesh=_sc_mesh(), out_type=(out, out),
        scratch_types=[pltpu.VMEM((SC_CHUNK,), jnp.int32), pltpu.VMEM((SC_CHUNK, d), table.dtype),
                       pltpu.SemaphoreType.DMA],
        name="sc_gather_rows")
    def gather(t_hbm, i1_hbm, i2_hbm, o1_hbm, o2_hbm, i_v, rows_v, sem):
        wid = lax.axis_index("s") * SC_CORES + lax.axis_index("c")

        @pl.loop(0, per_w // SC_CHUNK)
        def _(j):
            base = wid * per_w + j * SC_CHUNK
            for i_hbm, o_hbm in ((i1_hbm, o1_hbm), (i2_hbm, o2_hbm)):
                pltpu.sync_copy(i_hbm.at[pl.ds(base, SC_CHUNK)], i_v)
                pltpu.async_copy(t_hbm.at[i_v], rows_v, sem).wait()
                pltpu.sync_copy(rows_v, o_hbm.at[pl.ds(base, SC_CHUNK)])

    return gather(table, idx1, idx2)


def _dest_kernel(route_ref, starts_ref, d1_ref, d2_ref):
    tm = route_ref.shape[0]
    route = route_ref[...]
    lane = lax.broadcasted_iota(jnp.int32, (tm, LANES), 1)
    row = lax.broadcasted_iota(jnp.int32, (tm, LANES), 0)

    def field(n):
        return jnp.sum(jnp.where(lane == n, route, 0.0), axis=-1, keepdims=True)

    for e_lane, pos_lane, out_ref in ((ROUTE_E1, ROUTE_POS1, d1_ref), (ROUTE_E2, ROUTE_POS2, d2_ref)):
        expert = field(e_lane).astype(jnp.int32)
        start = jnp.sum(jnp.where(lane == expert, starts_ref[...], 0.0), axis=-1, keepdims=True)
        dest = start + field(pos_lane)
        spread = jnp.where(lane == row % LANES, dest, 0.0)
        out_ref[...] = jnp.sum(spread.reshape(tm // LANES, LANES, LANES), axis=1).astype(jnp.int32)


def _routing_tables(route, cnt, tm, n_tiles_max):
    n = route.shape[0]
    counts = cnt[0, :N_EXPERTS].astype(jnp.int32)
    padded = (counts + tm - 1) // tm * tm
    ends = jnp.cumsum(padded)
    starts = jnp.pad((ends - padded).astype(F32), (0, LANES - N_EXPERTS)).reshape(1, LANES)
    tr = min(1024, n)
    lane_dense = jax.ShapeDtypeStruct((n // LANES, LANES), jnp.int32)
    dest1, dest2 = pl.pallas_call(
        _dest_kernel,
        grid=(n // tr,),
        in_specs=[pl.BlockSpec((tr, LANES), lambda i: (i, 0)), pl.BlockSpec((1, LANES), lambda i: (0, 0))],
        out_specs=[pl.BlockSpec((tr // LANES, LANES), lambda i: (i, 0))] * 2,
        out_shape=[lane_dense, lane_dense],
        compiler_params=_params(("parallel",)),
        name="route_dest",
    )(route, starts)
    tile_start = jnp.arange(n_tiles_max, dtype=jnp.int32) * tm
    tile_expert = jnp.sum((ends[None, :] <= tile_start[:, None]).astype(jnp.int32), axis=1)
    tile_expert = jnp.minimum(tile_expert, N_EXPERTS - 1)
    n_tiles = (ends[-1] // tm).astype(jnp.int32).reshape(1)
    return dest1.reshape(n), dest2.reshape(n), tile_expert, n_tiles


def _moe_tiles_kernel(te_ref, nt_ref, xs_ref, wg_ref, wu_ref, wd_ref, ys_ref):
    del te_ref

    @pl.when(pl.program_id(0) < nt_ref[0])
    def _():
        x = _unpack_bf16_pairs(xs_ref[...]).astype(BF16)
        hg = jnp.dot(x, wg_ref[...], preferred_element_type=F32)
        hu = jnp.dot(x, wu_ref[...], preferred_element_type=F32)
        hid = (hg * jax.nn.sigmoid(hg) * hu).astype(BF16)
        ys_ref[...] = _pack_bf16_pairs(jnp.dot(hid, wd_ref[...], preferred_element_type=F32))


def _moe_tiles(xs, tile_expert, n_tiles, wg, wu, wd, tm):
    r, dh = xs.shape
    ne, d, de = wg.shape
    row = lambda t, te, nt: (t, 0)
    return pl.pallas_call(
        _moe_tiles_kernel,
        grid_spec=pltpu.PrefetchScalarGridSpec(
            num_scalar_prefetch=2,
            grid=(r // tm,),
            in_specs=[
                pl.BlockSpec((tm, dh), row),
                pl.BlockSpec((None, d, de), lambda t, te, nt: (te[t], 0, 0)),
                pl.BlockSpec((None, d, de), lambda t, te, nt: (te[t], 0, 0)),
                pl.BlockSpec((None, de, d), lambda t, te, nt: (te[t], 0, 0)),
            ],
            out_specs=pl.BlockSpec((tm, dh), row),
        ),
        out_shape=jax.ShapeDtypeStruct((r, dh), jnp.int32),
        compiler_params=_params(("arbitrary",)),
        name="moe_tiles",
    )(tile_expert, n_tiles, xs, wg, wu, wd)


def _combine_kernel(h_ref, y1_ref, y2_ref, route_ref, fg_ref, o_ref):
    lane = lax.broadcasted_iota(jnp.int32, route_ref.shape, 1)
    route = route_ref[...]
    w1 = jnp.sum(jnp.where(lane == ROUTE_W1, route, 0.0), axis=-1, keepdims=True)
    w2 = jnp.sum(jnp.where(lane == ROUTE_W2, route, 0.0), axis=-1, keepdims=True)
    y = h_ref[...] + w1 * _unpack_bf16_pairs(y1_ref[...]) + w2 * _unpack_bf16_pairs(y2_ref[...])
    o_ref[...] = y * lax.rsqrt(jnp.mean(y * y, axis=-1, keepdims=True) + EPS) * fg_ref[...]


def _combine(h, y1, y2, route, fg, tm):
    n, d = h.shape
    row = lambda i: (i, 0)
    return pl.pallas_call(
        _combine_kernel,
        grid=(n // tm,),
        in_specs=[
            pl.BlockSpec((tm, d), row),
            pl.BlockSpec((tm, d // 2), row),
            pl.BlockSpec((tm, d // 2), row),
            pl.BlockSpec((tm, LANES), row),
            pl.BlockSpec((1, d), lambda i: (0, 0)),
        ],
        out_specs=pl.BlockSpec((tm, d), row),
        out_shape=jax.ShapeDtypeStruct((n, d), F32),
        compiler_params=_params(("parallel",)),
        name="moe_combine",
    )(h, y1, y2, route, fg)


def _moe(a2p, route, cnt, h, wg, wu, wd, fg, tm):
    n = h.shape[0]
    n_tiles_max = (2 * n) // tm + N_EXPERTS
    dest1, dest2, tile_expert, n_tiles = _routing_tables(route, cnt, tm, n_tiles_max)
    xs = _sc_scatter_rows(a2p, dest1, dest2, n_tiles_max * tm)
    ys = _moe_tiles(xs, tile_expert, n_tiles, wg, wu, wd, tm)
    y1, y2 = _sc_gather_rows(ys, dest1, dest2)
    return _combine(h, y1, y2, route, fg, tm=min(512, n))


def _dup_heads(w, n_heads):
    d = w.shape[0]
    w = w.reshape(d, n_heads, 1, HEAD_DIM)
    return jnp.broadcast_to(w, (d, n_heads, 2, HEAD_DIM)).reshape(d, n_heads * 2 * HEAD_DIM)


def _encoder(x, proj_meta, wts):
    b, s, d = x.shape
    x2d = x.reshape(b * s, d)
    proj = _norm_proj(x2d, wts["g1"], wts["w_ext"], tm=512).reshape(b, s, -1)
    od = _diff_attn(proj, proj_meta, wts["lam_vecs"], wts["subln_g"], bq=256, bk=min(1024, s))
    os_ = _win_attn(proj, proj_meta, wts["sink"])
    h, a2p, route, cnt = _out_router(od.reshape(b * s, -1), os_.reshape(b * s, -1), x2d, wts["wod"], wts["wos"],
                                     wts["g2"], wts["wr_hi"], wts["wr_lo"], wts["br"], tm=512)
    y = _moe(a2p, route, cnt, h, wts["wg"], wts["wu"], wts["wd"], wts["fg"], tm=512)
    return y.reshape(b, s, d)


def kernel(x_prompt, x_sample, meta, norm1_g, w_in, lam_q1, lam_k1, lam_q2, lam_k2, subln_g, sink, w_out,
           norm2_g, w_gr, b_gr, w_er, b_er, w_gate, w_up, w_down, final_g):
    d = x_prompt.shape[-1]
    w = w_in[0]
    c_kd = 2 * DA_HEADS * HEAD_DIM
    c_vd = 2 * c_kd
    c_qs = c_vd + DA_HEADS * 2 * HEAD_DIM
    c_ks = c_qs + SW_HEADS * HEAD_DIM
    c_vs = c_ks + SW_KV_HEADS * HEAD_DIM
    w_ext = jnp.concatenate(
        [w[:, :c_ks], _dup_heads(w[:, c_ks:c_vs], SW_KV_HEADS), _dup_heads(w[:, c_vs:], SW_KV_HEADS)],
        axis=1).astype(BF16)
    w_router = jnp.concatenate([w_er[0], w_gr[0]], axis=1)
    w_router = jnp.pad(w_router, ((0, 0), (0, LANES - w_router.shape[1])))
    wr_hi = w_router.astype(BF16)
    wr_lo = (w_router - wr_hi.astype(F32)).astype(BF16)
    br = jnp.pad(jnp.concatenate([b_er[0], b_gr[0]]), (0, LANES - N_EXPERTS - N_GROUPS)).reshape(1, LANES)
    wo = w_out[0].astype(BF16)
    half = DA_HEADS * 2 * HEAD_DIM
    wts = dict(
        g1=norm1_g[0].reshape(1, d), w_ext=w_ext,
        lam_vecs=(lam_q1[0].reshape(1, -1), lam_k1[0].reshape(1, -1),
                  lam_q2[0].reshape(1, -1), lam_k2[0].reshape(1, -1)),
        subln_g=subln_g[0].reshape(1, -1), sink=sink[0].reshape(1, -1),
        wod=wo[:half], wos=wo[half:], g2=norm2_g[0].reshape(1, d),
        wr_hi=wr_hi, wr_lo=wr_lo, br=br,
        wg=w_gate[0].astype(BF16), wu=w_up[0].astype(BF16), wd=w_down[0].astype(BF16),
        fg=final_g.reshape(1, d),
    )
    proj_meta = _norm_proj(meta, wts["g1"], w_ext, tm=N_META)
    proj_meta = jnp.pad(proj_meta, ((0, LANES - N_META), (0, 0)))
    return _encoder(x_prompt, proj_meta, wts), _encoder(x_sample, proj_meta, wts)
```

```python
import functools
import math

import jax
import jax.numpy as jnp
import numpy as np
from jax import lax
from jax.experimental import pallas as pl
from jax.experimental.pallas import tpu as pltpu
from jax.experimental.pallas import tpu_sc as plsc

F32 = jnp.float32
BF16 = jnp.bfloat16

N_META = 16
HEAD_DIM = 64
DA_HEADS = 4
SW_HEADS = 8
SW_KV_HEADS = 2
SW_GROUP = SW_HEADS // SW_KV_HEADS
WINDOW = 128
N_GROUPS = 4
EXPERTS_PER_GROUP = 8
N_EXPERTS = N_GROUPS * EXPERTS_PER_GROUP
EPS = 1e-6
SUBLN_EPS = 1e-5
NEG_INF = -1e30
LAM_INIT = 0.8 - 0.6 * math.exp(-0.3 * 0)
LANES = 128
VMEM_LIMIT = 48 * 1024 * 1024

COL_QD, COL_KD, COL_VD, COL_QS, COL_KS, COL_VS, N_COLBLK = 0, 4, 8, 12, 16, 18, 20
CONTRACT_LAST = (((1,), (1,)), ((), ()))


def _params(sem):
    return pltpu.CompilerParams(dimension_semantics=sem, vmem_limit_bytes=VMEM_LIMIT)


def _norm_proj_kernel(x_ref, g_ref, w_ref, o_ref):
    x = x_ref[...]
    ms = jnp.mean(x * x, axis=-1, keepdims=True)
    y = (x * lax.rsqrt(ms + EPS) * g_ref[...]).astype(BF16)
    o_ref[...] = jnp.dot(y, w_ref[...], preferred_element_type=F32).astype(o_ref.dtype)


def _norm_proj(x2d, g, w_ext, tm):
    n, d = x2d.shape
    wcols = w_ext.shape[1]
    return pl.pallas_call(
        _norm_proj_kernel,
        grid=(n // tm,),
        in_specs=[
            pl.BlockSpec((tm, d), lambda i: (i, 0)),
            pl.BlockSpec((1, d), lambda i: (0, 0)),
            pl.BlockSpec((d, wcols), lambda i: (0, 0)),
        ],
        out_specs=pl.BlockSpec((tm, wcols), lambda i: (i, 0)),
        out_shape=jax.ShapeDtypeStruct((n, wcols), BF16),
        compiler_params=_params(("parallel",)),
        name="norm_proj",
    )(x2d, g, w_ext)


POS_SPLIT = 64
FEAT_ONE_A, FEAT_ONE_B, FEAT_HI, FEAT_LO = 0, 1, 2, 3


def _key_pos_features(seq):
    j = jnp.arange(seq, dtype=jnp.int32)[:, None]
    lane = jnp.arange(LANES, dtype=jnp.int32)[None, :]
    hi = (j // POS_SPLIT * POS_SPLIT).astype(F32)
    lo = (j % POS_SPLIT).astype(F32)
    feat = jnp.where(lane <= FEAT_ONE_B, 1.0, jnp.where(lane == FEAT_HI, hi, jnp.where(lane == FEAT_LO, lo, 0.0)))
    return feat.astype(BF16)


def _diff_attn_head(h, q_ref, k_ref, v_ref, kf_ref, km_ref, vm_ref, lam, sg_ref,
                    o_ref, qe_ref, acc_ref, m_ref, s0_ref, s1_ref, *, bq, bk, seq):
    i = pl.program_id(2)
    scale = 1.0 / math.sqrt(HEAD_DIM)
    slope = jnp.exp2(jnp.full((1, 1), -8.0 / DA_HEADS, F32) * (h + 1).astype(F32))
    lane = lax.broadcasted_iota(jnp.int32, (1, LANES), 1)
    q = q_ref[...] * jnp.asarray(scale, BF16)
    zero = jnp.zeros_like(q)
    qmaps = (jnp.where(lane < HEAD_DIM, q, zero), jnp.where(lane >= HEAD_DIM, q, zero))
    qpos = i * bq + lax.broadcasted_iota(jnp.int32, (bq, 1), 0)
    q_hi = (qpos // POS_SPLIT * POS_SPLIT).astype(F32)
    q_lo = (qpos % POS_SPLIT).astype(F32)
    feat = jnp.where(lane == FEAT_ONE_A, -slope * q_hi,
                     jnp.where(lane == FEAT_ONE_B, -slope * q_lo, jnp.where(lane <= FEAT_LO, slope, 0.0)))
    feats = (feat.astype(BF16), (-feat).astype(BF16))
    for side in range(2):
        for c in range(2):
            qe_ref[side, c * bq:(c + 1) * bq, :LANES] = qmaps[c]
            qe_ref[side, c * bq:(c + 1) * bq, LANES:] = feats[side]

    ones_blk = jnp.broadcast_to(jnp.where(lane == 0, 1.0, 0.0).astype(BF16), (bk, LANES))

    def attend(s, v_ext):
        m_old = m_ref[...]
        m_new = jnp.maximum(m_old, jnp.max(s, axis=-1, keepdims=True))
        alpha = jnp.exp(m_old - m_new)
        p = jnp.exp(s - jnp.tile(m_new, (1, s.shape[1] // LANES))).astype(BF16)
        acc_ref[...] = jnp.tile(alpha, (1, 2)) * acc_ref[...] + jnp.dot(p, v_ext, preferred_element_type=F32)
        m_ref[...] = m_new

    m_ref[...] = jnp.full(m_ref.shape, NEG_INF, F32)
    acc_ref[...] = jnp.zeros(acc_ref.shape, F32)

    nblk = seq // bk
    diag = (i * bq) // bk

    def key_block(pos):
        t = pos - 1
        return jnp.where(pos == 0, diag, t + jnp.where(t >= diag, 1, 0))

    def k_ext_at(kb):
        start = pl.multiple_of(kb * bk, bk)
        return jnp.concatenate([k_ref[pl.ds(start, bk), :], kf_ref[pl.ds(start, bk), :]], axis=1)

    def scores_into(dst_ref, pos):
        kb = key_block(pos)
        side = jnp.where(kb > diag, 1, 0)
        dst_ref[...] = lax.dot_general(qe_ref[side], k_ext_at(kb), CONTRACT_LAST, preferred_element_type=F32)

    def consume(src_ref, pos):
        start = pl.multiple_of(key_block(pos) * bk, bk)
        attend(src_ref[...], jnp.concatenate([v_ref[pl.ds(start, bk), :], ones_blk], axis=1))

    kpos = diag * bk + lax.broadcasted_iota(jnp.int32, (1, bk), 1)
    bias = -slope * jnp.abs(qpos - kpos).astype(F32)
    s0_ref[...] = lax.dot_general(qe_ref[0, :, :LANES], k_ref[pl.ds(pl.multiple_of(diag * bk, bk), bk), :],
                                  CONTRACT_LAST, preferred_element_type=F32) + jnp.tile(bias, (2, 1))

    s = lax.dot_general(qe_ref[0, :, :LANES], km_ref[...], CONTRACT_LAST, preferred_element_type=F32)
    s = jnp.where(lane < N_META, s, NEG_INF)
    attend(s, jnp.concatenate([vm_ref[...], ones_blk[:LANES]], axis=1))

    if nblk == 1:
        consume(s0_ref, 0)
    else:
        def body(u, carry):
            scores_into(s1_ref, 2 * u + 1)
            consume(s0_ref, 2 * u)
            scores_into(s0_ref, 2 * u + 2)
            consume(s1_ref, 2 * u + 1)
            return carry

        lax.fori_loop(0, (nblk - 2) // 2, body, 0)
        scores_into(s1_ref, nblk - 1)
        consume(s0_ref, nblk - 2)
        consume(s1_ref, nblk - 1)

    o1 = acc_ref[:bq, :LANES] / acc_ref[:bq, LANES:LANES + 1]
    o2 = acc_ref[bq:, :LANES] / acc_ref[bq:, LANES:LANES + 1]
    o = o1 - lam * o2
    o = o * lax.rsqrt(jnp.mean(o * o, axis=-1, keepdims=True) + SUBLN_EPS) * sg_ref[...]
    o_ref[...] = (o * (1.0 - LAM_INIT)).astype(o_ref.dtype)


HEADS_PER_STEP = 4


def _diff_attn_kernel(q_ref, k_ref, v_ref, kf_ref, km_ref, vm_ref, lq1_ref, lk1_ref, lq2_ref, lk2_ref, sg_ref,
                      o_ref, qe_ref, acc_ref, m_ref, s0_ref, s1_ref, **kw):
    lam = (jnp.exp(jnp.sum(lq1_ref[...] * lk1_ref[...], axis=-1, keepdims=True))
           - jnp.exp(jnp.sum(lq2_ref[...] * lk2_ref[...], axis=-1, keepdims=True)) + LAM_INIT)
    for hh in range(HEADS_PER_STEP):
        cols = slice(hh * LANES, (hh + 1) * LANES)
        _diff_attn_head(pl.program_id(1) * HEADS_PER_STEP + hh,
                        q_ref.at[:, cols], k_ref.at[:, cols], v_ref.at[:, cols], kf_ref,
                        km_ref.at[:, cols], vm_ref.at[:, cols], lam, sg_ref, o_ref.at[:, cols],
                        qe_ref.at[hh], acc_ref.at[hh], m_ref.at[hh], s0_ref.at[hh], s1_ref.at[hh], **kw)


def _diff_attn(proj, proj_meta, lam_vecs, subln_g, bq, bk):
    b, s, _ = proj.shape
    assert bk % bq == 0 and s % bk == 0 and (s // bk == 1 or (s // bk) % 2 == 0)
    hp = HEADS_PER_STEP
    w = hp * LANES
    vec = pl.BlockSpec((1, HEAD_DIM), lambda bi, h, i: (0, 0))
    return pl.pallas_call(
        functools.partial(_diff_attn_kernel, bq=bq, bk=bk, seq=s),
        grid=(b, DA_HEADS // hp, s // bq),
        in_specs=[
            pl.BlockSpec((None, bq, w), lambda bi, h, i: (bi, i, COL_QD // hp + h)),
            pl.BlockSpec((None, s, w), lambda bi, h, i: (bi, 0, COL_KD // hp + h)),
            pl.BlockSpec((None, s, w), lambda bi, h, i: (bi, 0, COL_VD // hp + h)),
            pl.BlockSpec((s, LANES), lambda bi, h, i: (0, 0)),
            pl.BlockSpec((LANES, w), lambda bi, h, i: (0, COL_KD // hp + h)),
            pl.BlockSpec((LANES, w), lambda bi, h, i: (0, COL_VD // hp + h)),
            vec, vec, vec, vec,
            pl.BlockSpec((1, LANES), lambda bi, h, i: (0, 0)),
        ],
        out_specs=pl.BlockSpec((None, bq, w), lambda bi, h, i: (bi, i, h)),
        out_shape=jax.ShapeDtypeStruct((b, s, DA_HEADS * LANES), BF16),
        scratch_shapes=[
            pltpu.VMEM((hp, 2, 2 * bq, 2 * LANES), BF16),
            pltpu.VMEM((hp, 2 * bq, 2 * LANES), F32),
            pltpu.VMEM((hp, 2 * bq, LANES), F32),
            pltpu.VMEM((hp, 2 * bq, bk), F32),
            pltpu.VMEM((hp, 2 * bq, bk), F32),
        ],
        compiler_params=_params(("parallel", "parallel", "arbitrary")),
        name="diff_attn",
    )(proj, proj, proj, _key_pos_features(s), proj_meta, proj_meta, *lam_vecs, subln_g)


WIN_KEYS = 4 * WINDOW
WIN_QBLOCKS = 4


def _win_tables():
    r = np.arange(WINDOW)
    qf = np.zeros((SW_KV_HEADS, SW_GROUP * WINDOW, LANES), np.float32)
    for head in range(SW_HEADS):
        slope = 2.0 ** (-8.0 * (head + 1) / SW_HEADS)
        i_rel = WINDOW + r
        hi, lo = i_rel // POS_SPLIT * POS_SPLIT, i_rel % POS_SPLIT
        rows = qf[head // SW_GROUP, (head % SW_GROUP) * WINDOW:(head % SW_GROUP + 1) * WINDOW]
        rows[:, 0], rows[:, 1], rows[:, 2], rows[:, 3] = -slope * hi, -slope * lo, slope, slope
        rows[:, 4:8] = -rows[:, 0:4]
    kf = np.zeros((4, WINDOW, LANES), np.float32)
    for n, (blk, right) in enumerate(((0, 0), (1, 0), (1, 1), (2, 1))):
        j_rel = blk * WINDOW + r
        o = 4 * right
        kf[n, :, o], kf[n, :, o + 1] = 1.0, 1.0
        kf[n, :, o + 2], kf[n, :, o + 3] = j_rel // POS_SPLIT * POS_SPLIT, j_rel % POS_SPLIT
    mask = np.zeros((2, WINDOW, WINDOW), np.float32)
    mask[0] = np.where(r[None, :] >= r[:, None], 0.0, NEG_INF)
    mask[1] = np.where(r[None, :] <= r[:, None], 0.0, NEG_INF)
    return jnp.asarray(qf, BF16), jnp.asarray(kf, BF16), jnp.asarray(mask, F32)


def _win_attn_kernel(q_ref, kp_ref, kc_ref, kn_ref, vp_ref, vc_ref, vn_ref, km_ref, vm_ref, qf_ref, kf_ref,
                     mask_ref, sink_ref, o_ref, *, nstep):
    c = pl.program_id(1)
    lane = lax.broadcasted_iota(jnp.int32, (1, LANES), 1)
    scale = jnp.asarray(1.0 / math.sqrt(HEAD_DIM), BF16)
    mask_meta = jnp.where(lane < N_META, 0.0, NEG_INF)
    row = lax.broadcasted_iota(jnp.int32, (SW_GROUP * WINDOW, 1), 0)
    ones_blk = jnp.broadcast_to(jnp.where(lane == 0, 1.0, 0.0).astype(BF16), (WIN_KEYS, LANES))
    zeros_blk = jnp.zeros((WINDOW, LANES), BF16)
    gw = SW_GROUP * HEAD_DIM
    own = [slice(j * WINDOW, (j + 1) * WINDOW) for j in range(WIN_QBLOCKS)]
    kblk = [kp_ref] + [kc_ref.at[r] for r in own] + [kn_ref]
    vblk = [vp_ref] + [vc_ref.at[r] for r in own] + [vn_ref]
    for j in range(WIN_QBLOCKS):
        rq = slice(j * WINDOW, (j + 1) * WINDOW)
        edge_p = jnp.where(c == 0, NEG_INF, 0.0) if j == 0 else 0.0
        edge_n = jnp.where(c == nstep - 1, NEG_INF, 0.0) if j == WIN_QBLOCKS - 1 else 0.0
        mask_prev = jnp.tile(mask_ref[0] + edge_p, (SW_GROUP, 1))
        mask_next = jnp.tile(mask_ref[1] + edge_n, (SW_GROUP, 1))
        for g in range(SW_KV_HEADS):
            kv = slice(g * LANES, (g + 1) * LANES)
            rows = []
            for hh in range(SW_GROUP):
                col = g * gw + (hh // 2) * LANES
                qb = q_ref[rq, col:col + LANES] * scale
                keep = (lane < HEAD_DIM) if hh % 2 == 0 else (lane >= HEAD_DIM)
                rows.append(jnp.where(keep, qb, jnp.zeros_like(qb)))
            q_ext = jnp.concatenate([jnp.concatenate(rows, axis=0), qf_ref[g]], axis=1)
            k_all = jnp.concatenate([
                jnp.concatenate([km_ref[:, kv], zeros_blk], axis=1),
                jnp.concatenate([kblk[j][:, kv], kf_ref[0]], axis=1),
                jnp.concatenate([kblk[j + 1][:, kv], kf_ref[1]], axis=1),
                jnp.concatenate([kblk[j + 1][:, kv], kf_ref[2]], axis=1),
                jnp.concatenate([kblk[j + 2][:, kv], kf_ref[3]], axis=1)], axis=0)
            s = lax.dot_general(q_ext, k_all, CONTRACT_LAST, preferred_element_type=F32)
            s_meta = s[:, :LANES] + mask_meta
            s_prev = s[:, LANES:2 * LANES] + mask_prev
            s_cur = jnp.minimum(s[:, 2 * LANES:3 * LANES], s[:, 3 * LANES:4 * LANES])
            s_next = s[:, 4 * LANES:] + mask_next
            sink = jnp.zeros((SW_GROUP * WINDOW, 1), F32)
            for hh in range(SW_GROUP):
                sink = jnp.where(row // WINDOW == hh, sink_ref[0, g * SW_GROUP + hh], sink)
            parts = (s_meta, s_prev, s_cur, s_next)
            m = jnp.maximum(jnp.maximum(s_meta, s_prev), jnp.maximum(s_cur, s_next))
            m = jnp.maximum(jnp.max(m, axis=-1, keepdims=True), sink)
            p = jnp.concatenate([jnp.exp(x - m) for x in parts], axis=1).astype(BF16)
            v_all = jnp.concatenate(
                [jnp.concatenate([vm_ref[:, kv], vblk[j][:, kv], vblk[j + 1][:, kv], vblk[j + 2][:, kv]], axis=0),
                 ones_blk], axis=1)
            acc = jnp.dot(p, v_all, preferred_element_type=F32)
            denom = acc[:, LANES:LANES + 1] + jnp.exp(sink - m)
            o = acc[:, :LANES] / denom
            for lb in range(SW_GROUP // 2):
                even = o[(2 * lb) * WINDOW:(2 * lb + 1) * WINDOW]
                odd = o[(2 * lb + 1) * WINDOW:(2 * lb + 2) * WINDOW]
                col = g * gw + lb * LANES
                o_ref[rq, col:col + LANES] = jnp.where(lane < HEAD_DIM, even, odd).astype(o_ref.dtype)


def _win_attn(proj, proj_meta, sink):
    b, s, _ = proj.shape
    nblk = s // WINDOW
    nq = WIN_QBLOCKS
    nstep = nblk // nq
    qw = SW_HEADS * HEAD_DIM
    kvw = SW_KV_HEADS * LANES
    qf, kf, mask = _win_tables()

    def edge_spec(col, shift):
        return pl.BlockSpec(
            (None, WINDOW, kvw),
            lambda bi, c: (bi, jnp.clip(nq * c + shift, 0, nblk - 1), col * LANES // kvw))

    def pair_spec(col):
        return pl.BlockSpec((None, nq * WINDOW, kvw), lambda bi, c: (bi, c, col * LANES // kvw))

    return pl.pallas_call(
        functools.partial(_win_attn_kernel, nstep=nstep),
        grid=(b, nstep),
        in_specs=[
            pl.BlockSpec((None, nq * WINDOW, qw), lambda bi, c: (bi, c, COL_QS * LANES // qw)),
            edge_spec(COL_KS, -1), pair_spec(COL_KS), edge_spec(COL_KS, nq),
            edge_spec(COL_VS, -1), pair_spec(COL_VS), edge_spec(COL_VS, nq),
            pl.BlockSpec((LANES, kvw), lambda bi, c: (0, COL_KS * LANES // kvw)),
            pl.BlockSpec((LANES, kvw), lambda bi, c: (0, COL_VS * LANES // kvw)),
            pl.BlockSpec((SW_KV_HEADS, SW_GROUP * WINDOW, LANES), lambda bi, c: (0, 0, 0)),
            pl.BlockSpec((4, WINDOW, LANES), lambda bi, c: (0, 0, 0)),
            pl.BlockSpec((2, WINDOW, WINDOW), lambda bi, c: (0, 0, 0)),
            pl.BlockSpec(memory_space=pltpu.SMEM),
        ],
        out_specs=pl.BlockSpec((None, nq * WINDOW, qw), lambda bi, c: (bi, c, 0)),
        out_shape=jax.ShapeDtypeStruct((b, s, qw), BF16),
        compiler_params=_params(("parallel", "arbitrary")),
        name="win_attn",
    )(proj, proj, proj, proj, proj, proj, proj, proj_meta, proj_meta, qf, kf, mask, sink)


ROUTE_E1, ROUTE_E2, ROUTE_POS1, ROUTE_POS2, ROUTE_W1, ROUTE_W2 = range(6)


def _pack_bf16_pairs(x):
    k = x.shape[1] // 2
    bits = lax.bitcast_convert_type(x.astype(BF16).astype(F32), jnp.uint32)
    return lax.bitcast_convert_type(bits[:, :k] | (bits[:, k:] >> 16), jnp.int32)


def _unpack_bf16_pairs(w):
    bits = lax.bitcast_convert_type(w, jnp.uint32)
    hi = lax.bitcast_convert_type(bits & jnp.uint32(0xFFFF0000), F32)
    lo = lax.bitcast_convert_type(bits << 16, F32)
    return jnp.concatenate([hi, lo], axis=1)


def _out_router_kernel(od_ref, os_ref, x_ref, wod_ref, wos_ref, g2_ref, wr_ref, br_ref,
                       h_ref, a_ref, route_ref, cnt_ref, base_ref):
    @pl.when(pl.program_id(0) == 0)
    def _():
        base_ref[...] = jnp.zeros_like(base_ref)

    h = (x_ref[...]
         + jnp.dot(od_ref[...], wod_ref[...], preferred_element_type=F32)
         + jnp.dot(os_ref[...], wos_ref[...], preferred_element_type=F32))
    h_ref[...] = h
    a = h * lax.rsqrt(jnp.mean(h * h, axis=-1, keepdims=True) + EPS) * g2_ref[...]
    a_hi = a.astype(BF16)
    a_ref[...] = _pack_bf16_pairs(a)
    a_lo = (a - a_hi.astype(F32)).astype(BF16)
    hi_terms = jnp.dot(a_hi, wr_ref[...], preferred_element_type=F32)
    logits = (hi_terms[:, :LANES] + hi_terms[:, LANES:]
              + jnp.dot(a_lo, wr_ref[:, :LANES], preferred_element_type=F32)) + br_ref[...]
    tm = logits.shape[0]
    lt = jnp.transpose(logits)
    le = lt[:N_EXPERTS]
    sub_g = lax.broadcasted_iota(jnp.int32, (8, tm), 0)
    lg = jnp.where(sub_g < N_GROUPS, lt[N_EXPERTS:N_EXPERTS + 8], NEG_INF)
    sub_e = lax.broadcasted_iota(jnp.int32, (N_EXPERTS, tm), 0)

    def first_argmax(vals, mask, sub, size):
        mx = jnp.max(vals, axis=0, keepdims=True)
        idx = jnp.min(jnp.where(mask & (vals == mx), sub, size), axis=0, keepdims=True)
        return mx, idx

    gmax, gidx = first_argmax(lg, sub_g < N_GROUPS, sub_g, 8)
    g_w = 1.0 / jnp.sum(jnp.exp(lg - gmax), axis=0, keepdims=True)
    in_group = (sub_e // EXPERTS_PER_GROUP) == gidx
    m1, i1 = first_argmax(jnp.where(in_group, le, NEG_INF), in_group, sub_e, N_EXPERTS)
    rest = in_group & (sub_e != i1)
    m2, i2 = first_argmax(jnp.where(rest, le, NEG_INF), rest, sub_e, N_EXPERTS)
    r = jnp.exp(m2 - m1)
    w1 = g_w / (1.0 + r)
    w2 = g_w * r / (1.0 + r)

    rix = lax.broadcasted_iota(jnp.int32, (tm, tm), 0)
    cix = lax.broadcasted_iota(jnp.int32, (tm, tm), 1)
    earlier = jnp.where(rix < cix, 1.0, 0.0).astype(BF16)
    oh1 = jnp.where(sub_e == i1, 1.0, 0.0)
    oh2 = jnp.where(sub_e == i2, 1.0, 0.0)
    before = jnp.dot(jnp.concatenate([oh1, oh2], axis=0).astype(BF16), earlier, preferred_element_type=F32)
    base1 = base_ref[:, :1]
    pos1 = jnp.sum(oh1 * (base1 + before[:N_EXPERTS]), axis=0, keepdims=True)
    base2 = base1 + jnp.sum(oh1, axis=1, keepdims=True)
    pos2 = jnp.sum(oh2 * (base2 + before[N_EXPERTS:]), axis=0, keepdims=True)
    total = jnp.broadcast_to(base2 + jnp.sum(oh2, axis=1, keepdims=True), base_ref.shape)
    base_ref[...] = total
    cnt_ref[...] = total

    fields = jnp.concatenate([i1.astype(F32), i2.astype(F32), pos1, pos2, w1, w2, jnp.zeros((2, tm), F32)], axis=0)
    route_ref[...] = jnp.transpose(jnp.concatenate([fields, jnp.zeros((LANES - 8, tm), F32)], axis=0))


def _out_router(od, os_, x2d, wod, wos, g2, wr, br, tm):
    n, d = x2d.shape
    half = od.shape[1]
    row = lambda i: (i, 0)
    const = lambda i: (0, 0)
    return pl.pallas_call(
        _out_router_kernel,
        grid=(n // tm,),
        in_specs=[
            pl.BlockSpec((tm, half), row),
            pl.BlockSpec((tm, half), row),
            pl.BlockSpec((tm, d), row),
            pl.BlockSpec((half, d), const),
            pl.BlockSpec((half, d), const),
            pl.BlockSpec((1, d), const),
            pl.BlockSpec((d, 2 * LANES), const),
            pl.BlockSpec((1, LANES), const),
        ],
        out_specs=[
            pl.BlockSpec((tm, d), row),
            pl.BlockSpec((tm, d // 2), row),
            pl.BlockSpec((tm, LANES), row),
            pl.BlockSpec((N_EXPERTS, LANES), const),
        ],
        out_shape=[
            jax.ShapeDtypeStruct((n, d), F32),
            jax.ShapeDtypeStruct((n, d // 2), jnp.int32),
            jax.ShapeDtypeStruct((n, LANES), F32),
            jax.ShapeDtypeStruct((N_EXPERTS, LANES), F32),
        ],
        scratch_shapes=[pltpu.VMEM((N_EXPERTS, LANES), F32)],
        compiler_params=_params(("arbitrary",)),
        name="out_router",
    )(od, os_, x2d, wod, wos, g2, wr, br)


SC_CORES, SC_SUBCORES = 2, 16
SC_WORKERS = SC_CORES * SC_SUBCORES
SC_CHUNK = 128


def _sc_mesh():
    return plsc.VectorSubcoreMesh(core_axis_name="c", subcore_axis_name="s",
                                  num_cores=SC_CORES, num_subcores=SC_SUBCORES)


def _sc_scatter_rows(x, idx1, idx2, n_out):
    n, d = x.shape
    assert n % (SC_WORKERS * SC_CHUNK) == 0
    per_w = n // SC_WORKERS

    @functools.partial(
        pl.kernel, mesh=_sc_mesh(), out_type=jax.ShapeDtypeStruct((n_out, d), x.dtype),
        scratch_types=[pltpu.VMEM((SC_CHUNK,), jnp.int32), pltpu.VMEM((SC_CHUNK,), jnp.int32),
                       pltpu.VMEM((SC_CHUNK, d), x.dtype), pltpu.SemaphoreType.DMA],
        name="sc_scatter_rows")
    def scatter(x_hbm, i1_hbm, i2_hbm, o_hbm, i1_v, i2_v, rows_v, sem):
        wid = lax.axis_index("s") * SC_CORES + lax.axis_index("c")

        @pl.loop(0, per_w // SC_CHUNK)
        def _(j):
            base = wid * per_w + j * SC_CHUNK
            pltpu.sync_copy(i1_hbm.at[pl.ds(base, SC_CHUNK)], i1_v)
            pltpu.sync_copy(i2_hbm.at[pl.ds(base, SC_CHUNK)], i2_v)
            pltpu.sync_copy(x_hbm.at[pl.ds(base, SC_CHUNK)], rows_v)
            pltpu.async_copy(rows_v, o_hbm.at[i1_v], sem).wait()
            pltpu.async_copy(rows_v, o_hbm.at[i2_v], sem).wait()

    return scatter(x, idx1, idx2)


def _sc_gather_rows(table, idx1, idx2):
    n = idx1.shape[0]
    d = table.shape[1]
    assert n % (SC_WORKERS * SC_CHUNK) == 0
    per_w = n // SC_WORKERS
    out = jax.ShapeDtypeStruct((n, d), table.dtype)

    @functools.partial(
        pl.kernel, mesh=_sc_mesh(), out_type=(out, out),
        scratch_types=[pltpu.VMEM((SC_CHUNK,), jnp.int32), pltpu.VMEM((SC_CHUNK, d), table.dtype),
                       pltpu.SemaphoreType.DMA],
        name="sc_gather_rows")
    def gather(t_hbm, i1_hbm, i2_hbm, o1_hbm, o2_hbm, i_v, rows_v, sem):
        wid = lax.axis_index("s") * SC_CORES + lax.axis_index("c")

        @pl.loop(0, per_w // SC_CHUNK)
        def _(j):
            base = wid * per_w + j * SC_CHUNK
            for i_hbm, o_hbm in ((i1_hbm, o1_hbm), (i2_hbm, o2_hbm)):
                pltpu.sync_copy(i_hbm.at[pl.ds(base, SC_CHUNK)], i_v)
                pltpu.async_copy(t_hbm.at[i_v], rows_v, sem).wait()
                pltpu.sync_copy(rows_v, o_hbm.at[pl.ds(base, SC_CHUNK)])

    return gather(table, idx1, idx2)


def _dest_kernel(route_ref, starts_ref, d1_ref, d2_ref):
    tm = route_ref.shape[0]
    route = route_ref[...]
    lane = lax.broadcasted_iota(jnp.int32, (tm, LANES), 1)
    row = lax.broadcasted_iota(jnp.int32, (tm, LANES), 0)

    def field(n):
        return jnp.sum(jnp.where(lane == n, route, 0.0), axis=-1, keepdims=True)

    for e_lane, pos_lane, out_ref in ((ROUTE_E1, ROUTE_POS1, d1_ref), (ROUTE_E2, ROUTE_POS2, d2_ref)):
        expert = field(e_lane).astype(jnp.int32)
        start = jnp.sum(jnp.where(lane == expert, starts_ref[...], 0.0), axis=-1, keepdims=True)
        dest = start + field(pos_lane)
        spread = jnp.where(lane == row % LANES, dest, 0.0)
        out_ref[...] = jnp.sum(spread.reshape(tm // LANES, LANES, LANES), axis=1).astype(jnp.int32)


def _routing_tables(route, cnt, tm, n_tiles_max):
    n = route.shape[0]
    counts = cnt[:, 0].astype(jnp.int32)
    padded = (counts + tm - 1) // tm * tm
    ends = jnp.cumsum(padded)
    starts = jnp.pad((ends - padded).astype(F32), (0, LANES - N_EXPERTS)).reshape(1, LANES)
    tr = min(1024, n)
    lane_dense = jax.ShapeDtypeStruct((n // LANES, LANES), jnp.int32)
    dest1, dest2 = pl.pallas_call(
        _dest_kernel,
        grid=(n // tr,),
        in_specs=[pl.BlockSpec((tr, LANES), lambda i: (i, 0)), pl.BlockSpec((1, LANES), lambda i: (0, 0))],
        out_specs=[pl.BlockSpec((tr // LANES, LANES), lambda i: (i, 0))] * 2,
        out_shape=[lane_dense, lane_dense],
        compiler_params=_params(("parallel",)),
        name="route_dest",
    )(route, starts)
    tile_start = jnp.arange(n_tiles_max, dtype=jnp.int32) * tm
    tile_expert = jnp.sum((ends[None, :] <= tile_start[:, None]).astype(jnp.int32), axis=1)
    tile_expert = jnp.minimum(tile_expert, N_EXPERTS - 1)
    n_tiles = (ends[-1] // tm).astype(jnp.int32).reshape(1)
    valid = tile_start < ends[-1]
    prev_expert = jnp.concatenate([jnp.full((1,), -1, jnp.int32), tile_expert[:-1]])
    first = (valid & (tile_expert != prev_expert)).astype(jnp.int32)
    slot = (jnp.cumsum(first) - 1) % 2
    e_idx = jnp.arange(N_EXPERTS, dtype=jnp.int32)
    later = (padded > 0)[None, :] & (e_idx[None, :] > e_idx[:, None])
    next_run = jnp.min(jnp.where(later, e_idx[None, :], N_EXPERTS), axis=1)
    next_run = jnp.where(next_run == N_EXPERTS, -1, next_run).astype(jnp.int32)
    plan = (tile_expert, first, slot.astype(jnp.int32), next_run[tile_expert], n_tiles)
    return dest1.reshape(n), dest2.reshape(n), plan


def _moe_tiles_kernel(te_ref, first_ref, slot_ref, nxt_ref, nt_ref, xs_ref, wg_hbm, wu_hbm, wd_hbm, ys_ref,
                      wg_buf, wu_buf, wd_buf, sem):
    t = pl.program_id(0)

    def weight_copies(expert, slot):
        return (pltpu.make_async_copy(wg_hbm.at[expert], wg_buf.at[slot], sem.at[slot, 0]),
                pltpu.make_async_copy(wu_hbm.at[expert], wu_buf.at[slot], sem.at[slot, 1]),
                pltpu.make_async_copy(wd_hbm.at[expert], wd_buf.at[slot], sem.at[slot, 2]))

    @pl.when(t < nt_ref[0])
    def _():
        slot = slot_ref[t]

        @pl.when(t == 0)
        def _():
            for c in weight_copies(te_ref[0], 0):
                c.start()

        @pl.when(first_ref[t] == 1)
        def _():
            for c in weight_copies(te_ref[t], slot):
                c.wait()

            @pl.when(nxt_ref[t] >= 0)
            def _():
                for c in weight_copies(nxt_ref[t], 1 - slot):
                    c.start()

        x = _unpack_bf16_pairs(xs_ref[...]).astype(BF16)
        hg = jnp.dot(x, wg_buf[slot], preferred_element_type=F32)
        hu = jnp.dot(x, wu_buf[slot], preferred_element_type=F32)
        hid = (hg * jax.nn.sigmoid(hg) * hu).astype(BF16)
        ys_ref[...] = _pack_bf16_pairs(jnp.dot(hid, wd_buf[slot], preferred_element_type=F32))


def _moe_tiles(xs, plan, wg, wu, wd, tm):
    r, dh = xs.shape
    ne, d, de = wg.shape
    row = lambda t, *_: (t, 0)
    hbm = pl.BlockSpec(memory_space=pl.ANY)
    return pl.pallas_call(
        _moe_tiles_kernel,
        grid_spec=pltpu.PrefetchScalarGridSpec(
            num_scalar_prefetch=len(plan),
            grid=(r // tm,),
            in_specs=[pl.BlockSpec((tm, dh), row), hbm, hbm, hbm],
            out_specs=pl.BlockSpec((tm, dh), row),
            scratch_shapes=[
                pltpu.VMEM((2, d, de), wg.dtype),
                pltpu.VMEM((2, d, de), wu.dtype),
                pltpu.VMEM((2, de, d), wd.dtype),
                pltpu.SemaphoreType.DMA((2, 3)),
            ],
        ),
        out_shape=jax.ShapeDtypeStruct((r, dh), jnp.int32),
        compiler_params=_params(("arbitrary",)),
        name="moe_tiles",
    )(*plan, xs, wg, wu, wd)


def _combine_kernel(h_ref, y1_ref, y2_ref, route_ref, fg_ref, o_ref):
    lane = lax.broadcasted_iota(jnp.int32, route_ref.shape, 1)
    route = route_ref[...]
    w1 = jnp.sum(jnp.where(lane == ROUTE_W1, route, 0.0), axis=-1, keepdims=True)
    w2 = jnp.sum(jnp.where(lane == ROUTE_W2, route, 0.0), axis=-1, keepdims=True)
    y = h_ref[...] + w1 * _unpack_bf16_pairs(y1_ref[...]) + w2 * _unpack_bf16_pairs(y2_ref[...])
    o_ref[...] = y * lax.rsqrt(jnp.mean(y * y, axis=-1, keepdims=True) + EPS) * fg_ref[...]


def _combine(h, y1, y2, route, fg, tm):
    n, d = h.shape
    row = lambda i: (i, 0)
    return pl.pallas_call(
        _combine_kernel,
        grid=(n // tm,),
        in_specs=[
            pl.BlockSpec((tm, d), row),
            pl.BlockSpec((tm, d // 2), row),
            pl.BlockSpec((tm, d // 2), row),
            pl.BlockSpec((tm, LANES), row),
            pl.BlockSpec((1, d), lambda i: (0, 0)),
        ],
        out_specs=pl.BlockSpec((tm, d), row),
        out_shape=jax.ShapeDtypeStruct((n, d), F32),
        compiler_params=_params(("parallel",)),
        name="moe_combine",
    )(h, y1, y2, route, fg)


def _moe(a2p, route, cnt, h, wg, wu, wd, fg, tm):
    n = h.shape[0]
    n_tiles_max = (2 * n) // tm + N_EXPERTS
    dest1, dest2, plan = _routing_tables(route, cnt, tm, n_tiles_max)
    xs = _sc_scatter_rows(a2p, dest1, dest2, n_tiles_max * tm)
    ys = _moe_tiles(xs, plan, wg, wu, wd, tm)
    y1, y2 = _sc_gather_rows(ys, dest1, dest2)
    return _combine(h, y1, y2, route, fg, tm=min(512, n))


def _dup_heads(w, n_heads):
    d = w.shape[0]
    w = w.reshape(d, n_heads, 1, HEAD_DIM)
    return jnp.broadcast_to(w, (d, n_heads, 2, HEAD_DIM)).reshape(d, n_heads * 2 * HEAD_DIM)


def _encoder(x, proj_meta, wts):
    b, s, d = x.shape
    x2d = x.reshape(b * s, d)
    proj = _norm_proj(x2d, wts["g1"], wts["w_ext"], tm=512).reshape(b, s, -1)
    od = _diff_attn(proj, proj_meta, wts["lam_vecs"], wts["subln_g"], bq=256, bk=min(1024, s))
    os_ = _win_attn(proj, proj_meta, wts["sink"])
    h, a2p, route, cnt = _out_router(od.reshape(b * s, -1), os_.reshape(b * s, -1), x2d, wts["wod"], wts["wos"],
                                     wts["g2"], wts["wr"], wts["br"], tm=512)
    y = _moe(a2p, route, cnt, h, wts["wg"], wts["wu"], wts["wd"], wts["fg"], tm=512)
    return y.reshape(b, s, d)


def kernel(x_prompt, x_sample, meta, norm1_g, w_in, lam_q1, lam_k1, lam_q2, lam_k2, subln_g, sink, w_out,
           norm2_g, w_gr, b_gr, w_er, b_er, w_gate, w_up, w_down, final_g):
    d = x_prompt.shape[-1]
    w = w_in[0]
    c_kd = 2 * DA_HEADS * HEAD_DIM
    c_vd = 2 * c_kd
    c_qs = c_vd + DA_HEADS * 2 * HEAD_DIM
    c_ks = c_qs + SW_HEADS * HEAD_DIM
    c_vs = c_ks + SW_KV_HEADS * HEAD_DIM
    w_ext = jnp.concatenate(
        [w[:, :c_ks], _dup_heads(w[:, c_ks:c_vs], SW_KV_HEADS), _dup_heads(w[:, c_vs:], SW_KV_HEADS)],
        axis=1).astype(BF16)
    w_router = jnp.concatenate([w_er[0], w_gr[0]], axis=1)
    w_router = jnp.pad(w_router, ((0, 0), (0, LANES - w_router.shape[1])))
    wr_hi = w_router.astype(BF16)
    wr_lo = (w_router - wr_hi.astype(F32)).astype(BF16)
    br = jnp.pad(jnp.concatenate([b_er[0], b_gr[0]]), (0, LANES - N_EXPERTS - N_GROUPS)).reshape(1, LANES)
    wo = w_out[0].astype(BF16)
    half = DA_HEADS * 2 * HEAD_DIM
    wts = dict(
        g1=norm1_g[0].reshape(1, d), w_ext=w_ext,
        lam_vecs=(lam_q1[0].reshape(1, -1), lam_k1[0].reshape(1, -1),
                  lam_q2[0].reshape(1, -1), lam_k2[0].reshape(1, -1)),
        subln_g=subln_g[0].reshape(1, -1), sink=sink[0].reshape(1, -1),
        wod=wo[:half], wos=wo[half:], g2=norm2_g[0].reshape(1, d),
        wr=jnp.concatenate([wr_hi, wr_lo], axis=1), br=br,
        wg=w_gate[0].astype(BF16), wu=w_up[0].astype(BF16), wd=w_down[0].astype(BF16),
        fg=final_g.reshape(1, d),
    )
    proj_meta = _norm_proj(meta, wts["g1"], w_ext, tm=N_META)
    proj_meta = jnp.pad(proj_meta, ((0, LANES - N_META), (0, 0)))
    return _encoder(x_prompt, proj_meta, wts), _encoder(x_sample, proj_meta, wts)
```

```python
import functools
import math

import jax
import jax.numpy as jnp
import numpy as np
from jax import lax
from jax.experimental import pallas as pl
from jax.experimental.pallas import tpu as pltpu
from jax.experimental.pallas import tpu_sc as plsc

F32 = jnp.float32
BF16 = jnp.bfloat16

N_META = 16
HEAD_DIM = 64
DA_HEADS = 4
SW_HEADS = 8
SW_KV_HEADS = 2
SW_GROUP = SW_HEADS // SW_KV_HEADS
WINDOW = 128
N_GROUPS = 4
EXPERTS_PER_GROUP = 8
N_EXPERTS = N_GROUPS * EXPERTS_PER_GROUP
EPS = 1e-6
SUBLN_EPS = 1e-5
NEG_INF = -1e30
LAM_INIT = 0.8 - 0.6 * math.exp(-0.3 * 0)
LANES = 128
VMEM_LIMIT = 48 * 1024 * 1024

COL_QD, COL_KD, COL_VD, COL_QS, COL_KS, COL_VS, N_COLBLK = 0, 4, 8, 12, 16, 18, 20
CONTRACT_LAST = (((1,), (1,)), ((), ()))


def _params(sem):
    return pltpu.CompilerParams(dimension_semantics=sem, vmem_limit_bytes=VMEM_LIMIT)


def _norm_proj_kernel(x_ref, g_ref, w_ref, o_ref):
    x = x_ref[...]
    ms = jnp.mean(x * x, axis=-1, keepdims=True)
    y = (x * lax.rsqrt(ms + EPS) * g_ref[...]).astype(BF16)
    o_ref[...] = jnp.dot(y, w_ref[...], preferred_element_type=F32).astype(o_ref.dtype)


def _norm_proj(x2d, g, w_ext, tm):
    n, d = x2d.shape
    wcols = w_ext.shape[1]
    return pl.pallas_call(
        _norm_proj_kernel,
        grid=(n // tm,),
        in_specs=[
            pl.BlockSpec((tm, d), lambda i: (i, 0)),
            pl.BlockSpec((1, d), lambda i: (0, 0)),
            pl.BlockSpec((d, wcols), lambda i: (0, 0)),
        ],
        out_specs=pl.BlockSpec((tm, wcols), lambda i: (i, 0)),
        out_shape=jax.ShapeDtypeStruct((n, wcols), BF16),
        compiler_params=_params(("parallel",)),
        name="norm_proj",
    )(x2d, g, w_ext)


POS_SPLIT = 64
FEAT_ONE_A, FEAT_ONE_B, FEAT_HI, FEAT_LO = 0, 1, 2, 3


def _key_pos_features(seq):
    j = jnp.arange(seq, dtype=jnp.int32)[:, None]
    lane = jnp.arange(LANES, dtype=jnp.int32)[None, :]
    hi = (j // POS_SPLIT * POS_SPLIT).astype(F32)
    lo = (j % POS_SPLIT).astype(F32)
    feat = jnp.where(lane <= FEAT_ONE_B, 1.0, jnp.where(lane == FEAT_HI, hi, jnp.where(lane == FEAT_LO, lo, 0.0)))
    return feat.astype(BF16)


def _diff_attn_head(h, q_ref, k_ref, v_ref, kf_ref, km_ref, vm_ref, lam, sg_ref,
                    o_ref, qe_ref, acc_ref, m_ref, sd_ref, s1_ref, *, bq, bk, seq):
    i = pl.program_id(2)
    scale = 1.0 / math.sqrt(HEAD_DIM)
    slope = jnp.exp2(jnp.full((1, 1), -8.0 / DA_HEADS, F32) * (h + 1).astype(F32))
    lane = lax.broadcasted_iota(jnp.int32, (1, LANES), 1)
    q = q_ref[...] * jnp.asarray(scale, BF16)
    zero = jnp.zeros_like(q)
    qmaps = (jnp.where(lane < HEAD_DIM, q, zero), jnp.where(lane >= HEAD_DIM, q, zero))
    qpos = i * bq + lax.broadcasted_iota(jnp.int32, (bq, 1), 0)
    q_hi = (qpos // POS_SPLIT * POS_SPLIT).astype(F32)
    q_lo = (qpos % POS_SPLIT).astype(F32)
    feat = jnp.where(lane == FEAT_ONE_A, -slope * q_hi,
                     jnp.where(lane == FEAT_ONE_B, -slope * q_lo, jnp.where(lane <= FEAT_LO, slope, 0.0)))
    feats = (feat.astype(BF16), (-feat).astype(BF16))
    for side in range(2):
        for c in range(2):
            qe_ref[side, c * bq:(c + 1) * bq, :LANES] = qmaps[c]
            qe_ref[side, c * bq:(c + 1) * bq, LANES:] = feats[side]

    ones_blk = jnp.broadcast_to(jnp.where(lane == 0, 1.0, 0.0).astype(BF16), (bk, LANES))

    def attend(s, v_ext):
        m_old = m_ref[...]
        m_new = jnp.maximum(m_old, jnp.max(s, axis=-1, keepdims=True))
        alpha = jnp.exp(m_old - m_new)
        p = jnp.exp(s - jnp.tile(m_new, (1, s.shape[1] // LANES))).astype(BF16)
        acc_ref[...] = jnp.tile(alpha, (1, 2)) * acc_ref[...] + jnp.dot(p, v_ext, preferred_element_type=F32)
        m_ref[...] = m_new

    m_ref[...] = jnp.full(m_ref.shape, NEG_INF, F32)
    acc_ref[...] = jnp.zeros(acc_ref.shape, F32)

    s0_ref = sd_ref.at[:, :bk]
    nblk = seq // bk
    diag = (i * bq) // bk

    def key_block(pos):
        t = pos - 1
        return t + jnp.where(t >= diag, 1, 0)

    def scores_into(dst_ref, pos):
        kb = key_block(pos)
        side = jnp.where(kb > diag, 1, 0)
        start = pl.multiple_of(kb * bk, bk)
        k_ext = jnp.concatenate([k_ref[pl.ds(start, bk), :], kf_ref[pl.ds(start, bk), :]], axis=1)
        dst_ref[...] = lax.dot_general(qe_ref[side], k_ext, CONTRACT_LAST, preferred_element_type=F32)

    def consume(src_ref, pos):
        start = pl.multiple_of(key_block(pos) * bk, bk)
        attend(src_ref[...], jnp.concatenate([v_ref[pl.ds(start, bk), :], ones_blk], axis=1))

    d_start = pl.multiple_of(diag * bk, bk)
    kpos = diag * bk + lax.broadcasted_iota(jnp.int32, (1, bk), 1)
    bias = jnp.concatenate([-slope * jnp.abs(qpos - kpos).astype(F32),
                            jnp.broadcast_to(jnp.where(lane < N_META, 0.0, NEG_INF), (bq, LANES))], axis=1)
    k_first = jnp.concatenate([k_ref[pl.ds(d_start, bk), :], km_ref[...]], axis=0)
    v_first = jnp.concatenate(
        [jnp.concatenate([v_ref[pl.ds(d_start, bk), :], vm_ref[...]], axis=0),
         jnp.broadcast_to(ones_blk[:1], (bk + LANES, LANES))], axis=1)
    sd_ref[...] = lax.dot_general(qe_ref[0, :, :LANES], k_first, CONTRACT_LAST,
                                  preferred_element_type=F32) + jnp.tile(bias, (2, 1))
    if nblk > 1:
        scores_into(s1_ref, 1)
    attend(sd_ref[...], v_first)
    if nblk > 1:
        for u in range((nblk - 2) // 2):
            scores_into(s0_ref, 2 * u + 2)
            consume(s1_ref, 2 * u + 1)
            scores_into(s1_ref, 2 * u + 3)
            consume(s0_ref, 2 * u + 2)
        consume(s1_ref, nblk - 1)

    o1 = acc_ref[:bq, :LANES] / acc_ref[:bq, LANES:LANES + 1]
    o2 = acc_ref[bq:, :LANES] / acc_ref[bq:, LANES:LANES + 1]
    o = o1 - lam * o2
    o = o * lax.rsqrt(jnp.mean(o * o, axis=-1, keepdims=True) + SUBLN_EPS) * sg_ref[...]
    o_ref[...] = (o * (1.0 - LAM_INIT)).astype(o_ref.dtype)


HEADS_PER_STEP = 4


def _diff_attn_kernel(q_ref, k_ref, v_ref, kf_ref, km_ref, vm_ref, lq1_ref, lk1_ref, lq2_ref, lk2_ref, sg_ref,
                      o_ref, qe_ref, acc_ref, m_ref, sd_ref, s1_ref, **kw):
    lam = (jnp.exp(jnp.sum(lq1_ref[...] * lk1_ref[...], axis=-1, keepdims=True))
           - jnp.exp(jnp.sum(lq2_ref[...] * lk2_ref[...], axis=-1, keepdims=True)) + LAM_INIT)
    for hh in range(HEADS_PER_STEP):
        cols = slice(hh * LANES, (hh + 1) * LANES)
        _diff_attn_head(pl.program_id(1) * HEADS_PER_STEP + hh,
                        q_ref.at[:, cols], k_ref.at[:, cols], v_ref.at[:, cols], kf_ref,
                        km_ref.at[:, cols], vm_ref.at[:, cols], lam, sg_ref, o_ref.at[:, cols],
                        qe_ref.at[hh], acc_ref.at[hh], m_ref.at[hh], sd_ref.at[hh], s1_ref.at[hh], **kw)


def _diff_attn(proj, proj_meta, lam_vecs, subln_g, bq, bk):
    b, s, _ = proj.shape
    assert bk % bq == 0 and s % bk == 0 and (s // bk == 1 or (s // bk) % 2 == 0)
    hp = HEADS_PER_STEP
    w = hp * LANES
    vec = pl.BlockSpec((1, HEAD_DIM), lambda bi, h, i: (0, 0))
    return pl.pallas_call(
        functools.partial(_diff_attn_kernel, bq=bq, bk=bk, seq=s),
        grid=(b, DA_HEADS // hp, s // bq),
        in_specs=[
            pl.BlockSpec((None, bq, w), lambda bi, h, i: (bi, i, COL_QD // hp + h)),
            pl.BlockSpec((None, s, w), lambda bi, h, i: (bi, 0, COL_KD // hp + h)),
            pl.BlockSpec((None, s, w), lambda bi, h, i: (bi, 0, COL_VD // hp + h)),
            pl.BlockSpec((s, LANES), lambda bi, h, i: (0, 0)),
            pl.BlockSpec((LANES, w), lambda bi, h, i: (0, COL_KD // hp + h)),
            pl.BlockSpec((LANES, w), lambda bi, h, i: (0, COL_VD // hp + h)),
            vec, vec, vec, vec,
            pl.BlockSpec((1, LANES), lambda bi, h, i: (0, 0)),
        ],
        out_specs=pl.BlockSpec((None, bq, w), lambda bi, h, i: (bi, i, h)),
        out_shape=jax.ShapeDtypeStruct((b, s, DA_HEADS * LANES), BF16),
        scratch_shapes=[
            pltpu.VMEM((hp, 2, 2 * bq, 2 * LANES), BF16),
            pltpu.VMEM((hp, 2 * bq, 2 * LANES), F32),
            pltpu.VMEM((hp, 2 * bq, LANES), F32),
            pltpu.VMEM((hp, 2 * bq, bk + LANES), F32),
            pltpu.VMEM((hp, 2 * bq, bk), F32),
        ],
        compiler_params=_params(("parallel", "parallel", "arbitrary")),
        name="diff_attn",
    )(proj, proj, proj, _key_pos_features(s), proj_meta, proj_meta, *lam_vecs, subln_g)


WIN_KEYS = 4 * WINDOW
WIN_QBLOCKS = 4


def _win_tables():
    r = np.arange(WINDOW)
    qf = np.zeros((SW_KV_HEADS, SW_GROUP * WINDOW, LANES), np.float32)
    for head in range(SW_HEADS):
        slope = 2.0 ** (-8.0 * (head + 1) / SW_HEADS)
        i_rel = WINDOW + r
        hi, lo = i_rel // POS_SPLIT * POS_SPLIT, i_rel % POS_SPLIT
        rows = qf[head // SW_GROUP, (head % SW_GROUP) * WINDOW:(head % SW_GROUP + 1) * WINDOW]
        rows[:, 0], rows[:, 1], rows[:, 2], rows[:, 3] = -slope * hi, -slope * lo, slope, slope
        rows[:, 4:8] = -rows[:, 0:4]
    kf = np.zeros((2, WINDOW, LANES), np.float32)
    for n, (blk, right) in enumerate(((0, 0), (2, 1))):
        j_rel = blk * WINDOW + r
        o = 4 * right
        kf[n, :, o], kf[n, :, o + 1] = 1.0, 1.0
        kf[n, :, o + 2], kf[n, :, o + 3] = j_rel // POS_SPLIT * POS_SPLIT, j_rel % POS_SPLIT
    mask = np.zeros((2, WINDOW, WINDOW), np.float32)
    mask[0] = np.where(r[None, :] >= r[:, None], 0.0, NEG_INF)
    mask[1] = np.where(r[None, :] <= r[:, None], 0.0, NEG_INF)
    cur = np.zeros((SW_KV_HEADS, SW_GROUP * WINDOW, WINDOW), np.float32)
    for head in range(SW_HEADS):
        slope = 2.0 ** (-8.0 * (head + 1) / SW_HEADS)
        cur[head // SW_GROUP, (head % SW_GROUP) * WINDOW:(head % SW_GROUP + 1) * WINDOW] = (
            -slope * np.abs(r[:, None] - r[None, :]))
    return jnp.asarray(qf, BF16), jnp.asarray(kf, BF16), jnp.asarray(mask, F32), jnp.asarray(cur, F32)


def _win_attn_kernel(q_ref, kp_ref, kc_ref, kn_ref, vp_ref, vc_ref, vn_ref, km_ref, vm_ref, qf_ref, kf_ref,
                     mask_ref, cur_ref, sink_ref, o_ref, *, nstep):
    c = pl.program_id(1)
    lane = lax.broadcasted_iota(jnp.int32, (1, LANES), 1)
    scale = jnp.asarray(1.0 / math.sqrt(HEAD_DIM), BF16)
    mask_meta = jnp.where(lane < N_META, 0.0, NEG_INF)
    row = lax.broadcasted_iota(jnp.int32, (SW_GROUP * WINDOW, 1), 0)
    ones_blk = jnp.broadcast_to(jnp.where(lane == 0, 1.0, 0.0).astype(BF16), (WIN_KEYS, LANES))
    zeros_blk = jnp.zeros((WINDOW, LANES), BF16)
    gw = SW_GROUP * HEAD_DIM
    own = [slice(j * WINDOW, (j + 1) * WINDOW) for j in range(WIN_QBLOCKS)]
    kblk = [kp_ref] + [kc_ref.at[r] for r in own] + [kn_ref]
    vblk = [vp_ref] + [vc_ref.at[r] for r in own] + [vn_ref]
    for j in range(WIN_QBLOCKS):
        rq = slice(j * WINDOW, (j + 1) * WINDOW)
        edge_p = jnp.where(c == 0, NEG_INF, 0.0) if j == 0 else 0.0
        edge_n = jnp.where(c == nstep - 1, NEG_INF, 0.0) if j == WIN_QBLOCKS - 1 else 0.0
        mask_prev = jnp.tile(mask_ref[0] + edge_p, (SW_GROUP, 1))
        mask_next = jnp.tile(mask_ref[1] + edge_n, (SW_GROUP, 1))
        for g in range(SW_KV_HEADS):
            kv = slice(g * LANES, (g + 1) * LANES)
            rows = []
            for hh in range(SW_GROUP):
                col = g * gw + (hh // 2) * LANES
                qb = q_ref[rq, col:col + LANES] * scale
                keep = (lane < HEAD_DIM) if hh % 2 == 0 else (lane >= HEAD_DIM)
                rows.append(jnp.where(keep, qb, jnp.zeros_like(qb)))
            q_ext = jnp.concatenate([jnp.concatenate(rows, axis=0), qf_ref[g]], axis=1)
            k_all = jnp.concatenate([
                jnp.concatenate([km_ref[:, kv], zeros_blk], axis=1),
                jnp.concatenate([kblk[j][:, kv], kf_ref[0]], axis=1),
                jnp.concatenate([kblk[j + 1][:, kv], zeros_blk], axis=1),
                jnp.concatenate([kblk[j + 2][:, kv], kf_ref[1]], axis=1)], axis=0)
            s = lax.dot_general(q_ext, k_all, CONTRACT_LAST, preferred_element_type=F32)
            s_meta = s[:, :LANES] + mask_meta
            s_prev = s[:, LANES:2 * LANES] + mask_prev
            s_cur = s[:, 2 * LANES:3 * LANES] + cur_ref[g]
            s_next = s[:, 3 * LANES:] + mask_next
            sink = jnp.zeros((SW_GROUP * WINDOW, 1), F32)
            for hh in range(SW_GROUP):
                sink = jnp.where(row // WINDOW == hh, sink_ref[0, g * SW_GROUP + hh], sink)
            parts = (s_meta, s_prev, s_cur, s_next)
            m = jnp.maximum(jnp.maximum(s_meta, s_prev), jnp.maximum(s_cur, s_next))
            m = jnp.maximum(jnp.max(m, axis=-1, keepdims=True), sink)
            p = jnp.concatenate([jnp.exp(x - m) for x in parts], axis=1).astype(BF16)
            v_all = jnp.concatenate(
                [jnp.concatenate([vm_ref[:, kv], vblk[j][:, kv], vblk[j + 1][:, kv], vblk[j + 2][:, kv]], axis=0),
                 ones_blk], axis=1)
            acc = jnp.dot(p, v_all, preferred_element_type=F32)
            denom = acc[:, LANES:LANES + 1] + jnp.exp(sink - m)
            o = acc[:, :LANES] / denom
            for lb in range(SW_GROUP // 2):
                even = o[(2 * lb) * WINDOW:(2 * lb + 1) * WINDOW]
                odd = o[(2 * lb + 1) * WINDOW:(2 * lb + 2) * WINDOW]
                col = g * gw + lb * LANES
                o_ref[rq, col:col + LANES] = jnp.where(lane < HEAD_DIM, even, odd).astype(o_ref.dtype)


def _win_attn(proj, proj_meta, sink):
    b, s, _ = proj.shape
    nblk = s // WINDOW
    nq = WIN_QBLOCKS
    nstep = nblk // nq
    qw = SW_HEADS * HEAD_DIM
    kvw = SW_KV_HEADS * LANES
    qf, kf, mask, cur = _win_tables()

    def edge_spec(col, shift):
        return pl.BlockSpec(
            (None, WINDOW, kvw),
            lambda bi, c: (bi, jnp.clip(nq * c + shift, 0, nblk - 1), col * LANES // kvw))

    def pair_spec(col):
        return pl.BlockSpec((None, nq * WINDOW, kvw), lambda bi, c: (bi, c, col * LANES // kvw))

    return pl.pallas_call(
        functools.partial(_win_attn_kernel, nstep=nstep),
        grid=(b, nstep),
        in_specs=[
            pl.BlockSpec((None, nq * WINDOW, qw), lambda bi, c: (bi, c, COL_QS * LANES // qw)),
            edge_spec(COL_KS, -1), pair_spec(COL_KS), edge_spec(COL_KS, nq),
            edge_spec(COL_VS, -1), pair_spec(COL_VS), edge_spec(COL_VS, nq),
            pl.BlockSpec((LANES, kvw), lambda bi, c: (0, COL_KS * LANES // kvw)),
            pl.BlockSpec((LANES, kvw), lambda bi, c: (0, COL_VS * LANES // kvw)),
            pl.BlockSpec((SW_KV_HEADS, SW_GROUP * WINDOW, LANES), lambda bi, c: (0, 0, 0)),
            pl.BlockSpec((2, WINDOW, LANES), lambda bi, c: (0, 0, 0)),
            pl.BlockSpec((2, WINDOW, WINDOW), lambda bi, c: (0, 0, 0)),
            pl.BlockSpec((SW_KV_HEADS, SW_GROUP * WINDOW, WINDOW), lambda bi, c: (0, 0, 0)),
            pl.BlockSpec(memory_space=pltpu.SMEM),
        ],
        out_specs=pl.BlockSpec((None, nq * WINDOW, qw), lambda bi, c: (bi, c, 0)),
        out_shape=jax.ShapeDtypeStruct((b, s, qw), BF16),
        compiler_params=_params(("parallel", "arbitrary")),
        name="win_attn",
    )(proj, proj, proj, proj, proj, proj, proj, proj_meta, proj_meta, qf, kf, mask, cur, sink)


ROUTE_E1, ROUTE_E2, ROUTE_POS1, ROUTE_POS2, ROUTE_W1, ROUTE_W2 = range(6)


def _pack_bf16_pairs(x):
    k = x.shape[1] // 2
    bits = lax.bitcast_convert_type(x.astype(BF16).astype(F32), jnp.uint32)
    return lax.bitcast_convert_type(bits[:, :k] | (bits[:, k:] >> 16), jnp.int32)


def _unpack_bf16_pairs(w):
    bits = lax.bitcast_convert_type(w, jnp.uint32)
    hi = lax.bitcast_convert_type(bits & jnp.uint32(0xFFFF0000), F32)
    lo = lax.bitcast_convert_type(bits << 16, F32)
    return jnp.concatenate([hi, lo], axis=1)


def _out_router_kernel(od_ref, os_ref, x_ref, wod_ref, wos_ref, g2_ref, wr_ref, br_ref,
                       h_ref, a_ref, route_ref, cnt_ref, base_ref):
    @pl.when(pl.program_id(0) == 0)
    def _():
        base_ref[...] = jnp.zeros_like(base_ref)

    h = (x_ref[...]
         + jnp.dot(od_ref[...], wod_ref[...], preferred_element_type=F32)
         + jnp.dot(os_ref[...], wos_ref[...], preferred_element_type=F32))
    h_ref[...] = h
    a = h * lax.rsqrt(jnp.mean(h * h, axis=-1, keepdims=True) + EPS) * g2_ref[...]
    a_hi = a.astype(BF16)
    a_ref[...] = _pack_bf16_pairs(a)
    a_lo = (a - a_hi.astype(F32)).astype(BF16)
    hi_terms = jnp.dot(a_hi, wr_ref[...], preferred_element_type=F32)
    logits = (hi_terms[:, :LANES] + hi_terms[:, LANES:]
              + jnp.dot(a_lo, wr_ref[:, :LANES], preferred_element_type=F32)) + br_ref[...]
    tm = logits.shape[0]
    lt = jnp.transpose(logits)
    le = lt[:N_EXPERTS]
    sub_g = lax.broadcasted_iota(jnp.int32, (8, tm), 0)
    lg = jnp.where(sub_g < N_GROUPS, lt[N_EXPERTS:N_EXPERTS + 8], NEG_INF)
    sub_e = lax.broadcasted_iota(jnp.int32, (N_EXPERTS, tm), 0)

    def first_argmax(vals, mask, sub, size):
        mx = jnp.max(vals, axis=0, keepdims=True)
        idx = jnp.min(jnp.where(mask & (vals == mx), sub, size), axis=0, keepdims=True)
        return mx, idx

    gmax, gidx = first_argmax(lg, sub_g < N_GROUPS, sub_g, 8)
    g_w = 1.0 / jnp.sum(jnp.exp(lg - gmax), axis=0, keepdims=True)
    in_group = (sub_e // EXPERTS_PER_GROUP) == gidx
    m1, i1 = first_argmax(jnp.where(in_group, le, NEG_INF), in_group, sub_e, N_EXPERTS)
    rest = in_group & (sub_e != i1)
    m2, i2 = first_argmax(jnp.where(rest, le, NEG_INF), rest, sub_e, N_EXPERTS)
    r = jnp.exp(m2 - m1)
    w1 = g_w / (1.0 + r)
    w2 = g_w * r / (1.0 + r)

    rix = lax.broadcasted_iota(jnp.int32, (tm, tm), 0)
    cix = lax.broadcasted_iota(jnp.int32, (tm, tm), 1)
    earlier = jnp.where(rix < cix, 1.0, 0.0).astype(BF16)
    oh1 = jnp.where(sub_e == i1, 1.0, 0.0)
    oh2 = jnp.where(sub_e == i2, 1.0, 0.0)
    before = jnp.dot(jnp.concatenate([oh1, oh2], axis=0).astype(BF16), earlier, preferred_element_type=F32)
    base1 = base_ref[:, :1]
    pos1 = jnp.sum(oh1 * (base1 + before[:N_EXPERTS]), axis=0, keepdims=True)
    base2 = base1 + jnp.sum(oh1, axis=1, keepdims=True)
    pos2 = jnp.sum(oh2 * (base2 + before[N_EXPERTS:]), axis=0, keepdims=True)
    total = jnp.broadcast_to(base2 + jnp.sum(oh2, axis=1, keepdims=True), base_ref.shape)
    base_ref[...] = total
    cnt_ref[...] = total

    fields = jnp.concatenate([i1.astype(F32), i2.astype(F32), pos1, pos2, w1, w2, jnp.zeros((2, tm), F32)], axis=0)
    route_ref[...] = jnp.transpose(jnp.concatenate([fields, jnp.zeros((LANES - 8, tm), F32)], axis=0))


def _out_router(od, os_, x2d, wod, wos, g2, wr, br, tm):
    n, d = x2d.shape
    half = od.shape[1]
    row = lambda i: (i, 0)
    const = lambda i: (0, 0)
    return pl.pallas_call(
        _out_router_kernel,
        grid=(n // tm,),
        in_specs=[
            pl.BlockSpec((tm, half), row),
            pl.BlockSpec((tm, half), row),
            pl.BlockSpec((tm, d), row),
            pl.BlockSpec((half, d), const),
            pl.BlockSpec((half, d), const),
            pl.BlockSpec((1, d), const),
            pl.BlockSpec((d, 2 * LANES), const),
            pl.BlockSpec((1, LANES), const),
        ],
        out_specs=[
            pl.BlockSpec((tm, d), row),
            pl.BlockSpec((tm, d // 2), row),
            pl.BlockSpec((tm, LANES), row),
            pl.BlockSpec((N_EXPERTS, LANES), const),
        ],
        out_shape=[
            jax.ShapeDtypeStruct((n, d), F32),
            jax.ShapeDtypeStruct((n, d // 2), jnp.int32),
            jax.ShapeDtypeStruct((n, LANES), F32),
            jax.ShapeDtypeStruct((N_EXPERTS, LANES), F32),
        ],
        scratch_shapes=[pltpu.VMEM((N_EXPERTS, LANES), F32)],
        compiler_params=_params(("arbitrary",)),
        name="out_router",
    )(od, os_, x2d, wod, wos, g2, wr, br)


SC_CORES, SC_SUBCORES = 2, 16
SC_WORKERS = SC_CORES * SC_SUBCORES
SC_CHUNK = 128


def _sc_mesh():
    return plsc.VectorSubcoreMesh(core_axis_name="c", subcore_axis_name="s",
                                  num_cores=SC_CORES, num_subcores=SC_SUBCORES)


def _sc_scatter_rows(x, idx1, idx2, n_out):
    n, d = x.shape
    assert n % (SC_WORKERS * SC_CHUNK) == 0
    per_w = n // SC_WORKERS

    @functools.partial(
        pl.kernel, mesh=_sc_mesh(), out_type=jax.ShapeDtypeStruct((n_out, d), x.dtype),
        scratch_types=[pltpu.VMEM((SC_CHUNK,), jnp.int32), pltpu.VMEM((SC_CHUNK,), jnp.int32),
                       pltpu.VMEM((SC_CHUNK, d), x.dtype), pltpu.SemaphoreType.DMA],
        name="sc_scatter_rows")
    def scatter(x_hbm, i1_hbm, i2_hbm, o_hbm, i1_v, i2_v, rows_v, sem):
        wid = lax.axis_index("s") * SC_CORES + lax.axis_index("c")

        @pl.loop(0, per_w // SC_CHUNK)
        def _(j):
            base = wid * per_w + j * SC_CHUNK
            pltpu.sync_copy(i1_hbm.at[pl.ds(base, SC_CHUNK)], i1_v)
            pltpu.sync_copy(i2_hbm.at[pl.ds(base, SC_CHUNK)], i2_v)
            pltpu.sync_copy(x_hbm.at[pl.ds(base, SC_CHUNK)], rows_v)
            pltpu.async_copy(rows_v, o_hbm.at[i1_v], sem).wait()
            pltpu.async_copy(rows_v, o_hbm.at[i2_v], sem).wait()

    return scatter(x, idx1, idx2)


def _sc_gather_rows(table, idx1, idx2):
    n = idx1.shape[0]
    d = table.shape[1]
    assert n % (SC_WORKERS * SC_CHUNK) == 0
    per_w = n // SC_WORKERS
    out = jax.ShapeDtypeStruct((n, d), table.dtype)

    @functools.partial(
        pl.kernel, mesh=_sc_mesh(), out_type=(out, out),
        scratch_types=[pltpu.VMEM((SC_CHUNK,), jnp.int32), pltpu.VMEM((SC_CHUNK, d), table.dtype),
                       pltpu.SemaphoreType.DMA],
        name="sc_gather_rows")
    def gather(t_hbm, i1_hbm, i2_hbm, o1_hbm, o2_hbm, i_v, rows_v, sem):
        wid = lax.axis_index("s") * SC_CORES + lax.axis_index("c")

        @pl.loop(0, per_w // SC_CHUNK)
        def _(j):
            base = wid * per_w + j * SC_CHUNK
            for i_hbm, o_hbm in ((i1_hbm, o1_hbm), (i2_hbm, o2_hbm)):
                pltpu.sync_copy(i_hbm.at[pl.ds(base, SC_CHUNK)], i_v)
                pltpu.async_copy(t_hbm.at[i_v], rows_v, sem).wait()
                pltpu.sync_copy(rows_v, o_hbm.at[pl.ds(base, SC_CHUNK)])

    return gather(table, idx1, idx2)


def _dest_kernel(route_ref, starts_ref, d1_ref, d2_ref):
    tm = route_ref.shape[0]
    route = route_ref[...]
    lane = lax.broadcasted_iota(jnp.int32, (tm, LANES), 1)
    row = lax.broadcasted_iota(jnp.int32, (tm, LANES), 0)

    def field(n):
        return jnp.sum(jnp.where(lane == n, route, 0.0), axis=-1, keepdims=True)

    for e_lane, pos_lane, out_ref in ((ROUTE_E1, ROUTE_POS1, d1_ref), (ROUTE_E2, ROUTE_POS2, d2_ref)):
        expert = field(e_lane).astype(jnp.int32)
        start = jnp.sum(jnp.where(lane == expert, starts_ref[...], 0.0), axis=-1, keepdims=True)
        dest = start + field(pos_lane)
        spread = jnp.where(lane == row % LANES, dest, 0.0)
        out_ref[...] = jnp.sum(spread.reshape(tm // LANES, LANES, LANES), axis=1).astype(jnp.int32)


def _routing_tables(route, cnt, tm, n_tiles_max):
    n = route.shape[0]
    counts = cnt[:, 0].astype(jnp.int32)
    padded = (counts + tm - 1) // tm * tm
    ends = jnp.cumsum(padded)
    starts = jnp.pad((ends - padded).astype(F32), (0, LANES - N_EXPERTS)).reshape(1, LANES)
    tr = min(1024, n)
    lane_dense = jax.ShapeDtypeStruct((n // LANES, LANES), jnp.int32)
    dest1, dest2 = pl.pallas_call(
        _dest_kernel,
        grid=(n // tr,),
        in_specs=[pl.BlockSpec((tr, LANES), lambda i: (i, 0)), pl.BlockSpec((1, LANES), lambda i: (0, 0))],
        out_specs=[pl.BlockSpec((tr // LANES, LANES), lambda i: (i, 0))] * 2,
        out_shape=[lane_dense, lane_dense],
        compiler_params=_params(("parallel",)),
        name="route_dest",
    )(route, starts)
    tile_start = jnp.arange(n_tiles_max, dtype=jnp.int32) * tm
    tile_expert = jnp.sum((ends[None, :] <= tile_start[:, None]).astype(jnp.int32), axis=1)
    tile_expert = jnp.minimum(tile_expert, N_EXPERTS - 1)
    n_tiles = (ends[-1] // tm).astype(jnp.int32).reshape(1)
    valid = tile_start < ends[-1]
    prev_expert = jnp.concatenate([jnp.full((1,), -1, jnp.int32), tile_expert[:-1]])
    first = (valid & (tile_expert != prev_expert)).astype(jnp.int32)
    slot = (jnp.cumsum(first) - 1) % 2
    e_idx = jnp.arange(N_EXPERTS, dtype=jnp.int32)
    later = (padded > 0)[None, :] & (e_idx[None, :] > e_idx[:, None])
    next_run = jnp.min(jnp.where(later, e_idx[None, :], N_EXPERTS), axis=1)
    next_run = jnp.where(next_run == N_EXPERTS, -1, next_run).astype(jnp.int32)
    plan = (tile_expert, first, slot.astype(jnp.int32), next_run[tile_expert], n_tiles)
    return dest1.reshape(n), dest2.reshape(n), plan


def _moe_tiles_kernel(te_ref, first_ref, slot_ref, nxt_ref, nt_ref, xs_ref, wg_hbm, wu_hbm, wd_hbm, ys_ref,
                      wg_buf, wu_buf, wd_buf, sem):
    t = pl.program_id(0)

    def weight_copies(expert, slot):
        return (pltpu.make_async_copy(wg_hbm.at[expert], wg_buf.at[slot], sem.at[slot, 0]),
                pltpu.make_async_copy(wu_hbm.at[expert], wu_buf.at[slot], sem.at[slot, 1]),
                pltpu.make_async_copy(wd_hbm.at[expert], wd_buf.at[slot], sem.at[slot, 2]))

    @pl.when(t < nt_ref[0])
    def _():
        slot = slot_ref[t]

        @pl.when(t == 0)
        def _():
            for c in weight_copies(te_ref[0], 0):
                c.start()

        @pl.when(first_ref[t] == 1)
        def _():
            for c in weight_copies(te_ref[t], slot):
                c.wait()

            @pl.when(nxt_ref[t] >= 0)
            def _():
                for c in weight_copies(nxt_ref[t], 1 - slot):
                    c.start()

        x = _unpack_bf16_pairs(xs_ref[...]).astype(BF16)
        hg = jnp.dot(x, wg_buf[slot], preferred_element_type=F32)
        hu = jnp.dot(x, wu_buf[slot], preferred_element_type=F32)
        hid = (hg * jax.nn.sigmoid(hg) * hu).astype(BF16)
        ys_ref[...] = _pack_bf16_pairs(jnp.dot(hid, wd_buf[slot], preferred_element_type=F32))


def _moe_tiles(xs, plan, wg, wu, wd, tm):
    r, dh = xs.shape
    ne, d, de = wg.shape
    row = lambda t, *_: (t, 0)
    hbm = pl.BlockSpec(memory_space=pl.ANY)
    return pl.pallas_call(
        _moe_tiles_kernel,
        grid_spec=pltpu.PrefetchScalarGridSpec(
            num_scalar_prefetch=len(plan),
            grid=(r // tm,),
            in_specs=[pl.BlockSpec((tm, dh), row), hbm, hbm, hbm],
            out_specs=pl.BlockSpec((tm, dh), row),
            scratch_shapes=[
                pltpu.VMEM((2, d, de), wg.dtype),
                pltpu.VMEM((2, d, de), wu.dtype),
                pltpu.VMEM((2, de, d), wd.dtype),
                pltpu.SemaphoreType.DMA((2, 3)),
            ],
        ),
        out_shape=jax.ShapeDtypeStruct((r, dh), jnp.int32),
        compiler_params=_params(("arbitrary",)),
        name="moe_tiles",
    )(*plan, xs, wg, wu, wd)


def _combine_kernel(h_ref, y1_ref, y2_ref, route_ref, fg_ref, o_ref):
    lane = lax.broadcasted_iota(jnp.int32, route_ref.shape, 1)
    route = route_ref[...]
    w1 = jnp.sum(jnp.where(lane == ROUTE_W1, route, 0.0), axis=-1, keepdims=True)
    w2 = jnp.sum(jnp.where(lane == ROUTE_W2, route, 0.0), axis=-1, keepdims=True)
    y = h_ref[...] + w1 * _unpack_bf16_pairs(y1_ref[...]) + w2 * _unpack_bf16_pairs(y2_ref[...])
    o_ref[...] = y * lax.rsqrt(jnp.mean(y * y, axis=-1, keepdims=True) + EPS) * fg_ref[...]


def _combine(h, y1, y2, route, fg, tm):
    n, d = h.shape
    row = lambda i: (i, 0)
    return pl.pallas_call(
        _combine_kernel,
        grid=(n // tm,),
        in_specs=[
            pl.BlockSpec((tm, d), row),
            pl.BlockSpec((tm, d // 2), row),
            pl.BlockSpec((tm, d // 2), row),
            pl.BlockSpec((tm, LANES), row),
            pl.BlockSpec((1, d), lambda i: (0, 0)),
        ],
        out_specs=pl.BlockSpec((tm, d), row),
        out_shape=jax.ShapeDtypeStruct((n, d), F32),
        compiler_params=_params(("parallel",)),
        name="moe_combine",
    )(h, y1, y2, route, fg)


def _moe(a2p, route, cnt, h, wg, wu, wd, fg, tm):
    n = h.shape[0]
    n_tiles_max = (2 * n) // tm + N_EXPERTS
    dest1, dest2, plan = _routing_tables(route, cnt, tm, n_tiles_max)
    xs = _sc_scatter_rows(a2p, dest1, dest2, n_tiles_max * tm)
    ys = _moe_tiles(xs, plan, wg, wu, wd, tm)
    y1, y2 = _sc_gather_rows(ys, dest1, dest2)
    return _combine(h, y1, y2, route, fg, tm=min(512, n))


def _dup_heads(w, n_heads):
    d = w.shape[0]
    w = w.reshape(d, n_heads, 1, HEAD_DIM)
    return jnp.broadcast_to(w, (d, n_heads, 2, HEAD_DIM)).reshape(d, n_heads * 2 * HEAD_DIM)


def _encoder(x, proj_meta, wts):
    b, s, d = x.shape
    x2d = x.reshape(b * s, d)
    proj = _norm_proj(x2d, wts["g1"], wts["w_ext"], tm=512).reshape(b, s, -1)
    od = _diff_attn(proj, proj_meta, wts["lam_vecs"], wts["subln_g"], bq=256, bk=min(1024, s))
    os_ = _win_attn(proj, proj_meta, wts["sink"])
    h, a2p, route, cnt = _out_router(od.reshape(b * s, -1), os_.reshape(b * s, -1), x2d, wts["wod"], wts["wos"],
                                     wts["g2"], wts["wr"], wts["br"], tm=512)
    y = _moe(a2p, route, cnt, h, wts["wg"], wts["wu"], wts["wd"], wts["fg"], tm=512)
    return y.reshape(b, s, d)


def kernel(x_prompt, x_sample, meta, norm1_g, w_in, lam_q1, lam_k1, lam_q2, lam_k2, subln_g, sink, w_out,
           norm2_g, w_gr, b_gr, w_er, b_er, w_gate, w_up, w_down, final_g):
    d = x_prompt.shape[-1]
    w = w_in[0]
    c_kd = 2 * DA_HEADS * HEAD_DIM
    c_vd = 2 * c_kd
    c_qs = c_vd + DA_HEADS * 2 * HEAD_DIM
    c_ks = c_qs + SW_HEADS * HEAD_DIM
    c_vs = c_ks + SW_KV_HEADS * HEAD_DIM
    w_ext = jnp.concatenate(
        [w[:, :c_ks], _dup_heads(w[:, c_ks:c_vs], SW_KV_HEADS), _dup_heads(w[:, c_vs:], SW_KV_HEADS)],
        axis=1).astype(BF16)
    w_router = jnp.concatenate([w_er[0], w_gr[0]], axis=1)
    w_router = jnp.pad(w_router, ((0, 0), (0, LANES - w_router.shape[1])))
    wr_hi = w_router.astype(BF16)
    wr_lo = (w_router - wr_hi.astype(F32)).astype(BF16)
    br = jnp.pad(jnp.concatenate([b_er[0], b_gr[0]]), (0, LANES - N_EXPERTS - N_GROUPS)).reshape(1, LANES)
    wo = w_out[0].astype(BF16)
    half = DA_HEADS * 2 * HEAD_DIM
    wts = dict(
        g1=norm1_g[0].reshape(1, d), w_ext=w_ext,
        lam_vecs=(lam_q1[0].reshape(1, -1), lam_k1[0].reshape(1, -1),
                  lam_q2[0].reshape(1, -1), lam_k2[0].reshape(1, -1)),
        subln_g=subln_g[0].reshape(1, -1), sink=sink[0].reshape(1, -1),
        wod=wo[:half], wos=wo[half:], g2=norm2_g[0].reshape(1, d),
        wr=jnp.concatenate([wr_hi, wr_lo], axis=1), br=br,
        wg=w_gate[0].astype(BF16), wu=w_up[0].astype(BF16), wd=w_down[0].astype(BF16),
        fg=final_g.reshape(1, d),
    )
    proj_meta = _norm_proj(meta, wts["g1"], w_ext, tm=N_META)
    proj_meta = jnp.pad(proj_meta, ((0, LANES - N_META), (0, 0)))
    return _encoder(x_prompt, proj_meta, wts), _encoder(x_sample, proj_meta, wts)
```

```python
import functools
import math

import jax
import jax.numpy as jnp
import numpy as np
from jax import lax
from jax.experimental import pallas as pl
from jax.experimental.pallas import tpu as pltpu
from jax.experimental.pallas import tpu_sc as plsc

F32 = jnp.float32
BF16 = jnp.bfloat16

N_META = 16
HEAD_DIM = 64
DA_HEADS = 4
SW_HEADS = 8
SW_KV_HEADS = 2
SW_GROUP = SW_HEADS // SW_KV_HEADS
WINDOW = 128
N_GROUPS = 4
EXPERTS_PER_GROUP = 8
N_EXPERTS = N_GROUPS * EXPERTS_PER_GROUP
EPS = 1e-6
SUBLN_EPS = 1e-5
NEG_INF = -1e30
LAM_INIT = 0.8 - 0.6 * math.exp(-0.3 * 0)
LANES = 128
VMEM_LIMIT = 48 * 1024 * 1024

COL_QD, COL_KD, COL_VD, COL_QS, COL_KS, COL_VS, N_COLBLK = 0, 4, 8, 12, 16, 18, 20
CONTRACT_LAST = (((1,), (1,)), ((), ()))


def _params(sem):
    return pltpu.CompilerParams(dimension_semantics=sem, vmem_limit_bytes=VMEM_LIMIT)


def _norm_proj_kernel(x_ref, g_ref, w_ref, o_ref):
    x = x_ref[...]
    ms = jnp.mean(x * x, axis=-1, keepdims=True)
    y = (x * lax.rsqrt(ms + EPS) * g_ref[...]).astype(BF16)
    o_ref[...] = jnp.dot(y, w_ref[...], preferred_element_type=F32).astype(o_ref.dtype)


def _norm_proj(x2d, g, w_ext, tm):
    n, d = x2d.shape
    wcols = w_ext.shape[1]
    return pl.pallas_call(
        _norm_proj_kernel,
        grid=(n // tm,),
        in_specs=[
            pl.BlockSpec((tm, d), lambda i: (i, 0)),
            pl.BlockSpec((1, d), lambda i: (0, 0)),
            pl.BlockSpec((d, wcols), lambda i: (0, 0)),
        ],
        out_specs=pl.BlockSpec((tm, wcols), lambda i: (i, 0)),
        out_shape=jax.ShapeDtypeStruct((n, wcols), BF16),
        compiler_params=_params(("parallel",)),
        name="norm_proj",
    )(x2d, g, w_ext)


POS_SPLIT = 64
FEAT_ONE_A, FEAT_ONE_B, FEAT_HI, FEAT_LO = 0, 1, 2, 3


def _key_pos_features(seq):
    j = jnp.arange(seq, dtype=jnp.int32)[:, None]
    lane = jnp.arange(LANES, dtype=jnp.int32)[None, :]
    hi = (j // POS_SPLIT * POS_SPLIT).astype(F32)
    lo = (j % POS_SPLIT).astype(F32)
    feat = jnp.where(lane <= FEAT_ONE_B, 1.0, jnp.where(lane == FEAT_HI, hi, jnp.where(lane == FEAT_LO, lo, 0.0)))
    return feat.astype(BF16)


SKIP_MARGIN = 100.0


def _diff_attn_head(h, q_ref, k_ref, v_ref, kf_ref, km_ref, vm_ref, lam, sg_ref,
                    o_ref, qe_ref, acc_ref, m_ref, sd_ref, s1_ref, kn_ref, *, bq, bk, seq, skip):
    i = pl.program_id(1)
    scale = 1.0 / math.sqrt(HEAD_DIM)
    slope = 2.0 ** (-8.0 * (h + 1) / DA_HEADS)
    lane = lax.broadcasted_iota(jnp.int32, (1, LANES), 1)
    q = q_ref[...] * jnp.asarray(scale, BF16)
    zero = jnp.zeros_like(q)
    qmaps = (jnp.where(lane < HEAD_DIM, q, zero), jnp.where(lane >= HEAD_DIM, q, zero))
    qpos = i * bq + lax.broadcasted_iota(jnp.int32, (bq, 1), 0)
    q_hi = (qpos // POS_SPLIT * POS_SPLIT).astype(F32)
    q_lo = (qpos % POS_SPLIT).astype(F32)
    feat = jnp.where(lane == FEAT_ONE_A, -slope * q_hi,
                     jnp.where(lane == FEAT_ONE_B, -slope * q_lo, jnp.where(lane <= FEAT_LO, slope, 0.0)))
    feats = (feat.astype(BF16), (-feat).astype(BF16))
    for side in range(2):
        for c in range(2):
            qe_ref[side, c * bq:(c + 1) * bq, :LANES] = qmaps[c]
            qe_ref[side, c * bq:(c + 1) * bq, LANES:] = feats[side]

    ones_blk = jnp.broadcast_to(jnp.where(lane == 0, 1.0, 0.0).astype(BF16), (bk, LANES))
    meta_mask = jnp.where(lane < N_META, 0.0, NEG_INF)

    def attend(s, v_ext):
        m_old = m_ref[...]
        m_new = jnp.maximum(m_old, jnp.max(s, axis=-1, keepdims=True))
        alpha = jnp.exp(m_old - m_new)
        p = jnp.exp(s - jnp.tile(m_new, (1, s.shape[1] // LANES))).astype(BF16)
        acc_ref[...] = jnp.tile(alpha, (1, 2)) * acc_ref[...] + jnp.dot(p, v_ext, preferred_element_type=F32)
        m_ref[...] = m_new

    m_ref[...] = jnp.full(m_ref.shape, NEG_INF, F32)
    acc_ref[...] = jnp.zeros(acc_ref.shape, F32)

    s0_ref = sd_ref.at[:, :bk]
    nblk = seq // bk
    diag = (i * bq) // bk
    lo = 0
    n_keep = nblk
    if skip and nblk > 1:
        @pl.when(i == 0)
        def _():
            kf32 = k_ref[...].astype(F32)
            kn_ref[0] = jnp.max(jnp.sum(kf32 * kf32, axis=-1, keepdims=True))
        qf32 = q.astype(F32)
        qn2 = jnp.max(jnp.sum(qf32 * qf32, axis=-1, keepdims=True), axis=0, keepdims=True)
        s_meta = lax.dot_general(qe_ref[0, :, :LANES], km_ref[...], CONTRACT_LAST,
                                 preferred_element_type=F32) + meta_mask
        m_low = jnp.min(jnp.max(s_meta, axis=-1, keepdims=True), axis=0, keepdims=True)
        reach = (jnp.sqrt(qn2 * kn_ref[0]) - m_low + SKIP_MARGIN) * (1.0 / slope)
        reach = jnp.clip(reach, 0.0, float(seq)).astype(jnp.int32)[0, 0] + 1
        q_first = i * bq
        q_last = q_first + bq - 1
        lo = jnp.minimum(jnp.maximum(q_first + 1 - reach, 0) // bk, diag)
        hi = jnp.maximum(jnp.minimum((q_last + reach - 1) // bk, nblk - 1), diag)
        odd = ((hi - lo + 1) % 2 == 1) & (hi - lo + 1 > 1)
        lo = jnp.where(odd & (lo > 0), lo - 1, lo)
        hi = jnp.where(odd & ((hi - lo + 1) % 2 == 1), hi + 1, hi)
        n_keep = hi - lo + 1

    def key_block(pos):
        kb = lo + pos - 1
        return kb + jnp.where(kb >= diag, 1, 0)

    def scores_into(dst_ref, pos):
        kb = key_block(pos)
        side = jnp.where(kb > diag, 1, 0)
        start = pl.multiple_of(kb * bk, bk)
        k_ext = jnp.concatenate([k_ref[pl.ds(start, bk), :], kf_ref[pl.ds(start, bk), :]], axis=1)
        dst_ref[...] = lax.dot_general(qe_ref[side], k_ext, CONTRACT_LAST, preferred_element_type=F32)

    def consume(src_ref, pos):
        start = pl.multiple_of(key_block(pos) * bk, bk)
        attend(src_ref[...], jnp.concatenate([v_ref[pl.ds(start, bk), :], ones_blk], axis=1))

    d_start = pl.multiple_of(diag * bk, bk)
    kpos = diag * bk + lax.broadcasted_iota(jnp.int32, (1, bk), 1)
    bias = jnp.concatenate([-slope * jnp.abs(qpos - kpos).astype(F32),
                            jnp.broadcast_to(meta_mask, (bq, LANES))], axis=1)
    k_first = jnp.concatenate([k_ref[pl.ds(d_start, bk), :], km_ref[...]], axis=0)
    v_first = jnp.concatenate(
        [jnp.concatenate([v_ref[pl.ds(d_start, bk), :], vm_ref[...]], axis=0),
         jnp.broadcast_to(ones_blk[:1], (bk + LANES, LANES))], axis=1)
    sd_ref[...] = lax.dot_general(qe_ref[0, :, :LANES], k_first, CONTRACT_LAST,
                                  preferred_element_type=F32) + jnp.tile(bias, (2, 1))

    def pipeline(loop):
        scores_into(s1_ref, 1)
        attend(sd_ref[...], v_first)

        def pair(u):
            scores_into(s0_ref, 2 * u + 2)
            consume(s1_ref, 2 * u + 1)
            scores_into(s1_ref, 2 * u + 3)
            consume(s0_ref, 2 * u + 2)

        loop((n_keep - 2) // 2, pair)
        consume(s1_ref, n_keep - 1)

    if nblk == 1:
        attend(sd_ref[...], v_first)
    elif skip:
        @pl.when(n_keep == 1)
        def _():
            attend(sd_ref[...], v_first)

        @pl.when(n_keep > 1)
        def _():
            pipeline(lambda trips, pair: lax.fori_loop(0, trips, lambda u, c: (pair(u), c)[1], 0))
    else:
        def unrolled(trips, pair):
            for u in range(trips):
                pair(u)
        pipeline(unrolled)

    o1 = acc_ref[:bq, :LANES] / acc_ref[:bq, LANES:LANES + 1]
    o2 = acc_ref[bq:, :LANES] / acc_ref[bq:, LANES:LANES + 1]
    o = o1 - lam * o2
    o = o * lax.rsqrt(jnp.mean(o * o, axis=-1, keepdims=True) + SUBLN_EPS) * sg_ref[...]
    o_ref[...] = (o * (1.0 - LAM_INIT)).astype(o_ref.dtype)


def _diff_attn_kernel(*refs, heads, **kw):
    nh = len(heads)
    q_refs, k_refs, v_refs = refs[:nh], refs[nh:2 * nh], refs[2 * nh:3 * nh]
    kf_ref = refs[3 * nh]
    km_refs, vm_refs = refs[3 * nh + 1:4 * nh + 1], refs[4 * nh + 1:5 * nh + 1]
    lq1_ref, lk1_ref, lq2_ref, lk2_ref, sg_ref, o_ref, qe_ref, acc_ref, m_ref, sd_ref, s1_ref, kn_ref = refs[5 * nh + 1:]
    lam = (jnp.exp(jnp.sum(lq1_ref[...] * lk1_ref[...], axis=-1, keepdims=True))
           - jnp.exp(jnp.sum(lq2_ref[...] * lk2_ref[...], axis=-1, keepdims=True)) + LAM_INIT)
    for n, h in enumerate(heads):
        _diff_attn_head(h, q_refs[n], k_refs[n], v_refs[n], kf_ref, km_refs[n], vm_refs[n], lam, sg_ref,
                        o_ref.at[:, n * LANES:(n + 1) * LANES],
                        qe_ref.at[n], acc_ref.at[n], m_ref.at[n], sd_ref.at[n], s1_ref.at[n], kn_ref, **kw)


def _diff_attn(proj, proj_meta, lam_vecs, subln_g, heads, bq, bk, skip):
    b, s, _ = proj.shape
    bq = min(bq, s)
    assert bk % bq == 0 and s % bk == 0 and (s // bk == 1 or (s // bk) % 2 == 0)
    nh = len(heads)
    vec = pl.BlockSpec((1, HEAD_DIM), lambda bi, i: (0, 0))
    q_specs = [pl.BlockSpec((None, bq, LANES), lambda bi, i, c=COL_QD + h: (bi, i, c)) for h in heads]
    k_specs = [pl.BlockSpec((None, s, LANES), lambda bi, i, c=COL_KD + h: (bi, 0, c)) for h in heads]
    v_specs = [pl.BlockSpec((None, s, LANES), lambda bi, i, c=COL_VD + h: (bi, 0, c)) for h in heads]
    km_specs = [pl.BlockSpec((LANES, LANES), lambda bi, i, c=COL_KD + h: (0, c)) for h in heads]
    vm_specs = [pl.BlockSpec((LANES, LANES), lambda bi, i, c=COL_VD + h: (0, c)) for h in heads]
    return pl.pallas_call(
        functools.partial(_diff_attn_kernel, heads=heads, bq=bq, bk=bk, seq=s, skip=skip),
        grid=(b, s // bq),
        in_specs=(q_specs + k_specs + v_specs + [pl.BlockSpec((s, LANES), lambda bi, i: (0, 0))]
                  + km_specs + vm_specs + [vec, vec, vec, vec, pl.BlockSpec((1, LANES), lambda bi, i: (0, 0))]),
        out_specs=pl.BlockSpec((None, bq, nh * LANES), lambda bi, i: (bi, i, 0)),
        out_shape=jax.ShapeDtypeStruct((b, s, nh * LANES), BF16),
        scratch_shapes=[
            pltpu.VMEM((nh, 2, 2 * bq, 2 * LANES), BF16),
            pltpu.VMEM((nh, 2 * bq, 2 * LANES), F32),
            pltpu.VMEM((nh, 2 * bq, LANES), F32),
            pltpu.VMEM((nh, 2 * bq, bk + LANES), F32),
            pltpu.VMEM((nh, 2 * bq, bk), F32),
            pltpu.SMEM((1,), F32),
        ],
        compiler_params=_params(("parallel", "arbitrary")),
        name="diff_attn",
    )(*([proj] * (3 * nh)), _key_pos_features(s), *([proj_meta] * (2 * nh)), *lam_vecs, subln_g)


WIN_KEYS = 4 * WINDOW
WIN_QBLOCKS = 4


def _win_tables():
    r = np.arange(WINDOW)
    qf = np.zeros((SW_KV_HEADS, SW_GROUP * WINDOW, LANES), np.float32)
    for head in range(SW_HEADS):
        slope = 2.0 ** (-8.0 * (head + 1) / SW_HEADS)
        i_rel = WINDOW + r
        hi, lo = i_rel // POS_SPLIT * POS_SPLIT, i_rel % POS_SPLIT
        rows = qf[head // SW_GROUP, (head % SW_GROUP) * WINDOW:(head % SW_GROUP + 1) * WINDOW]
        rows[:, 0], rows[:, 1], rows[:, 2], rows[:, 3] = -slope * hi, -slope * lo, slope, slope
        rows[:, 4:8] = -rows[:, 0:4]
    kf = np.zeros((2, WINDOW, LANES), np.float32)
    for n, (blk, right) in enumerate(((0, 0), (2, 1))):
        j_rel = blk * WINDOW + r
        o = 4 * right
        kf[n, :, o], kf[n, :, o + 1] = 1.0, 1.0
        kf[n, :, o + 2], kf[n, :, o + 3] = j_rel // POS_SPLIT * POS_SPLIT, j_rel % POS_SPLIT
    mask = np.zeros((2, WINDOW, WINDOW), np.float32)
    mask[0] = np.where(r[None, :] >= r[:, None], 0.0, NEG_INF)
    mask[1] = np.where(r[None, :] <= r[:, None], 0.0, NEG_INF)
    cur = np.zeros((SW_KV_HEADS, SW_GROUP * WINDOW, WINDOW), np.float32)
    for head in range(SW_HEADS):
        slope = 2.0 ** (-8.0 * (head + 1) / SW_HEADS)
        cur[head // SW_GROUP, (head % SW_GROUP) * WINDOW:(head % SW_GROUP + 1) * WINDOW] = (
            -slope * np.abs(r[:, None] - r[None, :]))
    return jnp.asarray(qf, BF16), jnp.asarray(kf, BF16), jnp.asarray(mask, F32), jnp.asarray(cur, F32)


def _win_attn_kernel(q_ref, kp_ref, kc_ref, kn_ref, vp_ref, vc_ref, vn_ref, km_ref, vm_ref, qf_ref, kf_ref,
                     mask_ref, cur_ref, sink_ref, o_ref, *, nstep):
    c = pl.program_id(1)
    lane = lax.broadcasted_iota(jnp.int32, (1, LANES), 1)
    scale = jnp.asarray(1.0 / math.sqrt(HEAD_DIM), BF16)
    mask_meta = jnp.where(lane < N_META, 0.0, NEG_INF)
    row = lax.broadcasted_iota(jnp.int32, (SW_GROUP * WINDOW, 1), 0)
    ones_blk = jnp.broadcast_to(jnp.where(lane == 0, 1.0, 0.0).astype(BF16), (WIN_KEYS, LANES))
    zeros_blk = jnp.zeros((WINDOW, LANES), BF16)
    gw = SW_GROUP * HEAD_DIM
    own = [slice(j * WINDOW, (j + 1) * WINDOW) for j in range(WIN_QBLOCKS)]
    kblk = [kp_ref] + [kc_ref.at[r] for r in own] + [kn_ref]
    vblk = [vp_ref] + [vc_ref.at[r] for r in own] + [vn_ref]
    for j in range(WIN_QBLOCKS):
        rq = slice(j * WINDOW, (j + 1) * WINDOW)
        edge_p = jnp.where(c == 0, NEG_INF, 0.0) if j == 0 else 0.0
        edge_n = jnp.where(c == nstep - 1, NEG_INF, 0.0) if j == WIN_QBLOCKS - 1 else 0.0
        mask_prev = jnp.tile(mask_ref[0] + edge_p, (SW_GROUP, 1))
        mask_next = jnp.tile(mask_ref[1] + edge_n, (SW_GROUP, 1))
        for g in range(SW_KV_HEADS):
            kv = slice(g * LANES, (g + 1) * LANES)
            rows = []
            for hh in range(SW_GROUP):
                col = g * gw + (hh // 2) * LANES
                qb = q_ref[rq, col:col + LANES] * scale
                keep = (lane < HEAD_DIM) if hh % 2 == 0 else (lane >= HEAD_DIM)
                rows.append(jnp.where(keep, qb, jnp.zeros_like(qb)))
            q_ext = jnp.concatenate([jnp.concatenate(rows, axis=0), qf_ref[g]], axis=1)
            k_all = jnp.concatenate([
                jnp.concatenate([km_ref[:, kv], zeros_blk], axis=1),
                jnp.concatenate([kblk[j][:, kv], kf_ref[0]], axis=1),
                jnp.concatenate([kblk[j + 1][:, kv], zeros_blk], axis=1),
                jnp.concatenate([kblk[j + 2][:, kv], kf_ref[1]], axis=1)], axis=0)
            s = lax.dot_general(q_ext, k_all, CONTRACT_LAST, preferred_element_type=F32)
            s_meta = s[:, :LANES] + mask_meta
            s_prev = s[:, LANES:2 * LANES] + mask_prev
            s_cur = s[:, 2 * LANES:3 * LANES] + cur_ref[g]
            s_next = s[:, 3 * LANES:] + mask_next
            sink = jnp.zeros((SW_GROUP * WINDOW, 1), F32)
            for hh in range(SW_GROUP):
                sink = jnp.where(row // WINDOW == hh, sink_ref[0, g * SW_GROUP + hh], sink)
            parts = (s_meta, s_prev, s_cur, s_next)
            m = jnp.maximum(jnp.maximum(s_meta, s_prev), jnp.maximum(s_cur, s_next))
            m = jnp.maximum(jnp.max(m, axis=-1, keepdims=True), sink)
            p = jnp.concatenate([jnp.exp(x - m) for x in parts], axis=1).astype(BF16)
            v_all = jnp.concatenate(
                [jnp.concatenate([vm_ref[:, kv], vblk[j][:, kv], vblk[j + 1][:, kv], vblk[j + 2][:, kv]], axis=0),
                 ones_blk], axis=1)
            acc = jnp.dot(p, v_all, preferred_element_type=F32)
            denom = acc[:, LANES:LANES + 1] + jnp.exp(sink - m)
            o = acc[:, :LANES] / denom
            for lb in range(SW_GROUP // 2):
                even = o[(2 * lb) * WINDOW:(2 * lb + 1) * WINDOW]
                odd = o[(2 * lb + 1) * WINDOW:(2 * lb + 2) * WINDOW]
                col = g * gw + lb * LANES
                o_ref[rq, col:col + LANES] = jnp.where(lane < HEAD_DIM, even, odd).astype(o_ref.dtype)


def _win_attn(proj, proj_meta, sink):
    b, s, _ = proj.shape
    nblk = s // WINDOW
    nq = WIN_QBLOCKS
    nstep = nblk // nq
    qw = SW_HEADS * HEAD_DIM
    kvw = SW_KV_HEADS * LANES
    qf, kf, mask, cur = _win_tables()

    def edge_spec(col, shift):
        return pl.BlockSpec(
            (None, WINDOW, kvw),
            lambda bi, c: (bi, jnp.clip(nq * c + shift, 0, nblk - 1), col * LANES // kvw))

    def pair_spec(col):
        return pl.BlockSpec((None, nq * WINDOW, kvw), lambda bi, c: (bi, c, col * LANES // kvw))

    return pl.pallas_call(
        functools.partial(_win_attn_kernel, nstep=nstep),
        grid=(b, nstep),
        in_specs=[
            pl.BlockSpec((None, nq * WINDOW, qw), lambda bi, c: (bi, c, COL_QS * LANES // qw)),
            edge_spec(COL_KS, -1), pair_spec(COL_KS), edge_spec(COL_KS, nq),
            edge_spec(COL_VS, -1), pair_spec(COL_VS), edge_spec(COL_VS, nq),
            pl.BlockSpec((LANES, kvw), lambda bi, c: (0, COL_KS * LANES // kvw)),
            pl.BlockSpec((LANES, kvw), lambda bi, c: (0, COL_VS * LANES // kvw)),
            pl.BlockSpec((SW_KV_HEADS, SW_GROUP * WINDOW, LANES), lambda bi, c: (0, 0, 0)),
            pl.BlockSpec((2, WINDOW, LANES), lambda bi, c: (0, 0, 0)),
            pl.BlockSpec((2, WINDOW, WINDOW), lambda bi, c: (0, 0, 0)),
            pl.BlockSpec((SW_KV_HEADS, SW_GROUP * WINDOW, WINDOW), lambda bi, c: (0, 0, 0)),
            pl.BlockSpec(memory_space=pltpu.SMEM),
        ],
        out_specs=pl.BlockSpec((None, nq * WINDOW, qw), lambda bi, c: (bi, c, 0)),
        out_shape=jax.ShapeDtypeStruct((b, s, qw), BF16),
        compiler_params=_params(("parallel", "arbitrary")),
        name="win_attn",
    )(proj, proj, proj, proj, proj, proj, proj, proj_meta, proj_meta, qf, kf, mask, cur, sink)


ROUTE_E1, ROUTE_E2, ROUTE_POS1, ROUTE_POS2, ROUTE_W1, ROUTE_W2 = range(6)


def _pack_bf16_pairs(x):
    k = x.shape[1] // 2
    bits = lax.bitcast_convert_type(x.astype(BF16).astype(F32), jnp.uint32)
    return lax.bitcast_convert_type(bits[:, :k] | (bits[:, k:] >> 16), jnp.int32)


def _unpack_bf16_pairs(w):
    bits = lax.bitcast_convert_type(w, jnp.uint32)
    hi = lax.bitcast_convert_type(bits & jnp.uint32(0xFFFF0000), F32)
    lo = lax.bitcast_convert_type(bits << 16, F32)
    return jnp.concatenate([hi, lo], axis=1)


def _out_router_kernel(od0_ref, od1_ref, os_ref, x_ref, wod_ref, wos_ref, g2_ref, wr_ref, br_ref,
                       h_ref, a_ref, route_ref, cnt_ref, base_ref):
    @pl.when(pl.program_id(0) == 0)
    def _():
        base_ref[...] = jnp.zeros_like(base_ref)

    h = (x_ref[...]
         + jnp.dot(jnp.concatenate([od0_ref[...], od1_ref[...]], axis=1), wod_ref[...],
                   preferred_element_type=F32)
         + jnp.dot(os_ref[...], wos_ref[...], preferred_element_type=F32))
    h_ref[...] = h
    a = h * lax.rsqrt(jnp.mean(h * h, axis=-1, keepdims=True) + EPS) * g2_ref[...]
    a_hi = a.astype(BF16)
    a_ref[...] = _pack_bf16_pairs(a)
    a_lo = (a - a_hi.astype(F32)).astype(BF16)
    hi_terms = jnp.dot(a_hi, wr_ref[...], preferred_element_type=F32)
    logits = (hi_terms[:, :LANES] + hi_terms[:, LANES:]
              + jnp.dot(a_lo, wr_ref[:, :LANES], preferred_element_type=F32)) + br_ref[...]
    tm = logits.shape[0]
    lt = jnp.transpose(logits)
    le = lt[:N_EXPERTS]
    sub_g = lax.broadcasted_iota(jnp.int32, (8, tm), 0)
    lg = jnp.where(sub_g < N_GROUPS, lt[N_EXPERTS:N_EXPERTS + 8], NEG_INF)
    sub_e = lax.broadcasted_iota(jnp.int32, (N_EXPERTS, tm), 0)

    def first_argmax(vals, mask, sub, size):
        mx = jnp.max(vals, axis=0, keepdims=True)
        idx = jnp.min(jnp.where(mask & (vals == mx), sub, size), axis=0, keepdims=True)
        return mx, idx

    gmax, gidx = first_argmax(lg, sub_g < N_GROUPS, sub_g, 8)
    g_w = 1.0 / jnp.sum(jnp.exp(lg - gmax), axis=0, keepdims=True)
    in_group = (sub_e // EXPERTS_PER_GROUP) == gidx
    m1, i1 = first_argmax(jnp.where(in_group, le, NEG_INF), in_group, sub_e, N_EXPERTS)
    rest = in_group & (sub_e != i1)
    m2, i2 = first_argmax(jnp.where(rest, le, NEG_INF), rest, sub_e, N_EXPERTS)
    r = jnp.exp(m2 - m1)
    w1 = g_w / (1.0 + r)
    w2 = g_w * r / (1.0 + r)

    rix = lax.broadcasted_iota(jnp.int32, (tm, tm), 0)
    cix = lax.broadcasted_iota(jnp.int32, (tm, tm), 1)
    earlier = jnp.where(rix < cix, 1.0, 0.0).astype(BF16)
    oh1 = jnp.where(sub_e == i1, 1.0, 0.0)
    oh2 = jnp.where(sub_e == i2, 1.0, 0.0)
    before = jnp.dot(jnp.concatenate([oh1, oh2], axis=0).astype(BF16), earlier, preferred_element_type=F32)
    base1 = base_ref[:, :1]
    pos1 = jnp.sum(oh1 * (base1 + before[:N_EXPERTS]), axis=0, keepdims=True)
    base2 = base1 + jnp.sum(oh1, axis=1, keepdims=True)
    pos2 = jnp.sum(oh2 * (base2 + before[N_EXPERTS:]), axis=0, keepdims=True)
    total = jnp.broadcast_to(base2 + jnp.sum(oh2, axis=1, keepdims=True), base_ref.shape)
    base_ref[...] = total
    cnt_ref[...] = total

    fields = jnp.concatenate([i1.astype(F32), i2.astype(F32), pos1, pos2, w1, w2, jnp.zeros((2, tm), F32)], axis=0)
    route_ref[...] = jnp.transpose(jnp.concatenate([fields, jnp.zeros((LANES - 8, tm), F32)], axis=0))


def _out_router(od0, od1, os_, x2d, wod, wos, g2, wr, br, tm):
    n, d = x2d.shape
    half = os_.shape[1]
    row = lambda i: (i, 0)
    const = lambda i: (0, 0)
    return pl.pallas_call(
        _out_router_kernel,
        grid=(n // tm,),
        in_specs=[
            pl.BlockSpec((tm, od0.shape[1]), row),
            pl.BlockSpec((tm, od1.shape[1]), row),
            pl.BlockSpec((tm, half), row),
            pl.BlockSpec((tm, d), row),
            pl.BlockSpec((half, d), const),
            pl.BlockSpec((half, d), const),
            pl.BlockSpec((1, d), const),
            pl.BlockSpec((d, 2 * LANES), const),
            pl.BlockSpec((1, LANES), const),
        ],
        out_specs=[
            pl.BlockSpec((tm, d), row),
            pl.BlockSpec((tm, d // 2), row),
            pl.BlockSpec((tm, LANES), row),
            pl.BlockSpec((N_EXPERTS, LANES), const),
        ],
        out_shape=[
            jax.ShapeDtypeStruct((n, d), F32),
            jax.ShapeDtypeStruct((n, d // 2), jnp.int32),
            jax.ShapeDtypeStruct((n, LANES), F32),
            jax.ShapeDtypeStruct((N_EXPERTS, LANES), F32),
        ],
        scratch_shapes=[pltpu.VMEM((N_EXPERTS, LANES), F32)],
        compiler_params=_params(("arbitrary",)),
        name="out_router",
    )(od0, od1, os_, x2d, wod, wos, g2, wr, br)


SC_CORES, SC_SUBCORES = 2, 16
SC_WORKERS = SC_CORES * SC_SUBCORES
SC_CHUNK = 128


def _sc_mesh():
    return plsc.VectorSubcoreMesh(core_axis_name="c", subcore_axis_name="s",
                                  num_cores=SC_CORES, num_subcores=SC_SUBCORES)


def _sc_scatter_rows(x, idx1, idx2, n_out):
    n, d = x.shape
    assert n % (SC_WORKERS * SC_CHUNK) == 0
    per_w = n // SC_WORKERS

    @functools.partial(
        pl.kernel, mesh=_sc_mesh(), out_type=jax.ShapeDtypeStruct((n_out, d), x.dtype),
        scratch_types=[pltpu.VMEM((SC_CHUNK,), jnp.int32), pltpu.VMEM((SC_CHUNK,), jnp.int32),
                       pltpu.VMEM((SC_CHUNK, d), x.dtype), pltpu.SemaphoreType.DMA],
        name="sc_scatter_rows")
    def scatter(x_hbm, i1_hbm, i2_hbm, o_hbm, i1_v, i2_v, rows_v, sem):
        wid = lax.axis_index("s") * SC_CORES + lax.axis_index("c")

        @pl.loop(0, per_w // SC_CHUNK)
        def _(j):
            base = wid * per_w + j * SC_CHUNK
            pltpu.sync_copy(i1_hbm.at[pl.ds(base, SC_CHUNK)], i1_v)
            pltpu.sync_copy(i2_hbm.at[pl.ds(base, SC_CHUNK)], i2_v)
            pltpu.sync_copy(x_hbm.at[pl.ds(base, SC_CHUNK)], rows_v)
            pltpu.async_copy(rows_v, o_hbm.at[i1_v], sem).wait()
            pltpu.async_copy(rows_v, o_hbm.at[i2_v], sem).wait()

    return scatter(x, idx1, idx2)


def _sc_gather_rows(table, idx1, idx2):
    n = idx1.shape[0]
    d = table.shape[1]
    assert n % (SC_WORKERS * SC_CHUNK) == 0
    per_w = n // SC_WORKERS
    out = jax.ShapeDtypeStruct((n, d), table.dtype)

    @functools.partial(
        pl.kernel, mesh=_sc_mesh(), out_type=(out, out),
        scratch_types=[pltpu.VMEM((SC_CHUNK,), jnp.int32), pltpu.VMEM((SC_CHUNK, d), table.dtype),
                       pltpu.SemaphoreType.DMA],
        name="sc_gather_rows")
    def gather(t_hbm, i1_hbm, i2_hbm, o1_hbm, o2_hbm, i_v, rows_v, sem):
        wid = lax.axis_index("s") * SC_CORES + lax.axis_index("c")

        @pl.loop(0, per_w // SC_CHUNK)
        def _(j):
            base = wid * per_w + j * SC_CHUNK
            for i_hbm, o_hbm in ((i1_hbm, o1_hbm), (i2_hbm, o2_hbm)):
                pltpu.sync_copy(i_hbm.at[pl.ds(base, SC_CHUNK)], i_v)
                pltpu.async_copy(t_hbm.at[i_v], rows_v, sem).wait()
                pltpu.sync_copy(rows_v, o_hbm.at[pl.ds(base, SC_CHUNK)])

    return gather(table, idx1, idx2)


def _dest_kernel(route_ref, starts_ref, d1_ref, d2_ref):
    tm = route_ref.shape[0]
    route = route_ref[...]
    lane = lax.broadcasted_iota(jnp.int32, (tm, LANES), 1)
    row = lax.broadcasted_iota(jnp.int32, (tm, LANES), 0)

    def field(n):
        return jnp.sum(jnp.where(lane == n, route, 0.0), axis=-1, keepdims=True)

    for e_lane, pos_lane, out_ref in ((ROUTE_E1, ROUTE_POS1, d1_ref), (ROUTE_E2, ROUTE_POS2, d2_ref)):
        expert = field(e_lane).astype(jnp.int32)
        start = jnp.sum(jnp.where(lane == expert, starts_ref[...], 0.0), axis=-1, keepdims=True)
        dest = start + field(pos_lane)
        spread = jnp.where(lane == row % LANES, dest, 0.0)
        out_ref[...] = jnp.sum(spread.reshape(tm // LANES, LANES, LANES), axis=1).astype(jnp.int32)


def _routing_tables(route, cnt, tm, n_tiles_max):
    n = route.shape[0]
    counts = cnt[:, 0].astype(jnp.int32)
    padded = (counts + tm - 1) // tm * tm
    ends = jnp.cumsum(padded)
    starts = jnp.pad((ends - padded).astype(F32), (0, LANES - N_EXPERTS)).reshape(1, LANES)
    tr = min(1024, n)
    lane_dense = jax.ShapeDtypeStruct((n // LANES, LANES), jnp.int32)
    dest1, dest2 = pl.pallas_call(
        _dest_kernel,
        grid=(n // tr,),
        in_specs=[pl.BlockSpec((tr, LANES), lambda i: (i, 0)), pl.BlockSpec((1, LANES), lambda i: (0, 0))],
        out_specs=[pl.BlockSpec((tr // LANES, LANES), lambda i: (i, 0))] * 2,
        out_shape=[lane_dense, lane_dense],
        compiler_params=_params(("parallel",)),
        name="route_dest",
    )(route, starts)
    tile_start = jnp.arange(n_tiles_max, dtype=jnp.int32) * tm
    tile_expert = jnp.sum((ends[None, :] <= tile_start[:, None]).astype(jnp.int32), axis=1)
    tile_expert = jnp.minimum(tile_expert, N_EXPERTS - 1)
    n_tiles = (ends[-1] // tm).astype(jnp.int32).reshape(1)
    valid = tile_start < ends[-1]
    prev_expert = jnp.concatenate([jnp.full((1,), -1, jnp.int32), tile_expert[:-1]])
    first = (valid & (tile_expert != prev_expert)).astype(jnp.int32)
    slot = (jnp.cumsum(first) - 1) % 2
    e_idx = jnp.arange(N_EXPERTS, dtype=jnp.int32)
    later = (padded > 0)[None, :] & (e_idx[None, :] > e_idx[:, None])
    next_run = jnp.min(jnp.where(later, e_idx[None, :], N_EXPERTS), axis=1)
    next_run = jnp.where(next_run == N_EXPERTS, -1, next_run).astype(jnp.int32)
    plan = (tile_expert, first, slot.astype(jnp.int32), next_run[tile_expert], n_tiles)
    return dest1.reshape(n), dest2.reshape(n), plan


def _moe_tiles_kernel(te_ref, first_ref, slot_ref, nxt_ref, nt_ref, xs_ref, wg_hbm, wu_hbm, wd_hbm, ys_ref,
                      wg_buf, wu_buf, wd_buf, sem):
    t = pl.program_id(0)

    def weight_copies(expert, slot):
        return (pltpu.make_async_copy(wg_hbm.at[expert], wg_buf.at[slot], sem.at[slot, 0]),
                pltpu.make_async_copy(wu_hbm.at[expert], wu_buf.at[slot], sem.at[slot, 1]),
                pltpu.make_async_copy(wd_hbm.at[expert], wd_buf.at[slot], sem.at[slot, 2]))

    @pl.when(t < nt_ref[0])
    def _():
        slot = slot_ref[t]

        @pl.when(t == 0)
        def _():
            for c in weight_copies(te_ref[0], 0):
                c.start()

        @pl.when(first_ref[t] == 1)
        def _():
            for c in weight_copies(te_ref[t], slot):
                c.wait()

            @pl.when(nxt_ref[t] >= 0)
            def _():
                for c in weight_copies(nxt_ref[t], 1 - slot):
                    c.start()

        x = _unpack_bf16_pairs(xs_ref[...]).astype(BF16)
        hg = jnp.dot(x, wg_buf[slot], preferred_element_type=F32)
        hu = jnp.dot(x, wu_buf[slot], preferred_element_type=F32)
        hid = (hg * jax.nn.sigmoid(hg) * hu).astype(BF16)
        ys_ref[...] = _pack_bf16_pairs(jnp.dot(hid, wd_buf[slot], preferred_element_type=F32))


def _moe_tiles(xs, plan, wg, wu, wd, tm):
    r, dh = xs.shape
    ne, d, de = wg.shape
    row = lambda t, *_: (t, 0)
    hbm = pl.BlockSpec(memory_space=pl.ANY)
    return pl.pallas_call(
        _moe_tiles_kernel,
        grid_spec=pltpu.PrefetchScalarGridSpec(
            num_scalar_prefetch=len(plan),
            grid=(r // tm,),
            in_specs=[pl.BlockSpec((tm, dh), row), hbm, hbm, hbm],
            out_specs=pl.BlockSpec((tm, dh), row),
            scratch_shapes=[
                pltpu.VMEM((2, d, de), wg.dtype),
                pltpu.VMEM((2, d, de), wu.dtype),
                pltpu.VMEM((2, de, d), wd.dtype),
                pltpu.SemaphoreType.DMA((2, 3)),
            ],
        ),
        out_shape=jax.ShapeDtypeStruct((r, dh), jnp.int32),
        compiler_params=_params(("arbitrary",)),
        name="moe_tiles",
    )(*plan, xs, wg, wu, wd)


def _combine_kernel(h_ref, y1_ref, y2_ref, route_ref, fg_ref, o_ref):
    lane = lax.broadcasted_iota(jnp.int32, route_ref.shape, 1)
    route = route_ref[...]
    w1 = jnp.sum(jnp.where(lane == ROUTE_W1, route, 0.0), axis=-1, keepdims=True)
    w2 = jnp.sum(jnp.where(lane == ROUTE_W2, route, 0.0), axis=-1, keepdims=True)
    y = h_ref[...] + w1 * _unpack_bf16_pairs(y1_ref[...]) + w2 * _unpack_bf16_pairs(y2_ref[...])
    o_ref[...] = y * lax.rsqrt(jnp.mean(y * y, axis=-1, keepdims=True) + EPS) * fg_ref[...]


def _combine(h, y1, y2, route, fg, tm):
    n, d = h.shape
    row = lambda i: (i, 0)
    return pl.pallas_call(
        _combine_kernel,
        grid=(n // tm,),
        in_specs=[
            pl.BlockSpec((tm, d), row),
            pl.BlockSpec((tm, d // 2), row),
            pl.BlockSpec((tm, d // 2), row),
            pl.BlockSpec((tm, LANES), row),
            pl.BlockSpec((1, d), lambda i: (0, 0)),
        ],
        out_specs=pl.BlockSpec((tm, d), row),
        out_shape=jax.ShapeDtypeStruct((n, d), F32),
        compiler_params=_params(("parallel",)),
        name="moe_combine",
    )(h, y1, y2, route, fg)


def _moe(a2p, route, cnt, h, wg, wu, wd, fg, tm):
    n = h.shape[0]
    n_tiles_max = (2 * n) // tm + N_EXPERTS
    dest1, dest2, plan = _routing_tables(route, cnt, tm, n_tiles_max)
    xs = _sc_scatter_rows(a2p, dest1, dest2, n_tiles_max * tm)
    ys = _moe_tiles(xs, plan, wg, wu, wd, tm)
    y1, y2 = _sc_gather_rows(ys, dest1, dest2)
    return _combine(h, y1, y2, route, fg, tm=min(512, n))


def _dup_heads(w, n_heads):
    d = w.shape[0]
    w = w.reshape(d, n_heads, 1, HEAD_DIM)
    return jnp.broadcast_to(w, (d, n_heads, 2, HEAD_DIM)).reshape(d, n_heads * 2 * HEAD_DIM)


def _encoder(x, proj_meta, wts):
    b, s, d = x.shape
    x2d = x.reshape(b * s, d)
    proj = _norm_proj(x2d, wts["g1"], wts["w_ext"], tm=512).reshape(b, s, -1)
    bk = min(1024, s)
    od0 = _diff_attn(proj, proj_meta, wts["lam_vecs"], wts["subln_g"], heads=(0,), bq=512, bk=bk, skip=True)
    od1 = _diff_attn(proj, proj_meta, wts["lam_vecs"], wts["subln_g"], heads=tuple(range(1, DA_HEADS)),
                     bq=256, bk=bk, skip=False)
    os_ = _win_attn(proj, proj_meta, wts["sink"])
    h, a2p, route, cnt = _out_router(od0.reshape(b * s, -1), od1.reshape(b * s, -1), os_.reshape(b * s, -1), x2d,
                                     wts["wod"], wts["wos"],
                                     wts["g2"], wts["wr"], wts["br"], tm=512)
    y = _moe(a2p, route, cnt, h, wts["wg"], wts["wu"], wts["wd"], wts["fg"], tm=512)
    return y.reshape(b, s, d)


def kernel(x_prompt, x_sample, meta, norm1_g, w_in, lam_q1, lam_k1, lam_q2, lam_k2, subln_g, sink, w_out,
           norm2_g, w_gr, b_gr, w_er, b_er, w_gate, w_up, w_down, final_g):
    d = x_prompt.shape[-1]
    w = w_in[0]
    c_kd = 2 * DA_HEADS * HEAD_DIM
    c_vd = 2 * c_kd
    c_qs = c_vd + DA_HEADS * 2 * HEAD_DIM
    c_ks = c_qs + SW_HEADS * HEAD_DIM
    c_vs = c_ks + SW_KV_HEADS * HEAD_DIM
    w_ext = jnp.concatenate(
        [w[:, :c_ks], _dup_heads(w[:, c_ks:c_vs], SW_KV_HEADS), _dup_heads(w[:, c_vs:], SW_KV_HEADS)],
        axis=1).astype(BF16)
    w_router = jnp.concatenate([w_er[0], w_gr[0]], axis=1)
    w_router = jnp.pad(w_router, ((0, 0), (0, LANES - w_router.shape[1])))
    wr_hi = w_router.astype(BF16)
    wr_lo = (w_router - wr_hi.astype(F32)).astype(BF16)
    br = jnp.pad(jnp.concatenate([b_er[0], b_gr[0]]), (0, LANES - N_EXPERTS - N_GROUPS)).reshape(1, LANES)
    wo = w_out[0].astype(BF16)
    half = DA_HEADS * 2 * HEAD_DIM
    wts = dict(
        g1=norm1_g[0].reshape(1, d), w_ext=w_ext,
        lam_vecs=(lam_q1[0].reshape(1, -1), lam_k1[0].reshape(1, -1),
                  lam_q2[0].reshape(1, -1), lam_k2[0].reshape(1, -1)),
        subln_g=subln_g[0].reshape(1, -1), sink=sink[0].reshape(1, -1),
        wod=wo[:half], wos=wo[half:], g2=norm2_g[0].reshape(1, d),
        wr=jnp.concatenate([wr_hi, wr_lo], axis=1), br=br,
        wg=w_gate[0].astype(BF16), wu=w_up[0].astype(BF16), wd=w_down[0].astype(BF16),
        fg=final_g.reshape(1, d),
    )
    proj_meta = _norm_proj(meta, wts["g1"], w_ext, tm=N_META)
    proj_meta = jnp.pad(proj_meta, ((0, LANES - N_META), (0, 0)))
    return _encoder(x_prompt, proj_meta, wts), _encoder(x_sample, proj_meta, wts)
```

```python
import functools
import math

import jax
import jax.numpy as jnp
import numpy as np
from jax import lax
from jax.experimental import pallas as pl
from jax.experimental.pallas import tpu as pltpu
from jax.experimental.pallas import tpu_sc as plsc

F32 = jnp.float32
BF16 = jnp.bfloat16

N_META = 16
HEAD_DIM = 64
DA_HEADS = 4
SW_HEADS = 8
SW_KV_HEADS = 2
SW_GROUP = SW_HEADS // SW_KV_HEADS
WINDOW = 128
N_GROUPS = 4
EXPERTS_PER_GROUP = 8
N_EXPERTS = N_GROUPS * EXPERTS_PER_GROUP
EPS = 1e-6
SUBLN_EPS = 1e-5
NEG_INF = -1e30
LAM_INIT = 0.8 - 0.6 * math.exp(-0.3 * 0)
LANES = 128
VMEM_LIMIT = 48 * 1024 * 1024

COL_QD, COL_KD, COL_VD, COL_QS, COL_KS, COL_VS, N_COLBLK = 0, 4, 8, 12, 16, 18, 20
CONTRACT_LAST = (((1,), (1,)), ((), ()))


def _params(sem):
    return pltpu.CompilerParams(dimension_semantics=sem, vmem_limit_bytes=VMEM_LIMIT)


def _norm_proj_kernel(x_ref, g_ref, w_ref, o_ref):
    x = x_ref[...]
    ms = jnp.mean(x * x, axis=-1, keepdims=True)
    y = (x * lax.rsqrt(ms + EPS) * g_ref[...]).astype(BF16)
    o_ref[...] = jnp.dot(y, w_ref[...], preferred_element_type=F32).astype(o_ref.dtype)


def _norm_proj(x2d, g, w_ext, tm):
    n, d = x2d.shape
    wcols = w_ext.shape[1]
    return pl.pallas_call(
        _norm_proj_kernel,
        grid=(n // tm,),
        in_specs=[
            pl.BlockSpec((tm, d), lambda i: (i, 0)),
            pl.BlockSpec((1, d), lambda i: (0, 0)),
            pl.BlockSpec((d, wcols), lambda i: (0, 0)),
        ],
        out_specs=pl.BlockSpec((tm, wcols), lambda i: (i, 0)),
        out_shape=jax.ShapeDtypeStruct((n, wcols), BF16),
        compiler_params=_params(("parallel",)),
        name="norm_proj",
    )(x2d, g, w_ext)


POS_SPLIT = 64
N_SPLIT = 3
LOG2E = 1.4426950408889634


def _key_pos_features(seq):
    j = jnp.arange(seq, dtype=jnp.int32)[:, None]
    lane = jnp.arange(LANES, dtype=jnp.int32)[None, :]
    hi = (j // POS_SPLIT * POS_SPLIT).astype(F32)
    lo = (j % POS_SPLIT).astype(F32)
    feat = jnp.where(lane < N_SPLIT, 1.0,
                     jnp.where(lane < 2 * N_SPLIT, hi, jnp.where(lane < 3 * N_SPLIT, lo, 0.0)))
    return feat.astype(BF16)


def _split_bf16(x):
    pieces = []
    for _ in range(N_SPLIT):
        p = x.astype(BF16).astype(F32)
        pieces.append(p)
        x = x - p
    return pieces


SKIP_MARGIN = 135.0


def _diff_attn_head(h, q_ref, k_ref, v_ref, kf_ref, km_ref, vm_ref, lam, sg_ref,
                    o_ref, qe_ref, acc_ref, m_ref, sd_ref, s1_ref, kn_ref, *, bq, bk, seq, skip):
    i = pl.program_id(1)
    slope = 2.0 ** (-8.0 * (h + 1) / DA_HEADS) * LOG2E
    lane = lax.broadcasted_iota(jnp.int32, (1, LANES), 1)
    q = (q_ref[...].astype(F32) * (LOG2E / math.sqrt(HEAD_DIM))).astype(BF16)
    zero = jnp.zeros_like(q)
    qmaps = (jnp.where(lane < HEAD_DIM, q, zero), jnp.where(lane >= HEAD_DIM, q, zero))
    qpos = i * bq + lax.broadcasted_iota(jnp.int32, (bq, 1), 0)
    row_pieces = _split_bf16(-slope * qpos.astype(F32))
    slope_pieces = _split_bf16(jnp.full((1, 1), slope, F32))
    feat = jnp.zeros((bq, LANES), F32)
    for n in range(N_SPLIT):
        feat = jnp.where(lane == n, row_pieces[n], feat)
        feat = jnp.where(lane == N_SPLIT + n, slope_pieces[n], feat)
        feat = jnp.where(lane == 2 * N_SPLIT + n, slope_pieces[n], feat)
    feats = (feat.astype(BF16), (-feat).astype(BF16))
    for side in range(2):
        for c in range(2):
            qe_ref[side, c * bq:(c + 1) * bq, :LANES] = qmaps[c]
            qe_ref[side, c * bq:(c + 1) * bq, LANES:] = feats[side]

    ones_blk = jnp.broadcast_to(jnp.where(lane == 0, 1.0, 0.0).astype(BF16), (bk, LANES))
    meta_mask = jnp.where(lane < N_META, 0.0, NEG_INF)

    def attend(s, v_ext):
        m_old = m_ref[...]
        m_new = jnp.maximum(m_old, jnp.max(s, axis=-1, keepdims=True))
        alpha = jnp.exp2(m_old - m_new)
        p = jnp.exp2(s - jnp.tile(m_new, (1, s.shape[1] // LANES))).astype(BF16)
        acc_ref[...] = jnp.tile(alpha, (1, 2)) * acc_ref[...] + jnp.dot(p, v_ext, preferred_element_type=F32)
        m_ref[...] = m_new

    m_ref[...] = jnp.full(m_ref.shape, NEG_INF, F32)
    acc_ref[...] = jnp.zeros(acc_ref.shape, F32)

    s0_ref = sd_ref.at[:, :bk]
    nblk = seq // bk
    diag = (i * bq) // bk

    def scores_block(dst_ref, kb, side, edge=None):
        start = pl.multiple_of(kb * bk, bk)
        k_ext = jnp.concatenate([k_ref[pl.ds(start, bk), :], kf_ref[pl.ds(start, bk), :]], axis=1)
        s = lax.dot_general(qe_ref[side], k_ext, CONTRACT_LAST, preferred_element_type=F32)
        dst_ref[...] = s if edge is None else s + edge

    def consume_block(src_ref, kb):
        start = pl.multiple_of(kb * bk, bk)
        attend(src_ref[...], jnp.concatenate([v_ref[pl.ds(start, bk), :], ones_blk], axis=1))

    d_start = pl.multiple_of(diag * bk, bk)
    kpos = diag * bk + lax.broadcasted_iota(jnp.int32, (1, bk), 1)
    bias = jnp.concatenate([-slope * jnp.abs(qpos - kpos).astype(F32),
                            jnp.broadcast_to(meta_mask, (bq, LANES))], axis=1)
    k_first = jnp.concatenate([k_ref[pl.ds(d_start, bk), :], km_ref[...]], axis=0)
    v_first = jnp.concatenate(
        [jnp.concatenate([v_ref[pl.ds(d_start, bk), :], vm_ref[...]], axis=0),
         jnp.broadcast_to(ones_blk[:1], (bk + LANES, LANES))], axis=1)
    sd_ref[...] = lax.dot_general(qe_ref[0, :, :LANES], k_first, CONTRACT_LAST,
                                  preferred_element_type=F32) + jnp.tile(bias, (2, 1))

    def all_blocks():
        def key_block(pos):
            t = pos - 1
            return t + jnp.where(t >= diag, 1, 0)

        def scores_into(dst_ref, pos):
            kb = key_block(pos)
            scores_block(dst_ref, kb, jnp.where(kb > diag, 1, 0))

        scores_into(s1_ref, 1)
        attend(sd_ref[...], v_first)
        for u in range((nblk - 2) // 2):
            scores_into(s0_ref, 2 * u + 2)
            consume_block(s1_ref, key_block(2 * u + 1))
            scores_into(s1_ref, 2 * u + 3)
            consume_block(s0_ref, key_block(2 * u + 2))
        consume_block(s1_ref, key_block(nblk - 1))

    def neighbours_only():
        prev = jnp.maximum(diag - 1, 0)
        nxt = jnp.minimum(diag + 1, nblk - 1)
        scores_block(s1_ref, prev, 0, edge=jnp.where(diag == 0, NEG_INF, 0.0))
        attend(sd_ref[...], v_first)
        scores_block(s0_ref, nxt, 1, edge=jnp.where(diag == nblk - 1, NEG_INF, 0.0))
        consume_block(s1_ref, prev)
        consume_block(s0_ref, nxt)

    if nblk == 1:
        attend(sd_ref[...], v_first)
    elif skip:
        assert bq == bk and nblk > 2
        @pl.when(i == 0)
        def _():
            kf32 = k_ref[...].astype(F32)
            kn_ref[0] = jnp.max(jnp.sum(kf32 * kf32, axis=-1, keepdims=True))
        qf32 = q.astype(F32)
        qn2 = jnp.max(jnp.sum(qf32 * qf32, axis=-1, keepdims=True), axis=0, keepdims=True)
        s_meta = lax.dot_general(qe_ref[0, :, :LANES], km_ref[...], CONTRACT_LAST,
                                 preferred_element_type=F32) + meta_mask
        m_low = jnp.min(jnp.max(s_meta, axis=-1, keepdims=True), axis=0, keepdims=True)
        reach = (jnp.sqrt(qn2 * kn_ref[0]) - m_low + SKIP_MARGIN) * (1.0 / slope)
        near = jnp.clip(reach, 0.0, float(seq))[0, 0] <= float(bk)

        @pl.when(near)
        def _():
            neighbours_only()

        @pl.when(jnp.logical_not(near))
        def _():
            all_blocks()
    else:
        all_blocks()

    o1 = acc_ref[:bq, :LANES] / acc_ref[:bq, LANES:LANES + 1]
    o2 = acc_ref[bq:, :LANES] / acc_ref[bq:, LANES:LANES + 1]
    o = o1 - lam * o2
    o = o * lax.rsqrt(jnp.mean(o * o, axis=-1, keepdims=True) + SUBLN_EPS) * sg_ref[...]
    o_ref[...] = (o * (1.0 - LAM_INIT)).astype(o_ref.dtype)


def _diff_attn_kernel(*refs, heads, **kw):
    nh = len(heads)
    q_refs, k_refs, v_refs = refs[:nh], refs[nh:2 * nh], refs[2 * nh:3 * nh]
    kf_ref = refs[3 * nh]
    km_refs, vm_refs = refs[3 * nh + 1:4 * nh + 1], refs[4 * nh + 1:5 * nh + 1]
    lq1_ref, lk1_ref, lq2_ref, lk2_ref, sg_ref, o_ref, qe_ref, acc_ref, m_ref, sd_ref, s1_ref, kn_ref = refs[5 * nh + 1:]
    lam = (jnp.exp(jnp.sum(lq1_ref[...] * lk1_ref[...], axis=-1, keepdims=True))
           - jnp.exp(jnp.sum(lq2_ref[...] * lk2_ref[...], axis=-1, keepdims=True)) + LAM_INIT)
    for n, h in enumerate(heads):
        _diff_attn_head(h, q_refs[n], k_refs[n], v_refs[n], kf_ref, km_refs[n], vm_refs[n], lam, sg_ref,
                        o_ref.at[:, n * LANES:(n + 1) * LANES],
                        qe_ref.at[n], acc_ref.at[n], m_ref.at[n], sd_ref.at[n], s1_ref.at[n], kn_ref, **kw)


def _diff_attn(proj, proj_meta, lam_vecs, subln_g, heads, bq, bk, skip):
    b, s, _ = proj.shape
    bq = min(bq, s)
    assert bk % bq == 0 and s % bk == 0 and (s // bk == 1 or (s // bk) % 2 == 0)
    nh = len(heads)
    vec = pl.BlockSpec((1, HEAD_DIM), lambda bi, i: (0, 0))
    q_specs = [pl.BlockSpec((None, bq, LANES), lambda bi, i, c=COL_QD + h: (bi, i, c)) for h in heads]
    k_specs = [pl.BlockSpec((None, s, LANES), lambda bi, i, c=COL_KD + h: (bi, 0, c)) for h in heads]
    v_specs = [pl.BlockSpec((None, s, LANES), lambda bi, i, c=COL_VD + h: (bi, 0, c)) for h in heads]
    km_specs = [pl.BlockSpec((LANES, LANES), lambda bi, i, c=COL_KD + h: (0, c)) for h in heads]
    vm_specs = [pl.BlockSpec((LANES, LANES), lambda bi, i, c=COL_VD + h: (0, c)) for h in heads]
    return pl.pallas_call(
        functools.partial(_diff_attn_kernel, heads=heads, bq=bq, bk=bk, seq=s, skip=skip),
        grid=(b, s // bq),
        in_specs=(q_specs + k_specs + v_specs + [pl.BlockSpec((s, LANES), lambda bi, i: (0, 0))]
                  + km_specs + vm_specs + [vec, vec, vec, vec, pl.BlockSpec((1, LANES), lambda bi, i: (0, 0))]),
        out_specs=pl.BlockSpec((None, bq, nh * LANES), lambda bi, i: (bi, i, 0)),
        out_shape=jax.ShapeDtypeStruct((b, s, nh * LANES), BF16),
        scratch_shapes=[
            pltpu.VMEM((nh, 2, 2 * bq, 2 * LANES), BF16),
            pltpu.VMEM((nh, 2 * bq, 2 * LANES), F32),
            pltpu.VMEM((nh, 2 * bq, LANES), F32),
            pltpu.VMEM((nh, 2 * bq, bk + LANES), F32),
            pltpu.VMEM((nh, 2 * bq, bk), F32),
            pltpu.SMEM((1,), F32),
        ],
        compiler_params=_params(("parallel", "arbitrary")),
        name="diff_attn",
    )(*([proj] * (3 * nh)), _key_pos_features(s), *([proj_meta] * (2 * nh)), *lam_vecs, subln_g)


WIN_KEYS = 4 * WINDOW
WIN_QBLOCKS = 4


def _win_tables():
    r = np.arange(WINDOW)
    qf = np.zeros((SW_KV_HEADS, SW_GROUP * WINDOW, LANES), np.float32)
    for head in range(SW_HEADS):
        slope = 2.0 ** (-8.0 * (head + 1) / SW_HEADS)
        i_rel = WINDOW + r
        hi, lo = i_rel // POS_SPLIT * POS_SPLIT, i_rel % POS_SPLIT
        rows = qf[head // SW_GROUP, (head % SW_GROUP) * WINDOW:(head % SW_GROUP + 1) * WINDOW]
        rows[:, 0], rows[:, 1], rows[:, 2], rows[:, 3] = -slope * hi, -slope * lo, slope, slope
        rows[:, 4:8] = -rows[:, 0:4]
    kf = np.zeros((2, WINDOW, LANES), np.float32)
    for n, (blk, right) in enumerate(((0, 0), (2, 1))):
        j_rel = blk * WINDOW + r
        o = 4 * right
        kf[n, :, o], kf[n, :, o + 1] = 1.0, 1.0
        kf[n, :, o + 2], kf[n, :, o + 3] = j_rel // POS_SPLIT * POS_SPLIT, j_rel % POS_SPLIT
    mask = np.zeros((2, WINDOW, WINDOW), np.float32)
    mask[0] = np.where(r[None, :] >= r[:, None], 0.0, NEG_INF)
    mask[1] = np.where(r[None, :] <= r[:, None], 0.0, NEG_INF)
    cur = np.zeros((SW_KV_HEADS, SW_GROUP * WINDOW, WINDOW), np.float32)
    for head in range(SW_HEADS):
        slope = 2.0 ** (-8.0 * (head + 1) / SW_HEADS)
        cur[head // SW_GROUP, (head % SW_GROUP) * WINDOW:(head % SW_GROUP + 1) * WINDOW] = (
            -slope * np.abs(r[:, None] - r[None, :]))
    return jnp.asarray(qf, BF16), jnp.asarray(kf, BF16), jnp.asarray(mask, F32), jnp.asarray(cur, F32)


def _win_attn_kernel(q_ref, kp_ref, kc_ref, kn_ref, vp_ref, vc_ref, vn_ref, km_ref, vm_ref, qf_ref, kf_ref,
                     mask_ref, cur_ref, sink_ref, o_ref, *, nstep):
    c = pl.program_id(1)
    lane = lax.broadcasted_iota(jnp.int32, (1, LANES), 1)
    scale = jnp.asarray(1.0 / math.sqrt(HEAD_DIM), BF16)
    mask_meta = jnp.where(lane < N_META, 0.0, NEG_INF)
    row = lax.broadcasted_iota(jnp.int32, (SW_GROUP * WINDOW, 1), 0)
    ones_blk = jnp.broadcast_to(jnp.where(lane == 0, 1.0, 0.0).astype(BF16), (WIN_KEYS, LANES))
    zeros_blk = jnp.zeros((WINDOW, LANES), BF16)
    gw = SW_GROUP * HEAD_DIM
    own = [slice(j * WINDOW, (j + 1) * WINDOW) for j in range(WIN_QBLOCKS)]
    kblk = [kp_ref] + [kc_ref.at[r] for r in own] + [kn_ref]
    vblk = [vp_ref] + [vc_ref.at[r] for r in own] + [vn_ref]
    for j in range(WIN_QBLOCKS):
        rq = slice(j * WINDOW, (j + 1) * WINDOW)
        edge_p = jnp.where(c == 0, NEG_INF, 0.0) if j == 0 else 0.0
        edge_n = jnp.where(c == nstep - 1, NEG_INF, 0.0) if j == WIN_QBLOCKS - 1 else 0.0
        mask_prev = jnp.tile(mask_ref[0] + edge_p, (SW_GROUP, 1))
        mask_next = jnp.tile(mask_ref[1] + edge_n, (SW_GROUP, 1))
        for g in range(SW_KV_HEADS):
            kv = slice(g * LANES, (g + 1) * LANES)
            rows = []
            for hh in range(SW_GROUP):
                col = g * gw + (hh // 2) * LANES
                qb = q_ref[rq, col:col + LANES] * scale
                keep = (lane < HEAD_DIM) if hh % 2 == 0 else (lane >= HEAD_DIM)
                rows.append(jnp.where(keep, qb, jnp.zeros_like(qb)))
            q_ext = jnp.concatenate([jnp.concatenate(rows, axis=0), qf_ref[g]], axis=1)
            k_all = jnp.concatenate([
                jnp.concatenate([km_ref[:, kv], zeros_blk], axis=1),
                jnp.concatenate([kblk[j][:, kv], kf_ref[0]], axis=1),
                jnp.concatenate([kblk[j + 1][:, kv], zeros_blk], axis=1),
                jnp.concatenate([kblk[j + 2][:, kv], kf_ref[1]], axis=1)], axis=0)
            s = lax.dot_general(q_ext, k_all, CONTRACT_LAST, preferred_element_type=F32)
            s_meta = s[:, :LANES] + mask_meta
            s_prev = s[:, LANES:2 * LANES] + mask_prev
            s_cur = s[:, 2 * LANES:3 * LANES] + cur_ref[g]
            s_next = s[:, 3 * LANES:] + mask_next
            sink = jnp.zeros((SW_GROUP * WINDOW, 1), F32)
            for hh in range(SW_GROUP):
                sink = jnp.where(row // WINDOW == hh, sink_ref[0, g * SW_GROUP + hh], sink)
            parts = (s_meta, s_prev, s_cur, s_next)
            m = jnp.maximum(jnp.maximum(s_meta, s_prev), jnp.maximum(s_cur, s_next))
            m = jnp.maximum(jnp.max(m, axis=-1, keepdims=True), sink)
            p = jnp.concatenate([jnp.exp(x - m) for x in parts], axis=1).astype(BF16)
            v_all = jnp.concatenate(
                [jnp.concatenate([vm_ref[:, kv], vblk[j][:, kv], vblk[j + 1][:, kv], vblk[j + 2][:, kv]], axis=0),
                 ones_blk], axis=1)
            acc = jnp.dot(p, v_all, preferred_element_type=F32)
            denom = acc[:, LANES:LANES + 1] + jnp.exp(sink - m)
            o = acc[:, :LANES] / denom
            for lb in range(SW_GROUP // 2):
                even = o[(2 * lb) * WINDOW:(2 * lb + 1) * WINDOW]
                odd = o[(2 * lb + 1) * WINDOW:(2 * lb + 2) * WINDOW]
                col = g * gw + lb * LANES
                o_ref[rq, col:col + LANES] = jnp.where(lane < HEAD_DIM, even, odd).astype(o_ref.dtype)


def _win_attn(proj, proj_meta, sink):
    b, s, _ = proj.shape
    nblk = s // WINDOW
    nq = WIN_QBLOCKS
    nstep = nblk // nq
    qw = SW_HEADS * HEAD_DIM
    kvw = SW_KV_HEADS * LANES
    qf, kf, mask, cur = _win_tables()

    def edge_spec(col, shift):
        return pl.BlockSpec(
            (None, WINDOW, kvw),
            lambda bi, c: (bi, jnp.clip(nq * c + shift, 0, nblk - 1), col * LANES // kvw))

    def pair_spec(col):
        return pl.BlockSpec((None, nq * WINDOW, kvw), lambda bi, c: (bi, c, col * LANES // kvw))

    return pl.pallas_call(
        functools.partial(_win_attn_kernel, nstep=nstep),
        grid=(b, nstep),
        in_specs=[
            pl.BlockSpec((None, nq * WINDOW, qw), lambda bi, c: (bi, c, COL_QS * LANES // qw)),
            edge_spec(COL_KS, -1), pair_spec(COL_KS), edge_spec(COL_KS, nq),
            edge_spec(COL_VS, -1), pair_spec(COL_VS), edge_spec(COL_VS, nq),
            pl.BlockSpec((LANES, kvw), lambda bi, c: (0, COL_KS * LANES // kvw)),
            pl.BlockSpec((LANES, kvw), lambda bi, c: (0, COL_VS * LANES // kvw)),
            pl.BlockSpec((SW_KV_HEADS, SW_GROUP * WINDOW, LANES), lambda bi, c: (0, 0, 0)),
            pl.BlockSpec((2, WINDOW, LANES), lambda bi, c: (0, 0, 0)),
            pl.BlockSpec((2, WINDOW, WINDOW), lambda bi, c: (0, 0, 0)),
            pl.BlockSpec((SW_KV_HEADS, SW_GROUP * WINDOW, WINDOW), lambda bi, c: (0, 0, 0)),
            pl.BlockSpec(memory_space=pltpu.SMEM),
        ],
        out_specs=pl.BlockSpec((None, nq * WINDOW, qw), lambda bi, c: (bi, c, 0)),
        out_shape=jax.ShapeDtypeStruct((b, s, qw), BF16),
        compiler_params=_params(("parallel", "arbitrary")),
        name="win_attn",
    )(proj, proj, proj, proj, proj, proj, proj, proj_meta, proj_meta, qf, kf, mask, cur, sink)


ROUTE_E1, ROUTE_E2, ROUTE_POS1, ROUTE_POS2, ROUTE_W1, ROUTE_W2 = range(6)


def _pack_bf16_pairs(x):
    k = x.shape[1] // 2
    bits = lax.bitcast_convert_type(x.astype(BF16).astype(F32), jnp.uint32)
    return lax.bitcast_convert_type(bits[:, :k] | (bits[:, k:] >> 16), jnp.int32)


def _unpack_bf16_pairs(w):
    bits = lax.bitcast_convert_type(w, jnp.uint32)
    hi = lax.bitcast_convert_type(bits & jnp.uint32(0xFFFF0000), F32)
    lo = lax.bitcast_convert_type(bits << 16, F32)
    return jnp.concatenate([hi, lo], axis=1)


def _out_router_kernel(od0_ref, od1_ref, os_ref, x_ref, wod_ref, wos_ref, g2_ref, wr_ref, br_ref,
                       h_ref, a_ref, route_ref, cnt_ref, base_ref):
    @pl.when(pl.program_id(0) == 0)
    def _():
        base_ref[...] = jnp.zeros_like(base_ref)

    h = (x_ref[...]
         + jnp.dot(jnp.concatenate([od0_ref[...], od1_ref[...]], axis=1), wod_ref[...],
                   preferred_element_type=F32)
         + jnp.dot(os_ref[...], wos_ref[...], preferred_element_type=F32))
    h_ref[...] = h
    a = h * lax.rsqrt(jnp.mean(h * h, axis=-1, keepdims=True) + EPS) * g2_ref[...]
    a_hi = a.astype(BF16)
    a_ref[...] = _pack_bf16_pairs(a)
    a_lo = (a - a_hi.astype(F32)).astype(BF16)
    hi_terms = jnp.dot(a_hi, wr_ref[...], preferred_element_type=F32)
    logits = (hi_terms[:, :LANES] + hi_terms[:, LANES:]
              + jnp.dot(a_lo, wr_ref[:, :LANES], preferred_element_type=F32)) + br_ref[...]
    tm = logits.shape[0]
    lt = jnp.transpose(logits)
    le = lt[:N_EXPERTS]
    sub_g = lax.broadcasted_iota(jnp.int32, (8, tm), 0)
    lg = jnp.where(sub_g < N_GROUPS, lt[N_EXPERTS:N_EXPERTS + 8], NEG_INF)
    sub_e = lax.broadcasted_iota(jnp.int32, (N_EXPERTS, tm), 0)

    def first_argmax(vals, mask, sub, size):
        mx = jnp.max(vals, axis=0, keepdims=True)
        idx = jnp.min(jnp.where(mask & (vals == mx), sub, size), axis=0, keepdims=True)
        return mx, idx

    gmax, gidx = first_argmax(lg, sub_g < N_GROUPS, sub_g, 8)
    g_w = 1.0 / jnp.sum(jnp.exp(lg - gmax), axis=0, keepdims=True)
    in_group = (sub_e // EXPERTS_PER_GROUP) == gidx
    m1, i1 = first_argmax(jnp.where(in_group, le, NEG_INF), in_group, sub_e, N_EXPERTS)
    rest = in_group & (sub_e != i1)
    m2, i2 = first_argmax(jnp.where(rest, le, NEG_INF), rest, sub_e, N_EXPERTS)
    r = jnp.exp(m2 - m1)
    w1 = g_w / (1.0 + r)
    w2 = g_w * r / (1.0 + r)

    rix = lax.broadcasted_iota(jnp.int32, (tm, tm), 0)
    cix = lax.broadcasted_iota(jnp.int32, (tm, tm), 1)
    earlier = jnp.where(rix < cix, 1.0, 0.0).astype(BF16)
    oh1 = jnp.where(sub_e == i1, 1.0, 0.0)
    oh2 = jnp.where(sub_e == i2, 1.0, 0.0)
    before = jnp.dot(jnp.concatenate([oh1, oh2], axis=0).astype(BF16), earlier, preferred_element_type=F32)
    base1 = base_ref[:, :1]
    pos1 = jnp.sum(oh1 * (base1 + before[:N_EXPERTS]), axis=0, keepdims=True)
    base2 = base1 + jnp.sum(oh1, axis=1, keepdims=True)
    pos2 = jnp.sum(oh2 * (base2 + before[N_EXPERTS:]), axis=0, keepdims=True)
    total = jnp.broadcast_to(base2 + jnp.sum(oh2, axis=1, keepdims=True), base_ref.shape)
    base_ref[...] = total
    cnt_ref[...] = total

    fields = jnp.concatenate([i1.astype(F32), i2.astype(F32), pos1, pos2, w1, w2, jnp.zeros((2, tm), F32)], axis=0)
    route_ref[...] = jnp.transpose(jnp.concatenate([fields, jnp.zeros((LANES - 8, tm), F32)], axis=0))


def _out_router(od0, od1, os_, x2d, wod, wos, g2, wr, br, tm):
    n, d = x2d.shape
    half = os_.shape[1]
    row = lambda i: (i, 0)
    const = lambda i: (0, 0)
    return pl.pallas_call(
        _out_router_kernel,
        grid=(n // tm,),
        in_specs=[
            pl.BlockSpec((tm, od0.shape[1]), row),
            pl.BlockSpec((tm, od1.shape[1]), row),
            pl.BlockSpec((tm, half), row),
            pl.BlockSpec((tm, d), row),
            pl.BlockSpec((half, d), const),
            pl.BlockSpec((half, d), const),
            pl.BlockSpec((1, d), const),
            pl.BlockSpec((d, 2 * LANES), const),
            pl.BlockSpec((1, LANES), const),
        ],
        out_specs=[
            pl.BlockSpec((tm, d), row),
            pl.BlockSpec((tm, d // 2), row),
            pl.BlockSpec((tm, LANES), row),
            pl.BlockSpec((N_EXPERTS, LANES), const),
        ],
        out_shape=[
            jax.ShapeDtypeStruct((n, d), F32),
            jax.ShapeDtypeStruct((n, d // 2), jnp.int32),
            jax.ShapeDtypeStruct((n, LANES), F32),
            jax.ShapeDtypeStruct((N_EXPERTS, LANES), F32),
        ],
        scratch_shapes=[pltpu.VMEM((N_EXPERTS, LANES), F32)],
        compiler_params=_params(("arbitrary",)),
        name="out_router",
    )(od0, od1, os_, x2d, wod, wos, g2, wr, br)


SC_CORES, SC_SUBCORES = 2, 16
SC_WORKERS = SC_CORES * SC_SUBCORES
SC_CHUNK = 128


def _sc_mesh():
    return plsc.VectorSubcoreMesh(core_axis_name="c", subcore_axis_name="s",
                                  num_cores=SC_CORES, num_subcores=SC_SUBCORES)


def _sc_scatter_rows(x, idx1, idx2, n_out):
    n, d = x.shape
    assert n % (SC_WORKERS * SC_CHUNK) == 0
    per_w = n // SC_WORKERS

    @functools.partial(
        pl.kernel, mesh=_sc_mesh(), out_type=jax.ShapeDtypeStruct((n_out, d), x.dtype),
        scratch_types=[pltpu.VMEM((SC_CHUNK,), jnp.int32), pltpu.VMEM((SC_CHUNK,), jnp.int32),
                       pltpu.VMEM((SC_CHUNK, d), x.dtype), pltpu.SemaphoreType.DMA],
        name="sc_scatter_rows")
    def scatter(x_hbm, i1_hbm, i2_hbm, o_hbm, i1_v, i2_v, rows_v, sem):
        wid = lax.axis_index("s") * SC_CORES + lax.axis_index("c")

        @pl.loop(0, per_w // SC_CHUNK)
        def _(j):
            base = wid * per_w + j * SC_CHUNK
            pltpu.sync_copy(i1_hbm.at[pl.ds(base, SC_CHUNK)], i1_v)
            pltpu.sync_copy(i2_hbm.at[pl.ds(base, SC_CHUNK)], i2_v)
            pltpu.sync_copy(x_hbm.at[pl.ds(base, SC_CHUNK)], rows_v)
            pltpu.async_copy(rows_v, o_hbm.at[i1_v], sem).wait()
            pltpu.async_copy(rows_v, o_hbm.at[i2_v], sem).wait()

    return scatter(x, idx1, idx2)


def _sc_gather_rows(table, idx1, idx2):
    n = idx1.shape[0]
    d = table.shape[1]
    assert n % (SC_WORKERS * SC_CHUNK) == 0
    per_w = n // SC_WORKERS
    out = jax.ShapeDtypeStruct((n, d), table.dtype)

    @functools.partial(
        pl.kernel, mesh=_sc_mesh(), out_type=(out, out),
        scratch_types=[pltpu.VMEM((SC_CHUNK,), jnp.int32), pltpu.VMEM((SC_CHUNK, d), table.dtype),
                       pltpu.SemaphoreType.DMA],
        name="sc_gather_rows")
    def gather(t_hbm, i1_hbm, i2_hbm, o1_hbm, o2_hbm, i_v, rows_v, sem):
        wid = lax.axis_index("s") * SC_CORES + lax.axis_index("c")

        @pl.loop(0, per_w // SC_CHUNK)
        def _(j):
            base = wid * per_w + j * SC_CHUNK
            for i_hbm, o_hbm in ((i1_hbm, o1_hbm), (i2_hbm, o2_hbm)):
                pltpu.sync_copy(i_hbm.at[pl.ds(base, SC_CHUNK)], i_v)
                pltpu.async_copy(t_hbm.at[i_v], rows_v, sem).wait()
                pltpu.sync_copy(rows_v, o_hbm.at[pl.ds(base, SC_CHUNK)])

    return gather(table, idx1, idx2)


def _dest_kernel(route_ref, starts_ref, d1_ref, d2_ref):
    tm = route_ref.shape[0]
    route = route_ref[...]
    lane = lax.broadcasted_iota(jnp.int32, (tm, LANES), 1)
    row = lax.broadcasted_iota(jnp.int32, (tm, LANES), 0)

    def field(n):
        return jnp.sum(jnp.where(lane == n, route, 0.0), axis=-1, keepdims=True)

    for e_lane, pos_lane, out_ref in ((ROUTE_E1, ROUTE_POS1, d1_ref), (ROUTE_E2, ROUTE_POS2, d2_ref)):
        expert = field(e_lane).astype(jnp.int32)
        start = jnp.sum(jnp.where(lane == expert, starts_ref[...], 0.0), axis=-1, keepdims=True)
        dest = start + field(pos_lane)
        spread = jnp.where(lane == row % LANES, dest, 0.0)
        out_ref[...] = jnp.sum(spread.reshape(tm // LANES, LANES, LANES), axis=1).astype(jnp.int32)


def _routing_tables(route, cnt, tm, n_tiles_max):
    n = route.shape[0]
    counts = cnt[:, 0].astype(jnp.int32)
    padded = (counts + tm - 1) // tm * tm
    ends = jnp.cumsum(padded)
    starts = jnp.pad((ends - padded).astype(F32), (0, LANES - N_EXPERTS)).reshape(1, LANES)
    tr = min(1024, n)
    lane_dense = jax.ShapeDtypeStruct((n // LANES, LANES), jnp.int32)
    dest1, dest2 = pl.pallas_call(
        _dest_kernel,
        grid=(n // tr,),
        in_specs=[pl.BlockSpec((tr, LANES), lambda i: (i, 0)), pl.BlockSpec((1, LANES), lambda i: (0, 0))],
        out_specs=[pl.BlockSpec((tr // LANES, LANES), lambda i: (i, 0))] * 2,
        out_shape=[lane_dense, lane_dense],
        compiler_params=_params(("parallel",)),
        name="route_dest",
    )(route, starts)
    tile_start = jnp.arange(n_tiles_max, dtype=jnp.int32) * tm
    tile_expert = jnp.sum((ends[None, :] <= tile_start[:, None]).astype(jnp.int32), axis=1)
    tile_expert = jnp.minimum(tile_expert, N_EXPERTS - 1)
    n_tiles = (ends[-1] // tm).astype(jnp.int32).reshape(1)
    valid = tile_start < ends[-1]
    prev_expert = jnp.concatenate([jnp.full((1,), -1, jnp.int32), tile_expert[:-1]])
    first = (valid & (tile_expert != prev_expert)).astype(jnp.int32)
    slot = (jnp.cumsum(first) - 1) % 2
    e_idx = jnp.arange(N_EXPERTS, dtype=jnp.int32)
    later = (padded > 0)[None, :] & (e_idx[None, :] > e_idx[:, None])
    next_run = jnp.min(jnp.where(later, e_idx[None, :], N_EXPERTS), axis=1)
    next_run = jnp.where(next_run == N_EXPERTS, -1, next_run).astype(jnp.int32)
    plan = (tile_expert, first, slot.astype(jnp.int32), next_run[tile_expert], n_tiles)
    return dest1.reshape(n), dest2.reshape(n), plan


def _moe_tiles_kernel(te_ref, first_ref, slot_ref, nxt_ref, nt_ref, xs_ref, wg_hbm, wu_hbm, wd_hbm, ys_ref,
                      wg_buf, wu_buf, wd_buf, sem):
    t = pl.program_id(0)

    def weight_copies(expert, slot):
        return (pltpu.make_async_copy(wg_hbm.at[expert], wg_buf.at[slot], sem.at[slot, 0]),
                pltpu.make_async_copy(wu_hbm.at[expert], wu_buf.at[slot], sem.at[slot, 1]),
                pltpu.make_async_copy(wd_hbm.at[expert], wd_buf.at[slot], sem.at[slot, 2]))

    @pl.when(t < nt_ref[0])
    def _():
        slot = slot_ref[t]

        @pl.when(t == 0)
        def _():
            for c in weight_copies(te_ref[0], 0):
                c.start()

        @pl.when(first_ref[t] == 1)
        def _():
            for c in weight_copies(te_ref[t], slot):
                c.wait()

            @pl.when(nxt_ref[t] >= 0)
            def _():
                for c in weight_copies(nxt_ref[t], 1 - slot):
                    c.start()

        x = _unpack_bf16_pairs(xs_ref[...]).astype(BF16)
        hg = jnp.dot(x, wg_buf[slot], preferred_element_type=F32)
        hu = jnp.dot(x, wu_buf[slot], preferred_element_type=F32)
        hid = (hg * jax.nn.sigmoid(hg) * hu).astype(BF16)
        ys_ref[...] = _pack_bf16_pairs(jnp.dot(hid, wd_buf[slot], preferred_element_type=F32))


def _moe_tiles(xs, plan, wg, wu, wd, tm):
    r, dh = xs.shape
    ne, d, de = wg.shape
    row = lambda t, *_: (t, 0)
    hbm = pl.BlockSpec(memory_space=pl.ANY)
    return pl.pallas_call(
        _moe_tiles_kernel,
        grid_spec=pltpu.PrefetchScalarGridSpec(
            num_scalar_prefetch=len(plan),
            grid=(r // tm,),
            in_specs=[pl.BlockSpec((tm, dh), row), hbm, hbm, hbm],
            out_specs=pl.BlockSpec((tm, dh), row),
            scratch_shapes=[
                pltpu.VMEM((2, d, de), wg.dtype),
                pltpu.VMEM((2, d, de), wu.dtype),
                pltpu.VMEM((2, de, d), wd.dtype),
                pltpu.SemaphoreType.DMA((2, 3)),
            ],
        ),
        out_shape=jax.ShapeDtypeStruct((r, dh), jnp.int32),
        compiler_params=_params(("arbitrary",)),
        name="moe_tiles",
    )(*plan, xs, wg, wu, wd)


def _combine_kernel(h_ref, y1_ref, y2_ref, route_ref, fg_ref, o_ref):
    lane = lax.broadcasted_iota(jnp.int32, route_ref.shape, 1)
    route = route_ref[...]
    w1 = jnp.sum(jnp.where(lane == ROUTE_W1, route, 0.0), axis=-1, keepdims=True)
    w2 = jnp.sum(jnp.where(lane == ROUTE_W2, route, 0.0), axis=-1, keepdims=True)
    y = h_ref[...] + w1 * _unpack_bf16_pairs(y1_ref[...]) + w2 * _unpack_bf16_pairs(y2_ref[...])
    o_ref[...] = y * lax.rsqrt(jnp.mean(y * y, axis=-1, keepdims=True) + EPS) * fg_ref[...]


def _combine(h, y1, y2, route, fg, tm):
    n, d = h.shape
    row = lambda i: (i, 0)
    return pl.pallas_call(
        _combine_kernel,
        grid=(n // tm,),
        in_specs=[
            pl.BlockSpec((tm, d), row),
            pl.BlockSpec((tm, d // 2), row),
            pl.BlockSpec((tm, d // 2), row),
            pl.BlockSpec((tm, LANES), row),
            pl.BlockSpec((1, d), lambda i: (0, 0)),
        ],
        out_specs=pl.BlockSpec((tm, d), row),
        out_shape=jax.ShapeDtypeStruct((n, d), F32),
        compiler_params=_params(("parallel",)),
        name="moe_combine",
    )(h, y1, y2, route, fg)


def _moe(a2p, route, cnt, h, wg, wu, wd, fg, tm):
    n = h.shape[0]
    n_tiles_max = (2 * n) // tm + N_EXPERTS
    dest1, dest2, plan = _routing_tables(route, cnt, tm, n_tiles_max)
    xs = _sc_scatter_rows(a2p, dest1, dest2, n_tiles_max * tm)
    ys = _moe_tiles(xs, plan, wg, wu, wd, tm)
    y1, y2 = _sc_gather_rows(ys, dest1, dest2)
    return _combine(h, y1, y2, route, fg, tm=min(512, n))


def _dup_heads(w, n_heads):
    d = w.shape[0]
    w = w.reshape(d, n_heads, 1, HEAD_DIM)
    return jnp.broadcast_to(w, (d, n_heads, 2, HEAD_DIM)).reshape(d, n_heads * 2 * HEAD_DIM)


def _encoder(x, proj_meta, wts):
    b, s, d = x.shape
    x2d = x.reshape(b * s, d)
    proj = _norm_proj(x2d, wts["g1"], wts["w_ext"], tm=512).reshape(b, s, -1)
    bk = min(1024, s)
    od0 = _diff_attn(proj, proj_meta, wts["lam_vecs"], wts["subln_g"], heads=(0,), bq=512, bk=min(512, s),
                     skip=True)
    od1 = _diff_attn(proj, proj_meta, wts["lam_vecs"], wts["subln_g"], heads=tuple(range(1, DA_HEADS)),
                     bq=256, bk=bk, skip=False)
    os_ = _win_attn(proj, proj_meta, wts["sink"])
    h, a2p, route, cnt = _out_router(od0.reshape(b * s, -1), od1.reshape(b * s, -1), os_.reshape(b * s, -1), x2d,
                                     wts["wod"], wts["wos"],
                                     wts["g2"], wts["wr"], wts["br"], tm=512)
    y = _moe(a2p, route, cnt, h, wts["wg"], wts["wu"], wts["wd"], wts["fg"], tm=512)
    return y.reshape(b, s, d)


def kernel(x_prompt, x_sample, meta, norm1_g, w_in, lam_q1, lam_k1, lam_q2, lam_k2, subln_g, sink, w_out,
           norm2_g, w_gr, b_gr, w_er, b_er, w_gate, w_up, w_down, final_g):
    d = x_prompt.shape[-1]
    w = w_in[0]
    c_kd = 2 * DA_HEADS * HEAD_DIM
    c_vd = 2 * c_kd
    c_qs = c_vd + DA_HEADS * 2 * HEAD_DIM
    c_ks = c_qs + SW_HEADS * HEAD_DIM
    c_vs = c_ks + SW_KV_HEADS * HEAD_DIM
    w_ext = jnp.concatenate(
        [w[:, :c_ks], _dup_heads(w[:, c_ks:c_vs], SW_KV_HEADS), _dup_heads(w[:, c_vs:], SW_KV_HEADS)],
        axis=1).astype(BF16)
    w_router = jnp.concatenate([w_er[0], w_gr[0]], axis=1)
    w_router = jnp.pad(w_router, ((0, 0), (0, LANES - w_router.shape[1])))
    wr_hi = w_router.astype(BF16)
    wr_lo = (w_router - wr_hi.astype(F32)).astype(BF16)
    br = jnp.pad(jnp.concatenate([b_er[0], b_gr[0]]), (0, LANES - N_EXPERTS - N_GROUPS)).reshape(1, LANES)
    wo = w_out[0].astype(BF16)
    half = DA_HEADS * 2 * HEAD_DIM
    wts = dict(
        g1=norm1_g[0].reshape(1, d), w_ext=w_ext,
        lam_vecs=(lam_q1[0].reshape(1, -1), lam_k1[0].reshape(1, -1),
                  lam_q2[0].reshape(1, -1), lam_k2[0].reshape(1, -1)),
        subln_g=subln_g[0].reshape(1, -1), sink=sink[0].reshape(1, -1),
        wod=wo[:half], wos=wo[half:], g2=norm2_g[0].reshape(1, d),
        wr=jnp.concatenate([wr_hi, wr_lo], axis=1), br=br,
        wg=w_gate[0].astype(BF16), wu=w_up[0].astype(BF16), wd=w_down[0].astype(BF16),
        fg=final_g.reshape(1, d),
    )
    proj_meta = _norm_proj(meta, wts["g1"], w_ext, tm=N_META)
    proj_meta = jnp.pad(proj_meta, ((0, LANES - N_META), (0, 0)))
    return _encoder(x_prompt, proj_meta, wts), _encoder(x_sample, proj_meta, wts)
```

```python
import functools
import math

import jax
import jax.numpy as jnp
import numpy as np
from jax import lax
from jax.experimental import pallas as pl
from jax.experimental.pallas import tpu as pltpu
from jax.experimental.pallas import tpu_sc as plsc

F32 = jnp.float32
BF16 = jnp.bfloat16

N_META = 16
HEAD_DIM = 64
DA_HEADS = 4
SW_HEADS = 8
SW_KV_HEADS = 2
SW_GROUP = SW_HEADS // SW_KV_HEADS
WINDOW = 128
N_GROUPS = 4
EXPERTS_PER_GROUP = 8
N_EXPERTS = N_GROUPS * EXPERTS_PER_GROUP
EPS = 1e-6
SUBLN_EPS = 1e-5
NEG_INF = -1e30
LAM_INIT = 0.8 - 0.6 * math.exp(-0.3 * 0)
LANES = 128
VMEM_LIMIT = 48 * 1024 * 1024

COL_QD, COL_KD, COL_VD, COL_QS, COL_KS, COL_VS, N_COLBLK = 0, 4, 8, 12, 16, 18, 20
CONTRACT_LAST = (((1,), (1,)), ((), ()))


def _params(sem):
    return pltpu.CompilerParams(dimension_semantics=sem, vmem_limit_bytes=VMEM_LIMIT)


def _norm_proj_kernel(x_ref, g_ref, w_ref, o_ref):
    x = x_ref[...]
    ms = jnp.mean(x * x, axis=-1, keepdims=True)
    y = (x * lax.rsqrt(ms + EPS) * g_ref[...]).astype(BF16)
    o_ref[...] = jnp.dot(y, w_ref[...], preferred_element_type=F32).astype(o_ref.dtype)


def _norm_proj(x2d, g, w_ext, tm):
    n, d = x2d.shape
    wcols = w_ext.shape[1]
    return pl.pallas_call(
        _norm_proj_kernel,
        grid=(n // tm,),
        in_specs=[
            pl.BlockSpec((tm, d), lambda i: (i, 0)),
            pl.BlockSpec((1, d), lambda i: (0, 0)),
            pl.BlockSpec((d, wcols), lambda i: (0, 0)),
        ],
        out_specs=pl.BlockSpec((tm, wcols), lambda i: (i, 0)),
        out_shape=jax.ShapeDtypeStruct((n, wcols), BF16),
        compiler_params=_params(("parallel",)),
        name="norm_proj",
    )(x2d, g, w_ext)


POS_SPLIT = 64
N_SPLIT = 3
LOG2E = 1.4426950408889634


def _key_pos_features(seq):
    j = jnp.arange(seq, dtype=jnp.int32)[:, None]
    lane = jnp.arange(LANES, dtype=jnp.int32)[None, :]
    hi = (j // POS_SPLIT * POS_SPLIT).astype(F32)
    lo = (j % POS_SPLIT).astype(F32)
    feat = jnp.where(lane < N_SPLIT, 1.0,
                     jnp.where(lane < 2 * N_SPLIT, hi, jnp.where(lane < 3 * N_SPLIT, lo, 0.0)))
    return feat.astype(BF16)


def _split_bf16(x):
    pieces = []
    for _ in range(N_SPLIT):
        p = x.astype(BF16).astype(F32)
        pieces.append(p)
        x = x - p
    return pieces


SKIP_MARGIN = 135.0


def _diff_attn_head(h, q_ref, k_ref, v_ref, kf_ref, km_ref, vm_ref, lam, sg_ref,
                    o_ref, qe_ref, acc_ref, m_ref, sd_ref, s1_ref, kn_ref, *, bq, bk, seq, skip):
    i = pl.program_id(1)
    slope = 2.0 ** (-8.0 * (h + 1) / DA_HEADS) * LOG2E
    lane = lax.broadcasted_iota(jnp.int32, (1, LANES), 1)
    q = (q_ref[...].astype(F32) * (LOG2E / math.sqrt(HEAD_DIM))).astype(BF16)
    zero = jnp.zeros_like(q)
    qmaps = (jnp.where(lane < HEAD_DIM, q, zero), jnp.where(lane >= HEAD_DIM, q, zero))
    qpos = i * bq + lax.broadcasted_iota(jnp.int32, (bq, 1), 0)
    row_pieces = _split_bf16(-slope * qpos.astype(F32))
    slope_pieces = _split_bf16(jnp.full((1, 1), slope, F32))
    feat = jnp.zeros((bq, LANES), F32)
    for n in range(N_SPLIT):
        feat = jnp.where(lane == n, row_pieces[n], feat)
        feat = jnp.where(lane == N_SPLIT + n, slope_pieces[n], feat)
        feat = jnp.where(lane == 2 * N_SPLIT + n, slope_pieces[n], feat)
    feats = (feat.astype(BF16), (-feat).astype(BF16))
    for side in range(2):
        for c in range(2):
            qe_ref[side, c * bq:(c + 1) * bq, :LANES] = qmaps[c]
            qe_ref[side, c * bq:(c + 1) * bq, LANES:] = feats[side]

    ones_blk = jnp.broadcast_to(jnp.where(lane == 0, 1.0, 0.0).astype(BF16), (bk, LANES))
    meta_mask = jnp.where(lane < N_META, 0.0, NEG_INF)

    def attend(s, v_ext):
        m_old = m_ref[...]
        m_new = jnp.maximum(m_old, jnp.max(s, axis=-1, keepdims=True))
        alpha = jnp.exp2(m_old - m_new)
        p = jnp.exp2(s - jnp.tile(m_new, (1, s.shape[1] // LANES))).astype(BF16)
        acc_ref[...] = jnp.tile(alpha, (1, 2)) * acc_ref[...] + jnp.dot(p, v_ext, preferred_element_type=F32)
        m_ref[...] = m_new

    m_ref[...] = jnp.full(m_ref.shape, NEG_INF, F32)
    acc_ref[...] = jnp.zeros(acc_ref.shape, F32)

    s0_ref = sd_ref.at[:, :bk]
    nblk = seq // bk
    diag = (i * bq) // bk

    def scores_block(dst_ref, kb, side, edge=None):
        start = pl.multiple_of(kb * bk, bk)
        k_ext = jnp.concatenate([k_ref[pl.ds(start, bk), :], kf_ref[pl.ds(start, bk), :]], axis=1)
        s = lax.dot_general(qe_ref[side], k_ext, CONTRACT_LAST, preferred_element_type=F32)
        dst_ref[...] = s if edge is None else s + edge

    def consume_block(src_ref, kb):
        start = pl.multiple_of(kb * bk, bk)
        attend(src_ref[...], jnp.concatenate([v_ref[pl.ds(start, bk), :], ones_blk], axis=1))

    d_start = pl.multiple_of(diag * bk, bk)
    kpos = diag * bk + lax.broadcasted_iota(jnp.int32, (1, bk), 1)
    bias = jnp.concatenate([-slope * jnp.abs(qpos - kpos).astype(F32),
                            jnp.broadcast_to(meta_mask, (bq, LANES))], axis=1)
    k_first = jnp.concatenate([k_ref[pl.ds(d_start, bk), :], km_ref[...]], axis=0)
    v_first = jnp.concatenate(
        [jnp.concatenate([v_ref[pl.ds(d_start, bk), :], vm_ref[...]], axis=0),
         jnp.broadcast_to(ones_blk[:1], (bk + LANES, LANES))], axis=1)
    sd_ref[...] = lax.dot_general(qe_ref[0, :, :LANES], k_first, CONTRACT_LAST,
                                  preferred_element_type=F32) + jnp.tile(bias, (2, 1))

    def all_blocks():
        def key_block(pos):
            t = pos - 1
            return t + jnp.where(t >= diag, 1, 0)

        def scores_into(dst_ref, pos):
            kb = key_block(pos)
            scores_block(dst_ref, kb, jnp.where(kb > diag, 1, 0))

        scores_into(s1_ref, 1)
        attend(sd_ref[...], v_first)
        for u in range((nblk - 2) // 2):
            scores_into(s0_ref, 2 * u + 2)
            consume_block(s1_ref, key_block(2 * u + 1))
            scores_into(s1_ref, 2 * u + 3)
            consume_block(s0_ref, key_block(2 * u + 2))
        consume_block(s1_ref, key_block(nblk - 1))

    def neighbours_only():
        prev = jnp.maximum(diag - 1, 0)
        nxt = jnp.minimum(diag + 1, nblk - 1)
        scores_block(s1_ref, prev, 0, edge=jnp.where(diag == 0, NEG_INF, 0.0))
        attend(sd_ref[...], v_first)
        scores_block(s0_ref, nxt, 1, edge=jnp.where(diag == nblk - 1, NEG_INF, 0.0))
        consume_block(s1_ref, prev)
        consume_block(s0_ref, nxt)

    if nblk == 1:
        attend(sd_ref[...], v_first)
    elif skip:
        assert bq == bk and nblk > 2
        @pl.when(i == 0)
        def _():
            kf32 = k_ref[...].astype(F32)
            kn_ref[0] = jnp.max(jnp.sum(kf32 * kf32, axis=-1, keepdims=True))
        qf32 = q.astype(F32)
        qn2 = jnp.max(jnp.sum(qf32 * qf32, axis=-1, keepdims=True), axis=0, keepdims=True)
        s_meta = lax.dot_general(qe_ref[0, :, :LANES], km_ref[...], CONTRACT_LAST,
                                 preferred_element_type=F32) + meta_mask
        m_low = jnp.min(jnp.max(s_meta, axis=-1, keepdims=True), axis=0, keepdims=True)
        reach = (jnp.sqrt(qn2 * kn_ref[0]) - m_low + SKIP_MARGIN) * (1.0 / slope)
        near = jnp.clip(reach, 0.0, float(seq))[0, 0] <= float(bk)

        @pl.when(near)
        def _():
            neighbours_only()

        @pl.when(jnp.logical_not(near))
        def _():
            all_blocks()
    else:
        all_blocks()

    o1 = acc_ref[:bq, :LANES] / acc_ref[:bq, LANES:LANES + 1]
    o2 = acc_ref[bq:, :LANES] / acc_ref[bq:, LANES:LANES + 1]
    o = o1 - lam * o2
    o = o * lax.rsqrt(jnp.mean(o * o, axis=-1, keepdims=True) + SUBLN_EPS) * sg_ref[...]
    o_ref[...] = (o * (1.0 - LAM_INIT)).astype(o_ref.dtype)


def _diff_attn_kernel(*refs, heads, **kw):
    nh = len(heads)
    q_refs, k_refs, v_refs = refs[:nh], refs[nh:2 * nh], refs[2 * nh:3 * nh]
    kf_ref = refs[3 * nh]
    km_refs, vm_refs = refs[3 * nh + 1:4 * nh + 1], refs[4 * nh + 1:5 * nh + 1]
    lq1_ref, lk1_ref, lq2_ref, lk2_ref, sg_ref, o_ref, qe_ref, acc_ref, m_ref, sd_ref, s1_ref, kn_ref = refs[5 * nh + 1:]
    lam = (jnp.exp(jnp.sum(lq1_ref[...] * lk1_ref[...], axis=-1, keepdims=True))
           - jnp.exp(jnp.sum(lq2_ref[...] * lk2_ref[...], axis=-1, keepdims=True)) + LAM_INIT)
    for n, h in enumerate(heads):
        _diff_attn_head(h, q_refs[n], k_refs[n], v_refs[n], kf_ref, km_refs[n], vm_refs[n], lam, sg_ref,
                        o_ref.at[:, n * LANES:(n + 1) * LANES],
                        qe_ref.at[n], acc_ref.at[n], m_ref.at[n], sd_ref.at[n], s1_ref.at[n], kn_ref, **kw)


def _diff_attn(proj, proj_meta, lam_vecs, subln_g, heads, bq, bk, skip):
    b, s, _ = proj.shape
    bq = min(bq, s)
    assert bk % bq == 0 and s % bk == 0 and (s // bk == 1 or (s // bk) % 2 == 0)
    nh = len(heads)
    vec = pl.BlockSpec((1, HEAD_DIM), lambda bi, i: (0, 0))
    q_specs = [pl.BlockSpec((None, bq, LANES), lambda bi, i, c=COL_QD + h: (bi, i, c)) for h in heads]
    k_specs = [pl.BlockSpec((None, s, LANES), lambda bi, i, c=COL_KD + h: (bi, 0, c)) for h in heads]
    v_specs = [pl.BlockSpec((None, s, LANES), lambda bi, i, c=COL_VD + h: (bi, 0, c)) for h in heads]
    km_specs = [pl.BlockSpec((LANES, LANES), lambda bi, i, c=COL_KD + h: (0, c)) for h in heads]
    vm_specs = [pl.BlockSpec((LANES, LANES), lambda bi, i, c=COL_VD + h: (0, c)) for h in heads]
    return pl.pallas_call(
        functools.partial(_diff_attn_kernel, heads=heads, bq=bq, bk=bk, seq=s, skip=skip),
        grid=(b, s // bq),
        in_specs=(q_specs + k_specs + v_specs + [pl.BlockSpec((s, LANES), lambda bi, i: (0, 0))]
                  + km_specs + vm_specs + [vec, vec, vec, vec, pl.BlockSpec((1, LANES), lambda bi, i: (0, 0))]),
        out_specs=pl.BlockSpec((None, bq, nh * LANES), lambda bi, i: (bi, i, 0)),
        out_shape=jax.ShapeDtypeStruct((b, s, nh * LANES), BF16),
        scratch_shapes=[
            pltpu.VMEM((nh, 2, 2 * bq, 2 * LANES), BF16),
            pltpu.VMEM((nh, 2 * bq, 2 * LANES), F32),
            pltpu.VMEM((nh, 2 * bq, LANES), F32),
            pltpu.VMEM((nh, 2 * bq, bk + LANES), F32),
            pltpu.VMEM((nh, 2 * bq, bk), F32),
            pltpu.SMEM((1,), F32),
        ],
        compiler_params=_params(("parallel", "arbitrary")),
        name="diff_attn",
    )(*([proj] * (3 * nh)), _key_pos_features(s), *([proj_meta] * (2 * nh)), *lam_vecs, subln_g)


WIN_KEYS = 4 * WINDOW
WIN_QBLOCKS = 4


def _win_tables():
    r = np.arange(WINDOW)
    qf = np.zeros((SW_KV_HEADS, SW_GROUP * WINDOW, LANES), np.float32)
    for head in range(SW_HEADS):
        slope = 2.0 ** (-8.0 * (head + 1) / SW_HEADS)
        i_rel = WINDOW + r
        hi, lo = i_rel // POS_SPLIT * POS_SPLIT, i_rel % POS_SPLIT
        rows = qf[head // SW_GROUP, (head % SW_GROUP) * WINDOW:(head % SW_GROUP + 1) * WINDOW]
        rows[:, 0], rows[:, 1], rows[:, 2], rows[:, 3] = -slope * hi, -slope * lo, slope, slope
        rows[:, 4:8] = -rows[:, 0:4]
    kf = np.zeros((2, WINDOW, LANES), np.float32)
    for n, (blk, right) in enumerate(((0, 0), (2, 1))):
        j_rel = blk * WINDOW + r
        o = 4 * right
        kf[n, :, o], kf[n, :, o + 1] = 1.0, 1.0
        kf[n, :, o + 2], kf[n, :, o + 3] = j_rel // POS_SPLIT * POS_SPLIT, j_rel % POS_SPLIT
    mask = np.zeros((2, WINDOW, WINDOW), np.float32)
    mask[0] = np.where(r[None, :] >= r[:, None], 0.0, NEG_INF)
    mask[1] = np.where(r[None, :] <= r[:, None], 0.0, NEG_INF)
    cur = np.zeros((SW_KV_HEADS, SW_GROUP * WINDOW, WINDOW), np.float32)
    for head in range(SW_HEADS):
        slope = 2.0 ** (-8.0 * (head + 1) / SW_HEADS)
        cur[head // SW_GROUP, (head % SW_GROUP) * WINDOW:(head % SW_GROUP + 1) * WINDOW] = (
            -slope * np.abs(r[:, None] - r[None, :]))
    return jnp.asarray(qf, BF16), jnp.asarray(kf, BF16), jnp.asarray(mask, F32), jnp.asarray(cur, F32)


def _win_attn_kernel(q_ref, kp_ref, kc_ref, kn_ref, vp_ref, vc_ref, vn_ref, km_ref, vm_ref, qf_ref, kf_ref,
                     mask_ref, cur_ref, sink_ref, o_ref, *, nstep):
    c = pl.program_id(1)
    lane = lax.broadcasted_iota(jnp.int32, (1, LANES), 1)
    scale = jnp.asarray(1.0 / math.sqrt(HEAD_DIM), BF16)
    mask_meta = jnp.where(lane < N_META, 0.0, NEG_INF)
    row = lax.broadcasted_iota(jnp.int32, (SW_GROUP * WINDOW, 1), 0)
    ones_blk = jnp.broadcast_to(jnp.where(lane == 0, 1.0, 0.0).astype(BF16), (WIN_KEYS, LANES))
    zeros_blk = jnp.zeros((WINDOW, LANES), BF16)
    gw = SW_GROUP * HEAD_DIM
    own = [slice(j * WINDOW, (j + 1) * WINDOW) for j in range(WIN_QBLOCKS)]
    kblk = [kp_ref] + [kc_ref.at[r] for r in own] + [kn_ref]
    vblk = [vp_ref] + [vc_ref.at[r] for r in own] + [vn_ref]
    for j in range(WIN_QBLOCKS):
        rq = slice(j * WINDOW, (j + 1) * WINDOW)
        edge_p = jnp.where(c == 0, NEG_INF, 0.0) if j == 0 else 0.0
        edge_n = jnp.where(c == nstep - 1, NEG_INF, 0.0) if j == WIN_QBLOCKS - 1 else 0.0
        mask_prev = jnp.tile(mask_ref[0] + edge_p, (SW_GROUP, 1))
        mask_next = jnp.tile(mask_ref[1] + edge_n, (SW_GROUP, 1))
        for g in range(SW_KV_HEADS):
            kv = slice(g * LANES, (g + 1) * LANES)
            rows = []
            for hh in range(SW_GROUP):
                col = g * gw + (hh // 2) * LANES
                qb = q_ref[rq, col:col + LANES] * scale
                keep = (lane < HEAD_DIM) if hh % 2 == 0 else (lane >= HEAD_DIM)
                rows.append(jnp.where(keep, qb, jnp.zeros_like(qb)))
            q_ext = jnp.concatenate([jnp.concatenate(rows, axis=0), qf_ref[g]], axis=1)
            k_all = jnp.concatenate([
                jnp.concatenate([km_ref[:, kv], zeros_blk], axis=1),
                jnp.concatenate([kblk[j][:, kv], kf_ref[0]], axis=1),
                jnp.concatenate([kblk[j + 1][:, kv], zeros_blk], axis=1),
                jnp.concatenate([kblk[j + 2][:, kv], kf_ref[1]], axis=1)], axis=0)
            s = lax.dot_general(q_ext, k_all, CONTRACT_LAST, preferred_element_type=F32)
            s_meta = s[:, :LANES] + mask_meta
            s_prev = s[:, LANES:2 * LANES] + mask_prev
            s_cur = s[:, 2 * LANES:3 * LANES] + cur_ref[g]
            s_next = s[:, 3 * LANES:] + mask_next
            sink = jnp.zeros((SW_GROUP * WINDOW, 1), F32)
            for hh in range(SW_GROUP):
                sink = jnp.where(row // WINDOW == hh, sink_ref[0, g * SW_GROUP + hh], sink)
            parts = (s_meta, s_prev, s_cur, s_next)
            m = jnp.maximum(jnp.maximum(s_meta, s_prev), jnp.maximum(s_cur, s_next))
            m = jnp.maximum(jnp.max(m, axis=-1, keepdims=True), sink)
            p = jnp.concatenate([jnp.exp(x - m) for x in parts], axis=1).astype(BF16)
            v_all = jnp.concatenate(
                [jnp.concatenate([vm_ref[:, kv], vblk[j][:, kv], vblk[j + 1][:, kv], vblk[j + 2][:, kv]], axis=0),
                 ones_blk], axis=1)
            acc = jnp.dot(p, v_all, preferred_element_type=F32)
            denom = acc[:, LANES:LANES + 1] + jnp.exp(sink - m)
            o = acc[:, :LANES] / denom
            for lb in range(SW_GROUP // 2):
                even = o[(2 * lb) * WINDOW:(2 * lb + 1) * WINDOW]
                odd = o[(2 * lb + 1) * WINDOW:(2 * lb + 2) * WINDOW]
                col = g * gw + lb * LANES
                o_ref[rq, col:col + LANES] = jnp.where(lane < HEAD_DIM, even, odd).astype(o_ref.dtype)


def _win_attn(proj, proj_meta, sink):
    b, s, _ = proj.shape
    nblk = s // WINDOW
    nq = WIN_QBLOCKS
    nstep = nblk // nq
    qw = SW_HEADS * HEAD_DIM
    kvw = SW_KV_HEADS * LANES
    qf, kf, mask, cur = _win_tables()

    def edge_spec(col, shift):
        return pl.BlockSpec(
            (None, WINDOW, kvw),
            lambda bi, c: (bi, jnp.clip(nq * c + shift, 0, nblk - 1), col * LANES // kvw))

    def pair_spec(col):
        return pl.BlockSpec((None, nq * WINDOW, kvw), lambda bi, c: (bi, c, col * LANES // kvw))

    return pl.pallas_call(
        functools.partial(_win_attn_kernel, nstep=nstep),
        grid=(b, nstep),
        in_specs=[
            pl.BlockSpec((None, nq * WINDOW, qw), lambda bi, c: (bi, c, COL_QS * LANES // qw)),
            edge_spec(COL_KS, -1), pair_spec(COL_KS), edge_spec(COL_KS, nq),
            edge_spec(COL_VS, -1), pair_spec(COL_VS), edge_spec(COL_VS, nq),
            pl.BlockSpec((LANES, kvw), lambda bi, c: (0, COL_KS * LANES // kvw)),
            pl.BlockSpec((LANES, kvw), lambda bi, c: (0, COL_VS * LANES // kvw)),
            pl.BlockSpec((SW_KV_HEADS, SW_GROUP * WINDOW, LANES), lambda bi, c: (0, 0, 0)),
            pl.BlockSpec((2, WINDOW, LANES), lambda bi, c: (0, 0, 0)),
            pl.BlockSpec((2, WINDOW, WINDOW), lambda bi, c: (0, 0, 0)),
            pl.BlockSpec((SW_KV_HEADS, SW_GROUP * WINDOW, WINDOW), lambda bi, c: (0, 0, 0)),
            pl.BlockSpec(memory_space=pltpu.SMEM),
        ],
        out_specs=pl.BlockSpec((None, nq * WINDOW, qw), lambda bi, c: (bi, c, 0)),
        out_shape=jax.ShapeDtypeStruct((b, s, qw), BF16),
        compiler_params=_params(("parallel", "arbitrary")),
        name="win_attn",
    )(proj, proj, proj, proj, proj, proj, proj, proj_meta, proj_meta, qf, kf, mask, cur, sink)


ROUTE_E1, ROUTE_E2, ROUTE_POS1, ROUTE_POS2, ROUTE_W1, ROUTE_W2 = range(6)


def _pack_bf16_pairs(x):
    k = x.shape[1] // 2
    bits = lax.bitcast_convert_type(x.astype(BF16).astype(F32), jnp.uint32)
    return lax.bitcast_convert_type(bits[:, :k] | (bits[:, k:] >> 16), jnp.int32)


def _unpack_bf16_pairs(w):
    bits = lax.bitcast_convert_type(w, jnp.uint32)
    hi = lax.bitcast_convert_type(bits & jnp.uint32(0xFFFF0000), F32)
    lo = lax.bitcast_convert_type(bits << 16, F32)
    return jnp.concatenate([hi, lo], axis=1)


def _out_router_kernel(od0_ref, od1_ref, os_ref, x_ref, wod_ref, wos_ref, g2_ref, wr_ref, br_ref,
                       h_ref, a_ref, route_ref, rt_ref, cnt_ref, base_ref):
    @pl.when(pl.program_id(0) == 0)
    def _():
        base_ref[...] = jnp.zeros_like(base_ref)

    h = (x_ref[...]
         + jnp.dot(jnp.concatenate([od0_ref[...], od1_ref[...]], axis=1), wod_ref[...],
                   preferred_element_type=F32)
         + jnp.dot(os_ref[...], wos_ref[...], preferred_element_type=F32))
    h_ref[...] = h
    a = h * lax.rsqrt(jnp.mean(h * h, axis=-1, keepdims=True) + EPS) * g2_ref[...]
    a_hi = a.astype(BF16)
    a_ref[...] = _pack_bf16_pairs(a)
    a_lo = (a - a_hi.astype(F32)).astype(BF16)
    hi_terms = jnp.dot(a_hi, wr_ref[...], preferred_element_type=F32)
    logits = (hi_terms[:, :LANES] + hi_terms[:, LANES:]
              + jnp.dot(a_lo, wr_ref[:, :LANES], preferred_element_type=F32)) + br_ref[...]
    tm = logits.shape[0]
    lt = jnp.transpose(logits)
    le = lt[:N_EXPERTS]
    sub_g = lax.broadcasted_iota(jnp.int32, (8, tm), 0)
    lg = jnp.where(sub_g < N_GROUPS, lt[N_EXPERTS:N_EXPERTS + 8], NEG_INF)
    sub_e = lax.broadcasted_iota(jnp.int32, (N_EXPERTS, tm), 0)

    def first_argmax(vals, mask, sub, size):
        mx = jnp.max(vals, axis=0, keepdims=True)
        idx = jnp.min(jnp.where(mask & (vals == mx), sub, size), axis=0, keepdims=True)
        return mx, idx

    gmax, gidx = first_argmax(lg, sub_g < N_GROUPS, sub_g, 8)
    g_w = 1.0 / jnp.sum(jnp.exp(lg - gmax), axis=0, keepdims=True)
    in_group = (sub_e // EXPERTS_PER_GROUP) == gidx
    m1, i1 = first_argmax(jnp.where(in_group, le, NEG_INF), in_group, sub_e, N_EXPERTS)
    rest = in_group & (sub_e != i1)
    m2, i2 = first_argmax(jnp.where(rest, le, NEG_INF), rest, sub_e, N_EXPERTS)
    r = jnp.exp(m2 - m1)
    w1 = g_w / (1.0 + r)
    w2 = g_w * r / (1.0 + r)

    rix = lax.broadcasted_iota(jnp.int32, (tm, tm), 0)
    cix = lax.broadcasted_iota(jnp.int32, (tm, tm), 1)
    earlier = jnp.where(rix < cix, 1.0, 0.0).astype(BF16)
    oh1 = jnp.where(sub_e == i1, 1.0, 0.0)
    oh2 = jnp.where(sub_e == i2, 1.0, 0.0)
    before = jnp.dot(jnp.concatenate([oh1, oh2], axis=0).astype(BF16), earlier, preferred_element_type=F32)
    base1 = base_ref[:, :1]
    pos1 = jnp.sum(oh1 * (base1 + before[:N_EXPERTS]), axis=0, keepdims=True)
    base2 = base1 + jnp.sum(oh1, axis=1, keepdims=True)
    pos2 = jnp.sum(oh2 * (base2 + before[N_EXPERTS:]), axis=0, keepdims=True)
    total = jnp.broadcast_to(base2 + jnp.sum(oh2, axis=1, keepdims=True), base_ref.shape)
    base_ref[...] = total
    cnt_ref[...] = total

    fields = jnp.concatenate([i1.astype(F32), i2.astype(F32), pos1, pos2, w1, w2, jnp.zeros((2, tm), F32)], axis=0)
    rt_ref[...] = fields
    route_ref[...] = jnp.transpose(jnp.concatenate([fields, jnp.zeros((LANES - 8, tm), F32)], axis=0))


def _out_router(od0, od1, os_, x2d, wod, wos, g2, wr, br, tm):
    n, d = x2d.shape
    half = os_.shape[1]
    row = lambda i: (i, 0)
    const = lambda i: (0, 0)
    return pl.pallas_call(
        _out_router_kernel,
        grid=(n // tm,),
        in_specs=[
            pl.BlockSpec((tm, od0.shape[1]), row),
            pl.BlockSpec((tm, od1.shape[1]), row),
            pl.BlockSpec((tm, half), row),
            pl.BlockSpec((tm, d), row),
            pl.BlockSpec((half, d), const),
            pl.BlockSpec((half, d), const),
            pl.BlockSpec((1, d), const),
            pl.BlockSpec((d, 2 * LANES), const),
            pl.BlockSpec((1, LANES), const),
        ],
        out_specs=[
            pl.BlockSpec((tm, d), row),
            pl.BlockSpec((tm, d // 2), row),
            pl.BlockSpec((tm, LANES), row),
            pl.BlockSpec((8, tm), lambda i: (0, i)),
            pl.BlockSpec((N_EXPERTS, LANES), const),
        ],
        out_shape=[
            jax.ShapeDtypeStruct((n, d), F32),
            jax.ShapeDtypeStruct((n, d // 2), jnp.int32),
            jax.ShapeDtypeStruct((n, LANES), F32),
            jax.ShapeDtypeStruct((8, n), F32),
            jax.ShapeDtypeStruct((N_EXPERTS, LANES), F32),
        ],
        scratch_shapes=[pltpu.VMEM((N_EXPERTS, LANES), F32)],
        compiler_params=_params(("arbitrary",)),
        name="out_router",
    )(od0, od1, os_, x2d, wod, wos, g2, wr, br)


SC_CORES, SC_SUBCORES = 2, 16
SC_WORKERS = SC_CORES * SC_SUBCORES
SC_CHUNK = 128


def _sc_mesh():
    return plsc.VectorSubcoreMesh(core_axis_name="c", subcore_axis_name="s",
                                  num_cores=SC_CORES, num_subcores=SC_SUBCORES)


def _sc_scatter_rows(x, idx1, idx2, n_out):
    n, d = x.shape
    assert n % (SC_WORKERS * SC_CHUNK) == 0
    per_w = n // SC_WORKERS

    @functools.partial(
        pl.kernel, mesh=_sc_mesh(), out_type=jax.ShapeDtypeStruct((n_out, d), x.dtype),
        scratch_types=[pltpu.VMEM((SC_CHUNK,), jnp.int32), pltpu.VMEM((SC_CHUNK,), jnp.int32),
                       pltpu.VMEM((SC_CHUNK, d), x.dtype), pltpu.SemaphoreType.DMA],
        name="sc_scatter_rows")
    def scatter(x_hbm, i1_hbm, i2_hbm, o_hbm, i1_v, i2_v, rows_v, sem):
        wid = lax.axis_index("s") * SC_CORES + lax.axis_index("c")

        @pl.loop(0, per_w // SC_CHUNK)
        def _(j):
            base = wid * per_w + j * SC_CHUNK
            pltpu.sync_copy(i1_hbm.at[pl.ds(base, SC_CHUNK)], i1_v)
            pltpu.sync_copy(i2_hbm.at[pl.ds(base, SC_CHUNK)], i2_v)
            pltpu.sync_copy(x_hbm.at[pl.ds(base, SC_CHUNK)], rows_v)
            pltpu.async_copy(rows_v, o_hbm.at[i1_v], sem).wait()
            pltpu.async_copy(rows_v, o_hbm.at[i2_v], sem).wait()

    return scatter(x, idx1, idx2)


def _sc_gather_rows(table, idx1, idx2):
    n = idx1.shape[0]
    d = table.shape[1]
    assert n % (SC_WORKERS * SC_CHUNK) == 0
    per_w = n // SC_WORKERS
    out = jax.ShapeDtypeStruct((n, d), table.dtype)

    @functools.partial(
        pl.kernel, mesh=_sc_mesh(), out_type=(out, out),
        scratch_types=[pltpu.VMEM((SC_CHUNK,), jnp.int32), pltpu.VMEM((SC_CHUNK, d), table.dtype),
                       pltpu.SemaphoreType.DMA],
        name="sc_gather_rows")
    def gather(t_hbm, i1_hbm, i2_hbm, o1_hbm, o2_hbm, i_v, rows_v, sem):
        wid = lax.axis_index("s") * SC_CORES + lax.axis_index("c")

        @pl.loop(0, per_w // SC_CHUNK)
        def _(j):
            base = wid * per_w + j * SC_CHUNK
            for i_hbm, o_hbm in ((i1_hbm, o1_hbm), (i2_hbm, o2_hbm)):
                pltpu.sync_copy(i_hbm.at[pl.ds(base, SC_CHUNK)], i_v)
                pltpu.async_copy(t_hbm.at[i_v], rows_v, sem).wait()
                pltpu.sync_copy(rows_v, o_hbm.at[pl.ds(base, SC_CHUNK)])

    return gather(table, idx1, idx2)


def _dest_kernel(rt_ref, starts_ref, d_ref):
    tr = rt_ref.shape[1]
    sub = lax.broadcasted_iota(jnp.int32, (N_EXPERTS, tr), 0)
    starts = starts_ref[:, :1]
    rows = []
    for e_row, pos_row in ((ROUTE_E1, ROUTE_POS1), (ROUTE_E2, ROUTE_POS2)):
        expert = rt_ref[e_row:e_row + 1, :].astype(jnp.int32)
        start = jnp.sum(jnp.where(sub == expert, starts, 0.0), axis=0, keepdims=True)
        rows.append(start + rt_ref[pos_row:pos_row + 1, :])
    d_ref[...] = jnp.concatenate(rows + [jnp.zeros((6, tr), F32)], axis=0).astype(jnp.int32)


def _routing_tables(rt, cnt, tm, n_tiles_max):
    n = rt.shape[1]
    counts = cnt[:, 0].astype(jnp.int32)
    padded = (counts + tm - 1) // tm * tm
    ends = jnp.cumsum(padded)
    starts = jnp.broadcast_to((ends - padded).astype(F32)[:, None], (N_EXPERTS, LANES))
    tr = min(2048, n)
    dest = pl.pallas_call(
        _dest_kernel,
        grid=(n // tr,),
        in_specs=[pl.BlockSpec((8, tr), lambda i: (0, i)), pl.BlockSpec((N_EXPERTS, LANES), lambda i: (0, 0))],
        out_specs=pl.BlockSpec((8, tr), lambda i: (0, i)),
        out_shape=jax.ShapeDtypeStruct((8, n), jnp.int32),
        compiler_params=_params(("parallel",)),
        name="route_dest",
    )(rt, starts)
    dest1, dest2 = dest[0], dest[1]
    tile_start = jnp.arange(n_tiles_max, dtype=jnp.int32) * tm
    tile_expert = jnp.sum((ends[None, :] <= tile_start[:, None]).astype(jnp.int32), axis=1)
    tile_expert = jnp.minimum(tile_expert, N_EXPERTS - 1)
    n_tiles = (ends[-1] // tm).astype(jnp.int32).reshape(1)
    valid = tile_start < ends[-1]
    prev_expert = jnp.concatenate([jnp.full((1,), -1, jnp.int32), tile_expert[:-1]])
    first = (valid & (tile_expert != prev_expert)).astype(jnp.int32)
    slot = (jnp.cumsum(first) - 1) % 2
    e_idx = jnp.arange(N_EXPERTS, dtype=jnp.int32)
    later = (padded > 0)[None, :] & (e_idx[None, :] > e_idx[:, None])
    next_run = jnp.min(jnp.where(later, e_idx[None, :], N_EXPERTS), axis=1)
    next_run = jnp.where(next_run == N_EXPERTS, -1, next_run).astype(jnp.int32)
    plan = (tile_expert, first, slot.astype(jnp.int32), next_run[tile_expert], n_tiles)
    return dest1, dest2, plan


def _moe_tiles_kernel(te_ref, first_ref, slot_ref, nxt_ref, nt_ref, xs_ref, wg_hbm, wu_hbm, wd_hbm, ys_ref,
                      wg_buf, wu_buf, wd_buf, sem):
    t = pl.program_id(0)

    def weight_copies(expert, slot):
        return (pltpu.make_async_copy(wg_hbm.at[expert], wg_buf.at[slot], sem.at[slot, 0]),
                pltpu.make_async_copy(wu_hbm.at[expert], wu_buf.at[slot], sem.at[slot, 1]),
                pltpu.make_async_copy(wd_hbm.at[expert], wd_buf.at[slot], sem.at[slot, 2]))

    @pl.when(t < nt_ref[0])
    def _():
        slot = slot_ref[t]

        @pl.when(t == 0)
        def _():
            for c in weight_copies(te_ref[0], 0):
                c.start()

        @pl.when(first_ref[t] == 1)
        def _():
            for c in weight_copies(te_ref[t], slot):
                c.wait()

            @pl.when(nxt_ref[t] >= 0)
            def _():
                for c in weight_copies(nxt_ref[t], 1 - slot):
                    c.start()

        x = _unpack_bf16_pairs(xs_ref[...]).astype(BF16)
        hg = jnp.dot(x, wg_buf[slot], preferred_element_type=F32)
        hu = jnp.dot(x, wu_buf[slot], preferred_element_type=F32)
        hid = (hg * jax.nn.sigmoid(hg) * hu).astype(BF16)
        ys_ref[...] = _pack_bf16_pairs(jnp.dot(hid, wd_buf[slot], preferred_element_type=F32))


def _moe_tiles(xs, plan, wg, wu, wd, tm):
    r, dh = xs.shape
    ne, d, de = wg.shape
    row = lambda t, *_: (t, 0)
    hbm = pl.BlockSpec(memory_space=pl.ANY)
    return pl.pallas_call(
        _moe_tiles_kernel,
        grid_spec=pltpu.PrefetchScalarGridSpec(
            num_scalar_prefetch=len(plan),
            grid=(r // tm,),
            in_specs=[pl.BlockSpec((tm, dh), row), hbm, hbm, hbm],
            out_specs=pl.BlockSpec((tm, dh), row),
            scratch_shapes=[
                pltpu.VMEM((2, d, de), wg.dtype),
                pltpu.VMEM((2, d, de), wu.dtype),
                pltpu.VMEM((2, de, d), wd.dtype),
                pltpu.SemaphoreType.DMA((2, 3)),
            ],
        ),
        out_shape=jax.ShapeDtypeStruct((r, dh), jnp.int32),
        compiler_params=_params(("arbitrary",)),
        name="moe_tiles",
    )(*plan, xs, wg, wu, wd)


def _combine_kernel(h_ref, y1_ref, y2_ref, route_ref, fg_ref, o_ref):
    lane = lax.broadcasted_iota(jnp.int32, route_ref.shape, 1)
    route = route_ref[...]
    w1 = jnp.sum(jnp.where(lane == ROUTE_W1, route, 0.0), axis=-1, keepdims=True)
    w2 = jnp.sum(jnp.where(lane == ROUTE_W2, route, 0.0), axis=-1, keepdims=True)
    y = h_ref[...] + w1 * _unpack_bf16_pairs(y1_ref[...]) + w2 * _unpack_bf16_pairs(y2_ref[...])
    o_ref[...] = y * lax.rsqrt(jnp.mean(y * y, axis=-1, keepdims=True) + EPS) * fg_ref[...]


def _combine(h, y1, y2, route, fg, tm):
    n, d = h.shape
    row = lambda i: (i, 0)
    return pl.pallas_call(
        _combine_kernel,
        grid=(n // tm,),
        in_specs=[
            pl.BlockSpec((tm, d), row),
            pl.BlockSpec((tm, d // 2), row),
            pl.BlockSpec((tm, d // 2), row),
            pl.BlockSpec((tm, LANES), row),
            pl.BlockSpec((1, d), lambda i: (0, 0)),
        ],
        out_specs=pl.BlockSpec((tm, d), row),
        out_shape=jax.ShapeDtypeStruct((n, d), F32),
        compiler_params=_params(("parallel",)),
        name="moe_combine",
    )(h, y1, y2, route, fg)


def _moe(a2p, route, rt, cnt, h, wg, wu, wd, fg, tm):
    n = h.shape[0]
    n_tiles_max = (2 * n) // tm + N_EXPERTS
    dest1, dest2, plan = _routing_tables(rt, cnt, tm, n_tiles_max)
    xs = _sc_scatter_rows(a2p, dest1, dest2, n_tiles_max * tm)
    ys = _moe_tiles(xs, plan, wg, wu, wd, tm)
    y1, y2 = _sc_gather_rows(ys, dest1, dest2)
    return _combine(h, y1, y2, route, fg, tm=min(1024, n))


def _dup_heads(w, n_heads):
    d = w.shape[0]
    w = w.reshape(d, n_heads, 1, HEAD_DIM)
    return jnp.broadcast_to(w, (d, n_heads, 2, HEAD_DIM)).reshape(d, n_heads * 2 * HEAD_DIM)


def _encoder(x, proj_meta, wts):
    b, s, d = x.shape
    x2d = x.reshape(b * s, d)
    proj = _norm_proj(x2d, wts["g1"], wts["w_ext"], tm=1024).reshape(b, s, -1)
    bk = min(1024, s)
    od0 = _diff_attn(proj, proj_meta, wts["lam_vecs"], wts["subln_g"], heads=(0,), bq=512, bk=min(512, s),
                     skip=True)
    od1 = _diff_attn(proj, proj_meta, wts["lam_vecs"], wts["subln_g"], heads=tuple(range(1, DA_HEADS)),
                     bq=256, bk=bk, skip=False)
    os_ = _win_attn(proj, proj_meta, wts["sink"])
    h, a2p, route, rt, cnt = _out_router(od0.reshape(b * s, -1), od1.reshape(b * s, -1), os_.reshape(b * s, -1), x2d,
                                     wts["wod"], wts["wos"],
                                     wts["g2"], wts["wr"], wts["br"], tm=1024)
    y = _moe(a2p, route, rt, cnt, h, wts["wg"], wts["wu"], wts["wd"], wts["fg"], tm=512)
    return y.reshape(b, s, d)


def kernel(x_prompt, x_sample, meta, norm1_g, w_in, lam_q1, lam_k1, lam_q2, lam_k2, subln_g, sink, w_out,
           norm2_g, w_gr, b_gr, w_er, b_er, w_gate, w_up, w_down, final_g):
    d = x_prompt.shape[-1]
    w = w_in[0]
    c_kd = 2 * DA_HEADS * HEAD_DIM
    c_vd = 2 * c_kd
    c_qs = c_vd + DA_HEADS * 2 * HEAD_DIM
    c_ks = c_qs + SW_HEADS * HEAD_DIM
    c_vs = c_ks + SW_KV_HEADS * HEAD_DIM
    w_ext = jnp.concatenate(
        [w[:, :c_ks], _dup_heads(w[:, c_ks:c_vs], SW_KV_HEADS), _dup_heads(w[:, c_vs:], SW_KV_HEADS)],
        axis=1).astype(BF16)
    w_router = jnp.concatenate([w_er[0], w_gr[0]], axis=1)
    w_router = jnp.pad(w_router, ((0, 0), (0, LANES - w_router.shape[1])))
    wr_hi = w_router.astype(BF16)
    wr_lo = (w_router - wr_hi.astype(F32)).astype(BF16)
    br = jnp.pad(jnp.concatenate([b_er[0], b_gr[0]]), (0, LANES - N_EXPERTS - N_GROUPS)).reshape(1, LANES)
    wo = w_out[0].astype(BF16)
    half = DA_HEADS * 2 * HEAD_DIM
    wts = dict(
        g1=norm1_g[0].reshape(1, d), w_ext=w_ext,
        lam_vecs=(lam_q1[0].reshape(1, -1), lam_k1[0].reshape(1, -1),
                  lam_q2[0].reshape(1, -1), lam_k2[0].reshape(1, -1)),
        subln_g=subln_g[0].reshape(1, -1), sink=sink[0].reshape(1, -1),
        wod=wo[:half], wos=wo[half:], g2=norm2_g[0].reshape(1, d),
        wr=jnp.concatenate([wr_hi, wr_lo], axis=1), br=br,
        wg=w_gate[0].astype(BF16), wu=w_up[0].astype(BF16), wd=w_down[0].astype(BF16),
        fg=final_g.reshape(1, d),
    )
    proj_meta = _norm_proj(meta, wts["g1"], w_ext, tm=N_META)
    proj_meta = jnp.pad(proj_meta, ((0, LANES - N_META), (0, 0)))
    return _encoder(x_prompt, proj_meta, wts), _encoder(x_sample, proj_meta, wts)
```

```python
import functools
import math

import jax
import jax.numpy as jnp
import numpy as np
from jax import lax
from jax.experimental import pallas as pl
from jax.experimental.pallas import tpu as pltpu
from jax.experimental.pallas import tpu_sc as plsc

F32 = jnp.float32
BF16 = jnp.bfloat16

N_META = 16
HEAD_DIM = 64
DA_HEADS = 4
SW_HEADS = 8
SW_KV_HEADS = 2
SW_GROUP = SW_HEADS // SW_KV_HEADS
WINDOW = 128
N_GROUPS = 4
EXPERTS_PER_GROUP = 8
N_EXPERTS = N_GROUPS * EXPERTS_PER_GROUP
EPS = 1e-6
SUBLN_EPS = 1e-5
NEG_INF = -1e30
LAM_INIT = 0.8 - 0.6 * math.exp(-0.3 * 0)
LANES = 128
VMEM_LIMIT = 48 * 1024 * 1024

COL_QD, COL_KD, COL_VD, COL_QS, COL_KS, COL_VS, N_COLBLK = 0, 4, 8, 12, 16, 18, 20
CONTRACT_LAST = (((1,), (1,)), ((), ()))


def _params(sem):
    return pltpu.CompilerParams(dimension_semantics=sem, vmem_limit_bytes=VMEM_LIMIT)


def _norm_proj_kernel(x_ref, g_ref, w_ref, o_ref):
    x = x_ref[...]
    ms = jnp.mean(x * x, axis=-1, keepdims=True)
    y = (x * lax.rsqrt(ms + EPS) * g_ref[...]).astype(BF16)
    o_ref[...] = jnp.dot(y, w_ref[...], preferred_element_type=F32).astype(o_ref.dtype)


def _norm_proj(x2d, g, w_ext, tm):
    n, d = x2d.shape
    wcols = w_ext.shape[1]
    return pl.pallas_call(
        _norm_proj_kernel,
        grid=(n // tm,),
        in_specs=[
            pl.BlockSpec((tm, d), lambda i: (i, 0)),
            pl.BlockSpec((1, d), lambda i: (0, 0)),
            pl.BlockSpec((d, wcols), lambda i: (0, 0)),
        ],
        out_specs=pl.BlockSpec((tm, wcols), lambda i: (i, 0)),
        out_shape=jax.ShapeDtypeStruct((n, wcols), BF16),
        compiler_params=_params(("parallel",)),
        name="norm_proj",
    )(x2d, g, w_ext)


POS_SPLIT = 64
N_SPLIT = 3
LOG2E = 1.4426950408889634


def _key_pos_features(seq):
    j = jnp.arange(seq, dtype=jnp.int32)[:, None]
    lane = jnp.arange(LANES, dtype=jnp.int32)[None, :]
    hi = (j // POS_SPLIT * POS_SPLIT).astype(F32)
    lo = (j % POS_SPLIT).astype(F32)
    feat = jnp.where(lane < N_SPLIT, 1.0,
                     jnp.where(lane < 2 * N_SPLIT, hi, jnp.where(lane < 3 * N_SPLIT, lo, 0.0)))
    return feat.astype(BF16)


def _split_bf16(x):
    pieces = []
    for _ in range(N_SPLIT):
        p = x.astype(BF16).astype(F32)
        pieces.append(p)
        x = x - p
    return pieces


SKIP_MARGIN = 135.0


def _diff_attn_head(h, q_ref, k_ref, v_ref, kf_ref, km_ref, vm_ref, lam, sg_ref,
                    o_ref, qe_ref, acc_ref, m_ref, sd_ref, s1_ref, kn_ref, *, bq, bk, seq, skip):
    i = pl.program_id(1)
    slope = 2.0 ** (-8.0 * (h + 1) / DA_HEADS) * LOG2E
    lane = lax.broadcasted_iota(jnp.int32, (1, LANES), 1)
    q = (q_ref[...].astype(F32) * (LOG2E / math.sqrt(HEAD_DIM))).astype(BF16)
    zero = jnp.zeros_like(q)
    qmaps = (jnp.where(lane < HEAD_DIM, q, zero), jnp.where(lane >= HEAD_DIM, q, zero))
    qpos = i * bq + lax.broadcasted_iota(jnp.int32, (bq, 1), 0)
    row_pieces = _split_bf16(-slope * qpos.astype(F32))
    slope_pieces = _split_bf16(jnp.full((1, 1), slope, F32))
    feat = jnp.zeros((bq, LANES), F32)
    for n in range(N_SPLIT):
        feat = jnp.where(lane == n, row_pieces[n], feat)
        feat = jnp.where(lane == N_SPLIT + n, slope_pieces[n], feat)
        feat = jnp.where(lane == 2 * N_SPLIT + n, slope_pieces[n], feat)
    feats = (feat.astype(BF16), (-feat).astype(BF16))
    for side in range(2):
        for c in range(2):
            qe_ref[side, c * bq:(c + 1) * bq, :LANES] = qmaps[c]
            qe_ref[side, c * bq:(c + 1) * bq, LANES:] = feats[side]

    ones_blk = jnp.broadcast_to(jnp.where(lane == 0, 1.0, 0.0).astype(BF16), (bk, LANES))
    meta_mask = jnp.where(lane < N_META, 0.0, NEG_INF)

    def attend(s, v_ext):
        m_old = m_ref[...]
        m_new = jnp.maximum(m_old, jnp.max(s, axis=-1, keepdims=True))
        alpha = jnp.exp2(m_old - m_new)
        p = jnp.exp2(s - jnp.tile(m_new, (1, s.shape[1] // LANES))).astype(BF16)
        acc_ref[...] = jnp.tile(alpha, (1, 2)) * acc_ref[...] + jnp.dot(p, v_ext, preferred_element_type=F32)
        m_ref[...] = m_new

    m_ref[...] = jnp.full(m_ref.shape, NEG_INF, F32)
    acc_ref[...] = jnp.zeros(acc_ref.shape, F32)

    s0_ref = sd_ref.at[:, :bk]
    nblk = seq // bk
    diag = (i * bq) // bk

    def scores_block(dst_ref, kb, side, edge=None):
        start = pl.multiple_of(kb * bk, bk)
        k_ext = jnp.concatenate([k_ref[pl.ds(start, bk), :], kf_ref[pl.ds(start, bk), :]], axis=1)
        s = lax.dot_general(qe_ref[side], k_ext, CONTRACT_LAST, preferred_element_type=F32)
        dst_ref[...] = s if edge is None else s + edge

    def consume_block(src_ref, kb):
        start = pl.multiple_of(kb * bk, bk)
        attend(src_ref[...], jnp.concatenate([v_ref[pl.ds(start, bk), :], ones_blk], axis=1))

    d_start = pl.multiple_of(diag * bk, bk)
    kpos = diag * bk + lax.broadcasted_iota(jnp.int32, (1, bk), 1)
    bias = jnp.concatenate([-slope * jnp.abs(qpos - kpos).astype(F32),
                            jnp.broadcast_to(meta_mask, (bq, LANES))], axis=1)
    k_first = jnp.concatenate([k_ref[pl.ds(d_start, bk), :], km_ref[...]], axis=0)
    v_first = jnp.concatenate(
        [jnp.concatenate([v_ref[pl.ds(d_start, bk), :], vm_ref[...]], axis=0),
         jnp.broadcast_to(ones_blk[:1], (bk + LANES, LANES))], axis=1)
    sd_ref[...] = lax.dot_general(qe_ref[0, :, :LANES], k_first, CONTRACT_LAST,
                                  preferred_element_type=F32) + jnp.tile(bias, (2, 1))

    def all_blocks():
        def key_block(pos):
            t = pos - 1
            return t + jnp.where(t >= diag, 1, 0)

        def scores_into(dst_ref, pos):
            kb = key_block(pos)
            scores_block(dst_ref, kb, jnp.where(kb > diag, 1, 0))

        scores_into(s1_ref, 1)
        attend(sd_ref[...], v_first)
        for u in range((nblk - 2) // 2):
            scores_into(s0_ref, 2 * u + 2)
            consume_block(s1_ref, key_block(2 * u + 1))
            scores_into(s1_ref, 2 * u + 3)
            consume_block(s0_ref, key_block(2 * u + 2))
        consume_block(s1_ref, key_block(nblk - 1))

    def neighbours_only():
        prev = jnp.maximum(diag - 1, 0)
        nxt = jnp.minimum(diag + 1, nblk - 1)
        scores_block(s1_ref, prev, 0, edge=jnp.where(diag == 0, NEG_INF, 0.0))
        attend(sd_ref[...], v_first)
        scores_block(s0_ref, nxt, 1, edge=jnp.where(diag == nblk - 1, NEG_INF, 0.0))
        consume_block(s1_ref, prev)
        consume_block(s0_ref, nxt)

    if nblk == 1:
        attend(sd_ref[...], v_first)
    elif skip:
        assert bq == bk and nblk > 2
        @pl.when(i == 0)
        def _():
            kf32 = k_ref[...].astype(F32)
            kn_ref[0] = jnp.max(jnp.sum(kf32 * kf32, axis=-1, keepdims=True))
        qf32 = q.astype(F32)
        qn2 = jnp.max(jnp.sum(qf32 * qf32, axis=-1, keepdims=True), axis=0, keepdims=True)
        s_meta = lax.dot_general(qe_ref[0, :, :LANES], km_ref[...], CONTRACT_LAST,
                                 preferred_element_type=F32) + meta_mask
        m_low = jnp.min(jnp.max(s_meta, axis=-1, keepdims=True), axis=0, keepdims=True)
        reach = (jnp.sqrt(qn2 * kn_ref[0]) - m_low + SKIP_MARGIN) * (1.0 / slope)
        near = jnp.clip(reach, 0.0, float(seq))[0, 0] <= float(bk)

        @pl.when(near)
        def _():
            neighbours_only()

        @pl.when(jnp.logical_not(near))
        def _():
            all_blocks()
    else:
        all_blocks()

    o1 = acc_ref[:bq, :LANES] / acc_ref[:bq, LANES:LANES + 1]
    o2 = acc_ref[bq:, :LANES] / acc_ref[bq:, LANES:LANES + 1]
    o = o1 - lam * o2
    o = o * lax.rsqrt(jnp.mean(o * o, axis=-1, keepdims=True) + SUBLN_EPS) * sg_ref[...]
    o_ref[...] = (o * (1.0 - LAM_INIT)).astype(o_ref.dtype)


def _diff_attn_kernel(*refs, heads, **kw):
    nh = len(heads)
    q_refs, k_refs, v_refs = refs[:nh], refs[nh:2 * nh], refs[2 * nh:3 * nh]
    kf_ref = refs[3 * nh]
    km_refs, vm_refs = refs[3 * nh + 1:4 * nh + 1], refs[4 * nh + 1:5 * nh + 1]
    lq1_ref, lk1_ref, lq2_ref, lk2_ref, sg_ref, o_ref, qe_ref, acc_ref, m_ref, sd_ref, s1_ref, kn_ref = refs[5 * nh + 1:]
    lam = (jnp.exp(jnp.sum(lq1_ref[...] * lk1_ref[...], axis=-1, keepdims=True))
           - jnp.exp(jnp.sum(lq2_ref[...] * lk2_ref[...], axis=-1, keepdims=True)) + LAM_INIT)
    for n, h in enumerate(heads):
        _diff_attn_head(h, q_refs[n], k_refs[n], v_refs[n], kf_ref, km_refs[n], vm_refs[n], lam, sg_ref,
                        o_ref.at[:, n * LANES:(n + 1) * LANES],
                        qe_ref.at[n], acc_ref.at[n], m_ref.at[n], sd_ref.at[n], s1_ref.at[n], kn_ref, **kw)


def _diff_attn(proj, proj_meta, lam_vecs, subln_g, heads, bq, bk, skip):
    b, s, _ = proj.shape
    bq = min(bq, s)
    assert bk % bq == 0 and s % bk == 0 and (s // bk == 1 or (s // bk) % 2 == 0)
    nh = len(heads)
    vec = pl.BlockSpec((1, HEAD_DIM), lambda bi, i: (0, 0))
    q_specs = [pl.BlockSpec((None, bq, LANES), lambda bi, i, c=COL_QD + h: (bi, i, c)) for h in heads]
    k_specs = [pl.BlockSpec((None, s, LANES), lambda bi, i, c=COL_KD + h: (bi, 0, c)) for h in heads]
    v_specs = [pl.BlockSpec((None, s, LANES), lambda bi, i, c=COL_VD + h: (bi, 0, c)) for h in heads]
    km_specs = [pl.BlockSpec((LANES, LANES), lambda bi, i, c=COL_KD + h: (0, c)) for h in heads]
    vm_specs = [pl.BlockSpec((LANES, LANES), lambda bi, i, c=COL_VD + h: (0, c)) for h in heads]
    return pl.pallas_call(
        functools.partial(_diff_attn_kernel, heads=heads, bq=bq, bk=bk, seq=s, skip=skip),
        grid=(b, s // bq),
        in_specs=(q_specs + k_specs + v_specs + [pl.BlockSpec((s, LANES), lambda bi, i: (0, 0))]
                  + km_specs + vm_specs + [vec, vec, vec, vec, pl.BlockSpec((1, LANES), lambda bi, i: (0, 0))]),
        out_specs=pl.BlockSpec((None, bq, nh * LANES), lambda bi, i: (bi, i, 0)),
        out_shape=jax.ShapeDtypeStruct((b, s, nh * LANES), BF16),
        scratch_shapes=[
            pltpu.VMEM((nh, 2, 2 * bq, 2 * LANES), BF16),
            pltpu.VMEM((nh, 2 * bq, 2 * LANES), F32),
            pltpu.VMEM((nh, 2 * bq, LANES), F32),
            pltpu.VMEM((nh, 2 * bq, bk + LANES), F32),
            pltpu.VMEM((nh, 2 * bq, bk), F32),
            pltpu.SMEM((1,), F32),
        ],
        compiler_params=_params(("parallel", "arbitrary")),
        name="diff_attn",
    )(*([proj] * (3 * nh)), _key_pos_features(s), *([proj_meta] * (2 * nh)), *lam_vecs, subln_g)


WIN_KEYS = 4 * WINDOW
WIN_QBLOCKS = 8


def _win_tables():
    r = np.arange(WINDOW)
    qf = np.zeros((SW_KV_HEADS, SW_GROUP * WINDOW, LANES), np.float32)
    for head in range(SW_HEADS):
        slope = 2.0 ** (-8.0 * (head + 1) / SW_HEADS)
        i_rel = WINDOW + r
        hi, lo = i_rel // POS_SPLIT * POS_SPLIT, i_rel % POS_SPLIT
        rows = qf[head // SW_GROUP, (head % SW_GROUP) * WINDOW:(head % SW_GROUP + 1) * WINDOW]
        rows[:, 0], rows[:, 1], rows[:, 2], rows[:, 3] = -slope * hi, -slope * lo, slope, slope
        rows[:, 4:8] = -rows[:, 0:4]
    kf = np.zeros((2, WINDOW, LANES), np.float32)
    for n, (blk, right) in enumerate(((0, 0), (2, 1))):
        j_rel = blk * WINDOW + r
        o = 4 * right
        kf[n, :, o], kf[n, :, o + 1] = 1.0, 1.0
        kf[n, :, o + 2], kf[n, :, o + 3] = j_rel // POS_SPLIT * POS_SPLIT, j_rel % POS_SPLIT
    mask = np.zeros((2, WINDOW, WINDOW), np.float32)
    mask[0] = np.where(r[None, :] >= r[:, None], 0.0, NEG_INF)
    mask[1] = np.where(r[None, :] <= r[:, None], 0.0, NEG_INF)
    cur = np.zeros((SW_KV_HEADS, SW_GROUP * WINDOW, WINDOW), np.float32)
    for head in range(SW_HEADS):
        slope = 2.0 ** (-8.0 * (head + 1) / SW_HEADS)
        cur[head // SW_GROUP, (head % SW_GROUP) * WINDOW:(head % SW_GROUP + 1) * WINDOW] = (
            -slope * np.abs(r[:, None] - r[None, :]))
    return jnp.asarray(qf, BF16), jnp.asarray(kf, BF16), jnp.asarray(mask, F32), jnp.asarray(cur, F32)


def _win_attn_kernel(q_ref, kp_ref, kc_ref, kn_ref, vp_ref, vc_ref, vn_ref, km_ref, vm_ref, qf_ref, kf_ref,
                     mask_ref, cur_ref, sink_ref, o_ref, *, nstep):
    c = pl.program_id(1)
    lane = lax.broadcasted_iota(jnp.int32, (1, LANES), 1)
    scale = jnp.asarray(1.0 / math.sqrt(HEAD_DIM), BF16)
    mask_meta = jnp.where(lane < N_META, 0.0, NEG_INF)
    row = lax.broadcasted_iota(jnp.int32, (SW_GROUP * WINDOW, 1), 0)
    ones_blk = jnp.broadcast_to(jnp.where(lane == 0, 1.0, 0.0).astype(BF16), (WIN_KEYS, LANES))
    zeros_blk = jnp.zeros((WINDOW, LANES), BF16)
    gw = SW_GROUP * HEAD_DIM
    own = [slice(j * WINDOW, (j + 1) * WINDOW) for j in range(WIN_QBLOCKS)]
    kblk = [kp_ref] + [kc_ref.at[r] for r in own] + [kn_ref]
    vblk = [vp_ref] + [vc_ref.at[r] for r in own] + [vn_ref]
    for j in range(WIN_QBLOCKS):
        rq = slice(j * WINDOW, (j + 1) * WINDOW)
        edge_p = jnp.where(c == 0, NEG_INF, 0.0) if j == 0 else 0.0
        edge_n = jnp.where(c == nstep - 1, NEG_INF, 0.0) if j == WIN_QBLOCKS - 1 else 0.0
        mask_prev = jnp.tile(mask_ref[0] + edge_p, (SW_GROUP, 1))
        mask_next = jnp.tile(mask_ref[1] + edge_n, (SW_GROUP, 1))
        for g in range(SW_KV_HEADS):
            kv = slice(g * LANES, (g + 1) * LANES)
            rows = []
            for hh in range(SW_GROUP):
                col = g * gw + (hh // 2) * LANES
                qb = q_ref[rq, col:col + LANES] * scale
                keep = (lane < HEAD_DIM) if hh % 2 == 0 else (lane >= HEAD_DIM)
                rows.append(jnp.where(keep, qb, jnp.zeros_like(qb)))
            q_ext = jnp.concatenate([jnp.concatenate(rows, axis=0), qf_ref[g]], axis=1)
            k_all = jnp.concatenate([
                jnp.concatenate([km_ref[:, kv], zeros_blk], axis=1),
                jnp.concatenate([kblk[j][:, kv], kf_ref[0]], axis=1),
                jnp.concatenate([kblk[j + 1][:, kv], zeros_blk], axis=1),
                jnp.concatenate([kblk[j + 2][:, kv], kf_ref[1]], axis=1)], axis=0)
            s = lax.dot_general(q_ext, k_all, CONTRACT_LAST, preferred_element_type=F32)
            s_meta = s[:, :LANES] + mask_meta
            s_prev = s[:, LANES:2 * LANES] + mask_prev
            s_cur = s[:, 2 * LANES:3 * LANES] + cur_ref[g]
            s_next = s[:, 3 * LANES:] + mask_next
            sink = jnp.zeros((SW_GROUP * WINDOW, 1), F32)
            for hh in range(SW_GROUP):
                sink = jnp.where(row // WINDOW == hh, sink_ref[0, g * SW_GROUP + hh], sink)
            parts = (s_meta, s_prev, s_cur, s_next)
            m = jnp.maximum(jnp.maximum(s_meta, s_prev), jnp.maximum(s_cur, s_next))
            m = jnp.maximum(jnp.max(m, axis=-1, keepdims=True), sink)
            p = jnp.concatenate([jnp.exp(x - m) for x in parts], axis=1).astype(BF16)
            v_all = jnp.concatenate(
                [jnp.concatenate([vm_ref[:, kv], vblk[j][:, kv], vblk[j + 1][:, kv], vblk[j + 2][:, kv]], axis=0),
                 ones_blk], axis=1)
            acc = jnp.dot(p, v_all, preferred_element_type=F32)
            denom = acc[:, LANES:LANES + 1] + jnp.exp(sink - m)
            o = acc[:, :LANES] / denom
            for lb in range(SW_GROUP // 2):
                even = o[(2 * lb) * WINDOW:(2 * lb + 1) * WINDOW]
                odd = o[(2 * lb + 1) * WINDOW:(2 * lb + 2) * WINDOW]
                col = g * gw + lb * LANES
                o_ref[rq, col:col + LANES] = jnp.where(lane < HEAD_DIM, even, odd).astype(o_ref.dtype)


def _win_attn(proj, proj_meta, sink):
    b, s, _ = proj.shape
    nblk = s // WINDOW
    nq = WIN_QBLOCKS
    nstep = nblk // nq
    qw = SW_HEADS * HEAD_DIM
    kvw = SW_KV_HEADS * LANES
    qf, kf, mask, cur = _win_tables()

    def edge_spec(col, shift):
        return pl.BlockSpec(
            (None, WINDOW, kvw),
            lambda bi, c: (bi, jnp.clip(nq * c + shift, 0, nblk - 1), col * LANES // kvw))

    def pair_spec(col):
        return pl.BlockSpec((None, nq * WINDOW, kvw), lambda bi, c: (bi, c, col * LANES // kvw))

    return pl.pallas_call(
        functools.partial(_win_attn_kernel, nstep=nstep),
        grid=(b, nstep),
        in_specs=[
            pl.BlockSpec((None, nq * WINDOW, qw), lambda bi, c: (bi, c, COL_QS * LANES // qw)),
            edge_spec(COL_KS, -1), pair_spec(COL_KS), edge_spec(COL_KS, nq),
            edge_spec(COL_VS, -1), pair_spec(COL_VS), edge_spec(COL_VS, nq),
            pl.BlockSpec((LANES, kvw), lambda bi, c: (0, COL_KS * LANES // kvw)),
            pl.BlockSpec((LANES, kvw), lambda bi, c: (0, COL_VS * LANES // kvw)),
            pl.BlockSpec((SW_KV_HEADS, SW_GROUP * WINDOW, LANES), lambda bi, c: (0, 0, 0)),
            pl.BlockSpec((2, WINDOW, LANES), lambda bi, c: (0, 0, 0)),
            pl.BlockSpec((2, WINDOW, WINDOW), lambda bi, c: (0, 0, 0)),
            pl.BlockSpec((SW_KV_HEADS, SW_GROUP * WINDOW, WINDOW), lambda bi, c: (0, 0, 0)),
            pl.BlockSpec(memory_space=pltpu.SMEM),
        ],
        out_specs=pl.BlockSpec((None, nq * WINDOW, qw), lambda bi, c: (bi, c, 0)),
        out_shape=jax.ShapeDtypeStruct((b, s, qw), BF16),
        compiler_params=_params(("parallel", "arbitrary")),
        name="win_attn",
    )(proj, proj, proj, proj, proj, proj, proj, proj_meta, proj_meta, qf, kf, mask, cur, sink)


ROUTE_E1, ROUTE_E2, ROUTE_POS1, ROUTE_POS2, ROUTE_W1, ROUTE_W2 = range(6)


def _pack_bf16_pairs(x):
    k = x.shape[1] // 2
    bits = lax.bitcast_convert_type(x.astype(BF16).astype(F32), jnp.uint32)
    return lax.bitcast_convert_type(bits[:, :k] | (bits[:, k:] >> 16), jnp.int32)


def _unpack_bf16_pairs(w):
    bits = lax.bitcast_convert_type(w, jnp.uint32)
    hi = lax.bitcast_convert_type(bits & jnp.uint32(0xFFFF0000), F32)
    lo = lax.bitcast_convert_type(bits << 16, F32)
    return jnp.concatenate([hi, lo], axis=1)


def _out_router_kernel(od0_ref, od1_ref, os_ref, x_ref, wod_ref, wos_ref, g2_ref, wr_ref, br_ref,
                       h_ref, a_ref, route_ref, rt_ref, cnt_ref, base_ref):
    @pl.when(pl.program_id(0) == 0)
    def _():
        base_ref[...] = jnp.zeros_like(base_ref)

    h = (x_ref[...]
         + jnp.dot(jnp.concatenate([od0_ref[...], od1_ref[...]], axis=1), wod_ref[...],
                   preferred_element_type=F32)
         + jnp.dot(os_ref[...], wos_ref[...], preferred_element_type=F32))
    h_ref[...] = h
    a = h * lax.rsqrt(jnp.mean(h * h, axis=-1, keepdims=True) + EPS) * g2_ref[...]
    a_hi = a.astype(BF16)
    a_ref[...] = _pack_bf16_pairs(a)
    a_lo = (a - a_hi.astype(F32)).astype(BF16)
    hi_terms = jnp.dot(a_hi, wr_ref[...], preferred_element_type=F32)
    logits = (hi_terms[:, :LANES] + hi_terms[:, LANES:]
              + jnp.dot(a_lo, wr_ref[:, :LANES], preferred_element_type=F32)) + br_ref[...]
    tm = logits.shape[0]
    lt = jnp.transpose(logits)
    le = lt[:N_EXPERTS]
    sub_g = lax.broadcasted_iota(jnp.int32, (8, tm), 0)
    lg = jnp.where(sub_g < N_GROUPS, lt[N_EXPERTS:N_EXPERTS + 8], NEG_INF)
    sub_e = lax.broadcasted_iota(jnp.int32, (N_EXPERTS, tm), 0)

    def first_argmax(vals, mask, sub, size):
        mx = jnp.max(vals, axis=0, keepdims=True)
        idx = jnp.min(jnp.where(mask & (vals == mx), sub, size), axis=0, keepdims=True)
        return mx, idx

    gmax, gidx = first_argmax(lg, sub_g < N_GROUPS, sub_g, 8)
    g_w = 1.0 / jnp.sum(jnp.exp(lg - gmax), axis=0, keepdims=True)
    in_group = (sub_e // EXPERTS_PER_GROUP) == gidx
    m1, i1 = first_argmax(jnp.where(in_group, le, NEG_INF), in_group, sub_e, N_EXPERTS)
    rest = in_group & (sub_e != i1)
    m2, i2 = first_argmax(jnp.where(rest, le, NEG_INF), rest, sub_e, N_EXPERTS)
    r = jnp.exp(m2 - m1)
    w1 = g_w / (1.0 + r)
    w2 = g_w * r / (1.0 + r)

    rix = lax.broadcasted_iota(jnp.int32, (tm, tm), 0)
    cix = lax.broadcasted_iota(jnp.int32, (tm, tm), 1)
    earlier = jnp.where(rix < cix, 1.0, 0.0).astype(BF16)
    oh1 = jnp.where(sub_e == i1, 1.0, 0.0)
    oh2 = jnp.where(sub_e == i2, 1.0, 0.0)
    before = jnp.dot(jnp.concatenate([oh1, oh2], axis=0).astype(BF16), earlier, preferred_element_type=F32)
    base1 = base_ref[:, :1]
    pos1 = jnp.sum(oh1 * (base1 + before[:N_EXPERTS]), axis=0, keepdims=True)
    base2 = base1 + jnp.sum(oh1, axis=1, keepdims=True)
    pos2 = jnp.sum(oh2 * (base2 + before[N_EXPERTS:]), axis=0, keepdims=True)
    total = jnp.broadcast_to(base2 + jnp.sum(oh2, axis=1, keepdims=True), base_ref.shape)
    base_ref[...] = total
    cnt_ref[...] = total

    fields = jnp.concatenate([i1.astype(F32), i2.astype(F32), pos1, pos2, w1, w2, jnp.zeros((2, tm), F32)], axis=0)
    rt_ref[...] = fields
    route_ref[...] = jnp.transpose(jnp.concatenate([fields, jnp.zeros((LANES - 8, tm), F32)], axis=0))


def _out_router(od0, od1, os_, x2d, wod, wos, g2, wr, br, tm):
    n, d = x2d.shape
    half = os_.shape[1]
    row = lambda i: (i, 0)
    const = lambda i: (0, 0)
    return pl.pallas_call(
        _out_router_kernel,
        grid=(n // tm,),
        in_specs=[
            pl.BlockSpec((tm, od0.shape[1]), row),
            pl.BlockSpec((tm, od1.shape[1]), row),
            pl.BlockSpec((tm, half), row),
            pl.BlockSpec((tm, d), row),
            pl.BlockSpec((half, d), const),
            pl.BlockSpec((half, d), const),
            pl.BlockSpec((1, d), const),
            pl.BlockSpec((d, 2 * LANES), const),
            pl.BlockSpec((1, LANES), const),
        ],
        out_specs=[
            pl.BlockSpec((tm, d), row),
            pl.BlockSpec((tm, d // 2), row),
            pl.BlockSpec((tm, LANES), row),
            pl.BlockSpec((8, tm), lambda i: (0, i)),
            pl.BlockSpec((N_EXPERTS, LANES), const),
        ],
        out_shape=[
            jax.ShapeDtypeStruct((n, d), F32),
            jax.ShapeDtypeStruct((n, d // 2), jnp.int32),
            jax.ShapeDtypeStruct((n, LANES), F32),
            jax.ShapeDtypeStruct((8, n), F32),
            jax.ShapeDtypeStruct((N_EXPERTS, LANES), F32),
        ],
        scratch_shapes=[pltpu.VMEM((N_EXPERTS, LANES), F32)],
        compiler_params=_params(("arbitrary",)),
        name="out_router",
    )(od0, od1, os_, x2d, wod, wos, g2, wr, br)


SC_CORES, SC_SUBCORES = 2, 16
SC_WORKERS = SC_CORES * SC_SUBCORES
SC_CHUNK = 128


def _sc_mesh():
    return plsc.VectorSubcoreMesh(core_axis_name="c", subcore_axis_name="s",
                                  num_cores=SC_CORES, num_subcores=SC_SUBCORES)


def _sc_scatter_rows(x, idx1, idx2, n_out):
    n, d = x.shape
    assert n % (SC_WORKERS * SC_CHUNK) == 0
    per_w = n // SC_WORKERS

    @functools.partial(
        pl.kernel, mesh=_sc_mesh(), out_type=jax.ShapeDtypeStruct((n_out, d), x.dtype),
        scratch_types=[pltpu.VMEM((SC_CHUNK,), jnp.int32), pltpu.VMEM((SC_CHUNK,), jnp.int32),
                       pltpu.VMEM((SC_CHUNK, d), x.dtype), pltpu.SemaphoreType.DMA],
        name="sc_scatter_rows")
    def scatter(x_hbm, i1_hbm, i2_hbm, o_hbm, i1_v, i2_v, rows_v, sem):
        wid = lax.axis_index("s") * SC_CORES + lax.axis_index("c")

        @pl.loop(0, per_w // SC_CHUNK)
        def _(j):
            base = wid * per_w + j * SC_CHUNK
            pltpu.sync_copy(i1_hbm.at[pl.ds(base, SC_CHUNK)], i1_v)
            pltpu.sync_copy(i2_hbm.at[pl.ds(base, SC_CHUNK)], i2_v)
            pltpu.sync_copy(x_hbm.at[pl.ds(base, SC_CHUNK)], rows_v)
            pltpu.async_copy(rows_v, o_hbm.at[i1_v], sem).wait()
            pltpu.async_copy(rows_v, o_hbm.at[i2_v], sem).wait()

    return scatter(x, idx1, idx2)


def _sc_gather_rows(table, idx1, idx2):
    n = idx1.shape[0]
    d = table.shape[1]
    assert n % (SC_WORKERS * SC_CHUNK) == 0
    per_w = n // SC_WORKERS
    out = jax.ShapeDtypeStruct((n, d), table.dtype)

    @functools.partial(
        pl.kernel, mesh=_sc_mesh(), out_type=(out, out),
        scratch_types=[pltpu.VMEM((SC_CHUNK,), jnp.int32), pltpu.VMEM((SC_CHUNK, d), table.dtype),
                       pltpu.SemaphoreType.DMA],
        name="sc_gather_rows")
    def gather(t_hbm, i1_hbm, i2_hbm, o1_hbm, o2_hbm, i_v, rows_v, sem):
        wid = lax.axis_index("s") * SC_CORES + lax.axis_index("c")

        @pl.loop(0, per_w // SC_CHUNK)
        def _(j):
            base = wid * per_w + j * SC_CHUNK
            for i_hbm, o_hbm in ((i1_hbm, o1_hbm), (i2_hbm, o2_hbm)):
                pltpu.sync_copy(i_hbm.at[pl.ds(base, SC_CHUNK)], i_v)
                pltpu.async_copy(t_hbm.at[i_v], rows_v, sem).wait()
                pltpu.sync_copy(rows_v, o_hbm.at[pl.ds(base, SC_CHUNK)])

    return gather(table, idx1, idx2)


def _dest_kernel(rt_ref, starts_ref, d_ref):
    tr = rt_ref.shape[1]
    sub = lax.broadcasted_iota(jnp.int32, (N_EXPERTS, tr), 0)
    starts = starts_ref[:, :1]
    rows = []
    for e_row, pos_row in ((ROUTE_E1, ROUTE_POS1), (ROUTE_E2, ROUTE_POS2)):
        expert = rt_ref[e_row:e_row + 1, :].astype(jnp.int32)
        start = jnp.sum(jnp.where(sub == expert, starts, 0.0), axis=0, keepdims=True)
        rows.append(start + rt_ref[pos_row:pos_row + 1, :])
    d_ref[...] = jnp.concatenate(rows + [jnp.zeros((6, tr), F32)], axis=0).astype(jnp.int32)


def _routing_tables(rt, cnt, tm, n_tiles_max):
    n = rt.shape[1]
    counts = cnt[:, 0].astype(jnp.int32)
    padded = (counts + tm - 1) // tm * tm
    ends = jnp.cumsum(padded)
    starts = jnp.broadcast_to((ends - padded).astype(F32)[:, None], (N_EXPERTS, LANES))
    tr = min(2048, n)
    dest = pl.pallas_call(
        _dest_kernel,
        grid=(n // tr,),
        in_specs=[pl.BlockSpec((8, tr), lambda i: (0, i)), pl.BlockSpec((N_EXPERTS, LANES), lambda i: (0, 0))],
        out_specs=pl.BlockSpec((8, tr), lambda i: (0, i)),
        out_shape=jax.ShapeDtypeStruct((8, n), jnp.int32),
        compiler_params=_params(("parallel",)),
        name="route_dest",
    )(rt, starts)
    dest1, dest2 = dest[0], dest[1]
    tile_start = jnp.arange(n_tiles_max, dtype=jnp.int32) * tm
    tile_expert = jnp.sum((ends[None, :] <= tile_start[:, None]).astype(jnp.int32), axis=1)
    tile_expert = jnp.minimum(tile_expert, N_EXPERTS - 1)
    n_tiles = (ends[-1] // tm).astype(jnp.int32).reshape(1)
    valid = tile_start < ends[-1]
    prev_expert = jnp.concatenate([jnp.full((1,), -1, jnp.int32), tile_expert[:-1]])
    first = (valid & (tile_expert != prev_expert)).astype(jnp.int32)
    slot = (jnp.cumsum(first) - 1) % 2
    e_idx = jnp.arange(N_EXPERTS, dtype=jnp.int32)
    later = (padded > 0)[None, :] & (e_idx[None, :] > e_idx[:, None])
    next_run = jnp.min(jnp.where(later, e_idx[None, :], N_EXPERTS), axis=1)
    next_run = jnp.where(next_run == N_EXPERTS, -1, next_run).astype(jnp.int32)
    plan = (tile_expert, first, slot.astype(jnp.int32), next_run[tile_expert], n_tiles)
    return dest1, dest2, plan


WEIGHT_CHUNKS = 4


def _moe_tiles_kernel(te_ref, first_ref, slot_ref, nxt_ref, nt_ref, xs_ref, wg_hbm, wu_hbm, wd_hbm, ys_ref,
                      wg_buf, wu_buf, wd_buf, sem):
    t = pl.program_id(0)

    def weight_copies(expert, slot):
        copies = []
        for m, (hbm, buf) in enumerate(((wg_hbm, wg_buf), (wu_hbm, wu_buf), (wd_hbm, wd_buf))):
            rows = buf.shape[1] // WEIGHT_CHUNKS
            for c in range(WEIGHT_CHUNKS):
                part = pl.ds(c * rows, rows)
                copies.append(pltpu.make_async_copy(hbm.at[expert, part], buf.at[slot, part], sem.at[slot, m, c]))
        return copies

    @pl.when(t < nt_ref[0])
    def _():
        slot = slot_ref[t]

        @pl.when(t == 0)
        def _():
            for c in weight_copies(te_ref[0], 0):
                c.start()

        @pl.when(first_ref[t] == 1)
        def _():
            for c in weight_copies(te_ref[t], slot):
                c.wait()

            @pl.when(nxt_ref[t] >= 0)
            def _():
                for c in weight_copies(nxt_ref[t], 1 - slot):
                    c.start()

        x = _unpack_bf16_pairs(xs_ref[...]).astype(BF16)
        hg = jnp.dot(x, wg_buf[slot], preferred_element_type=F32)
        hu = jnp.dot(x, wu_buf[slot], preferred_element_type=F32)
        hid = (hg * jax.nn.sigmoid(hg) * hu).astype(BF16)
        ys_ref[...] = _pack_bf16_pairs(jnp.dot(hid, wd_buf[slot], preferred_element_type=F32))


def _moe_tiles(xs, plan, wg, wu, wd, tm):
    r, dh = xs.shape
    ne, d, de = wg.shape
    row = lambda t, *_: (t, 0)
    hbm = pl.BlockSpec(memory_space=pl.ANY)
    return pl.pallas_call(
        _moe_tiles_kernel,
        grid_spec=pltpu.PrefetchScalarGridSpec(
            num_scalar_prefetch=len(plan),
            grid=(r // tm,),
            in_specs=[pl.BlockSpec((tm, dh), row), hbm, hbm, hbm],
            out_specs=pl.BlockSpec((tm, dh), row),
            scratch_shapes=[
                pltpu.VMEM((2, d, de), wg.dtype),
                pltpu.VMEM((2, d, de), wu.dtype),
                pltpu.VMEM((2, de, d), wd.dtype),
                pltpu.SemaphoreType.DMA((2, 3, WEIGHT_CHUNKS)),
            ],
        ),
        out_shape=jax.ShapeDtypeStruct((r, dh), jnp.int32),
        compiler_params=_params(("arbitrary",)),
        name="moe_tiles",
    )(*plan, xs, wg, wu, wd)


def _combine_kernel(h_ref, y1_ref, y2_ref, route_ref, fg_ref, o_ref):
    lane = lax.broadcasted_iota(jnp.int32, route_ref.shape, 1)
    route = route_ref[...]
    w1 = jnp.sum(jnp.where(lane == ROUTE_W1, route, 0.0), axis=-1, keepdims=True)
    w2 = jnp.sum(jnp.where(lane == ROUTE_W2, route, 0.0), axis=-1, keepdims=True)
    y = h_ref[...] + w1 * _unpack_bf16_pairs(y1_ref[...]) + w2 * _unpack_bf16_pairs(y2_ref[...])
    o_ref[...] = y * lax.rsqrt(jnp.mean(y * y, axis=-1, keepdims=True) + EPS) * fg_ref[...]


def _combine(h, y1, y2, route, fg, tm):
    n, d = h.shape
    row = lambda i: (i, 0)
    return pl.pallas_call(
        _combine_kernel,
        grid=(n // tm,),
        in_specs=[
            pl.BlockSpec((tm, d), row),
            pl.BlockSpec((tm, d // 2), row),
            pl.BlockSpec((tm, d // 2), row),
            pl.BlockSpec((tm, LANES), row),
            pl.BlockSpec((1, d), lambda i: (0, 0)),
        ],
        out_specs=pl.BlockSpec((tm, d), row),
        out_shape=jax.ShapeDtypeStruct((n, d), F32),
        compiler_params=_params(("parallel",)),
        name="moe_combine",
    )(h, y1, y2, route, fg)


def _moe(a2p, route, rt, cnt, h, wg, wu, wd, fg, tm):
    n = h.shape[0]
    n_tiles_max = (2 * n) // tm + N_EXPERTS
    dest1, dest2, plan = _routing_tables(rt, cnt, tm, n_tiles_max)
    xs = _sc_scatter_rows(a2p, dest1, dest2, n_tiles_max * tm)
    ys = _moe_tiles(xs, plan, wg, wu, wd, tm)
    y1, y2 = _sc_gather_rows(ys, dest1, dest2)
    return _combine(h, y1, y2, route, fg, tm=min(1024, n))


def _dup_heads(w, n_heads):
    d = w.shape[0]
    w = w.reshape(d, n_heads, 1, HEAD_DIM)
    return jnp.broadcast_to(w, (d, n_heads, 2, HEAD_DIM)).reshape(d, n_heads * 2 * HEAD_DIM)


def _encoder(x, proj_meta, wts):
    b, s, d = x.shape
    x2d = x.reshape(b * s, d)
    proj = _norm_proj(x2d, wts["g1"], wts["w_ext"], tm=1024).reshape(b, s, -1)
    bk = min(1024, s)
    od0 = _diff_attn(proj, proj_meta, wts["lam_vecs"], wts["subln_g"], heads=(0,), bq=512, bk=min(512, s),
                     skip=True)
    od1 = _diff_attn(proj, proj_meta, wts["lam_vecs"], wts["subln_g"], heads=tuple(range(1, DA_HEADS)),
                     bq=256, bk=bk, skip=False)
    os_ = _win_attn(proj, proj_meta, wts["sink"])
    h, a2p, route, rt, cnt = _out_router(od0.reshape(b * s, -1), od1.reshape(b * s, -1), os_.reshape(b * s, -1), x2d,
                                     wts["wod"], wts["wos"],
                                     wts["g2"], wts["wr"], wts["br"], tm=1024)
    y = _moe(a2p, route, rt, cnt, h, wts["wg"], wts["wu"], wts["wd"], wts["fg"], tm=512)
    return y.reshape(b, s, d)


def kernel(x_prompt, x_sample, meta, norm1_g, w_in, lam_q1, lam_k1, lam_q2, lam_k2, subln_g, sink, w_out,
           norm2_g, w_gr, b_gr, w_er, b_er, w_gate, w_up, w_down, final_g):
    d = x_prompt.shape[-1]
    w = w_in[0]
    c_kd = 2 * DA_HEADS * HEAD_DIM
    c_vd = 2 * c_kd
    c_qs = c_vd + DA_HEADS * 2 * HEAD_DIM
    c_ks = c_qs + SW_HEADS * HEAD_DIM
    c_vs = c_ks + SW_KV_HEADS * HEAD_DIM
    w_ext = jnp.concatenate(
        [w[:, :c_ks], _dup_heads(w[:, c_ks:c_vs], SW_KV_HEADS), _dup_heads(w[:, c_vs:], SW_KV_HEADS)],
        axis=1).astype(BF16)
    w_router = jnp.concatenate([w_er[0], w_gr[0]], axis=1)
    w_router = jnp.pad(w_router, ((0, 0), (0, LANES - w_router.shape[1])))
    wr_hi = w_router.astype(BF16)
    wr_lo = (w_router - wr_hi.astype(F32)).astype(BF16)
    br = jnp.pad(jnp.concatenate([b_er[0], b_gr[0]]), (0, LANES - N_EXPERTS - N_GROUPS)).reshape(1, LANES)
    wo = w_out[0].astype(BF16)
    half = DA_HEADS * 2 * HEAD_DIM
    wts = dict(
        g1=norm1_g[0].reshape(1, d), w_ext=w_ext,
        lam_vecs=(lam_q1[0].reshape(1, -1), lam_k1[0].reshape(1, -1),
                  lam_q2[0].reshape(1, -1), lam_k2[0].reshape(1, -1)),
        subln_g=subln_g[0].reshape(1, -1), sink=sink[0].reshape(1, -1),
        wod=wo[:half], wos=wo[half:], g2=norm2_g[0].reshape(1, d),
        wr=jnp.concatenate([wr_hi, wr_lo], axis=1), br=br,
        wg=w_gate[0].astype(BF16), wu=w_up[0].astype(BF16), wd=w_down[0].astype(BF16),
        fg=final_g.reshape(1, d),
    )
    proj_meta = _norm_proj(meta, wts["g1"], w_ext, tm=N_META)
    proj_meta = jnp.pad(proj_meta, ((0, LANES - N_META), (0, 0)))
    return _encoder(x_prompt, proj_meta, wts), _encoder(x_sample, proj_meta, wts)
```

```python
import functools
import math

import jax
import jax.numpy as jnp
import numpy as np
from jax import lax
from jax.experimental import pallas as pl
from jax.experimental.pallas import tpu as pltpu
from jax.experimental.pallas import tpu_sc as plsc

F32 = jnp.float32
BF16 = jnp.bfloat16

N_META = 16
HEAD_DIM = 64
DA_HEADS = 4
SW_HEADS = 8
SW_KV_HEADS = 2
SW_GROUP = SW_HEADS // SW_KV_HEADS
WINDOW = 128
N_GROUPS = 4
EXPERTS_PER_GROUP = 8
N_EXPERTS = N_GROUPS * EXPERTS_PER_GROUP
EPS = 1e-6
SUBLN_EPS = 1e-5
NEG_INF = -1e30
LAM_INIT = 0.8 - 0.6 * math.exp(-0.3 * 0)
LANES = 128
VMEM_LIMIT = 48 * 1024 * 1024

TOKEN_TILE = 1024
MOE_TILE = 1024
DIFF_BQ, DIFF_BK = 256, 1024
SKIP_BLOCK = 512

COL_QD, COL_KD, COL_VD, COL_QS, COL_KS, COL_VS, N_COLBLK = 0, 4, 8, 12, 16, 18, 20
CONTRACT_LAST = (((1,), (1,)), ((), ()))


def _params(sem):
    return pltpu.CompilerParams(dimension_semantics=sem, vmem_limit_bytes=VMEM_LIMIT)


def _norm_proj_kernel(x_ref, g_ref, w_ref, o_ref):
    x = x_ref[...]
    ms = jnp.mean(x * x, axis=-1, keepdims=True)
    y = (x * lax.rsqrt(ms + EPS) * g_ref[...]).astype(BF16)
    o_ref[...] = jnp.dot(y, w_ref[...], preferred_element_type=F32).astype(o_ref.dtype)


def _norm_proj(x2d, g, w_ext, tm):
    n, d = x2d.shape
    wcols = w_ext.shape[1]
    return pl.pallas_call(
        _norm_proj_kernel,
        grid=(n // tm,),
        in_specs=[
            pl.BlockSpec((tm, d), lambda i: (i, 0)),
            pl.BlockSpec((1, d), lambda i: (0, 0)),
            pl.BlockSpec((d, wcols), lambda i: (0, 0)),
        ],
        out_specs=pl.BlockSpec((tm, wcols), lambda i: (i, 0)),
        out_shape=jax.ShapeDtypeStruct((n, wcols), BF16),
        compiler_params=_params(("parallel",)),
        name="norm_proj",
    )(x2d, g, w_ext)


POS_SPLIT = 64
N_SPLIT = 3
LOG2E = 1.4426950408889634


def _key_pos_features(seq):
    j = jnp.arange(seq, dtype=jnp.int32)[:, None]
    lane = jnp.arange(LANES, dtype=jnp.int32)[None, :]
    hi = (j // POS_SPLIT * POS_SPLIT).astype(F32)
    lo = (j % POS_SPLIT).astype(F32)
    feat = jnp.where(lane < N_SPLIT, 1.0,
                     jnp.where(lane < 2 * N_SPLIT, hi, jnp.where(lane < 3 * N_SPLIT, lo, 0.0)))
    return feat.astype(BF16)


def _split_bf16(x):
    pieces = []
    for _ in range(N_SPLIT):
        p = x.astype(BF16).astype(F32)
        pieces.append(p)
        x = x - p
    return pieces


SKIP_MARGIN = 135.0


def _diff_attn_head(h, q_ref, k_ref, v_ref, kf_ref, km_ref, vm_ref, lam, sg_ref,
                    o_ref, qe_ref, acc_ref, m_ref, sd_ref, s1_ref, kn_ref, *, bq, bk, seq, skip):
    i = pl.program_id(1)
    slope = 2.0 ** (-8.0 * (h + 1) / DA_HEADS) * LOG2E
    lane = lax.broadcasted_iota(jnp.int32, (1, LANES), 1)
    q = (q_ref[...].astype(F32) * (LOG2E / math.sqrt(HEAD_DIM))).astype(BF16)
    zero = jnp.zeros_like(q)
    qmaps = (jnp.where(lane < HEAD_DIM, q, zero), jnp.where(lane >= HEAD_DIM, q, zero))
    qpos = i * bq + lax.broadcasted_iota(jnp.int32, (bq, 1), 0)
    row_pieces = _split_bf16(-slope * qpos.astype(F32))
    slope_pieces = _split_bf16(jnp.full((1, 1), slope, F32))
    feat = jnp.zeros((bq, LANES), F32)
    for n in range(N_SPLIT):
        feat = jnp.where(lane == n, row_pieces[n], feat)
        feat = jnp.where(lane == N_SPLIT + n, slope_pieces[n], feat)
        feat = jnp.where(lane == 2 * N_SPLIT + n, slope_pieces[n], feat)
    feats = (feat.astype(BF16), (-feat).astype(BF16))
    for side in range(2):
        for c in range(2):
            qe_ref[side, c * bq:(c + 1) * bq, :LANES] = qmaps[c]
            qe_ref[side, c * bq:(c + 1) * bq, LANES:] = feats[side]

    ones_blk = jnp.broadcast_to(jnp.where(lane == 0, 1.0, 0.0).astype(BF16), (bk, LANES))
    meta_mask = jnp.where(lane < N_META, 0.0, NEG_INF)

    def attend(s, v_ext):
        m_old = m_ref[...]
        m_new = jnp.maximum(m_old, jnp.max(s, axis=-1, keepdims=True))
        alpha = jnp.exp2(m_old - m_new)
        p = jnp.exp2(s - jnp.tile(m_new, (1, s.shape[1] // LANES))).astype(BF16)
        acc_ref[...] = jnp.tile(alpha, (1, 2)) * acc_ref[...] + jnp.dot(p, v_ext, preferred_element_type=F32)
        m_ref[...] = m_new

    m_ref[...] = jnp.full(m_ref.shape, NEG_INF, F32)
    acc_ref[...] = jnp.zeros(acc_ref.shape, F32)

    s0_ref = sd_ref.at[:, :bk]
    nblk = seq // bk
    diag = (i * bq) // bk

    def scores_block(dst_ref, kb, side, edge=None):
        start = pl.multiple_of(kb * bk, bk)
        k_ext = jnp.concatenate([k_ref[pl.ds(start, bk), :], kf_ref[pl.ds(start, bk), :]], axis=1)
        s = lax.dot_general(qe_ref[side], k_ext, CONTRACT_LAST, preferred_element_type=F32)
        dst_ref[...] = s if edge is None else s + edge

    def consume_block(src_ref, kb):
        start = pl.multiple_of(kb * bk, bk)
        attend(src_ref[...], jnp.concatenate([v_ref[pl.ds(start, bk), :], ones_blk], axis=1))

    d_start = pl.multiple_of(diag * bk, bk)
    kpos = diag * bk + lax.broadcasted_iota(jnp.int32, (1, bk), 1)
    bias = jnp.concatenate([-slope * jnp.abs(qpos - kpos).astype(F32),
                            jnp.broadcast_to(meta_mask, (bq, LANES))], axis=1)
    k_first = jnp.concatenate([k_ref[pl.ds(d_start, bk), :], km_ref[...]], axis=0)
    v_first = jnp.concatenate(
        [jnp.concatenate([v_ref[pl.ds(d_start, bk), :], vm_ref[...]], axis=0),
         jnp.broadcast_to(ones_blk[:1], (bk + LANES, LANES))], axis=1)
    sd_ref[...] = lax.dot_general(qe_ref[0, :, :LANES], k_first, CONTRACT_LAST,
                                  preferred_element_type=F32) + jnp.tile(bias, (2, 1))

    def all_blocks():
        def key_block(pos):
            t = pos - 1
            return t + jnp.where(t >= diag, 1, 0)

        def scores_into(dst_ref, pos):
            kb = key_block(pos)
            scores_block(dst_ref, kb, jnp.where(kb > diag, 1, 0))

        scores_into(s1_ref, 1)
        attend(sd_ref[...], v_first)
        for u in range((nblk - 2) // 2):
            scores_into(s0_ref, 2 * u + 2)
            consume_block(s1_ref, key_block(2 * u + 1))
            scores_into(s1_ref, 2 * u + 3)
            consume_block(s0_ref, key_block(2 * u + 2))
        consume_block(s1_ref, key_block(nblk - 1))

    def neighbours_only():
        prev = jnp.maximum(diag - 1, 0)
        nxt = jnp.minimum(diag + 1, nblk - 1)
        scores_block(s1_ref, prev, 0, edge=jnp.where(diag == 0, NEG_INF, 0.0))
        attend(sd_ref[...], v_first)
        scores_block(s0_ref, nxt, 1, edge=jnp.where(diag == nblk - 1, NEG_INF, 0.0))
        consume_block(s1_ref, prev)
        consume_block(s0_ref, nxt)

    if nblk == 1:
        attend(sd_ref[...], v_first)
    elif skip:
        assert bq == bk and nblk > 2
        @pl.when(i == 0)
        def _():
            kf32 = k_ref[...].astype(F32)
            kn_ref[0] = jnp.max(jnp.sum(kf32 * kf32, axis=-1, keepdims=True))
        qf32 = q.astype(F32)
        qn2 = jnp.max(jnp.sum(qf32 * qf32, axis=-1, keepdims=True), axis=0, keepdims=True)
        s_meta = lax.dot_general(qe_ref[0, :, :LANES], km_ref[...], CONTRACT_LAST,
                                 preferred_element_type=F32) + meta_mask
        m_low = jnp.min(jnp.max(s_meta, axis=-1, keepdims=True), axis=0, keepdims=True)
        reach = (jnp.sqrt(qn2 * kn_ref[0]) - m_low + SKIP_MARGIN) * (1.0 / slope)
        near = jnp.clip(reach, 0.0, float(seq))[0, 0] <= float(bk)

        @pl.when(near)
        def _():
            neighbours_only()

        @pl.when(jnp.logical_not(near))
        def _():
            all_blocks()
    else:
        all_blocks()

    o1 = acc_ref[:bq, :LANES] / acc_ref[:bq, LANES:LANES + 1]
    o2 = acc_ref[bq:, :LANES] / acc_ref[bq:, LANES:LANES + 1]
    o = o1 - lam * o2
    o = o * lax.rsqrt(jnp.mean(o * o, axis=-1, keepdims=True) + SUBLN_EPS) * sg_ref[...]
    o_ref[...] = (o * (1.0 - LAM_INIT)).astype(o_ref.dtype)


def _diff_attn_kernel(*refs, heads, **kw):
    nh = len(heads)
    q_refs, k_refs, v_refs = refs[:nh], refs[nh:2 * nh], refs[2 * nh:3 * nh]
    kf_ref = refs[3 * nh]
    km_refs, vm_refs = refs[3 * nh + 1:4 * nh + 1], refs[4 * nh + 1:5 * nh + 1]
    lq1_ref, lk1_ref, lq2_ref, lk2_ref, sg_ref, o_ref, qe_ref, acc_ref, m_ref, sd_ref, s1_ref, kn_ref = refs[5 * nh + 1:]
    lam = (jnp.exp(jnp.sum(lq1_ref[...] * lk1_ref[...], axis=-1, keepdims=True))
           - jnp.exp(jnp.sum(lq2_ref[...] * lk2_ref[...], axis=-1, keepdims=True)) + LAM_INIT)
    for n, h in enumerate(heads):
        _diff_attn_head(h, q_refs[n], k_refs[n], v_refs[n], kf_ref, km_refs[n], vm_refs[n], lam, sg_ref,
                        o_ref.at[:, n * LANES:(n + 1) * LANES],
                        qe_ref.at[n], acc_ref.at[n], m_ref.at[n], sd_ref.at[n], s1_ref.at[n], kn_ref, **kw)


def _diff_attn(proj, proj_meta, lam_vecs, subln_g, heads, bq, bk, skip):
    b, s, _ = proj.shape
    bq = min(bq, s)
    assert bk % bq == 0 and s % bk == 0 and (s // bk == 1 or (s // bk) % 2 == 0)
    nh = len(heads)
    vec = pl.BlockSpec((1, HEAD_DIM), lambda bi, i: (0, 0))
    q_specs = [pl.BlockSpec((None, bq, LANES), lambda bi, i, c=COL_QD + h: (bi, i, c)) for h in heads]
    k_specs = [pl.BlockSpec((None, s, LANES), lambda bi, i, c=COL_KD + h: (bi, 0, c)) for h in heads]
    v_specs = [pl.BlockSpec((None, s, LANES), lambda bi, i, c=COL_VD + h: (bi, 0, c)) for h in heads]
    km_specs = [pl.BlockSpec((LANES, LANES), lambda bi, i, c=COL_KD + h: (0, c)) for h in heads]
    vm_specs = [pl.BlockSpec((LANES, LANES), lambda bi, i, c=COL_VD + h: (0, c)) for h in heads]
    return pl.pallas_call(
        functools.partial(_diff_attn_kernel, heads=heads, bq=bq, bk=bk, seq=s, skip=skip),
        grid=(b, s // bq),
        in_specs=(q_specs + k_specs + v_specs + [pl.BlockSpec((s, LANES), lambda bi, i: (0, 0))]
                  + km_specs + vm_specs + [vec, vec, vec, vec, pl.BlockSpec((1, LANES), lambda bi, i: (0, 0))]),
        out_specs=pl.BlockSpec((None, bq, nh * LANES), lambda bi, i: (bi, i, 0)),
        out_shape=jax.ShapeDtypeStruct((b, s, nh * LANES), BF16),
        scratch_shapes=[
            pltpu.VMEM((nh, 2, 2 * bq, 2 * LANES), BF16),
            pltpu.VMEM((nh, 2 * bq, 2 * LANES), F32),
            pltpu.VMEM((nh, 2 * bq, LANES), F32),
            pltpu.VMEM((nh, 2 * bq, bk + LANES), F32),
            pltpu.VMEM((nh, 2 * bq, bk), F32),
            pltpu.SMEM((1,), F32),
        ],
        compiler_params=_params(("parallel", "arbitrary")),
        name="diff_attn",
    )(*([proj] * (3 * nh)), _key_pos_features(s), *([proj_meta] * (2 * nh)), *lam_vecs, subln_g)


WIN_KEYS = 4 * WINDOW
WIN_QBLOCKS = 8


def _win_tables():
    r = np.arange(WINDOW)
    qf = np.zeros((SW_KV_HEADS, SW_GROUP * WINDOW, LANES), np.float32)
    for head in range(SW_HEADS):
        slope = 2.0 ** (-8.0 * (head + 1) / SW_HEADS)
        i_rel = WINDOW + r
        hi, lo = i_rel // POS_SPLIT * POS_SPLIT, i_rel % POS_SPLIT
        rows = qf[head // SW_GROUP, (head % SW_GROUP) * WINDOW:(head % SW_GROUP + 1) * WINDOW]
        rows[:, 0], rows[:, 1], rows[:, 2], rows[:, 3] = -slope * hi, -slope * lo, slope, slope
        rows[:, 4:8] = -rows[:, 0:4]
    kf = np.zeros((2, WINDOW, LANES), np.float32)
    for n, (blk, right) in enumerate(((0, 0), (2, 1))):
        j_rel = blk * WINDOW + r
        o = 4 * right
        kf[n, :, o], kf[n, :, o + 1] = 1.0, 1.0
        kf[n, :, o + 2], kf[n, :, o + 3] = j_rel // POS_SPLIT * POS_SPLIT, j_rel % POS_SPLIT
    mask = np.zeros((2, WINDOW, WINDOW), np.float32)
    mask[0] = np.where(r[None, :] >= r[:, None], 0.0, NEG_INF)
    mask[1] = np.where(r[None, :] <= r[:, None], 0.0, NEG_INF)
    cur = np.zeros((SW_KV_HEADS, SW_GROUP * WINDOW, WINDOW), np.float32)
    for head in range(SW_HEADS):
        slope = 2.0 ** (-8.0 * (head + 1) / SW_HEADS)
        cur[head // SW_GROUP, (head % SW_GROUP) * WINDOW:(head % SW_GROUP + 1) * WINDOW] = (
            -slope * np.abs(r[:, None] - r[None, :]))
    return jnp.asarray(qf, BF16), jnp.asarray(kf, BF16), jnp.asarray(mask, F32), jnp.asarray(cur, F32)


def _win_attn_kernel(q_ref, kp_ref, kc_ref, kn_ref, vp_ref, vc_ref, vn_ref, km_ref, vm_ref, qf_ref, kf_ref,
                     mask_ref, cur_ref, sink_ref, o_ref, *, nstep):
    c = pl.program_id(1)
    lane = lax.broadcasted_iota(jnp.int32, (1, LANES), 1)
    scale = jnp.asarray(1.0 / math.sqrt(HEAD_DIM), BF16)
    mask_meta = jnp.where(lane < N_META, 0.0, NEG_INF)
    row = lax.broadcasted_iota(jnp.int32, (SW_GROUP * WINDOW, 1), 0)
    ones_blk = jnp.broadcast_to(jnp.where(lane == 0, 1.0, 0.0).astype(BF16), (WIN_KEYS, LANES))
    zeros_blk = jnp.zeros((WINDOW, LANES), BF16)
    gw = SW_GROUP * HEAD_DIM
    own = [slice(j * WINDOW, (j + 1) * WINDOW) for j in range(WIN_QBLOCKS)]
    kblk = [kp_ref] + [kc_ref.at[r] for r in own] + [kn_ref]
    vblk = [vp_ref] + [vc_ref.at[r] for r in own] + [vn_ref]
    for j in range(WIN_QBLOCKS):
        rq = slice(j * WINDOW, (j + 1) * WINDOW)
        edge_p = jnp.where(c == 0, NEG_INF, 0.0) if j == 0 else 0.0
        edge_n = jnp.where(c == nstep - 1, NEG_INF, 0.0) if j == WIN_QBLOCKS - 1 else 0.0
        mask_prev = jnp.tile(mask_ref[0] + edge_p, (SW_GROUP, 1))
        mask_next = jnp.tile(mask_ref[1] + edge_n, (SW_GROUP, 1))
        for g in range(SW_KV_HEADS):
            kv = slice(g * LANES, (g + 1) * LANES)
            rows = []
            for hh in range(SW_GROUP):
                col = g * gw + (hh // 2) * LANES
                qb = q_ref[rq, col:col + LANES] * scale
                keep = (lane < HEAD_DIM) if hh % 2 == 0 else (lane >= HEAD_DIM)
                rows.append(jnp.where(keep, qb, jnp.zeros_like(qb)))
            q_ext = jnp.concatenate([jnp.concatenate(rows, axis=0), qf_ref[g]], axis=1)
            k_all = jnp.concatenate([
                jnp.concatenate([km_ref[:, kv], zeros_blk], axis=1),
                jnp.concatenate([kblk[j][:, kv], kf_ref[0]], axis=1),
                jnp.concatenate([kblk[j + 1][:, kv], zeros_blk], axis=1),
                jnp.concatenate([kblk[j + 2][:, kv], kf_ref[1]], axis=1)], axis=0)
            s = lax.dot_general(q_ext, k_all, CONTRACT_LAST, preferred_element_type=F32)
            s_meta = s[:, :LANES] + mask_meta
            s_prev = s[:, LANES:2 * LANES] + mask_prev
            s_cur = s[:, 2 * LANES:3 * LANES] + cur_ref[g]
            s_next = s[:, 3 * LANES:] + mask_next
            sink = jnp.zeros((SW_GROUP * WINDOW, 1), F32)
            for hh in range(SW_GROUP):
                sink = jnp.where(row // WINDOW == hh, sink_ref[0, g * SW_GROUP + hh], sink)
            parts = (s_meta, s_prev, s_cur, s_next)
            m = jnp.maximum(jnp.maximum(s_meta, s_prev), jnp.maximum(s_cur, s_next))
            m = jnp.maximum(jnp.max(m, axis=-1, keepdims=True), sink)
            p = jnp.concatenate([jnp.exp(x - m) for x in parts], axis=1).astype(BF16)
            v_all = jnp.concatenate(
                [jnp.concatenate([vm_ref[:, kv], vblk[j][:, kv], vblk[j + 1][:, kv], vblk[j + 2][:, kv]], axis=0),
                 ones_blk], axis=1)
            acc = jnp.dot(p, v_all, preferred_element_type=F32)
            denom = acc[:, LANES:LANES + 1] + jnp.exp(sink - m)
            o = acc[:, :LANES] / denom
            for lb in range(SW_GROUP // 2):
                even = o[(2 * lb) * WINDOW:(2 * lb + 1) * WINDOW]
                odd = o[(2 * lb + 1) * WINDOW:(2 * lb + 2) * WINDOW]
                col = g * gw + lb * LANES
                o_ref[rq, col:col + LANES] = jnp.where(lane < HEAD_DIM, even, odd).astype(o_ref.dtype)


def _win_attn(proj, proj_meta, sink):
    b, s, _ = proj.shape
    nblk = s // WINDOW
    nq = WIN_QBLOCKS
    nstep = nblk // nq
    qw = SW_HEADS * HEAD_DIM
    kvw = SW_KV_HEADS * LANES
    qf, kf, mask, cur = _win_tables()

    def edge_spec(col, shift):
        return pl.BlockSpec(
            (None, WINDOW, kvw),
            lambda bi, c: (bi, jnp.clip(nq * c + shift, 0, nblk - 1), col * LANES // kvw))

    def pair_spec(col):
        return pl.BlockSpec((None, nq * WINDOW, kvw), lambda bi, c: (bi, c, col * LANES // kvw))

    return pl.pallas_call(
        functools.partial(_win_attn_kernel, nstep=nstep),
        grid=(b, nstep),
        in_specs=[
            pl.BlockSpec((None, nq * WINDOW, qw), lambda bi, c: (bi, c, COL_QS * LANES // qw)),
            edge_spec(COL_KS, -1), pair_spec(COL_KS), edge_spec(COL_KS, nq),
            edge_spec(COL_VS, -1), pair_spec(COL_VS), edge_spec(COL_VS, nq),
            pl.BlockSpec((LANES, kvw), lambda bi, c: (0, COL_KS * LANES // kvw)),
            pl.BlockSpec((LANES, kvw), lambda bi, c: (0, COL_VS * LANES // kvw)),
            pl.BlockSpec((SW_KV_HEADS, SW_GROUP * WINDOW, LANES), lambda bi, c: (0, 0, 0)),
            pl.BlockSpec((2, WINDOW, LANES), lambda bi, c: (0, 0, 0)),
            pl.BlockSpec((2, WINDOW, WINDOW), lambda bi, c: (0, 0, 0)),
            pl.BlockSpec((SW_KV_HEADS, SW_GROUP * WINDOW, WINDOW), lambda bi, c: (0, 0, 0)),
            pl.BlockSpec(memory_space=pltpu.SMEM),
        ],
        out_specs=pl.BlockSpec((None, nq * WINDOW, qw), lambda bi, c: (bi, c, 0)),
        out_shape=jax.ShapeDtypeStruct((b, s, qw), BF16),
        compiler_params=_params(("parallel", "arbitrary")),
        name="win_attn",
    )(proj, proj, proj, proj, proj, proj, proj, proj_meta, proj_meta, qf, kf, mask, cur, sink)


ROUTE_E1, ROUTE_E2, ROUTE_POS1, ROUTE_POS2, ROUTE_W1, ROUTE_W2 = range(6)


def _pack_bf16_pairs(x):
    k = x.shape[1] // 2
    bits = lax.bitcast_convert_type(x.astype(BF16).astype(F32), jnp.uint32)
    return lax.bitcast_convert_type(bits[:, :k] | (bits[:, k:] >> 16), jnp.int32)


def _unpack_bf16_pairs(w):
    bits = lax.bitcast_convert_type(w, jnp.uint32)
    hi = lax.bitcast_convert_type(bits & jnp.uint32(0xFFFF0000), F32)
    lo = lax.bitcast_convert_type(bits << 16, F32)
    return jnp.concatenate([hi, lo], axis=1)


def _out_router_kernel(od0_ref, od1_ref, os_ref, x_ref, wod_ref, wos_ref, g2_ref, wr_ref, br_ref,
                       h_ref, a_ref, route_ref, rt_ref, cnt_ref, base_ref):
    @pl.when(pl.program_id(0) == 0)
    def _():
        base_ref[...] = jnp.zeros_like(base_ref)

    h = (x_ref[...]
         + jnp.dot(jnp.concatenate([od0_ref[...], od1_ref[...]], axis=1), wod_ref[...],
                   preferred_element_type=F32)
         + jnp.dot(os_ref[...], wos_ref[...], preferred_element_type=F32))
    h_ref[...] = h
    a = h * lax.rsqrt(jnp.mean(h * h, axis=-1, keepdims=True) + EPS) * g2_ref[...]
    a_hi = a.astype(BF16)
    a_ref[...] = _pack_bf16_pairs(a)
    a_lo = (a - a_hi.astype(F32)).astype(BF16)
    hi_terms = jnp.dot(a_hi, wr_ref[...], preferred_element_type=F32)
    logits = (hi_terms[:, :LANES] + hi_terms[:, LANES:]
              + jnp.dot(a_lo, wr_ref[:, :LANES], preferred_element_type=F32)) + br_ref[...]
    tm = logits.shape[0]
    lt = jnp.transpose(logits)
    le = lt[:N_EXPERTS]
    sub_g = lax.broadcasted_iota(jnp.int32, (8, tm), 0)
    lg = jnp.where(sub_g < N_GROUPS, lt[N_EXPERTS:N_EXPERTS + 8], NEG_INF)
    sub_e = lax.broadcasted_iota(jnp.int32, (N_EXPERTS, tm), 0)

    def first_argmax(vals, mask, sub, size):
        mx = jnp.max(vals, axis=0, keepdims=True)
        idx = jnp.min(jnp.where(mask & (vals == mx), sub, size), axis=0, keepdims=True)
        return mx, idx

    gmax, gidx = first_argmax(lg, sub_g < N_GROUPS, sub_g, 8)
    g_w = 1.0 / jnp.sum(jnp.exp(lg - gmax), axis=0, keepdims=True)
    in_group = (sub_e // EXPERTS_PER_GROUP) == gidx
    m1, i1 = first_argmax(jnp.where(in_group, le, NEG_INF), in_group, sub_e, N_EXPERTS)
    rest = in_group & (sub_e != i1)
    m2, i2 = first_argmax(jnp.where(rest, le, NEG_INF), rest, sub_e, N_EXPERTS)
    r = jnp.exp(m2 - m1)
    w1 = g_w / (1.0 + r)
    w2 = g_w * r / (1.0 + r)

    rix = lax.broadcasted_iota(jnp.int32, (tm, tm), 0)
    cix = lax.broadcasted_iota(jnp.int32, (tm, tm), 1)
    earlier = jnp.where(rix < cix, 1.0, 0.0).astype(BF16)
    oh1 = jnp.where(sub_e == i1, 1.0, 0.0)
    oh2 = jnp.where(sub_e == i2, 1.0, 0.0)
    before = jnp.dot(jnp.concatenate([oh1, oh2], axis=0).astype(BF16), earlier, preferred_element_type=F32)
    base1 = base_ref[:, :1]
    pos1 = jnp.sum(oh1 * (base1 + before[:N_EXPERTS]), axis=0, keepdims=True)
    base2 = base1 + jnp.sum(oh1, axis=1, keepdims=True)
    pos2 = jnp.sum(oh2 * (base2 + before[N_EXPERTS:]), axis=0, keepdims=True)
    total = jnp.broadcast_to(base2 + jnp.sum(oh2, axis=1, keepdims=True), base_ref.shape)
    base_ref[...] = total
    cnt_ref[...] = total

    fields = jnp.concatenate([i1.astype(F32), i2.astype(F32), pos1, pos2, w1, w2, jnp.zeros((2, tm), F32)], axis=0)
    rt_ref[...] = fields
    route_ref[...] = jnp.transpose(jnp.concatenate([fields, jnp.zeros((LANES - 8, tm), F32)], axis=0))


def _out_router(od0, od1, os_, x2d, wod, wos, g2, wr, br, tm):
    n, d = x2d.shape
    half = os_.shape[1]
    row = lambda i: (i, 0)
    const = lambda i: (0, 0)
    return pl.pallas_call(
        _out_router_kernel,
        grid=(n // tm,),
        in_specs=[
            pl.BlockSpec((tm, od0.shape[1]), row),
            pl.BlockSpec((tm, od1.shape[1]), row),
            pl.BlockSpec((tm, half), row),
            pl.BlockSpec((tm, d), row),
            pl.BlockSpec((half, d), const),
            pl.BlockSpec((half, d), const),
            pl.BlockSpec((1, d), const),
            pl.BlockSpec((d, 2 * LANES), const),
            pl.BlockSpec((1, LANES), const),
        ],
        out_specs=[
            pl.BlockSpec((tm, d), row),
            pl.BlockSpec((tm, d // 2), row),
            pl.BlockSpec((tm, LANES), row),
            pl.BlockSpec((8, tm), lambda i: (0, i)),
            pl.BlockSpec((N_EXPERTS, LANES), const),
        ],
        out_shape=[
            jax.ShapeDtypeStruct((n, d), F32),
            jax.ShapeDtypeStruct((n, d // 2), jnp.int32),
            jax.ShapeDtypeStruct((n, LANES), F32),
            jax.ShapeDtypeStruct((8, n), F32),
            jax.ShapeDtypeStruct((N_EXPERTS, LANES), F32),
        ],
        scratch_shapes=[pltpu.VMEM((N_EXPERTS, LANES), F32)],
        compiler_params=_params(("arbitrary",)),
        name="out_router",
    )(od0, od1, os_, x2d, wod, wos, g2, wr, br)


SC_CORES, SC_SUBCORES = 2, 16
SC_WORKERS = SC_CORES * SC_SUBCORES
SC_CHUNK = 128


def _sc_mesh():
    return plsc.VectorSubcoreMesh(core_axis_name="c", subcore_axis_name="s",
                                  num_cores=SC_CORES, num_subcores=SC_SUBCORES)


def _sc_scatter_rows(x, idx1, idx2, n_out):
    n, d = x.shape
    assert n % (SC_WORKERS * SC_CHUNK) == 0
    per_w = n // SC_WORKERS

    @functools.partial(
        pl.kernel, mesh=_sc_mesh(), out_type=jax.ShapeDtypeStruct((n_out, d), x.dtype),
        scratch_types=[pltpu.VMEM((SC_CHUNK,), jnp.int32), pltpu.VMEM((SC_CHUNK,), jnp.int32),
                       pltpu.VMEM((SC_CHUNK, d), x.dtype), pltpu.SemaphoreType.DMA],
        name="sc_scatter_rows")
    def scatter(x_hbm, i1_hbm, i2_hbm, o_hbm, i1_v, i2_v, rows_v, sem):
        wid = lax.axis_index("s") * SC_CORES + lax.axis_index("c")

        @pl.loop(0, per_w // SC_CHUNK)
        def _(j):
            base = wid * per_w + j * SC_CHUNK
            pltpu.sync_copy(i1_hbm.at[pl.ds(base, SC_CHUNK)], i1_v)
            pltpu.sync_copy(i2_hbm.at[pl.ds(base, SC_CHUNK)], i2_v)
            pltpu.sync_copy(x_hbm.at[pl.ds(base, SC_CHUNK)], rows_v)
            pltpu.async_copy(rows_v, o_hbm.at[i1_v], sem).wait()
            pltpu.async_copy(rows_v, o_hbm.at[i2_v], sem).wait()

    return scatter(x, idx1, idx2)


def _sc_gather_rows(table, idx1, idx2):
    n = idx1.shape[0]
    d = table.shape[1]
    assert n % (SC_WORKERS * SC_CHUNK) == 0
    per_w = n // SC_WORKERS
    out = jax.ShapeDtypeStruct((n, d), table.dtype)

    @functools.partial(
        pl.kernel, mesh=_sc_mesh(), out_type=(out, out),
        scratch_types=[pltpu.VMEM((SC_CHUNK,), jnp.int32), pltpu.VMEM((SC_CHUNK, d), table.dtype),
                       pltpu.SemaphoreType.DMA],
        name="sc_gather_rows")
    def gather(t_hbm, i1_hbm, i2_hbm, o1_hbm, o2_hbm, i_v, rows_v, sem):
        wid = lax.axis_index("s") * SC_CORES + lax.axis_index("c")

        @pl.loop(0, per_w // SC_CHUNK)
        def _(j):
            base = wid * per_w + j * SC_CHUNK
            for i_hbm, o_hbm in ((i1_hbm, o1_hbm), (i2_hbm, o2_hbm)):
                pltpu.sync_copy(i_hbm.at[pl.ds(base, SC_CHUNK)], i_v)
                pltpu.async_copy(t_hbm.at[i_v], rows_v, sem).wait()
                pltpu.sync_copy(rows_v, o_hbm.at[pl.ds(base, SC_CHUNK)])

    return gather(table, idx1, idx2)


def _dest_kernel(rt_ref, starts_ref, d_ref):
    tr = rt_ref.shape[1]
    sub = lax.broadcasted_iota(jnp.int32, (N_EXPERTS, tr), 0)
    starts = starts_ref[:, :1]
    rows = []
    for e_row, pos_row in ((ROUTE_E1, ROUTE_POS1), (ROUTE_E2, ROUTE_POS2)):
        expert = rt_ref[e_row:e_row + 1, :].astype(jnp.int32)
        start = jnp.sum(jnp.where(sub == expert, starts, 0.0), axis=0, keepdims=True)
        rows.append(start + rt_ref[pos_row:pos_row + 1, :])
    d_ref[...] = jnp.concatenate(rows + [jnp.zeros((6, tr), F32)], axis=0).astype(jnp.int32)


def _routing_tables(rt, cnt, tm, n_tiles_max):
    n = rt.shape[1]
    counts = cnt[:, 0].astype(jnp.int32)
    padded = (counts + tm - 1) // tm * tm
    ends = jnp.cumsum(padded)
    starts = jnp.broadcast_to((ends - padded).astype(F32)[:, None], (N_EXPERTS, LANES))
    tr = min(2048, n)
    dest = pl.pallas_call(
        _dest_kernel,
        grid=(n // tr,),
        in_specs=[pl.BlockSpec((8, tr), lambda i: (0, i)), pl.BlockSpec((N_EXPERTS, LANES), lambda i: (0, 0))],
        out_specs=pl.BlockSpec((8, tr), lambda i: (0, i)),
        out_shape=jax.ShapeDtypeStruct((8, n), jnp.int32),
        compiler_params=_params(("parallel",)),
        name="route_dest",
    )(rt, starts)
    dest1, dest2 = dest[0], dest[1]
    tile_start = jnp.arange(n_tiles_max, dtype=jnp.int32) * tm
    tile_expert = jnp.sum((ends[None, :] <= tile_start[:, None]).astype(jnp.int32), axis=1)
    tile_expert = jnp.minimum(tile_expert, N_EXPERTS - 1)
    n_tiles = (ends[-1] // tm).astype(jnp.int32).reshape(1)
    valid = tile_start < ends[-1]
    prev_expert = jnp.concatenate([jnp.full((1,), -1, jnp.int32), tile_expert[:-1]])
    first = (valid & (tile_expert != prev_expert)).astype(jnp.int32)
    slot = (jnp.cumsum(first) - 1) % 2
    e_idx = jnp.arange(N_EXPERTS, dtype=jnp.int32)
    later = (padded > 0)[None, :] & (e_idx[None, :] > e_idx[:, None])
    next_run = jnp.min(jnp.where(later, e_idx[None, :], N_EXPERTS), axis=1)
    next_run = jnp.where(next_run == N_EXPERTS, -1, next_run).astype(jnp.int32)
    plan = (tile_expert, first, slot.astype(jnp.int32), next_run[tile_expert], n_tiles)
    return dest1, dest2, plan


def _moe_tiles_kernel(te_ref, first_ref, slot_ref, nxt_ref, nt_ref, xs_ref, wg_hbm, wu_hbm, wd_hbm, ys_ref,
                      wg_buf, wu_buf, wd_buf, sem):
    t = pl.program_id(0)

    def weight_copies(expert, slot):
        return (pltpu.make_async_copy(wg_hbm.at[expert], wg_buf.at[slot], sem.at[slot, 0]),
                pltpu.make_async_copy(wu_hbm.at[expert], wu_buf.at[slot], sem.at[slot, 1]),
                pltpu.make_async_copy(wd_hbm.at[expert], wd_buf.at[slot], sem.at[slot, 2]))

    @pl.when(t < nt_ref[0])
    def _():
        slot = slot_ref[t]

        @pl.when(t == 0)
        def _():
            for c in weight_copies(te_ref[0], 0):
                c.start()

        @pl.when(first_ref[t] == 1)
        def _():
            for c in weight_copies(te_ref[t], slot):
                c.wait()

            @pl.when(nxt_ref[t] >= 0)
            def _():
                for c in weight_copies(nxt_ref[t], 1 - slot):
                    c.start()

        x = _unpack_bf16_pairs(xs_ref[...]).astype(BF16)
        hg = jnp.dot(x, wg_buf[slot], preferred_element_type=F32)
        hu = jnp.dot(x, wu_buf[slot], preferred_element_type=F32)
        hid = (hg * jax.nn.sigmoid(hg) * hu).astype(BF16)
        ys_ref[...] = _pack_bf16_pairs(jnp.dot(hid, wd_buf[slot], preferred_element_type=F32))


def _moe_tiles(xs, plan, wg, wu, wd, tm):
    r, dh = xs.shape
    ne, d, de = wg.shape
    row = lambda t, *_: (t, 0)
    hbm = pl.BlockSpec(memory_space=pl.ANY)
    return pl.pallas_call(
        _moe_tiles_kernel,
        grid_spec=pltpu.PrefetchScalarGridSpec(
            num_scalar_prefetch=len(plan),
            grid=(r // tm,),
            in_specs=[pl.BlockSpec((tm, dh), row), hbm, hbm, hbm],
            out_specs=pl.BlockSpec((tm, dh), row),
            scratch_shapes=[
                pltpu.VMEM((2, d, de), wg.dtype),
                pltpu.VMEM((2, d, de), wu.dtype),
                pltpu.VMEM((2, de, d), wd.dtype),
                pltpu.SemaphoreType.DMA((2, 3)),
            ],
        ),
        out_shape=jax.ShapeDtypeStruct((r, dh), jnp.int32),
        compiler_params=_params(("arbitrary",)),
        name="moe_tiles",
    )(*plan, xs, wg, wu, wd)


def _combine_kernel(h_ref, y1_ref, y2_ref, route_ref, fg_ref, o_ref):
    lane = lax.broadcasted_iota(jnp.int32, route_ref.shape, 1)
    route = route_ref[...]
    w1 = jnp.sum(jnp.where(lane == ROUTE_W1, route, 0.0), axis=-1, keepdims=True)
    w2 = jnp.sum(jnp.where(lane == ROUTE_W2, route, 0.0), axis=-1, keepdims=True)
    y = h_ref[...] + w1 * _unpack_bf16_pairs(y1_ref[...]) + w2 * _unpack_bf16_pairs(y2_ref[...])
    o_ref[...] = y * lax.rsqrt(jnp.mean(y * y, axis=-1, keepdims=True) + EPS) * fg_ref[...]


def _combine(h, y1, y2, route, fg, tm):
    n, d = h.shape
    row = lambda i: (i, 0)
    return pl.pallas_call(
        _combine_kernel,
        grid=(n // tm,),
        in_specs=[
            pl.BlockSpec((tm, d), row),
            pl.BlockSpec((tm, d // 2), row),
            pl.BlockSpec((tm, d // 2), row),
            pl.BlockSpec((tm, LANES), row),
            pl.BlockSpec((1, d), lambda i: (0, 0)),
        ],
        out_specs=pl.BlockSpec((tm, d), row),
        out_shape=jax.ShapeDtypeStruct((n, d), F32),
        compiler_params=_params(("parallel",)),
        name="moe_combine",
    )(h, y1, y2, route, fg)


def _moe(a2p, route, rt, cnt, h, wg, wu, wd, fg, tm):
    n = h.shape[0]
    n_tiles_max = (2 * n) // tm + N_EXPERTS
    dest1, dest2, plan = _routing_tables(rt, cnt, tm, n_tiles_max)
    xs = _sc_scatter_rows(a2p, dest1, dest2, n_tiles_max * tm)
    ys = _moe_tiles(xs, plan, wg, wu, wd, tm)
    y1, y2 = _sc_gather_rows(ys, dest1, dest2)
    return _combine(h, y1, y2, route, fg, tm=min(TOKEN_TILE, n))


def _dup_heads(w, n_heads):
    d = w.shape[0]
    w = w.reshape(d, n_heads, 1, HEAD_DIM)
    return jnp.broadcast_to(w, (d, n_heads, 2, HEAD_DIM)).reshape(d, n_heads * 2 * HEAD_DIM)


def _encoder(x, proj_meta, wts):
    b, s, d = x.shape
    x2d = x.reshape(b * s, d)
    proj = _norm_proj(x2d, wts["g1"], wts["w_ext"], tm=TOKEN_TILE).reshape(b, s, -1)
    od0 = _diff_attn(proj, proj_meta, wts["lam_vecs"], wts["subln_g"], heads=(0,), bq=SKIP_BLOCK,
                     bk=min(SKIP_BLOCK, s), skip=True)
    od1 = _diff_attn(proj, proj_meta, wts["lam_vecs"], wts["subln_g"], heads=tuple(range(1, DA_HEADS)),
                     bq=DIFF_BQ, bk=min(DIFF_BK, s), skip=False)
    os_ = _win_attn(proj, proj_meta, wts["sink"])
    h, a2p, route, rt, cnt = _out_router(od0.reshape(b * s, -1), od1.reshape(b * s, -1), os_.reshape(b * s, -1), x2d,
                                     wts["wod"], wts["wos"],
                                     wts["g2"], wts["wr"], wts["br"], tm=TOKEN_TILE)
    y = _moe(a2p, route, rt, cnt, h, wts["wg"], wts["wu"], wts["wd"], wts["fg"], tm=MOE_TILE)
    return y.reshape(b, s, d)


def kernel(x_prompt, x_sample, meta, norm1_g, w_in, lam_q1, lam_k1, lam_q2, lam_k2, subln_g, sink, w_out,
           norm2_g, w_gr, b_gr, w_er, b_er, w_gate, w_up, w_down, final_g):
    d = x_prompt.shape[-1]
    w = w_in[0]
    c_kd = 2 * DA_HEADS * HEAD_DIM
    c_vd = 2 * c_kd
    c_qs = c_vd + DA_HEADS * 2 * HEAD_DIM
    c_ks = c_qs + SW_HEADS * HEAD_DIM
    c_vs = c_ks + SW_KV_HEADS * HEAD_DIM
    w_ext = jnp.concatenate(
        [w[:, :c_ks], _dup_heads(w[:, c_ks:c_vs], SW_KV_HEADS), _dup_heads(w[:, c_vs:], SW_KV_HEADS)],
        axis=1).astype(BF16)
    w_router = jnp.concatenate([w_er[0], w_gr[0]], axis=1)
    w_router = jnp.pad(w_router, ((0, 0), (0, LANES - w_router.shape[1])))
    wr_hi = w_router.astype(BF16)
    wr_lo = (w_router - wr_hi.astype(F32)).astype(BF16)
    br = jnp.pad(jnp.concatenate([b_er[0], b_gr[0]]), (0, LANES - N_EXPERTS - N_GROUPS)).reshape(1, LANES)
    wo = w_out[0].astype(BF16)
    half = DA_HEADS * 2 * HEAD_DIM
    wts = dict(
        g1=norm1_g[0].reshape(1, d), w_ext=w_ext,
        lam_vecs=(lam_q1[0].reshape(1, -1), lam_k1[0].reshape(1, -1),
                  lam_q2[0].reshape(1, -1), lam_k2[0].reshape(1, -1)),
        subln_g=subln_g[0].reshape(1, -1), sink=sink[0].reshape(1, -1),
        wod=wo[:half], wos=wo[half:], g2=norm2_g[0].reshape(1, d),
        wr=jnp.concatenate([wr_hi, wr_lo], axis=1), br=br,
        wg=w_gate[0].astype(BF16), wu=w_up[0].astype(BF16), wd=w_down[0].astype(BF16),
        fg=final_g.reshape(1, d),
    )
    proj_meta = _norm_proj(meta, wts["g1"], w_ext, tm=N_META)
    proj_meta = jnp.pad(proj_meta, ((0, LANES - N_META), (0, 0)))
    return _encoder(x_prompt, proj_meta, wts), _encoder(x_sample, proj_meta, wts)
```

```python
import functools
import math

import jax
import jax.numpy as jnp
import numpy as np
from jax import lax
from jax.experimental import pallas as pl
from jax.experimental.pallas import tpu as pltpu
from jax.experimental.pallas import tpu_sc as plsc

F32 = jnp.float32
BF16 = jnp.bfloat16

N_META = 16
HEAD_DIM = 64
DA_HEADS = 4
SW_HEADS = 8
SW_KV_HEADS = 2
SW_GROUP = SW_HEADS // SW_KV_HEADS
WINDOW = 128
N_GROUPS = 4
EXPERTS_PER_GROUP = 8
N_EXPERTS = N_GROUPS * EXPERTS_PER_GROUP
EPS = 1e-6
SUBLN_EPS = 1e-5
NEG_INF = -1e30
LAM_INIT = 0.8 - 0.6 * math.exp(-0.3 * 0)
LANES = 128
VMEM_LIMIT = 48 * 1024 * 1024

TOKEN_TILE = 1024
MOE_TILE = 1024
DIFF_BQ, DIFF_BK = 256, 1024
SKIP_BLOCK = 512

COL_QD, COL_KD, COL_VD, COL_QS, COL_KS, COL_VS, N_COLBLK = 0, 4, 8, 12, 16, 18, 20
CONTRACT_LAST = (((1,), (1,)), ((), ()))


def _params(sem):
    return pltpu.CompilerParams(dimension_semantics=sem, vmem_limit_bytes=VMEM_LIMIT)


def _norm_proj_kernel(x_ref, g_ref, w_ref, o_ref):
    x = x_ref[...]
    ms = jnp.mean(x * x, axis=-1, keepdims=True)
    y = (x * lax.rsqrt(ms + EPS) * g_ref[...]).astype(BF16)
    o_ref[...] = jnp.dot(y, w_ref[...], preferred_element_type=F32).astype(o_ref.dtype)


def _norm_proj(x2d, g, w_ext, tm):
    n, d = x2d.shape
    wcols = w_ext.shape[1]
    return pl.pallas_call(
        _norm_proj_kernel,
        grid=(n // tm,),
        in_specs=[
            pl.BlockSpec((tm, d), lambda i: (i, 0)),
            pl.BlockSpec((1, d), lambda i: (0, 0)),
            pl.BlockSpec((d, wcols), lambda i: (0, 0)),
        ],
        out_specs=pl.BlockSpec((tm, wcols), lambda i: (i, 0)),
        out_shape=jax.ShapeDtypeStruct((n, wcols), BF16),
        compiler_params=_params(("parallel",)),
        name="norm_proj",
    )(x2d, g, w_ext)


POS_SPLIT = 64
N_SPLIT = 3
LOG2E = 1.4426950408889634


def _key_pos_features(seq):
    j = jnp.arange(seq, dtype=jnp.int32)[:, None]
    lane = jnp.arange(LANES, dtype=jnp.int32)[None, :]
    hi = (j // POS_SPLIT * POS_SPLIT).astype(F32)
    lo = (j % POS_SPLIT).astype(F32)
    feat = jnp.where(lane < N_SPLIT, 1.0,
                     jnp.where(lane < 2 * N_SPLIT, hi, jnp.where(lane < 3 * N_SPLIT, lo, 0.0)))
    return feat.astype(BF16)


def _split_bf16(x):
    pieces = []
    for _ in range(N_SPLIT):
        p = x.astype(BF16).astype(F32)
        pieces.append(p)
        x = x - p
    return pieces


SKIP_MARGIN = 135.0


def _diff_attn_head(h, q_ref, k_ref, v_ref, kf_ref, km_ref, vm_ref, lam, sg_ref,
                    o_ref, qe_ref, acc_ref, m_ref, sd_ref, s1_ref, kn_ref, *, bq, bk, seq, skip):
    i = pl.program_id(1)
    slope = 2.0 ** (-8.0 * (h + 1) / DA_HEADS) * LOG2E
    lane = lax.broadcasted_iota(jnp.int32, (1, LANES), 1)
    q = (q_ref[...].astype(F32) * (LOG2E / math.sqrt(HEAD_DIM))).astype(BF16)
    zero = jnp.zeros_like(q)
    qmaps = (jnp.where(lane < HEAD_DIM, q, zero), jnp.where(lane >= HEAD_DIM, q, zero))
    qpos = i * bq + lax.broadcasted_iota(jnp.int32, (bq, 1), 0)
    row_pieces = _split_bf16(-slope * qpos.astype(F32))
    slope_pieces = _split_bf16(jnp.full((1, 1), slope, F32))
    feat = jnp.zeros((bq, LANES), F32)
    for n in range(N_SPLIT):
        feat = jnp.where(lane == n, row_pieces[n], feat)
        feat = jnp.where(lane == N_SPLIT + n, slope_pieces[n], feat)
        feat = jnp.where(lane == 2 * N_SPLIT + n, slope_pieces[n], feat)
    feats = (feat.astype(BF16), (-feat).astype(BF16))
    for side in range(2):
        for c in range(2):
            qe_ref[side, c * bq:(c + 1) * bq, :LANES] = qmaps[c]
            qe_ref[side, c * bq:(c + 1) * bq, LANES:] = feats[side]

    ones_blk = jnp.broadcast_to(jnp.where(lane == 0, 1.0, 0.0).astype(BF16), (bk, LANES))
    meta_mask = jnp.where(lane < N_META, 0.0, NEG_INF)

    def attend(s, v_ext):
        m_old = m_ref[...]
        m_new = jnp.maximum(m_old, jnp.max(s, axis=-1, keepdims=True))
        alpha = jnp.exp2(m_old - m_new)
        p = jnp.exp2(s - jnp.tile(m_new, (1, s.shape[1] // LANES))).astype(BF16)
        acc_ref[...] = jnp.tile(alpha, (1, 2)) * acc_ref[...] + jnp.dot(p, v_ext, preferred_element_type=F32)
        m_ref[...] = m_new

    m_ref[...] = jnp.full(m_ref.shape, NEG_INF, F32)
    acc_ref[...] = jnp.zeros(acc_ref.shape, F32)

    s0_ref = sd_ref.at[:, :bk]
    nblk = seq // bk
    diag = (i * bq) // bk

    def scores_block(dst_ref, kb, side, edge=None):
        start = pl.multiple_of(kb * bk, bk)
        k_ext = jnp.concatenate([k_ref[pl.ds(start, bk), :], kf_ref[pl.ds(start, bk), :]], axis=1)
        s = lax.dot_general(qe_ref[side], k_ext, CONTRACT_LAST, preferred_element_type=F32)
        dst_ref[...] = s if edge is None else s + edge

    def consume_block(src_ref, kb):
        start = pl.multiple_of(kb * bk, bk)
        attend(src_ref[...], jnp.concatenate([v_ref[pl.ds(start, bk), :], ones_blk], axis=1))

    d_start = pl.multiple_of(diag * bk, bk)
    kpos = diag * bk + lax.broadcasted_iota(jnp.int32, (1, bk), 1)
    bias = jnp.concatenate([-slope * jnp.abs(qpos - kpos).astype(F32),
                            jnp.broadcast_to(meta_mask, (bq, LANES))], axis=1)
    k_first = jnp.concatenate([k_ref[pl.ds(d_start, bk), :], km_ref[...]], axis=0)
    v_first = jnp.concatenate(
        [jnp.concatenate([v_ref[pl.ds(d_start, bk), :], vm_ref[...]], axis=0),
         jnp.broadcast_to(ones_blk[:1], (bk + LANES, LANES))], axis=1)
    sd_ref[...] = lax.dot_general(qe_ref[0, :, :LANES], k_first, CONTRACT_LAST,
                                  preferred_element_type=F32) + jnp.tile(bias, (2, 1))

    def all_blocks():
        def key_block(pos):
            t = pos - 1
            return t + jnp.where(t >= diag, 1, 0)

        def scores_into(dst_ref, pos):
            kb = key_block(pos)
            scores_block(dst_ref, kb, jnp.where(kb > diag, 1, 0))

        scores_into(s1_ref, 1)
        attend(sd_ref[...], v_first)
        for u in range((nblk - 2) // 2):
            scores_into(s0_ref, 2 * u + 2)
            consume_block(s1_ref, key_block(2 * u + 1))
            scores_into(s1_ref, 2 * u + 3)
            consume_block(s0_ref, key_block(2 * u + 2))
        consume_block(s1_ref, key_block(nblk - 1))

    def neighbours_only():
        prev = jnp.maximum(diag - 1, 0)
        nxt = jnp.minimum(diag + 1, nblk - 1)
        scores_block(s1_ref, prev, 0, edge=jnp.where(diag == 0, NEG_INF, 0.0))
        attend(sd_ref[...], v_first)
        scores_block(s0_ref, nxt, 1, edge=jnp.where(diag == nblk - 1, NEG_INF, 0.0))
        consume_block(s1_ref, prev)
        consume_block(s0_ref, nxt)

    if nblk == 1:
        attend(sd_ref[...], v_first)
    elif skip:
        assert bq == bk and nblk > 2
        @pl.when(i == 0)
        def _():
            kf32 = k_ref[...].astype(F32)
            kn_ref[0] = jnp.max(jnp.sum(kf32 * kf32, axis=-1, keepdims=True))
        qf32 = q.astype(F32)
        qn2 = jnp.max(jnp.sum(qf32 * qf32, axis=-1, keepdims=True), axis=0, keepdims=True)
        s_meta = lax.dot_general(qe_ref[0, :, :LANES], km_ref[...], CONTRACT_LAST,
                                 preferred_element_type=F32) + meta_mask
        m_low = jnp.min(jnp.max(s_meta, axis=-1, keepdims=True), axis=0, keepdims=True)
        reach = (jnp.sqrt(qn2 * kn_ref[0]) - m_low + SKIP_MARGIN) * (1.0 / slope)
        near = jnp.clip(reach, 0.0, float(seq))[0, 0] <= float(bk)

        @pl.when(near)
        def _():
            neighbours_only()

        @pl.when(jnp.logical_not(near))
        def _():
            all_blocks()
    else:
        all_blocks()

    o1 = acc_ref[:bq, :LANES] / acc_ref[:bq, LANES:LANES + 1]
    o2 = acc_ref[bq:, :LANES] / acc_ref[bq:, LANES:LANES + 1]
    o = o1 - lam * o2
    o = o * lax.rsqrt(jnp.mean(o * o, axis=-1, keepdims=True) + SUBLN_EPS) * sg_ref[...]
    o_ref[...] = (o * (1.0 - LAM_INIT)).astype(o_ref.dtype)


def _diff_attn_kernel(*refs, heads, **kw):
    nh = len(heads)
    q_refs, k_refs, v_refs = refs[:nh], refs[nh:2 * nh], refs[2 * nh:3 * nh]
    kf_ref = refs[3 * nh]
    km_refs, vm_refs = refs[3 * nh + 1:4 * nh + 1], refs[4 * nh + 1:5 * nh + 1]
    lq1_ref, lk1_ref, lq2_ref, lk2_ref, sg_ref, o_ref, qe_ref, acc_ref, m_ref, sd_ref, s1_ref, kn_ref = refs[5 * nh + 1:]
    lam = (jnp.exp(jnp.sum(lq1_ref[...] * lk1_ref[...], axis=-1, keepdims=True))
           - jnp.exp(jnp.sum(lq2_ref[...] * lk2_ref[...], axis=-1, keepdims=True)) + LAM_INIT)
    for n, h in enumerate(heads):
        _diff_attn_head(h, q_refs[n], k_refs[n], v_refs[n], kf_ref, km_refs[n], vm_refs[n], lam, sg_ref,
                        o_ref.at[:, n * LANES:(n + 1) * LANES],
                        qe_ref.at[n], acc_ref.at[n], m_ref.at[n], sd_ref.at[n], s1_ref.at[n], kn_ref, **kw)


def _diff_attn(proj, proj_meta, lam_vecs, subln_g, heads, bq, bk, skip):
    b, s, _ = proj.shape
    bq = min(bq, s)
    assert bk % bq == 0 and s % bk == 0 and (s // bk == 1 or (s // bk) % 2 == 0)
    nh = len(heads)
    vec = pl.BlockSpec((1, HEAD_DIM), lambda bi, i: (0, 0))
    q_specs = [pl.BlockSpec((None, bq, LANES), lambda bi, i, c=COL_QD + h: (bi, i, c)) for h in heads]
    k_specs = [pl.BlockSpec((None, s, LANES), lambda bi, i, c=COL_KD + h: (bi, 0, c)) for h in heads]
    v_specs = [pl.BlockSpec((None, s, LANES), lambda bi, i, c=COL_VD + h: (bi, 0, c)) for h in heads]
    km_specs = [pl.BlockSpec((LANES, LANES), lambda bi, i, c=COL_KD + h: (0, c)) for h in heads]
    vm_specs = [pl.BlockSpec((LANES, LANES), lambda bi, i, c=COL_VD + h: (0, c)) for h in heads]
    return pl.pallas_call(
        functools.partial(_diff_attn_kernel, heads=heads, bq=bq, bk=bk, seq=s, skip=skip),
        grid=(b, s // bq),
        in_specs=(q_specs + k_specs + v_specs + [pl.BlockSpec((s, LANES), lambda bi, i: (0, 0))]
                  + km_specs + vm_specs + [vec, vec, vec, vec, pl.BlockSpec((1, LANES), lambda bi, i: (0, 0))]),
        out_specs=pl.BlockSpec((None, bq, nh * LANES), lambda bi, i: (bi, i, 0)),
        out_shape=jax.ShapeDtypeStruct((b, s, nh * LANES), BF16),
        scratch_shapes=[
            pltpu.VMEM((nh, 2, 2 * bq, 2 * LANES), BF16),
            pltpu.VMEM((nh, 2 * bq, 2 * LANES), F32),
            pltpu.VMEM((nh, 2 * bq, LANES), F32),
            pltpu.VMEM((nh, 2 * bq, bk + LANES), F32),
            pltpu.VMEM((nh, 2 * bq, bk), F32),
            pltpu.SMEM((1,), F32),
        ],
        compiler_params=_params(("parallel", "arbitrary")),
        name="diff_attn",
    )(*([proj] * (3 * nh)), _key_pos_features(s), *([proj_meta] * (2 * nh)), *lam_vecs, subln_g)


WIN_KEYS = 4 * WINDOW
WIN_QBLOCKS = 8


def _win_tables():
    r = np.arange(WINDOW)
    qf = np.zeros((SW_KV_HEADS, SW_GROUP * WINDOW, LANES), np.float32)
    for head in range(SW_HEADS):
        slope = 2.0 ** (-8.0 * (head + 1) / SW_HEADS)
        i_rel = WINDOW + r
        hi, lo = i_rel // POS_SPLIT * POS_SPLIT, i_rel % POS_SPLIT
        rows = qf[head // SW_GROUP, (head % SW_GROUP) * WINDOW:(head % SW_GROUP + 1) * WINDOW]
        rows[:, 0], rows[:, 1], rows[:, 2], rows[:, 3] = -slope * hi, -slope * lo, slope, slope
        rows[:, 4:8] = -rows[:, 0:4]
    kf = np.zeros((2, WINDOW, LANES), np.float32)
    for n, (blk, right) in enumerate(((0, 0), (2, 1))):
        j_rel = blk * WINDOW + r
        o = 4 * right
        kf[n, :, o], kf[n, :, o + 1] = 1.0, 1.0
        kf[n, :, o + 2], kf[n, :, o + 3] = j_rel // POS_SPLIT * POS_SPLIT, j_rel % POS_SPLIT
    mask = np.zeros((2, WINDOW, WINDOW), np.float32)
    mask[0] = np.where(r[None, :] >= r[:, None], 0.0, NEG_INF)
    mask[1] = np.where(r[None, :] <= r[:, None], 0.0, NEG_INF)
    cur = np.zeros((SW_KV_HEADS, SW_GROUP * WINDOW, WINDOW), np.float32)
    for head in range(SW_HEADS):
        slope = 2.0 ** (-8.0 * (head + 1) / SW_HEADS)
        cur[head // SW_GROUP, (head % SW_GROUP) * WINDOW:(head % SW_GROUP + 1) * WINDOW] = (
            -slope * np.abs(r[:, None] - r[None, :]))
    return jnp.asarray(qf, BF16), jnp.asarray(kf, BF16), jnp.asarray(mask, F32), jnp.asarray(cur, F32)


def _win_attn_kernel(q_ref, kp_ref, kc_ref, kn_ref, vp_ref, vc_ref, vn_ref, km_ref, vm_ref, qf_ref, kf_ref,
                     mask_ref, cur_ref, sink_ref, o_ref, *, nstep):
    c = pl.program_id(1)
    lane = lax.broadcasted_iota(jnp.int32, (1, LANES), 1)
    scale = jnp.asarray(1.0 / math.sqrt(HEAD_DIM), BF16)
    mask_meta = jnp.where(lane < N_META, 0.0, NEG_INF)
    row = lax.broadcasted_iota(jnp.int32, (SW_GROUP * WINDOW, 1), 0)
    ones_blk = jnp.broadcast_to(jnp.where(lane == 0, 1.0, 0.0).astype(BF16), (WIN_KEYS, LANES))
    zeros_blk = jnp.zeros((WINDOW, LANES), BF16)
    gw = SW_GROUP * HEAD_DIM
    own = [slice(j * WINDOW, (j + 1) * WINDOW) for j in range(WIN_QBLOCKS)]
    kblk = [kp_ref] + [kc_ref.at[r] for r in own] + [kn_ref]
    vblk = [vp_ref] + [vc_ref.at[r] for r in own] + [vn_ref]
    for j in range(WIN_QBLOCKS):
        rq = slice(j * WINDOW, (j + 1) * WINDOW)
        edge_p = jnp.where(c == 0, NEG_INF, 0.0) if j == 0 else 0.0
        edge_n = jnp.where(c == nstep - 1, NEG_INF, 0.0) if j == WIN_QBLOCKS - 1 else 0.0
        mask_prev = jnp.tile(mask_ref[0] + edge_p, (SW_GROUP, 1))
        mask_next = jnp.tile(mask_ref[1] + edge_n, (SW_GROUP, 1))
        for g in range(SW_KV_HEADS):
            kv = slice(g * LANES, (g + 1) * LANES)
            rows = []
            for hh in range(SW_GROUP):
                col = g * gw + (hh // 2) * LANES
                qb = q_ref[rq, col:col + LANES] * scale
                keep = (lane < HEAD_DIM) if hh % 2 == 0 else (lane >= HEAD_DIM)
                rows.append(jnp.where(keep, qb, jnp.zeros_like(qb)))
            q_ext = jnp.concatenate([jnp.concatenate(rows, axis=0), qf_ref[g]], axis=1)
            k_all = jnp.concatenate([
                jnp.concatenate([km_ref[:, kv], zeros_blk], axis=1),
                jnp.concatenate([kblk[j][:, kv], kf_ref[0]], axis=1),
                jnp.concatenate([kblk[j + 1][:, kv], zeros_blk], axis=1),
                jnp.concatenate([kblk[j + 2][:, kv], kf_ref[1]], axis=1)], axis=0)
            s = lax.dot_general(q_ext, k_all, CONTRACT_LAST, preferred_element_type=F32)
            s_meta = s[:, :LANES] + mask_meta
            s_prev = s[:, LANES:2 * LANES] + mask_prev
            s_cur = s[:, 2 * LANES:3 * LANES] + cur_ref[g]
            s_next = s[:, 3 * LANES:] + mask_next
            sink = jnp.zeros((SW_GROUP * WINDOW, 1), F32)
            for hh in range(SW_GROUP):
                sink = jnp.where(row // WINDOW == hh, sink_ref[0, g * SW_GROUP + hh], sink)
            parts = (s_meta, s_prev, s_cur, s_next)
            m = jnp.maximum(jnp.maximum(s_meta, s_prev), jnp.maximum(s_cur, s_next))
            m = jnp.maximum(jnp.max(m, axis=-1, keepdims=True), sink)
            p = jnp.concatenate([jnp.exp(x - m) for x in parts], axis=1).astype(BF16)
            v_all = jnp.concatenate(
                [jnp.concatenate([vm_ref[:, kv], vblk[j][:, kv], vblk[j + 1][:, kv], vblk[j + 2][:, kv]], axis=0),
                 ones_blk], axis=1)
            acc = jnp.dot(p, v_all, preferred_element_type=F32)
            denom = acc[:, LANES:LANES + 1] + jnp.exp(sink - m)
            o = acc[:, :LANES] / denom
            for lb in range(SW_GROUP // 2):
                even = o[(2 * lb) * WINDOW:(2 * lb + 1) * WINDOW]
                odd = o[(2 * lb + 1) * WINDOW:(2 * lb + 2) * WINDOW]
                col = g * gw + lb * LANES
                o_ref[rq, col:col + LANES] = jnp.where(lane < HEAD_DIM, even, odd).astype(o_ref.dtype)


def _win_attn(proj, proj_meta, sink):
    b, s, _ = proj.shape
    nblk = s // WINDOW
    nq = WIN_QBLOCKS
    nstep = nblk // nq
    qw = SW_HEADS * HEAD_DIM
    kvw = SW_KV_HEADS * LANES
    qf, kf, mask, cur = _win_tables()

    def edge_spec(col, shift):
        return pl.BlockSpec(
            (None, WINDOW, kvw),
            lambda bi, c: (bi, jnp.clip(nq * c + shift, 0, nblk - 1), col * LANES // kvw))

    def pair_spec(col):
        return pl.BlockSpec((None, nq * WINDOW, kvw), lambda bi, c: (bi, c, col * LANES // kvw))

    return pl.pallas_call(
        functools.partial(_win_attn_kernel, nstep=nstep),
        grid=(b, nstep),
        in_specs=[
            pl.BlockSpec((None, nq * WINDOW, qw), lambda bi, c: (bi, c, COL_QS * LANES // qw)),
            edge_spec(COL_KS, -1), pair_spec(COL_KS), edge_spec(COL_KS, nq),
            edge_spec(COL_VS, -1), pair_spec(COL_VS), edge_spec(COL_VS, nq),
            pl.BlockSpec((LANES, kvw), lambda bi, c: (0, COL_KS * LANES // kvw)),
            pl.BlockSpec((LANES, kvw), lambda bi, c: (0, COL_VS * LANES // kvw)),
            pl.BlockSpec((SW_KV_HEADS, SW_GROUP * WINDOW, LANES), lambda bi, c: (0, 0, 0)),
            pl.BlockSpec((2, WINDOW, LANES), lambda bi, c: (0, 0, 0)),
            pl.BlockSpec((2, WINDOW, WINDOW), lambda bi, c: (0, 0, 0)),
            pl.BlockSpec((SW_KV_HEADS, SW_GROUP * WINDOW, WINDOW), lambda bi, c: (0, 0, 0)),
            pl.BlockSpec(memory_space=pltpu.SMEM),
        ],
        out_specs=pl.BlockSpec((None, nq * WINDOW, qw), lambda bi, c: (bi, c, 0)),
        out_shape=jax.ShapeDtypeStruct((b, s, qw), BF16),
        compiler_params=_params(("parallel", "arbitrary")),
        name="win_attn",
    )(proj, proj, proj, proj, proj, proj, proj, proj_meta, proj_meta, qf, kf, mask, cur, sink)


ROUTE_E1, ROUTE_E2, ROUTE_POS1, ROUTE_POS2, ROUTE_W1, ROUTE_W2 = range(6)


def _pack_bf16_pairs(x):
    k = x.shape[1] // 2
    bits = lax.bitcast_convert_type(x.astype(BF16).astype(F32), jnp.uint32)
    return lax.bitcast_convert_type(bits[:, :k] | (bits[:, k:] >> 16), jnp.int32)


def _unpack_bf16_pairs(w):
    bits = lax.bitcast_convert_type(w, jnp.uint32)
    hi = lax.bitcast_convert_type(bits & jnp.uint32(0xFFFF0000), F32)
    lo = lax.bitcast_convert_type(bits << 16, F32)
    return jnp.concatenate([hi, lo], axis=1)


def _out_router_kernel(od0_ref, od1_ref, os_ref, x_ref, wod_ref, wos_ref, g2_ref, wr_ref, br_ref,
                       h_ref, a_ref, route_ref, rt_ref, cnt_ref, base_ref):
    @pl.when(pl.program_id(0) == 0)
    def _():
        base_ref[...] = jnp.zeros_like(base_ref)

    h = (x_ref[...]
         + jnp.dot(jnp.concatenate([od0_ref[...], od1_ref[...]], axis=1), wod_ref[...],
                   preferred_element_type=F32)
         + jnp.dot(os_ref[...], wos_ref[...], preferred_element_type=F32))
    h_ref[...] = h
    a = h * lax.rsqrt(jnp.mean(h * h, axis=-1, keepdims=True) + EPS) * g2_ref[...]
    a_hi = a.astype(BF16)
    a_ref[...] = _pack_bf16_pairs(a)
    a_lo = (a - a_hi.astype(F32)).astype(BF16)
    hi_terms = jnp.dot(a_hi, wr_ref[...], preferred_element_type=F32)
    logits = (hi_terms[:, :LANES] + hi_terms[:, LANES:]
              + jnp.dot(a_lo, wr_ref[:, :LANES], preferred_element_type=F32)) + br_ref[...]
    tm = logits.shape[0]
    lt = jnp.transpose(logits)
    le = lt[:N_EXPERTS]
    sub_g = lax.broadcasted_iota(jnp.int32, (8, tm), 0)
    lg = jnp.where(sub_g < N_GROUPS, lt[N_EXPERTS:N_EXPERTS + 8], NEG_INF)
    sub_e = lax.broadcasted_iota(jnp.int32, (N_EXPERTS, tm), 0)

    def first_argmax(vals, mask, sub, size):
        mx = jnp.max(vals, axis=0, keepdims=True)
        idx = jnp.min(jnp.where(mask & (vals == mx), sub, size), axis=0, keepdims=True)
        return mx, idx

    gmax, gidx = first_argmax(lg, sub_g < N_GROUPS, sub_g, 8)
    g_w = 1.0 / jnp.sum(jnp.exp(lg - gmax), axis=0, keepdims=True)
    in_group = (sub_e // EXPERTS_PER_GROUP) == gidx
    m1, i1 = first_argmax(jnp.where(in_group, le, NEG_INF), in_group, sub_e, N_EXPERTS)
    rest = in_group & (sub_e != i1)
    m2, i2 = first_argmax(jnp.where(rest, le, NEG_INF), rest, sub_e, N_EXPERTS)
    r = jnp.exp(m2 - m1)
    w1 = g_w / (1.0 + r)
    w2 = g_w * r / (1.0 + r)

    rix = lax.broadcasted_iota(jnp.int32, (tm, tm), 0)
    cix = lax.broadcasted_iota(jnp.int32, (tm, tm), 1)
    earlier = jnp.where(rix < cix, 1.0, 0.0).astype(BF16)
    oh1 = jnp.where(sub_e == i1, 1.0, 0.0)
    oh2 = jnp.where(sub_e == i2, 1.0, 0.0)
    before = jnp.dot(jnp.concatenate([oh1, oh2], axis=0).astype(BF16), earlier, preferred_element_type=F32)
    base1 = base_ref[:, :1]
    pos1 = jnp.sum(oh1 * (base1 + before[:N_EXPERTS]), axis=0, keepdims=True)
    base2 = base1 + jnp.sum(oh1, axis=1, keepdims=True)
    pos2 = jnp.sum(oh2 * (base2 + before[N_EXPERTS:]), axis=0, keepdims=True)
    total = jnp.broadcast_to(base2 + jnp.sum(oh2, axis=1, keepdims=True), base_ref.shape)
    base_ref[...] = total
    cnt_ref[...] = total

    fields = jnp.concatenate([i1.astype(F32), i2.astype(F32), pos1, pos2, w1, w2, jnp.zeros((2, tm), F32)], axis=0)
    rt_ref[...] = fields
    route_ref[...] = jnp.transpose(jnp.concatenate([fields, jnp.zeros((LANES - 8, tm), F32)], axis=0))


def _out_router(od0, od1, os_, x2d, wod, wos, g2, wr, br, tm):
    n, d = x2d.shape
    half = os_.shape[1]
    row = lambda i: (i, 0)
    const = lambda i: (0, 0)
    return pl.pallas_call(
        _out_router_kernel,
        grid=(n // tm,),
        in_specs=[
            pl.BlockSpec((tm, od0.shape[1]), row),
            pl.BlockSpec((tm, od1.shape[1]), row),
            pl.BlockSpec((tm, half), row),
            pl.BlockSpec((tm, d), row),
            pl.BlockSpec((half, d), const),
            pl.BlockSpec((half, d), const),
            pl.BlockSpec((1, d), const),
            pl.BlockSpec((d, 2 * LANES), const),
            pl.BlockSpec((1, LANES), const),
        ],
        out_specs=[
            pl.BlockSpec((tm, d), row),
            pl.BlockSpec((tm, d // 2), row),
            pl.BlockSpec((tm, LANES), row),
            pl.BlockSpec((8, tm), lambda i: (0, i)),
            pl.BlockSpec((N_EXPERTS, LANES), const),
        ],
        out_shape=[
            jax.ShapeDtypeStruct((n, d), F32),
            jax.ShapeDtypeStruct((n, d // 2), jnp.int32),
            jax.ShapeDtypeStruct((n, LANES), F32),
            jax.ShapeDtypeStruct((8, n), F32),
            jax.ShapeDtypeStruct((N_EXPERTS, LANES), F32),
        ],
        scratch_shapes=[pltpu.VMEM((N_EXPERTS, LANES), F32)],
        compiler_params=_params(("arbitrary",)),
        name="out_router",
    )(od0, od1, os_, x2d, wod, wos, g2, wr, br)


SC_CORES, SC_SUBCORES = 2, 16
SC_WORKERS = SC_CORES * SC_SUBCORES
SC_CHUNK = 128


def _sc_mesh():
    return plsc.VectorSubcoreMesh(core_axis_name="c", subcore_axis_name="s",
                                  num_cores=SC_CORES, num_subcores=SC_SUBCORES)


def _sc_scatter_rows(x, idx1, idx2, n_out):
    n, d = x.shape
    assert n % (SC_WORKERS * SC_CHUNK) == 0
    per_w = n // SC_WORKERS

    @functools.partial(
        pl.kernel, mesh=_sc_mesh(), out_type=jax.ShapeDtypeStruct((n_out, d), x.dtype),
        scratch_types=[pltpu.VMEM((SC_CHUNK,), jnp.int32), pltpu.VMEM((SC_CHUNK,), jnp.int32),
                       pltpu.VMEM((SC_CHUNK, d), x.dtype), pltpu.SemaphoreType.DMA],
        name="sc_scatter_rows")
    def scatter(x_hbm, i1_hbm, i2_hbm, o_hbm, i1_v, i2_v, rows_v, sem):
        wid = lax.axis_index("s") * SC_CORES + lax.axis_index("c")

        @pl.loop(0, per_w // SC_CHUNK)
        def _(j):
            base = wid * per_w + j * SC_CHUNK
            pltpu.sync_copy(i1_hbm.at[pl.ds(base, SC_CHUNK)], i1_v)
            pltpu.sync_copy(i2_hbm.at[pl.ds(base, SC_CHUNK)], i2_v)
            pltpu.sync_copy(x_hbm.at[pl.ds(base, SC_CHUNK)], rows_v)
            pltpu.async_copy(rows_v, o_hbm.at[i1_v], sem).wait()
            pltpu.async_copy(rows_v, o_hbm.at[i2_v], sem).wait()

    return scatter(x, idx1, idx2)


def _sc_gather_rows(table, idx1, idx2):
    n = idx1.shape[0]
    d = table.shape[1]
    assert n % (SC_WORKERS * SC_CHUNK) == 0
    per_w = n // SC_WORKERS
    out = jax.ShapeDtypeStruct((n, d), table.dtype)

    @functools.partial(
        pl.kernel, mesh=_sc_mesh(), out_type=(out, out),
        scratch_types=[pltpu.VMEM((SC_CHUNK,), jnp.int32), pltpu.VMEM((SC_CHUNK, d), table.dtype),
                       pltpu.SemaphoreType.DMA],
        name="sc_gather_rows")
    def gather(t_hbm, i1_hbm, i2_hbm, o1_hbm, o2_hbm, i_v, rows_v, sem):
        wid = lax.axis_index("s") * SC_CORES + lax.axis_index("c")

        @pl.loop(0, per_w // SC_CHUNK)
        def _(j):
            base = wid * per_w + j * SC_CHUNK
            for i_hbm, o_hbm in ((i1_hbm, o1_hbm), (i2_hbm, o2_hbm)):
                pltpu.sync_copy(i_hbm.at[pl.ds(base, SC_CHUNK)], i_v)
                pltpu.async_copy(t_hbm.at[i_v], rows_v, sem).wait()
                pltpu.sync_copy(rows_v, o_hbm.at[pl.ds(base, SC_CHUNK)])

    return gather(table, idx1, idx2)


def _dest_kernel(rt_ref, starts_ref, d_ref):
    tr = rt_ref.shape[1]
    sub = lax.broadcasted_iota(jnp.int32, (N_EXPERTS, tr), 0)
    starts = starts_ref[:, :1]
    rows = []
    for e_row, pos_row in ((ROUTE_E1, ROUTE_POS1), (ROUTE_E2, ROUTE_POS2)):
        expert = rt_ref[e_row:e_row + 1, :].astype(jnp.int32)
        start = jnp.sum(jnp.where(sub == expert, starts, 0.0), axis=0, keepdims=True)
        rows.append(start + rt_ref[pos_row:pos_row + 1, :])
    d_ref[...] = jnp.concatenate(rows + [jnp.zeros((6, tr), F32)], axis=0).astype(jnp.int32)


def _routing_tables(rt, cnt, tm, n_tiles_max):
    n = rt.shape[1]
    counts = cnt[:, 0].astype(jnp.int32)
    padded = (counts + tm - 1) // tm * tm
    ends = jnp.cumsum(padded)
    starts = jnp.broadcast_to((ends - padded).astype(F32)[:, None], (N_EXPERTS, LANES))
    tr = min(2048, n)
    dest = pl.pallas_call(
        _dest_kernel,
        grid=(n // tr,),
        in_specs=[pl.BlockSpec((8, tr), lambda i: (0, i)), pl.BlockSpec((N_EXPERTS, LANES), lambda i: (0, 0))],
        out_specs=pl.BlockSpec((8, tr), lambda i: (0, i)),
        out_shape=jax.ShapeDtypeStruct((8, n), jnp.int32),
        compiler_params=_params(("parallel",)),
        name="route_dest",
    )(rt, starts)
    dest1, dest2 = dest[0], dest[1]
    tile_start = jnp.arange(n_tiles_max, dtype=jnp.int32) * tm
    tile_expert = jnp.sum((ends[None, :] <= tile_start[:, None]).astype(jnp.int32), axis=1)
    tile_expert = jnp.minimum(tile_expert, N_EXPERTS - 1)
    n_tiles = (ends[-1] // tm).astype(jnp.int32).reshape(1)
    valid = tile_start < ends[-1]
    prev_expert = jnp.concatenate([jnp.full((1,), -1, jnp.int32), tile_expert[:-1]])
    first = (valid & (tile_expert != prev_expert)).astype(jnp.int32)
    slot = (jnp.cumsum(first) - 1) % 2
    e_idx = jnp.arange(N_EXPERTS, dtype=jnp.int32)
    later = (padded > 0)[None, :] & (e_idx[None, :] > e_idx[:, None])
    next_run = jnp.min(jnp.where(later, e_idx[None, :], N_EXPERTS), axis=1)
    next_run = jnp.where(next_run == N_EXPERTS, -1, next_run).astype(jnp.int32)
    plan = (tile_expert, first, slot.astype(jnp.int32), next_run[tile_expert], n_tiles)
    return dest1, dest2, plan


def _moe_tiles_kernel(te_ref, first_ref, slot_ref, nxt_ref, nt_ref, xs_ref, wgu_hbm, wd_hbm, ys_ref,
                      wgu_buf, wd_buf, sem):
    t = pl.program_id(0)

    def weight_copies(expert, slot):
        return (pltpu.make_async_copy(wgu_hbm.at[expert], wgu_buf.at[slot], sem.at[slot, 0]),
                pltpu.make_async_copy(wd_hbm.at[expert], wd_buf.at[slot], sem.at[slot, 1]))

    @pl.when(t < nt_ref[0])
    def _():
        slot = slot_ref[t]

        @pl.when(t == 0)
        def _():
            for c in weight_copies(te_ref[0], 0):
                c.start()

        @pl.when(first_ref[t] == 1)
        def _():
            for c in weight_copies(te_ref[t], slot):
                c.wait()

            @pl.when(nxt_ref[t] >= 0)
            def _():
                for c in weight_copies(nxt_ref[t], 1 - slot):
                    c.start()

        x = _unpack_bf16_pairs(xs_ref[...]).astype(BF16)
        gu = jnp.dot(x, wgu_buf[slot], preferred_element_type=F32)
        de = gu.shape[1] // 2
        hg, hu = gu[:, :de], gu[:, de:]
        hid = (hg * jax.nn.sigmoid(hg) * hu).astype(BF16)
        ys_ref[...] = _pack_bf16_pairs(jnp.dot(hid, wd_buf[slot], preferred_element_type=F32))


def _moe_tiles(xs, plan, wgu, wd, tm):
    r, dh = xs.shape
    ne, de, d = wd.shape
    row = lambda t, *_: (t, 0)
    hbm = pl.BlockSpec(memory_space=pl.ANY)
    return pl.pallas_call(
        _moe_tiles_kernel,
        grid_spec=pltpu.PrefetchScalarGridSpec(
            num_scalar_prefetch=len(plan),
            grid=(r // tm,),
            in_specs=[pl.BlockSpec((tm, dh), row), hbm, hbm],
            out_specs=pl.BlockSpec((tm, dh), row),
            scratch_shapes=[
                pltpu.VMEM((2, d, 2 * de), wgu.dtype),
                pltpu.VMEM((2, de, d), wd.dtype),
                pltpu.SemaphoreType.DMA((2, 2)),
            ],
        ),
        out_shape=jax.ShapeDtypeStruct((r, dh), jnp.int32),
        compiler_params=_params(("arbitrary",)),
        name="moe_tiles",
    )(*plan, xs, wgu, wd)


def _combine_kernel(h_ref, y1_ref, y2_ref, route_ref, fg_ref, o_ref):
    lane = lax.broadcasted_iota(jnp.int32, route_ref.shape, 1)
    route = route_ref[...]
    w1 = jnp.sum(jnp.where(lane == ROUTE_W1, route, 0.0), axis=-1, keepdims=True)
    w2 = jnp.sum(jnp.where(lane == ROUTE_W2, route, 0.0), axis=-1, keepdims=True)
    y = h_ref[...] + w1 * _unpack_bf16_pairs(y1_ref[...]) + w2 * _unpack_bf16_pairs(y2_ref[...])
    o_ref[...] = y * lax.rsqrt(jnp.mean(y * y, axis=-1, keepdims=True) + EPS) * fg_ref[...]


def _combine(h, y1, y2, route, fg, tm):
    n, d = h.shape
    row = lambda i: (i, 0)
    return pl.pallas_call(
        _combine_kernel,
        grid=(n // tm,),
        in_specs=[
            pl.BlockSpec((tm, d), row),
            pl.BlockSpec((tm, d // 2), row),
            pl.BlockSpec((tm, d // 2), row),
            pl.BlockSpec((tm, LANES), row),
            pl.BlockSpec((1, d), lambda i: (0, 0)),
        ],
        out_specs=pl.BlockSpec((tm, d), row),
        out_shape=jax.ShapeDtypeStruct((n, d), F32),
        compiler_params=_params(("parallel",)),
        name="moe_combine",
    )(h, y1, y2, route, fg)


def _moe(a2p, route, rt, cnt, h, wgu, wd, fg, tm):
    n = h.shape[0]
    n_tiles_max = (2 * n) // tm + N_EXPERTS
    dest1, dest2, plan = _routing_tables(rt, cnt, tm, n_tiles_max)
    xs = _sc_scatter_rows(a2p, dest1, dest2, n_tiles_max * tm)
    ys = _moe_tiles(xs, plan, wgu, wd, tm)
    y1, y2 = _sc_gather_rows(ys, dest1, dest2)
    return _combine(h, y1, y2, route, fg, tm=min(TOKEN_TILE, n))


def _dup_heads(w, n_heads):
    d = w.shape[0]
    w = w.reshape(d, n_heads, 1, HEAD_DIM)
    return jnp.broadcast_to(w, (d, n_heads, 2, HEAD_DIM)).reshape(d, n_heads * 2 * HEAD_DIM)


def _encoder(x, proj_meta, wts):
    b, s, d = x.shape
    x2d = x.reshape(b * s, d)
    proj = _norm_proj(x2d, wts["g1"], wts["w_ext"], tm=TOKEN_TILE).reshape(b, s, -1)
    od0 = _diff_attn(proj, proj_meta, wts["lam_vecs"], wts["subln_g"], heads=(0,), bq=SKIP_BLOCK,
                     bk=min(SKIP_BLOCK, s), skip=True)
    od1 = _diff_attn(proj, proj_meta, wts["lam_vecs"], wts["subln_g"], heads=tuple(range(1, DA_HEADS)),
                     bq=DIFF_BQ, bk=min(DIFF_BK, s), skip=False)
    os_ = _win_attn(proj, proj_meta, wts["sink"])
    h, a2p, route, rt, cnt = _out_router(od0.reshape(b * s, -1), od1.reshape(b * s, -1), os_.reshape(b * s, -1), x2d,
                                     wts["wod"], wts["wos"],
                                     wts["g2"], wts["wr"], wts["br"], tm=TOKEN_TILE)
    y = _moe(a2p, route, rt, cnt, h, wts["wgu"], wts["wd"], wts["fg"], tm=MOE_TILE)
    return y.reshape(b, s, d)


def kernel(x_prompt, x_sample, meta, norm1_g, w_in, lam_q1, lam_k1, lam_q2, lam_k2, subln_g, sink, w_out,
           norm2_g, w_gr, b_gr, w_er, b_er, w_gate, w_up, w_down, final_g):
    d = x_prompt.shape[-1]
    w = w_in[0]
    c_kd = 2 * DA_HEADS * HEAD_DIM
    c_vd = 2 * c_kd
    c_qs = c_vd + DA_HEADS * 2 * HEAD_DIM
    c_ks = c_qs + SW_HEADS * HEAD_DIM
    c_vs = c_ks + SW_KV_HEADS * HEAD_DIM
    w_ext = jnp.concatenate(
        [w[:, :c_ks], _dup_heads(w[:, c_ks:c_vs], SW_KV_HEADS), _dup_heads(w[:, c_vs:], SW_KV_HEADS)],
        axis=1).astype(BF16)
    w_router = jnp.concatenate([w_er[0], w_gr[0]], axis=1)
    w_router = jnp.pad(w_router, ((0, 0), (0, LANES - w_router.shape[1])))
    wr_hi = w_router.astype(BF16)
    wr_lo = (w_router - wr_hi.astype(F32)).astype(BF16)
    br = jnp.pad(jnp.concatenate([b_er[0], b_gr[0]]), (0, LANES - N_EXPERTS - N_GROUPS)).reshape(1, LANES)
    wo = w_out[0].astype(BF16)
    half = DA_HEADS * 2 * HEAD_DIM
    wts = dict(
        g1=norm1_g[0].reshape(1, d), w_ext=w_ext,
        lam_vecs=(lam_q1[0].reshape(1, -1), lam_k1[0].reshape(1, -1),
                  lam_q2[0].reshape(1, -1), lam_k2[0].reshape(1, -1)),
        subln_g=subln_g[0].reshape(1, -1), sink=sink[0].reshape(1, -1),
        wod=wo[:half], wos=wo[half:], g2=norm2_g[0].reshape(1, d),
        wr=jnp.concatenate([wr_hi, wr_lo], axis=1), br=br,
        wgu=jnp.concatenate([w_gate[0], w_up[0]], axis=2).astype(BF16), wd=w_down[0].astype(BF16),
        fg=final_g.reshape(1, d),
    )
    proj_meta = _norm_proj(meta, wts["g1"], w_ext, tm=N_META)
    proj_meta = jnp.pad(proj_meta, ((0, LANES - N_META), (0, 0)))
    return _encoder(x_prompt, proj_meta, wts), _encoder(x_sample, proj_meta, wts)
```

```python
import functools
import math

import jax
import jax.numpy as jnp
import numpy as np
from jax import lax
from jax.experimental import pallas as pl
from jax.experimental.pallas import tpu as pltpu
from jax.experimental.pallas import tpu_sc as plsc

F32 = jnp.float32
BF16 = jnp.bfloat16

N_META = 16
HEAD_DIM = 64
DA_HEADS = 4
SW_HEADS = 8
SW_KV_HEADS = 2
SW_GROUP = SW_HEADS // SW_KV_HEADS
WINDOW = 128
N_GROUPS = 4
EXPERTS_PER_GROUP = 8
N_EXPERTS = N_GROUPS * EXPERTS_PER_GROUP
EPS = 1e-6
SUBLN_EPS = 1e-5
NEG_INF = -1e30
LAM_INIT = 0.8 - 0.6 * math.exp(-0.3 * 0)
LANES = 128
VMEM_LIMIT = 48 * 1024 * 1024

TOKEN_TILE = 1024
MOE_TILE = 1024
DIFF_BQ, DIFF_BK = 256, 1024
SKIP_BLOCK = 512

COL_QD, COL_KD, COL_VD, COL_QS, COL_KS, COL_VS, N_COLBLK = 0, 4, 8, 12, 16, 18, 20
CONTRACT_LAST = (((1,), (1,)), ((), ()))


def _params(sem):
    return pltpu.CompilerParams(dimension_semantics=sem, vmem_limit_bytes=VMEM_LIMIT)


def _norm_proj_kernel(x_ref, g_ref, w_ref, o_ref):
    x = x_ref[...]
    ms = jnp.mean(x * x, axis=-1, keepdims=True)
    y = (x * lax.rsqrt(ms + EPS) * g_ref[...]).astype(BF16)
    o_ref[...] = jnp.dot(y, w_ref[...], preferred_element_type=F32).astype(o_ref.dtype)


def _norm_proj(x2d, g, w_ext, tm):
    n, d = x2d.shape
    wcols = w_ext.shape[1]
    return pl.pallas_call(
        _norm_proj_kernel,
        grid=(n // tm,),
        in_specs=[
            pl.BlockSpec((tm, d), lambda i: (i, 0)),
            pl.BlockSpec((1, d), lambda i: (0, 0)),
            pl.BlockSpec((d, wcols), lambda i: (0, 0)),
        ],
        out_specs=pl.BlockSpec((tm, wcols), lambda i: (i, 0)),
        out_shape=jax.ShapeDtypeStruct((n, wcols), BF16),
        compiler_params=_params(("parallel",)),
        name="norm_proj",
    )(x2d, g, w_ext)


POS_SPLIT = 64
N_SPLIT = 3
LOG2E = 1.4426950408889634


def _key_pos_features(seq):
    j = jnp.arange(seq, dtype=jnp.int32)[:, None]
    lane = jnp.arange(LANES, dtype=jnp.int32)[None, :]
    hi = (j // POS_SPLIT * POS_SPLIT).astype(F32)
    lo = (j % POS_SPLIT).astype(F32)
    feat = jnp.where(lane < N_SPLIT, 1.0,
                     jnp.where(lane < 2 * N_SPLIT, hi, jnp.where(lane < 3 * N_SPLIT, lo, 0.0)))
    return feat.astype(BF16)


def _split_bf16(x):
    pieces = []
    for _ in range(N_SPLIT):
        p = x.astype(BF16).astype(F32)
        pieces.append(p)
        x = x - p
    return pieces


SKIP_MARGIN = 135.0


def _diff_attn_head(h, q_ref, k_ref, v_ref, kf_ref, km_ref, vm_ref, lam, sg_ref,
                    o_ref, qe_ref, acc_ref, m_ref, sd_ref, s1_ref, kn_ref, *, bq, bk, seq, skip):
    i = pl.program_id(1)
    slope = 2.0 ** (-8.0 * (h + 1) / DA_HEADS) * LOG2E
    lane = lax.broadcasted_iota(jnp.int32, (1, LANES), 1)
    q = (q_ref[...].astype(F32) * (LOG2E / math.sqrt(HEAD_DIM))).astype(BF16)
    zero = jnp.zeros_like(q)
    qmaps = (jnp.where(lane < HEAD_DIM, q, zero), jnp.where(lane >= HEAD_DIM, q, zero))
    qpos = i * bq + lax.broadcasted_iota(jnp.int32, (bq, 1), 0)
    row_pieces = _split_bf16(-slope * qpos.astype(F32))
    slope_pieces = _split_bf16(jnp.full((1, 1), slope, F32))
    feat = jnp.zeros((bq, LANES), F32)
    for n in range(N_SPLIT):
        feat = jnp.where(lane == n, row_pieces[n], feat)
        feat = jnp.where(lane == N_SPLIT + n, slope_pieces[n], feat)
        feat = jnp.where(lane == 2 * N_SPLIT + n, slope_pieces[n], feat)
    feats = (feat.astype(BF16), (-feat).astype(BF16))
    for side in range(2):
        for c in range(2):
            qe_ref[side, c * bq:(c + 1) * bq, :LANES] = qmaps[c]
            qe_ref[side, c * bq:(c + 1) * bq, LANES:] = feats[side]

    ones_blk = jnp.broadcast_to(jnp.where(lane == 0, 1.0, 0.0).astype(BF16), (bk, LANES))
    meta_mask = jnp.where(lane < N_META, 0.0, NEG_INF)

    def attend(s, v_ext):
        m_old = m_ref[...]
        m_new = jnp.maximum(m_old, jnp.max(s, axis=-1, keepdims=True))
        alpha = jnp.exp2(m_old - m_new)
        p = jnp.exp2(s - jnp.tile(m_new, (1, s.shape[1] // LANES))).astype(BF16)
        acc_ref[...] = jnp.tile(alpha, (1, 2)) * acc_ref[...] + jnp.dot(p, v_ext, preferred_element_type=F32)
        m_ref[...] = m_new

    m_ref[...] = jnp.full(m_ref.shape, NEG_INF, F32)
    acc_ref[...] = jnp.zeros(acc_ref.shape, F32)

    s0_ref = sd_ref.at[:, :bk]
    nblk = seq // bk
    diag = (i * bq) // bk

    def scores_block(dst_ref, kb, side, edge=None):
        start = pl.multiple_of(kb * bk, bk)
        k_ext = jnp.concatenate([k_ref[pl.ds(start, bk), :], kf_ref[pl.ds(start, bk), :]], axis=1)
        s = lax.dot_general(qe_ref[side], k_ext, CONTRACT_LAST, preferred_element_type=F32)
        dst_ref[...] = s if edge is None else s + edge

    def consume_block(src_ref, kb):
        start = pl.multiple_of(kb * bk, bk)
        attend(src_ref[...], jnp.concatenate([v_ref[pl.ds(start, bk), :], ones_blk], axis=1))

    d_start = pl.multiple_of(diag * bk, bk)
    kpos = diag * bk + lax.broadcasted_iota(jnp.int32, (1, bk), 1)
    bias = jnp.concatenate([-slope * jnp.abs(qpos - kpos).astype(F32),
                            jnp.broadcast_to(meta_mask, (bq, LANES))], axis=1)
    k_first = jnp.concatenate([k_ref[pl.ds(d_start, bk), :], km_ref[...]], axis=0)
    v_first = jnp.concatenate(
        [jnp.concatenate([v_ref[pl.ds(d_start, bk), :], vm_ref[...]], axis=0),
         jnp.broadcast_to(ones_blk[:1], (bk + LANES, LANES))], axis=1)
    sd_ref[...] = lax.dot_general(qe_ref[0, :, :LANES], k_first, CONTRACT_LAST,
                                  preferred_element_type=F32) + jnp.tile(bias, (2, 1))

    def all_blocks():
        def key_block(pos):
            t = pos - 1
            return t + jnp.where(t >= diag, 1, 0)

        def scores_into(dst_ref, pos):
            kb = key_block(pos)
            scores_block(dst_ref, kb, jnp.where(kb > diag, 1, 0))

        scores_into(s1_ref, 1)
        attend(sd_ref[...], v_first)
        for u in range((nblk - 2) // 2):
            scores_into(s0_ref, 2 * u + 2)
            consume_block(s1_ref, key_block(2 * u + 1))
            scores_into(s1_ref, 2 * u + 3)
            consume_block(s0_ref, key_block(2 * u + 2))
        consume_block(s1_ref, key_block(nblk - 1))

    def neighbours_only():
        prev = jnp.maximum(diag - 1, 0)
        nxt = jnp.minimum(diag + 1, nblk - 1)
        scores_block(s1_ref, prev, 0, edge=jnp.where(diag == 0, NEG_INF, 0.0))
        attend(sd_ref[...], v_first)
        scores_block(s0_ref, nxt, 1, edge=jnp.where(diag == nblk - 1, NEG_INF, 0.0))
        consume_block(s1_ref, prev)
        consume_block(s0_ref, nxt)

    if nblk == 1:
        attend(sd_ref[...], v_first)
    elif skip:
        assert bq == bk and nblk > 2
        @pl.when(i == 0)
        def _():
            kf32 = k_ref[...].astype(F32)
            kn_ref[0] = jnp.max(jnp.sum(kf32 * kf32, axis=-1, keepdims=True))
        qf32 = q.astype(F32)
        qn2 = jnp.max(jnp.sum(qf32 * qf32, axis=-1, keepdims=True), axis=0, keepdims=True)
        s_meta = lax.dot_general(qe_ref[0, :, :LANES], km_ref[...], CONTRACT_LAST,
                                 preferred_element_type=F32) + meta_mask
        m_low = jnp.min(jnp.max(s_meta, axis=-1, keepdims=True), axis=0, keepdims=True)
        reach = (jnp.sqrt(qn2 * kn_ref[0]) - m_low + SKIP_MARGIN) * (1.0 / slope)
        near = jnp.clip(reach, 0.0, float(seq))[0, 0] <= float(bk)

        @pl.when(near)
        def _():
            neighbours_only()

        @pl.when(jnp.logical_not(near))
        def _():
            all_blocks()
    else:
        all_blocks()

    o1 = acc_ref[:bq, :LANES] / acc_ref[:bq, LANES:LANES + 1]
    o2 = acc_ref[bq:, :LANES] / acc_ref[bq:, LANES:LANES + 1]
    o = o1 - lam * o2
    o = o * lax.rsqrt(jnp.mean(o * o, axis=-1, keepdims=True) + SUBLN_EPS) * sg_ref[...]
    o_ref[...] = (o * (1.0 - LAM_INIT)).astype(o_ref.dtype)


def _diff_attn_kernel(*refs, heads, **kw):
    nh = len(heads)
    q_refs, k_refs, v_refs = refs[:nh], refs[nh:2 * nh], refs[2 * nh:3 * nh]
    kf_ref = refs[3 * nh]
    km_refs, vm_refs = refs[3 * nh + 1:4 * nh + 1], refs[4 * nh + 1:5 * nh + 1]
    lq1_ref, lk1_ref, lq2_ref, lk2_ref, sg_ref, o_ref, qe_ref, acc_ref, m_ref, sd_ref, s1_ref, kn_ref = refs[5 * nh + 1:]
    lam = (jnp.exp(jnp.sum(lq1_ref[...] * lk1_ref[...], axis=-1, keepdims=True))
           - jnp.exp(jnp.sum(lq2_ref[...] * lk2_ref[...], axis=-1, keepdims=True)) + LAM_INIT)
    for n, h in enumerate(heads):
        _diff_attn_head(h, q_refs[n], k_refs[n], v_refs[n], kf_ref, km_refs[n], vm_refs[n], lam, sg_ref,
                        o_ref.at[:, n * LANES:(n + 1) * LANES],
                        qe_ref.at[n], acc_ref.at[n], m_ref.at[n], sd_ref.at[n], s1_ref.at[n], kn_ref, **kw)


def _diff_attn(proj, proj_meta, lam_vecs, subln_g, heads, bq, bk, skip):
    b, s, _ = proj.shape
    bq = min(bq, s)
    assert bk % bq == 0 and s % bk == 0 and (s // bk == 1 or (s // bk) % 2 == 0)
    nh = len(heads)
    vec = pl.BlockSpec((1, HEAD_DIM), lambda bi, i: (0, 0))
    q_specs = [pl.BlockSpec((None, bq, LANES), lambda bi, i, c=COL_QD + h: (bi, i, c)) for h in heads]
    k_specs = [pl.BlockSpec((None, s, LANES), lambda bi, i, c=COL_KD + h: (bi, 0, c)) for h in heads]
    v_specs = [pl.BlockSpec((None, s, LANES), lambda bi, i, c=COL_VD + h: (bi, 0, c)) for h in heads]
    km_specs = [pl.BlockSpec((LANES, LANES), lambda bi, i, c=COL_KD + h: (0, c)) for h in heads]
    vm_specs = [pl.BlockSpec((LANES, LANES), lambda bi, i, c=COL_VD + h: (0, c)) for h in heads]
    return pl.pallas_call(
        functools.partial(_diff_attn_kernel, heads=heads, bq=bq, bk=bk, seq=s, skip=skip),
        grid=(b, s // bq),
        in_specs=(q_specs + k_specs + v_specs + [pl.BlockSpec((s, LANES), lambda bi, i: (0, 0))]
                  + km_specs + vm_specs + [vec, vec, vec, vec, pl.BlockSpec((1, LANES), lambda bi, i: (0, 0))]),
        out_specs=pl.BlockSpec((None, bq, nh * LANES), lambda bi, i: (bi, i, 0)),
        out_shape=jax.ShapeDtypeStruct((b, s, nh * LANES), BF16),
        scratch_shapes=[
            pltpu.VMEM((nh, 2, 2 * bq, 2 * LANES), BF16),
            pltpu.VMEM((nh, 2 * bq, 2 * LANES), F32),
            pltpu.VMEM((nh, 2 * bq, LANES), F32),
            pltpu.VMEM((nh, 2 * bq, bk + LANES), F32),
            pltpu.VMEM((nh, 2 * bq, bk), F32),
            pltpu.SMEM((1,), F32),
        ],
        compiler_params=_params(("parallel", "arbitrary")),
        name="diff_attn",
    )(*([proj] * (3 * nh)), _key_pos_features(s), *([proj_meta] * (2 * nh)), *lam_vecs, subln_g)


WIN_KEYS = 4 * WINDOW
WIN_QBLOCKS = 8


def _win_tables():
    r = np.arange(WINDOW)
    qf = np.zeros((SW_KV_HEADS, SW_GROUP * WINDOW, LANES), np.float32)
    for head in range(SW_HEADS):
        slope = 2.0 ** (-8.0 * (head + 1) / SW_HEADS)
        i_rel = WINDOW + r
        hi, lo = i_rel // POS_SPLIT * POS_SPLIT, i_rel % POS_SPLIT
        rows = qf[head // SW_GROUP, (head % SW_GROUP) * WINDOW:(head % SW_GROUP + 1) * WINDOW]
        rows[:, 0], rows[:, 1], rows[:, 2], rows[:, 3] = -slope * hi, -slope * lo, slope, slope
        rows[:, 4:8] = -rows[:, 0:4]
    kf = np.zeros((2, WINDOW, LANES), np.float32)
    for n, (blk, right) in enumerate(((0, 0), (2, 1))):
        j_rel = blk * WINDOW + r
        o = 4 * right
        kf[n, :, o], kf[n, :, o + 1] = 1.0, 1.0
        kf[n, :, o + 2], kf[n, :, o + 3] = j_rel // POS_SPLIT * POS_SPLIT, j_rel % POS_SPLIT
    mask = np.zeros((2, WINDOW, WINDOW), np.float32)
    mask[0] = np.where(r[None, :] >= r[:, None], 0.0, NEG_INF)
    mask[1] = np.where(r[None, :] <= r[:, None], 0.0, NEG_INF)
    cur = np.zeros((SW_KV_HEADS, SW_GROUP * WINDOW, WINDOW), np.float32)
    for head in range(SW_HEADS):
        slope = 2.0 ** (-8.0 * (head + 1) / SW_HEADS)
        cur[head // SW_GROUP, (head % SW_GROUP) * WINDOW:(head % SW_GROUP + 1) * WINDOW] = (
            -slope * np.abs(r[:, None] - r[None, :]))
    return jnp.asarray(qf, BF16), jnp.asarray(kf, BF16), jnp.asarray(mask, F32), jnp.asarray(cur, F32)


def _win_attn_kernel(q_ref, kp_ref, kc_ref, kn_ref, vp_ref, vc_ref, vn_ref, km_ref, vm_ref, qf_ref, kf_ref,
                     mask_ref, cur_ref, sink_ref, o_ref, *, nstep):
    c = pl.program_id(1)
    lane = lax.broadcasted_iota(jnp.int32, (1, LANES), 1)
    scale = jnp.asarray(1.0 / math.sqrt(HEAD_DIM), BF16)
    mask_meta = jnp.where(lane < N_META, 0.0, NEG_INF)
    row = lax.broadcasted_iota(jnp.int32, (SW_GROUP * WINDOW, 1), 0)
    ones_blk = jnp.broadcast_to(jnp.where(lane == 0, 1.0, 0.0).astype(BF16), (WIN_KEYS, LANES))
    zeros_blk = jnp.zeros((WINDOW, LANES), BF16)
    gw = SW_GROUP * HEAD_DIM
    own = [slice(j * WINDOW, (j + 1) * WINDOW) for j in range(WIN_QBLOCKS)]
    kblk = [kp_ref] + [kc_ref.at[r] for r in own] + [kn_ref]
    vblk = [vp_ref] + [vc_ref.at[r] for r in own] + [vn_ref]
    for j in range(WIN_QBLOCKS):
        rq = slice(j * WINDOW, (j + 1) * WINDOW)
        edge_p = jnp.where(c == 0, NEG_INF, 0.0) if j == 0 else 0.0
        edge_n = jnp.where(c == nstep - 1, NEG_INF, 0.0) if j == WIN_QBLOCKS - 1 else 0.0
        mask_prev = jnp.tile(mask_ref[0] + edge_p, (SW_GROUP, 1))
        mask_next = jnp.tile(mask_ref[1] + edge_n, (SW_GROUP, 1))
        for g in range(SW_KV_HEADS):
            kv = slice(g * LANES, (g + 1) * LANES)
            rows = []
            for hh in range(SW_GROUP):
                col = g * gw + (hh // 2) * LANES
                qb = q_ref[rq, col:col + LANES] * scale
                keep = (lane < HEAD_DIM) if hh % 2 == 0 else (lane >= HEAD_DIM)
                rows.append(jnp.where(keep, qb, jnp.zeros_like(qb)))
            q_ext = jnp.concatenate([jnp.concatenate(rows, axis=0), qf_ref[g]], axis=1)
            k_all = jnp.concatenate([
                jnp.concatenate([km_ref[:, kv], zeros_blk], axis=1),
                jnp.concatenate([kblk[j][:, kv], kf_ref[0]], axis=1),
                jnp.concatenate([kblk[j + 1][:, kv], zeros_blk], axis=1),
                jnp.concatenate([kblk[j + 2][:, kv], kf_ref[1]], axis=1)], axis=0)
            s = lax.dot_general(q_ext, k_all, CONTRACT_LAST, preferred_element_type=F32)
            s_meta = s[:, :LANES] + mask_meta
            s_prev = s[:, LANES:2 * LANES] + mask_prev
            s_cur = s[:, 2 * LANES:3 * LANES] + cur_ref[g]
            s_next = s[:, 3 * LANES:] + mask_next
            sink = jnp.zeros((SW_GROUP * WINDOW, 1), F32)
            for hh in range(SW_GROUP):
                sink = jnp.where(row // WINDOW == hh, sink_ref[0, g * SW_GROUP + hh], sink)
            parts = (s_meta, s_prev, s_cur, s_next)
            m = jnp.maximum(jnp.maximum(s_meta, s_prev), jnp.maximum(s_cur, s_next))
            m = jnp.maximum(jnp.max(m, axis=-1, keepdims=True), sink)
            p = jnp.concatenate([jnp.exp(x - m) for x in parts], axis=1).astype(BF16)
            v_all = jnp.concatenate(
                [jnp.concatenate([vm_ref[:, kv], vblk[j][:, kv], vblk[j + 1][:, kv], vblk[j + 2][:, kv]], axis=0),
                 ones_blk], axis=1)
            acc = jnp.dot(p, v_all, preferred_element_type=F32)
            denom = acc[:, LANES:LANES + 1] + jnp.exp(sink - m)
            o = acc[:, :LANES] / denom
            for lb in range(SW_GROUP // 2):
                even = o[(2 * lb) * WINDOW:(2 * lb + 1) * WINDOW]
                odd = o[(2 * lb + 1) * WINDOW:(2 * lb + 2) * WINDOW]
                col = g * gw + lb * LANES
                o_ref[rq, col:col + LANES] = jnp.where(lane < HEAD_DIM, even, odd).astype(o_ref.dtype)


def _win_attn(proj, proj_meta, sink):
    b, s, _ = proj.shape
    nblk = s // WINDOW
    nq = WIN_QBLOCKS
    nstep = nblk // nq
    qw = SW_HEADS * HEAD_DIM
    kvw = SW_KV_HEADS * LANES
    qf, kf, mask, cur = _win_tables()

    def edge_spec(col, shift):
        return pl.BlockSpec(
            (None, WINDOW, kvw),
            lambda bi, c: (bi, jnp.clip(nq * c + shift, 0, nblk - 1), col * LANES // kvw))

    def pair_spec(col):
        return pl.BlockSpec((None, nq * WINDOW, kvw), lambda bi, c: (bi, c, col * LANES // kvw))

    return pl.pallas_call(
        functools.partial(_win_attn_kernel, nstep=nstep),
        grid=(b, nstep),
        in_specs=[
            pl.BlockSpec((None, nq * WINDOW, qw), lambda bi, c: (bi, c, COL_QS * LANES // qw)),
            edge_spec(COL_KS, -1), pair_spec(COL_KS), edge_spec(COL_KS, nq),
            edge_spec(COL_VS, -1), pair_spec(COL_VS), edge_spec(COL_VS, nq),
            pl.BlockSpec((LANES, kvw), lambda bi, c: (0, COL_KS * LANES // kvw)),
            pl.BlockSpec((LANES, kvw), lambda bi, c: (0, COL_VS * LANES // kvw)),
            pl.BlockSpec((SW_KV_HEADS, SW_GROUP * WINDOW, LANES), lambda bi, c: (0, 0, 0)),
            pl.BlockSpec((2, WINDOW, LANES), lambda bi, c: (0, 0, 0)),
            pl.BlockSpec((2, WINDOW, WINDOW), lambda bi, c: (0, 0, 0)),
            pl.BlockSpec((SW_KV_HEADS, SW_GROUP * WINDOW, WINDOW), lambda bi, c: (0, 0, 0)),
            pl.BlockSpec(memory_space=pltpu.SMEM),
        ],
        out_specs=pl.BlockSpec((None, nq * WINDOW, qw), lambda bi, c: (bi, c, 0)),
        out_shape=jax.ShapeDtypeStruct((b, s, qw), BF16),
        compiler_params=_params(("parallel", "arbitrary")),
        name="win_attn",
    )(proj, proj, proj, proj, proj, proj, proj, proj_meta, proj_meta, qf, kf, mask, cur, sink)


ROUTE_E1, ROUTE_E2, ROUTE_POS1, ROUTE_POS2, ROUTE_W1, ROUTE_W2 = range(6)


def _pack_bf16_pairs(x):
    k = x.shape[1] // 2
    bits = lax.bitcast_convert_type(x.astype(BF16).astype(F32), jnp.uint32)
    return lax.bitcast_convert_type(bits[:, :k] | (bits[:, k:] >> 16), jnp.int32)


def _unpack_bf16_pairs(w):
    bits = lax.bitcast_convert_type(w, jnp.uint32)
    hi = lax.bitcast_convert_type(bits & jnp.uint32(0xFFFF0000), F32)
    lo = lax.bitcast_convert_type(bits << 16, F32)
    return jnp.concatenate([hi, lo], axis=1)


def _out_router_kernel(od0_ref, od1_ref, os_ref, x_ref, wod_ref, wos_ref, g2_ref, wr_ref, br_ref,
                       h_ref, a_ref, route_ref, rt_ref, cnt_ref, base_ref):
    @pl.when(pl.program_id(0) == 0)
    def _():
        base_ref[...] = jnp.zeros_like(base_ref)

    h = (x_ref[...]
         + jnp.dot(jnp.concatenate([od0_ref[...], od1_ref[...]], axis=1), wod_ref[...],
                   preferred_element_type=F32)
         + jnp.dot(os_ref[...], wos_ref[...], preferred_element_type=F32))
    h_ref[...] = h
    a = h * lax.rsqrt(jnp.mean(h * h, axis=-1, keepdims=True) + EPS) * g2_ref[...]
    a_hi = a.astype(BF16)
    a_ref[...] = _pack_bf16_pairs(a)
    a_lo = (a - a_hi.astype(F32)).astype(BF16)
    hi_terms = jnp.dot(a_hi, wr_ref[...], preferred_element_type=F32)
    logits = (hi_terms[:, :LANES] + hi_terms[:, LANES:]
              + jnp.dot(a_lo, wr_ref[:, :LANES], preferred_element_type=F32)) + br_ref[...]
    tm = logits.shape[0]
    lt = jnp.transpose(logits)
    le = lt[:N_EXPERTS]
    sub_g = lax.broadcasted_iota(jnp.int32, (8, tm), 0)
    lg = jnp.where(sub_g < N_GROUPS, lt[N_EXPERTS:N_EXPERTS + 8], NEG_INF)
    sub_e = lax.broadcasted_iota(jnp.int32, (N_EXPERTS, tm), 0)

    def first_argmax(vals, mask, sub, size):
        mx = jnp.max(vals, axis=0, keepdims=True)
        idx = jnp.min(jnp.where(mask & (vals == mx), sub, size), axis=0, keepdims=True)
        return mx, idx

    gmax, gidx = first_argmax(lg, sub_g < N_GROUPS, sub_g, 8)
    g_w = 1.0 / jnp.sum(jnp.exp(lg - gmax), axis=0, keepdims=True)
    in_group = (sub_e // EXPERTS_PER_GROUP) == gidx
    m1, i1 = first_argmax(jnp.where(in_group, le, NEG_INF), in_group, sub_e, N_EXPERTS)
    rest = in_group & (sub_e != i1)
    m2, i2 = first_argmax(jnp.where(rest, le, NEG_INF), rest, sub_e, N_EXPERTS)
    r = jnp.exp(m2 - m1)
    w1 = g_w / (1.0 + r)
    w2 = g_w * r / (1.0 + r)

    rix = lax.broadcasted_iota(jnp.int32, (tm, tm), 0)
    cix = lax.broadcasted_iota(jnp.int32, (tm, tm), 1)
    earlier = jnp.where(rix < cix, 1.0, 0.0).astype(BF16)
    oh1 = jnp.where(sub_e == i1, 1.0, 0.0)
    oh2 = jnp.where(sub_e == i2, 1.0, 0.0)
    before = jnp.dot(jnp.concatenate([oh1, oh2], axis=0).astype(BF16), earlier, preferred_element_type=F32)
    base1 = base_ref[:, :1]
    pos1 = jnp.sum(oh1 * (base1 + before[:N_EXPERTS]), axis=0, keepdims=True)
    base2 = base1 + jnp.sum(oh1, axis=1, keepdims=True)
    pos2 = jnp.sum(oh2 * (base2 + before[N_EXPERTS:]), axis=0, keepdims=True)
    total = jnp.broadcast_to(base2 + jnp.sum(oh2, axis=1, keepdims=True), base_ref.shape)
    base_ref[...] = total
    cnt_ref[...] = total

    fields = jnp.concatenate([i1.astype(F32), i2.astype(F32), pos1, pos2, w1, w2, jnp.zeros((2, tm), F32)], axis=0)
    rt_ref[...] = fields
    route_ref[...] = jnp.transpose(jnp.concatenate([fields, jnp.zeros((LANES - 8, tm), F32)], axis=0))


def _out_router(od0, od1, os_, x2d, wod, wos, g2, wr, br, tm):
    n, d = x2d.shape
    half = os_.shape[1]
    row = lambda i: (i, 0)
    const = lambda i: (0, 0)
    return pl.pallas_call(
        _out_router_kernel,
        grid=(n // tm,),
        in_specs=[
            pl.BlockSpec((tm, od0.shape[1]), row),
            pl.BlockSpec((tm, od1.shape[1]), row),
            pl.BlockSpec((tm, half), row),
            pl.BlockSpec((tm, d), row),
            pl.BlockSpec((half, d), const),
            pl.BlockSpec((half, d), const),
            pl.BlockSpec((1, d), const),
            pl.BlockSpec((d, 2 * LANES), const),
            pl.BlockSpec((1, LANES), const),
        ],
        out_specs=[
            pl.BlockSpec((tm, d), row),
            pl.BlockSpec((tm, d // 2), row),
            pl.BlockSpec((tm, LANES), row),
            pl.BlockSpec((8, tm), lambda i: (0, i)),
            pl.BlockSpec((N_EXPERTS, LANES), const),
        ],
        out_shape=[
            jax.ShapeDtypeStruct((n, d), F32),
            jax.ShapeDtypeStruct((n, d // 2), jnp.int32),
            jax.ShapeDtypeStruct((n, LANES), F32),
            jax.ShapeDtypeStruct((8, n), F32),
            jax.ShapeDtypeStruct((N_EXPERTS, LANES), F32),
        ],
        scratch_shapes=[pltpu.VMEM((N_EXPERTS, LANES), F32)],
        compiler_params=_params(("arbitrary",)),
        name="out_router",
    )(od0, od1, os_, x2d, wod, wos, g2, wr, br)


SC_CORES, SC_SUBCORES = 2, 16
SC_WORKERS = SC_CORES * SC_SUBCORES
SC_CHUNK = 128


def _sc_mesh():
    return plsc.VectorSubcoreMesh(core_axis_name="c", subcore_axis_name="s",
                                  num_cores=SC_CORES, num_subcores=SC_SUBCORES)


def _sc_scatter_rows(x, idx1, idx2, n_out):
    n, d = x.shape
    assert n % (SC_WORKERS * SC_CHUNK) == 0
    per_w = n // SC_WORKERS

    @functools.partial(
        pl.kernel, mesh=_sc_mesh(), out_type=jax.ShapeDtypeStruct((n_out, d), x.dtype),
        scratch_types=[pltpu.VMEM((SC_CHUNK,), jnp.int32), pltpu.VMEM((SC_CHUNK,), jnp.int32),
                       pltpu.VMEM((SC_CHUNK, d), x.dtype), pltpu.SemaphoreType.DMA],
        name="sc_scatter_rows")
    def scatter(x_hbm, i1_hbm, i2_hbm, o_hbm, i1_v, i2_v, rows_v, sem):
        wid = lax.axis_index("s") * SC_CORES + lax.axis_index("c")

        @pl.loop(0, per_w // SC_CHUNK)
        def _(j):
            base = wid * per_w + j * SC_CHUNK
            pltpu.sync_copy(i1_hbm.at[pl.ds(base, SC_CHUNK)], i1_v)
            pltpu.sync_copy(i2_hbm.at[pl.ds(base, SC_CHUNK)], i2_v)
            pltpu.sync_copy(x_hbm.at[pl.ds(base, SC_CHUNK)], rows_v)
            pltpu.async_copy(rows_v, o_hbm.at[i1_v], sem).wait()
            pltpu.async_copy(rows_v, o_hbm.at[i2_v], sem).wait()

    return scatter(x, idx1, idx2)


def _sc_gather_rows(table, idx1, idx2):
    n = idx1.shape[0]
    d = table.shape[1]
    assert n % (SC_WORKERS * SC_CHUNK) == 0
    per_w = n // SC_WORKERS
    out = jax.ShapeDtypeStruct((n, d), table.dtype)

    @functools.partial(
        pl.kernel, mesh=_sc_mesh(), out_type=(out, out),
        scratch_types=[pltpu.VMEM((SC_CHUNK,), jnp.int32), pltpu.VMEM((SC_CHUNK, d), table.dtype),
                       pltpu.SemaphoreType.DMA],
        name="sc_gather_rows")
    def gather(t_hbm, i1_hbm, i2_hbm, o1_hbm, o2_hbm, i_v, rows_v, sem):
        wid = lax.axis_index("s") * SC_CORES + lax.axis_index("c")

        @pl.loop(0, per_w // SC_CHUNK)
        def _(j):
            base = wid * per_w + j * SC_CHUNK
            for i_hbm, o_hbm in ((i1_hbm, o1_hbm), (i2_hbm, o2_hbm)):
                pltpu.sync_copy(i_hbm.at[pl.ds(base, SC_CHUNK)], i_v)
                pltpu.async_copy(t_hbm.at[i_v], rows_v, sem).wait()
                pltpu.sync_copy(rows_v, o_hbm.at[pl.ds(base, SC_CHUNK)])

    return gather(table, idx1, idx2)


def _dest_kernel(rt_ref, starts_ref, d_ref):
    tr = rt_ref.shape[1]
    sub = lax.broadcasted_iota(jnp.int32, (N_EXPERTS, tr), 0)
    starts = starts_ref[:, :1]
    rows = []
    for e_row, pos_row in ((ROUTE_E1, ROUTE_POS1), (ROUTE_E2, ROUTE_POS2)):
        expert = rt_ref[e_row:e_row + 1, :].astype(jnp.int32)
        start = jnp.sum(jnp.where(sub == expert, starts, 0.0), axis=0, keepdims=True)
        rows.append(start + rt_ref[pos_row:pos_row + 1, :])
    d_ref[...] = jnp.concatenate(rows + [jnp.zeros((6, tr), F32)], axis=0).astype(jnp.int32)


def _routing_tables(rt, cnt, tm, n_tiles_max):
    n = rt.shape[1]
    counts = cnt[:, 0].astype(jnp.int32)
    padded = (counts + tm - 1) // tm * tm
    ends = jnp.cumsum(padded)
    starts = jnp.broadcast_to((ends - padded).astype(F32)[:, None], (N_EXPERTS, LANES))
    tr = min(2048, n)
    dest = pl.pallas_call(
        _dest_kernel,
        grid=(n // tr,),
        in_specs=[pl.BlockSpec((8, tr), lambda i: (0, i)), pl.BlockSpec((N_EXPERTS, LANES), lambda i: (0, 0))],
        out_specs=pl.BlockSpec((8, tr), lambda i: (0, i)),
        out_shape=jax.ShapeDtypeStruct((8, n), jnp.int32),
        compiler_params=_params(("parallel",)),
        name="route_dest",
    )(rt, starts)
    dest1, dest2 = dest[0], dest[1]
    tile_start = jnp.arange(n_tiles_max, dtype=jnp.int32) * tm
    tile_expert = jnp.sum((ends[None, :] <= tile_start[:, None]).astype(jnp.int32), axis=1)
    tile_expert = jnp.minimum(tile_expert, N_EXPERTS - 1)
    n_tiles = (ends[-1] // tm).astype(jnp.int32).reshape(1)
    valid = tile_start < ends[-1]
    prev_expert = jnp.concatenate([jnp.full((1,), -1, jnp.int32), tile_expert[:-1]])
    first = (valid & (tile_expert != prev_expert)).astype(jnp.int32)
    slot = (jnp.cumsum(first) - 1) % 2
    e_idx = jnp.arange(N_EXPERTS, dtype=jnp.int32)
    later = (padded > 0)[None, :] & (e_idx[None, :] > e_idx[:, None])
    next_run = jnp.min(jnp.where(later, e_idx[None, :], N_EXPERTS), axis=1)
    next_run = jnp.where(next_run == N_EXPERTS, -1, next_run).astype(jnp.int32)
    plan = (tile_expert, first, slot.astype(jnp.int32), next_run[tile_expert], n_tiles)
    return dest1, dest2, plan


def _moe_tiles_kernel(te_ref, first_ref, slot_ref, nxt_ref, nt_ref, xs_ref, wg_hbm, wu_hbm, wd_hbm, ys_ref,
                      wg_buf, wu_buf, wd_buf, sem):
    t = pl.program_id(0)

    def weight_copies(expert, slot):
        return (pltpu.make_async_copy(wg_hbm.at[expert], wg_buf.at[slot], sem.at[slot, 0]),
                pltpu.make_async_copy(wu_hbm.at[expert], wu_buf.at[slot], sem.at[slot, 1]),
                pltpu.make_async_copy(wd_hbm.at[expert], wd_buf.at[slot], sem.at[slot, 2]))

    @pl.when(t < nt_ref[0])
    def _():
        slot = slot_ref[t]

        @pl.when(t == 0)
        def _():
            for c in weight_copies(te_ref[0], 0):
                c.start()

        @pl.when(first_ref[t] == 1)
        def _():
            for c in weight_copies(te_ref[t], slot):
                c.wait()

            @pl.when(nxt_ref[t] >= 0)
            def _():
                for c in weight_copies(nxt_ref[t], 1 - slot):
                    c.start()

        x = _unpack_bf16_pairs(xs_ref[...]).astype(BF16)
        hg = jnp.dot(x, wg_buf[slot], preferred_element_type=F32)
        hu = jnp.dot(x, wu_buf[slot], preferred_element_type=F32)
        hid = (hg * jax.nn.sigmoid(hg) * hu).astype(BF16)
        ys_ref[...] = _pack_bf16_pairs(jnp.dot(hid, wd_buf[slot], preferred_element_type=F32))


def _moe_tiles(xs, plan, wg, wu, wd, tm):
    r, dh = xs.shape
    ne, d, de = wg.shape
    row = lambda t, te, first, slot, nxt, nt: (jnp.minimum(t, nt[0] - 1), 0)
    hbm = pl.BlockSpec(memory_space=pl.ANY)
    return pl.pallas_call(
        _moe_tiles_kernel,
        grid_spec=pltpu.PrefetchScalarGridSpec(
            num_scalar_prefetch=len(plan),
            grid=(r // tm,),
            in_specs=[pl.BlockSpec((tm, dh), row), hbm, hbm, hbm],
            out_specs=pl.BlockSpec((tm, dh), row),
            scratch_shapes=[
                pltpu.VMEM((2, d, de), wg.dtype),
                pltpu.VMEM((2, d, de), wu.dtype),
                pltpu.VMEM((2, de, d), wd.dtype),
                pltpu.SemaphoreType.DMA((2, 3)),
            ],
        ),
        out_shape=jax.ShapeDtypeStruct((r, dh), jnp.int32),
        compiler_params=_params(("arbitrary",)),
        name="moe_tiles",
    )(*plan, xs, wg, wu, wd)


def _combine_kernel(h_ref, y1_ref, y2_ref, route_ref, fg_ref, o_ref):
    lane = lax.broadcasted_iota(jnp.int32, route_ref.shape, 1)
    route = route_ref[...]
    w1 = jnp.sum(jnp.where(lane == ROUTE_W1, route, 0.0), axis=-1, keepdims=True)
    w2 = jnp.sum(jnp.where(lane == ROUTE_W2, route, 0.0), axis=-1, keepdims=True)
    y = h_ref[...] + w1 * _unpack_bf16_pairs(y1_ref[...]) + w2 * _unpack_bf16_pairs(y2_ref[...])
    o_ref[...] = y * lax.rsqrt(jnp.mean(y * y, axis=-1, keepdims=True) + EPS) * fg_ref[...]


def _combine(h, y1, y2, route, fg, tm):
    n, d = h.shape
    row = lambda i: (i, 0)
    return pl.pallas_call(
        _combine_kernel,
        grid=(n // tm,),
        in_specs=[
            pl.BlockSpec((tm, d), row),
            pl.BlockSpec((tm, d // 2), row),
            pl.BlockSpec((tm, d // 2), row),
            pl.BlockSpec((tm, LANES), row),
            pl.BlockSpec((1, d), lambda i: (0, 0)),
        ],
        out_specs=pl.BlockSpec((tm, d), row),
        out_shape=jax.ShapeDtypeStruct((n, d), F32),
        compiler_params=_params(("parallel",)),
        name="moe_combine",
    )(h, y1, y2, route, fg)


def _moe(a2p, route, rt, cnt, h, wg, wu, wd, fg, tm):
    n = h.shape[0]
    n_tiles_max = (2 * n) // tm + N_EXPERTS
    dest1, dest2, plan = _routing_tables(rt, cnt, tm, n_tiles_max)
    xs = _sc_scatter_rows(a2p, dest1, dest2, n_tiles_max * tm)
    ys = _moe_tiles(xs, plan, wg, wu, wd, tm)
    y1, y2 = _sc_gather_rows(ys, dest1, dest2)
    return _combine(h, y1, y2, route, fg, tm=min(TOKEN_TILE, n))


def _dup_heads(w, n_heads):
    d = w.shape[0]
    w = w.reshape(d, n_heads, 1, HEAD_DIM)
    return jnp.broadcast_to(w, (d, n_heads, 2, HEAD_DIM)).reshape(d, n_heads * 2 * HEAD_DIM)


def _encoder(x, proj_meta, wts):
    b, s, d = x.shape
    x2d = x.reshape(b * s, d)
    proj = _norm_proj(x2d, wts["g1"], wts["w_ext"], tm=TOKEN_TILE).reshape(b, s, -1)
    od0 = _diff_attn(proj, proj_meta, wts["lam_vecs"], wts["subln_g"], heads=(0,), bq=SKIP_BLOCK,
                     bk=min(SKIP_BLOCK, s), skip=True)
    od1 = _diff_attn(proj, proj_meta, wts["lam_vecs"], wts["subln_g"], heads=tuple(range(1, DA_HEADS)),
                     bq=DIFF_BQ, bk=min(DIFF_BK, s), skip=False)
    os_ = _win_attn(proj, proj_meta, wts["sink"])
    h, a2p, route, rt, cnt = _out_router(od0.reshape(b * s, -1), od1.reshape(b * s, -1), os_.reshape(b * s, -1), x2d,
                                     wts["wod"], wts["wos"],
                                     wts["g2"], wts["wr"], wts["br"], tm=TOKEN_TILE)
    y = _moe(a2p, route, rt, cnt, h, wts["wg"], wts["wu"], wts["wd"], wts["fg"], tm=MOE_TILE)
    return y.reshape(b, s, d)


def kernel(x_prompt, x_sample, meta, norm1_g, w_in, lam_q1, lam_k1, lam_q2, lam_k2, subln_g, sink, w_out,
           norm2_g, w_gr, b_gr, w_er, b_er, w_gate, w_up, w_down, final_g):
    d = x_prompt.shape[-1]
    w = w_in[0]
    c_kd = 2 * DA_HEADS * HEAD_DIM
    c_vd = 2 * c_kd
    c_qs = c_vd + DA_HEADS * 2 * HEAD_DIM
    c_ks = c_qs + SW_HEADS * HEAD_DIM
    c_vs = c_ks + SW_KV_HEADS * HEAD_DIM
    w_ext = jnp.concatenate(
        [w[:, :c_ks], _dup_heads(w[:, c_ks:c_vs], SW_KV_HEADS), _dup_heads(w[:, c_vs:], SW_KV_HEADS)],
        axis=1).astype(BF16)
    w_router = jnp.concatenate([w_er[0], w_gr[0]], axis=1)
    w_router = jnp.pad(w_router, ((0, 0), (0, LANES - w_router.shape[1])))
    wr_hi = w_router.astype(BF16)
    wr_lo = (w_router - wr_hi.astype(F32)).astype(BF16)
    br = jnp.pad(jnp.concatenate([b_er[0], b_gr[0]]), (0, LANES - N_EXPERTS - N_GROUPS)).reshape(1, LANES)
    wo = w_out[0].astype(BF16)
    half = DA_HEADS * 2 * HEAD_DIM
    wts = dict(
        g1=norm1_g[0].reshape(1, d), w_ext=w_ext,
        lam_vecs=(lam_q1[0].reshape(1, -1), lam_k1[0].reshape(1, -1),
                  lam_q2[0].reshape(1, -1), lam_k2[0].reshape(1, -1)),
        subln_g=subln_g[0].reshape(1, -1), sink=sink[0].reshape(1, -1),
        wod=wo[:half], wos=wo[half:], g2=norm2_g[0].reshape(1, d),
        wr=jnp.concatenate([wr_hi, wr_lo], axis=1), br=br,
        wg=w_gate[0].astype(BF16), wu=w_up[0].astype(BF16), wd=w_down[0].astype(BF16),
        fg=final_g.reshape(1, d),
    )
    proj_meta = _norm_proj(meta, wts["g1"], w_ext, tm=N_META)
    proj_meta = jnp.pad(proj_meta, ((0, LANES - N_META), (0, 0)))
    return _encoder(x_prompt, proj_meta, wts), _encoder(x_sample, proj_meta, wts)
```

```python
import functools
import math

import jax
import jax.numpy as jnp
import numpy as np
from jax import lax
from jax.experimental import pallas as pl
from jax.experimental.pallas import tpu as pltpu
from jax.experimental.pallas import tpu_sc as plsc

F32 = jnp.float32
BF16 = jnp.bfloat16

N_META = 16
HEAD_DIM = 64
DA_HEADS = 4
SW_HEADS = 8
SW_KV_HEADS = 2
SW_GROUP = SW_HEADS // SW_KV_HEADS
WINDOW = 128
N_GROUPS = 4
EXPERTS_PER_GROUP = 8
N_EXPERTS = N_GROUPS * EXPERTS_PER_GROUP
EPS = 1e-6
SUBLN_EPS = 1e-5
NEG_INF = -1e30
LAM_INIT = 0.8 - 0.6 * math.exp(-0.3 * 0)
LANES = 128
VMEM_LIMIT = 48 * 1024 * 1024

TOKEN_TILE = 1024
MOE_TILE = 1024
DIFF_BQ, DIFF_BK = 256, 1024
SKIP_BLOCK = 512

COL_QD, COL_KD, COL_VD, COL_QS, COL_KS, COL_VS, N_COLBLK = 0, 4, 8, 12, 16, 18, 20
CONTRACT_LAST = (((1,), (1,)), ((), ()))


def _params(sem):
    return pltpu.CompilerParams(dimension_semantics=sem, vmem_limit_bytes=VMEM_LIMIT)


def _norm_proj_kernel(x_ref, g_ref, w_ref, o_ref):
    x = x_ref[...]
    ms = jnp.mean(x * x, axis=-1, keepdims=True)
    y = (x * lax.rsqrt(ms + EPS) * g_ref[...]).astype(BF16)
    o_ref[...] = jnp.dot(y, w_ref[...], preferred_element_type=F32).astype(o_ref.dtype)


def _norm_proj(x2d, g, w_ext, tm):
    n, d = x2d.shape
    wcols = w_ext.shape[1]
    return pl.pallas_call(
        _norm_proj_kernel,
        grid=(n // tm,),
        in_specs=[
            pl.BlockSpec((tm, d), lambda i: (i, 0)),
            pl.BlockSpec((1, d), lambda i: (0, 0)),
            pl.BlockSpec((d, wcols), lambda i: (0, 0)),
        ],
        out_specs=pl.BlockSpec((tm, wcols), lambda i: (i, 0)),
        out_shape=jax.ShapeDtypeStruct((n, wcols), BF16),
        compiler_params=_params(("parallel",)),
        name="norm_proj",
    )(x2d, g, w_ext)


POS_SPLIT = 64
N_SPLIT = 3
LOG2E = 1.4426950408889634


def _key_pos_features(seq):
    j = jnp.arange(seq, dtype=jnp.int32)[:, None]
    lane = jnp.arange(LANES, dtype=jnp.int32)[None, :]
    hi = (j // POS_SPLIT * POS_SPLIT).astype(F32)
    lo = (j % POS_SPLIT).astype(F32)
    feat = jnp.where(lane < N_SPLIT, 1.0,
                     jnp.where(lane < 2 * N_SPLIT, hi, jnp.where(lane < 3 * N_SPLIT, lo, 0.0)))
    return feat.astype(BF16)


def _split_bf16(x):
    pieces = []
    for _ in range(N_SPLIT):
        p = x.astype(BF16).astype(F32)
        pieces.append(p)
        x = x - p
    return pieces


SKIP_MARGIN = 135.0


def _diff_attn_head(h, q_ref, k_ref, v_ref, kf_ref, km_ref, vm_ref, lam, sg_ref,
                    o_ref, qe_ref, acc_ref, m_ref, sd_ref, s1_ref, kn_ref, *, bq, bk, seq, skip):
    i = pl.program_id(1)
    slope = 2.0 ** (-8.0 * (h + 1) / DA_HEADS) * LOG2E
    lane = lax.broadcasted_iota(jnp.int32, (1, LANES), 1)
    q = (q_ref[...].astype(F32) * (LOG2E / math.sqrt(HEAD_DIM))).astype(BF16)
    zero = jnp.zeros_like(q)
    qmaps = (jnp.where(lane < HEAD_DIM, q, zero), jnp.where(lane >= HEAD_DIM, q, zero))
    qpos = i * bq + lax.broadcasted_iota(jnp.int32, (bq, 1), 0)
    row_pieces = _split_bf16(-slope * qpos.astype(F32))
    slope_pieces = _split_bf16(jnp.full((1, 1), slope, F32))
    feat = jnp.zeros((bq, LANES), F32)
    for n in range(N_SPLIT):
        feat = jnp.where(lane == n, row_pieces[n], feat)
        feat = jnp.where(lane == N_SPLIT + n, slope_pieces[n], feat)
        feat = jnp.where(lane == 2 * N_SPLIT + n, slope_pieces[n], feat)
    feats = (feat.astype(BF16), (-feat).astype(BF16))
    for side in range(2):
        for c in range(2):
            qe_ref[side, c * bq:(c + 1) * bq, :LANES] = qmaps[c]
            qe_ref[side, c * bq:(c + 1) * bq, LANES:] = feats[side]

    ones_blk = jnp.broadcast_to(jnp.where(lane == 0, 1.0, 0.0).astype(BF16), (bk, LANES))
    meta_mask = jnp.where(lane < N_META, 0.0, NEG_INF)

    def attend(s, v_ext):
        m_old = m_ref[...]
        m_new = jnp.maximum(m_old, jnp.max(s, axis=-1, keepdims=True))
        alpha = jnp.exp2(m_old - m_new)
        p = jnp.exp2(s - jnp.tile(m_new, (1, s.shape[1] // LANES))).astype(BF16)
        acc_ref[...] = jnp.tile(alpha, (1, 2)) * acc_ref[...] + jnp.dot(p, v_ext, preferred_element_type=F32)
        m_ref[...] = m_new

    m_ref[...] = jnp.full(m_ref.shape, NEG_INF, F32)
    acc_ref[...] = jnp.zeros(acc_ref.shape, F32)

    s0_ref = sd_ref.at[:, :bk]
    nblk = seq // bk
    diag = (i * bq) // bk

    def scores_block(dst_ref, kb, side, edge=None):
        start = pl.multiple_of(kb * bk, bk)
        k_ext = jnp.concatenate([k_ref[pl.ds(start, bk), :], kf_ref[pl.ds(start, bk), :]], axis=1)
        s = lax.dot_general(qe_ref[side], k_ext, CONTRACT_LAST, preferred_element_type=F32)
        dst_ref[...] = s if edge is None else s + edge

    def consume_block(src_ref, kb):
        start = pl.multiple_of(kb * bk, bk)
        attend(src_ref[...], jnp.concatenate([v_ref[pl.ds(start, bk), :], ones_blk], axis=1))

    d_start = pl.multiple_of(diag * bk, bk)
    kpos = diag * bk + lax.broadcasted_iota(jnp.int32, (1, bk), 1)
    bias = jnp.concatenate([-slope * jnp.abs(qpos - kpos).astype(F32),
                            jnp.broadcast_to(meta_mask, (bq, LANES))], axis=1)
    k_first = jnp.concatenate([k_ref[pl.ds(d_start, bk), :], km_ref[...]], axis=0)
    v_first = jnp.concatenate(
        [jnp.concatenate([v_ref[pl.ds(d_start, bk), :], vm_ref[...]], axis=0),
         jnp.broadcast_to(ones_blk[:1], (bk + LANES, LANES))], axis=1)
    sd_ref[...] = lax.dot_general(qe_ref[0, :, :LANES], k_first, CONTRACT_LAST,
                                  preferred_element_type=F32) + jnp.tile(bias, (2, 1))

    def all_blocks():
        def key_block(pos):
            t = pos - 1
            return t + jnp.where(t >= diag, 1, 0)

        def scores_into(dst_ref, pos):
            kb = key_block(pos)
            scores_block(dst_ref, kb, jnp.where(kb > diag, 1, 0))

        scores_into(s1_ref, 1)
        attend(sd_ref[...], v_first)
        for u in range((nblk - 2) // 2):
            scores_into(s0_ref, 2 * u + 2)
            consume_block(s1_ref, key_block(2 * u + 1))
            scores_into(s1_ref, 2 * u + 3)
            consume_block(s0_ref, key_block(2 * u + 2))
        consume_block(s1_ref, key_block(nblk - 1))

    def neighbours_only():
        prev = jnp.maximum(diag - 1, 0)
        nxt = jnp.minimum(diag + 1, nblk - 1)
        scores_block(s1_ref, prev, 0, edge=jnp.where(diag == 0, NEG_INF, 0.0))
        attend(sd_ref[...], v_first)
        scores_block(s0_ref, nxt, 1, edge=jnp.where(diag == nblk - 1, NEG_INF, 0.0))
        consume_block(s1_ref, prev)
        consume_block(s0_ref, nxt)

    if nblk == 1:
        attend(sd_ref[...], v_first)
    elif skip:
        assert bq == bk and nblk > 2
        @pl.when(i == 0)
        def _():
            kf32 = k_ref[...].astype(F32)
            kn_ref[0] = jnp.max(jnp.sum(kf32 * kf32, axis=-1, keepdims=True))
        qf32 = q.astype(F32)
        qn2 = jnp.max(jnp.sum(qf32 * qf32, axis=-1, keepdims=True), axis=0, keepdims=True)
        s_meta = lax.dot_general(qe_ref[0, :, :LANES], km_ref[...], CONTRACT_LAST,
                                 preferred_element_type=F32) + meta_mask
        m_low = jnp.min(jnp.max(s_meta, axis=-1, keepdims=True), axis=0, keepdims=True)
        reach = (jnp.sqrt(qn2 * kn_ref[0]) - m_low + SKIP_MARGIN) * (1.0 / slope)
        near = jnp.clip(reach, 0.0, float(seq))[0, 0] <= float(bk)

        @pl.when(near)
        def _():
            neighbours_only()

        @pl.when(jnp.logical_not(near))
        def _():
            all_blocks()
    else:
        all_blocks()

    o1 = acc_ref[:bq, :LANES] / acc_ref[:bq, LANES:LANES + 1]
    o2 = acc_ref[bq:, :LANES] / acc_ref[bq:, LANES:LANES + 1]
    o = o1 - lam * o2
    o = o * lax.rsqrt(jnp.mean(o * o, axis=-1, keepdims=True) + SUBLN_EPS) * sg_ref[...]
    o_ref[...] = (o * (1.0 - LAM_INIT)).astype(o_ref.dtype)


def _diff_attn_kernel(*refs, heads, **kw):
    nh = len(heads)
    q_refs, k_refs, v_refs = refs[:nh], refs[nh:2 * nh], refs[2 * nh:3 * nh]
    kf_ref = refs[3 * nh]
    km_refs, vm_refs = refs[3 * nh + 1:4 * nh + 1], refs[4 * nh + 1:5 * nh + 1]
    lq1_ref, lk1_ref, lq2_ref, lk2_ref, sg_ref, o_ref, qe_ref, acc_ref, m_ref, sd_ref, s1_ref, kn_ref = refs[5 * nh + 1:]
    lam = (jnp.exp(jnp.sum(lq1_ref[...] * lk1_ref[...], axis=-1, keepdims=True))
           - jnp.exp(jnp.sum(lq2_ref[...] * lk2_ref[...], axis=-1, keepdims=True)) + LAM_INIT)
    for n, h in enumerate(heads):
        _diff_attn_head(h, q_refs[n], k_refs[n], v_refs[n], kf_ref, km_refs[n], vm_refs[n], lam, sg_ref,
                        o_ref.at[:, n * LANES:(n + 1) * LANES],
                        qe_ref.at[n], acc_ref.at[n], m_ref.at[n], sd_ref.at[n], s1_ref.at[n], kn_ref, **kw)


def _diff_attn(proj, proj_meta, lam_vecs, subln_g, heads, bq, bk, skip):
    b, s, _ = proj.shape
    bq = min(bq, s)
    assert bk % bq == 0 and s % bk == 0 and (s // bk == 1 or (s // bk) % 2 == 0)
    nh = len(heads)
    vec = pl.BlockSpec((1, HEAD_DIM), lambda bi, i: (0, 0))
    q_specs = [pl.BlockSpec((None, bq, LANES), lambda bi, i, c=COL_QD + h: (bi, i, c)) for h in heads]
    k_specs = [pl.BlockSpec((None, s, LANES), lambda bi, i, c=COL_KD + h: (bi, 0, c)) for h in heads]
    v_specs = [pl.BlockSpec((None, s, LANES), lambda bi, i, c=COL_VD + h: (bi, 0, c)) for h in heads]
    km_specs = [pl.BlockSpec((LANES, LANES), lambda bi, i, c=COL_KD + h: (0, c)) for h in heads]
    vm_specs = [pl.BlockSpec((LANES, LANES), lambda bi, i, c=COL_VD + h: (0, c)) for h in heads]
    return pl.pallas_call(
        functools.partial(_diff_attn_kernel, heads=heads, bq=bq, bk=bk, seq=s, skip=skip),
        grid=(b, s // bq),
        in_specs=(q_specs + k_specs + v_specs + [pl.BlockSpec((s, LANES), lambda bi, i: (0, 0))]
                  + km_specs + vm_specs + [vec, vec, vec, vec, pl.BlockSpec((1, LANES), lambda bi, i: (0, 0))]),
        out_specs=pl.BlockSpec((None, bq, nh * LANES), lambda bi, i: (bi, i, 0)),
        out_shape=jax.ShapeDtypeStruct((b, s, nh * LANES), BF16),
        scratch_shapes=[
            pltpu.VMEM((nh, 2, 2 * bq, 2 * LANES), BF16),
            pltpu.VMEM((nh, 2 * bq, 2 * LANES), F32),
            pltpu.VMEM((nh, 2 * bq, LANES), F32),
            pltpu.VMEM((nh, 2 * bq, bk + LANES), F32),
            pltpu.VMEM((nh, 2 * bq, bk), F32),
            pltpu.SMEM((1,), F32),
        ],
        compiler_params=_params(("parallel", "arbitrary")),
        name="diff_attn",
    )(*([proj] * (3 * nh)), _key_pos_features(s), *([proj_meta] * (2 * nh)), *lam_vecs, subln_g)


WIN_KEYS = 4 * WINDOW
WIN_QBLOCKS = 8


def _win_tables():
    r = np.arange(WINDOW)
    qf = np.zeros((SW_KV_HEADS, SW_GROUP * WINDOW, LANES), np.float32)
    for head in range(SW_HEADS):
        slope = 2.0 ** (-8.0 * (head + 1) / SW_HEADS)
        i_rel = WINDOW + r
        hi, lo = i_rel // POS_SPLIT * POS_SPLIT, i_rel % POS_SPLIT
        rows = qf[head // SW_GROUP, (head % SW_GROUP) * WINDOW:(head % SW_GROUP + 1) * WINDOW]
        rows[:, 0], rows[:, 1], rows[:, 2], rows[:, 3] = -slope * hi, -slope * lo, slope, slope
        rows[:, 4:8] = -rows[:, 0:4]
    kf = np.zeros((2, WINDOW, LANES), np.float32)
    for n, (blk, right) in enumerate(((0, 0), (2, 1))):
        j_rel = blk * WINDOW + r
        o = 4 * right
        kf[n, :, o], kf[n, :, o + 1] = 1.0, 1.0
        kf[n, :, o + 2], kf[n, :, o + 3] = j_rel // POS_SPLIT * POS_SPLIT, j_rel % POS_SPLIT
    mask = np.zeros((2, WINDOW, WINDOW), np.float32)
    mask[0] = np.where(r[None, :] >= r[:, None], 0.0, NEG_INF)
    mask[1] = np.where(r[None, :] <= r[:, None], 0.0, NEG_INF)
    cur = np.zeros((SW_KV_HEADS, SW_GROUP * WINDOW, WINDOW), np.float32)
    for head in range(SW_HEADS):
        slope = 2.0 ** (-8.0 * (head + 1) / SW_HEADS)
        cur[head // SW_GROUP, (head % SW_GROUP) * WINDOW:(head % SW_GROUP + 1) * WINDOW] = (
            -slope * np.abs(r[:, None] - r[None, :]))
    return jnp.asarray(qf, BF16), jnp.asarray(kf, BF16), jnp.asarray(mask, F32), jnp.asarray(cur, F32)


def _win_attn_kernel(q_ref, kp_ref, kc_ref, kn_ref, vp_ref, vc_ref, vn_ref, km_ref, vm_ref, qf_ref, kf_ref,
                     mask_ref, cur_ref, sink_ref, o_ref, *, nstep):
    c = pl.program_id(1)
    lane = lax.broadcasted_iota(jnp.int32, (1, LANES), 1)
    scale = jnp.asarray(1.0 / math.sqrt(HEAD_DIM), BF16)
    mask_meta = jnp.where(lane < N_META, 0.0, NEG_INF)
    row = lax.broadcasted_iota(jnp.int32, (SW_GROUP * WINDOW, 1), 0)
    ones_blk = jnp.broadcast_to(jnp.where(lane == 0, 1.0, 0.0).astype(BF16), (WIN_KEYS, LANES))
    zeros_blk = jnp.zeros((WINDOW, LANES), BF16)
    gw = SW_GROUP * HEAD_DIM
    own = [slice(j * WINDOW, (j + 1) * WINDOW) for j in range(WIN_QBLOCKS)]
    kblk = [kp_ref] + [kc_ref.at[r] for r in own] + [kn_ref]
    vblk = [vp_ref] + [vc_ref.at[r] for r in own] + [vn_ref]
    for j in range(WIN_QBLOCKS):
        rq = slice(j * WINDOW, (j + 1) * WINDOW)
        edge_p = jnp.where(c == 0, NEG_INF, 0.0) if j == 0 else 0.0
        edge_n = jnp.where(c == nstep - 1, NEG_INF, 0.0) if j == WIN_QBLOCKS - 1 else 0.0
        mask_prev = jnp.tile(mask_ref[0] + edge_p, (SW_GROUP, 1))
        mask_next = jnp.tile(mask_ref[1] + edge_n, (SW_GROUP, 1))
        for g in range(SW_KV_HEADS):
            kv = slice(g * LANES, (g + 1) * LANES)
            rows = []
            for hh in range(SW_GROUP):
                col = g * gw + (hh // 2) * LANES
                qb = q_ref[rq, col:col + LANES] * scale
                keep = (lane < HEAD_DIM) if hh % 2 == 0 else (lane >= HEAD_DIM)
                rows.append(jnp.where(keep, qb, jnp.zeros_like(qb)))
            q_ext = jnp.concatenate([jnp.concatenate(rows, axis=0), qf_ref[g]], axis=1)
            k_all = jnp.concatenate([
                jnp.concatenate([km_ref[:, kv], zeros_blk], axis=1),
                jnp.concatenate([kblk[j][:, kv], kf_ref[0]], axis=1),
                jnp.concatenate([kblk[j + 1][:, kv], zeros_blk], axis=1),
                jnp.concatenate([kblk[j + 2][:, kv], kf_ref[1]], axis=1)], axis=0)
            s = lax.dot_general(q_ext, k_all, CONTRACT_LAST, preferred_element_type=F32)
            s_meta = s[:, :LANES] + mask_meta
            s_prev = s[:, LANES:2 * LANES] + mask_prev
            s_cur = s[:, 2 * LANES:3 * LANES] + cur_ref[g]
            s_next = s[:, 3 * LANES:] + mask_next
            sink = jnp.zeros((SW_GROUP * WINDOW, 1), F32)
            for hh in range(SW_GROUP):
                sink = jnp.where(row // WINDOW == hh, sink_ref[0, g * SW_GROUP + hh], sink)
            parts = (s_meta, s_prev, s_cur, s_next)
            m = jnp.maximum(jnp.maximum(s_meta, s_prev), jnp.maximum(s_cur, s_next))
            m = jnp.maximum(jnp.max(m, axis=-1, keepdims=True), sink)
            p = jnp.concatenate([jnp.exp(x - m) for x in parts], axis=1).astype(BF16)
            v_all = jnp.concatenate(
                [jnp.concatenate([vm_ref[:, kv], vblk[j][:, kv], vblk[j + 1][:, kv], vblk[j + 2][:, kv]], axis=0),
                 ones_blk], axis=1)
            acc = jnp.dot(p, v_all, preferred_element_type=F32)
            denom = acc[:, LANES:LANES + 1] + jnp.exp(sink - m)
            o = acc[:, :LANES] / denom
            for lb in range(SW_GROUP // 2):
                even = o[(2 * lb) * WINDOW:(2 * lb + 1) * WINDOW]
                odd = o[(2 * lb + 1) * WINDOW:(2 * lb + 2) * WINDOW]
                col = g * gw + lb * LANES
                o_ref[rq, col:col + LANES] = jnp.where(lane < HEAD_DIM, even, odd).astype(o_ref.dtype)


def _win_attn(proj, proj_meta, sink):
    b, s, _ = proj.shape
    nblk = s // WINDOW
    nq = WIN_QBLOCKS
    nstep = nblk // nq
    qw = SW_HEADS * HEAD_DIM
    kvw = SW_KV_HEADS * LANES
    qf, kf, mask, cur = _win_tables()

    def edge_spec(col, shift):
        return pl.BlockSpec(
            (None, WINDOW, kvw),
            lambda bi, c: (bi, jnp.clip(nq * c + shift, 0, nblk - 1), col * LANES // kvw))

    def pair_spec(col):
        return pl.BlockSpec((None, nq * WINDOW, kvw), lambda bi, c: (bi, c, col * LANES // kvw))

    return pl.pallas_call(
        functools.partial(_win_attn_kernel, nstep=nstep),
        grid=(b, nstep),
        in_specs=[
            pl.BlockSpec((None, nq * WINDOW, qw), lambda bi, c: (bi, c, COL_QS * LANES // qw)),
            edge_spec(COL_KS, -1), pair_spec(COL_KS), edge_spec(COL_KS, nq),
            edge_spec(COL_VS, -1), pair_spec(COL_VS), edge_spec(COL_VS, nq),
            pl.BlockSpec((LANES, kvw), lambda bi, c: (0, COL_KS * LANES // kvw)),
            pl.BlockSpec((LANES, kvw), lambda bi, c: (0, COL_VS * LANES // kvw)),
            pl.BlockSpec((SW_KV_HEADS, SW_GROUP * WINDOW, LANES), lambda bi, c: (0, 0, 0)),
            pl.BlockSpec((2, WINDOW, LANES), lambda bi, c: (0, 0, 0)),
            pl.BlockSpec((2, WINDOW, WINDOW), lambda bi, c: (0, 0, 0)),
            pl.BlockSpec((SW_KV_HEADS, SW_GROUP * WINDOW, WINDOW), lambda bi, c: (0, 0, 0)),
            pl.BlockSpec(memory_space=pltpu.SMEM),
        ],
        out_specs=pl.BlockSpec((None, nq * WINDOW, qw), lambda bi, c: (bi, c, 0)),
        out_shape=jax.ShapeDtypeStruct((b, s, qw), BF16),
        compiler_params=_params(("parallel", "arbitrary")),
        name="win_attn",
    )(proj, proj, proj, proj, proj, proj, proj, proj_meta, proj_meta, qf, kf, mask, cur, sink)


ROUTE_E1, ROUTE_E2, ROUTE_POS1, ROUTE_POS2, ROUTE_W1, ROUTE_W2 = range(6)


def _pack_bf16_pairs(x):
    k = x.shape[1] // 2
    bits = lax.bitcast_convert_type(x.astype(BF16).astype(F32), jnp.uint32)
    return lax.bitcast_convert_type(bits[:, :k] | (bits[:, k:] >> 16), jnp.int32)


def _unpack_bf16_pairs(w):
    bits = lax.bitcast_convert_type(w, jnp.uint32)
    hi = lax.bitcast_convert_type(bits & jnp.uint32(0xFFFF0000), F32)
    lo = lax.bitcast_convert_type(bits << 16, F32)
    return jnp.concatenate([hi, lo], axis=1)


def _out_router_kernel(od0_ref, od1_ref, os_ref, x_ref, wod_ref, wos_ref, g2_ref, wr_ref, br_ref,
                       h_ref, a_ref, route_ref, rt_ref, cnt_ref, base_ref):
    @pl.when(pl.program_id(0) == 0)
    def _():
        base_ref[...] = jnp.zeros_like(base_ref)

    h = (x_ref[...]
         + jnp.dot(jnp.concatenate([od0_ref[...], od1_ref[...]], axis=1), wod_ref[...],
                   preferred_element_type=F32)
         + jnp.dot(os_ref[...], wos_ref[...], preferred_element_type=F32))
    h_ref[...] = h
    a = h * lax.rsqrt(jnp.mean(h * h, axis=-1, keepdims=True) + EPS) * g2_ref[...]
    a_hi = a.astype(BF16)
    a_ref[...] = _pack_bf16_pairs(a)
    a_lo = (a - a_hi.astype(F32)).astype(BF16)
    hi_terms = jnp.dot(a_hi, wr_ref[...], preferred_element_type=F32)
    logits = (hi_terms[:, :LANES] + hi_terms[:, LANES:]
              + jnp.dot(a_lo, wr_ref[:, :LANES], preferred_element_type=F32)) + br_ref[...]
    tm = logits.shape[0]
    lt = jnp.transpose(logits)
    le = lt[:N_EXPERTS]
    sub_g = lax.broadcasted_iota(jnp.int32, (8, tm), 0)
    lg = jnp.where(sub_g < N_GROUPS, lt[N_EXPERTS:N_EXPERTS + 8], NEG_INF)
    sub_e = lax.broadcasted_iota(jnp.int32, (N_EXPERTS, tm), 0)

    def first_argmax(vals, mask, sub, size):
        mx = jnp.max(vals, axis=0, keepdims=True)
        idx = jnp.min(jnp.where(mask & (vals == mx), sub, size), axis=0, keepdims=True)
        return mx, idx

    gmax, gidx = first_argmax(lg, sub_g < N_GROUPS, sub_g, 8)
    g_w = 1.0 / jnp.sum(jnp.exp(lg - gmax), axis=0, keepdims=True)
    in_group = (sub_e // EXPERTS_PER_GROUP) == gidx
    m1, i1 = first_argmax(jnp.where(in_group, le, NEG_INF), in_group, sub_e, N_EXPERTS)
    rest = in_group & (sub_e != i1)
    m2, i2 = first_argmax(jnp.where(rest, le, NEG_INF), rest, sub_e, N_EXPERTS)
    r = jnp.exp(m2 - m1)
    w1 = g_w / (1.0 + r)
    w2 = g_w * r / (1.0 + r)

    rix = lax.broadcasted_iota(jnp.int32, (tm, tm), 0)
    cix = lax.broadcasted_iota(jnp.int32, (tm, tm), 1)
    earlier = jnp.where(rix < cix, 1.0, 0.0).astype(BF16)
    oh1 = jnp.where(sub_e == i1, 1.0, 0.0)
    oh2 = jnp.where(sub_e == i2, 1.0, 0.0)
    before = jnp.dot(jnp.concatenate([oh1, oh2], axis=0).astype(BF16), earlier, preferred_element_type=F32)
    base1 = base_ref[:, :1]
    pos1 = jnp.sum(oh1 * (base1 + before[:N_EXPERTS]), axis=0, keepdims=True)
    base2 = base1 + jnp.sum(oh1, axis=1, keepdims=True)
    pos2 = jnp.sum(oh2 * (base2 + before[N_EXPERTS:]), axis=0, keepdims=True)
    total = jnp.broadcast_to(base2 + jnp.sum(oh2, axis=1, keepdims=True), base_ref.shape)
    base_ref[...] = total
    cnt_ref[...] = total

    fields = jnp.concatenate([i1.astype(F32), i2.astype(F32), pos1, pos2, w1, w2, jnp.zeros((2, tm), F32)], axis=0)
    rt_ref[...] = fields
    route_ref[...] = jnp.transpose(jnp.concatenate([fields, jnp.zeros((LANES - 8, tm), F32)], axis=0))


def _out_router(od0, od1, os_, x2d, wod, wos, g2, wr, br, tm):
    n, d = x2d.shape
    half = os_.shape[1]
    row = lambda i: (i, 0)
    const = lambda i: (0, 0)
    return pl.pallas_call(
        _out_router_kernel,
        grid=(n // tm,),
        in_specs=[
            pl.BlockSpec((tm, od0.shape[1]), row),
            pl.BlockSpec((tm, od1.shape[1]), row),
            pl.BlockSpec((tm, half), row),
            pl.BlockSpec((tm, d), row),
            pl.BlockSpec((half, d), const),
            pl.BlockSpec((half, d), const),
            pl.BlockSpec((1, d), const),
            pl.BlockSpec((d, 2 * LANES), const),
            pl.BlockSpec((1, LANES), const),
        ],
        out_specs=[
            pl.BlockSpec((tm, d), row),
            pl.BlockSpec((tm, d // 2), row),
            pl.BlockSpec((tm, LANES), row),
            pl.BlockSpec((8, tm), lambda i: (0, i)),
            pl.BlockSpec((N_EXPERTS, LANES), const),
        ],
        out_shape=[
            jax.ShapeDtypeStruct((n, d), F32),
            jax.ShapeDtypeStruct((n, d // 2), jnp.int32),
            jax.ShapeDtypeStruct((n, LANES), F32),
            jax.ShapeDtypeStruct((8, n), F32),
            jax.ShapeDtypeStruct((N_EXPERTS, LANES), F32),
        ],
        scratch_shapes=[pltpu.VMEM((N_EXPERTS, LANES), F32)],
        compiler_params=_params(("arbitrary",)),
        name="out_router",
    )(od0, od1, os_, x2d, wod, wos, g2, wr, br)


SC_CORES, SC_SUBCORES = 2, 16
SC_WORKERS = SC_CORES * SC_SUBCORES
SC_CHUNK = 128


def _sc_mesh():
    return plsc.VectorSubcoreMesh(core_axis_name="c", subcore_axis_name="s",
                                  num_cores=SC_CORES, num_subcores=SC_SUBCORES)


def _sc_scatter_rows(x, idx1, idx2, n_out):
    n, d = x.shape
    assert n % (SC_WORKERS * SC_CHUNK) == 0
    per_w = n // SC_WORKERS

    @functools.partial(
        pl.kernel, mesh=_sc_mesh(), out_type=jax.ShapeDtypeStruct((n_out, d), x.dtype),
        scratch_types=[pltpu.VMEM((SC_CHUNK,), jnp.int32), pltpu.VMEM((SC_CHUNK,), jnp.int32),
                       pltpu.VMEM((SC_CHUNK, d), x.dtype), pltpu.SemaphoreType.DMA],
        name="sc_scatter_rows")
    def scatter(x_hbm, i1_hbm, i2_hbm, o_hbm, i1_v, i2_v, rows_v, sem):
        wid = lax.axis_index("s") * SC_CORES + lax.axis_index("c")

        @pl.loop(0, per_w // SC_CHUNK)
        def _(j):
            base = wid * per_w + j * SC_CHUNK
            pltpu.sync_copy(i1_hbm.at[pl.ds(base, SC_CHUNK)], i1_v)
            pltpu.sync_copy(i2_hbm.at[pl.ds(base, SC_CHUNK)], i2_v)
            pltpu.sync_copy(x_hbm.at[pl.ds(base, SC_CHUNK)], rows_v)
            pltpu.async_copy(rows_v, o_hbm.at[i1_v], sem).wait()
            pltpu.async_copy(rows_v, o_hbm.at[i2_v], sem).wait()

    return scatter(x, idx1, idx2)


def _sc_gather_rows(table, idx1, idx2):
    n = idx1.shape[0]
    d = table.shape[1]
    assert n % (SC_WORKERS * SC_CHUNK) == 0
    per_w = n // SC_WORKERS
    out = jax.ShapeDtypeStruct((n, d), table.dtype)

    @functools.partial(
        pl.kernel, mesh=_sc_mesh(), out_type=(out, out),
        scratch_types=[pltpu.VMEM((SC_CHUNK,), jnp.int32), pltpu.VMEM((SC_CHUNK, d), table.dtype),
                       pltpu.SemaphoreType.DMA],
        name="sc_gather_rows")
    def gather(t_hbm, i1_hbm, i2_hbm, o1_hbm, o2_hbm, i_v, rows_v, sem):
        wid = lax.axis_index("s") * SC_CORES + lax.axis_index("c")

        @pl.loop(0, per_w // SC_CHUNK)
        def _(j):
            base = wid * per_w + j * SC_CHUNK
            for i_hbm, o_hbm in ((i1_hbm, o1_hbm), (i2_hbm, o2_hbm)):
                pltpu.sync_copy(i_hbm.at[pl.ds(base, SC_CHUNK)], i_v)
                pltpu.async_copy(t_hbm.at[i_v], rows_v, sem).wait()
                pltpu.sync_copy(rows_v, o_hbm.at[pl.ds(base, SC_CHUNK)])

    return gather(table, idx1, idx2)


def _dest_kernel(rt_ref, starts_ref, d_ref):
    tr = rt_ref.shape[1]
    sub = lax.broadcasted_iota(jnp.int32, (N_EXPERTS, tr), 0)
    starts = starts_ref[:, :1]
    rows = []
    for e_row, pos_row in ((ROUTE_E1, ROUTE_POS1), (ROUTE_E2, ROUTE_POS2)):
        expert = rt_ref[e_row:e_row + 1, :].astype(jnp.int32)
        start = jnp.sum(jnp.where(sub == expert, starts, 0.0), axis=0, keepdims=True)
        rows.append(start + rt_ref[pos_row:pos_row + 1, :])
    d_ref[...] = jnp.concatenate(rows + [jnp.zeros((6, tr), F32)], axis=0).astype(jnp.int32)


def _routing_tables(rt, cnt, tm, n_tiles_max):
    n = rt.shape[1]
    counts = cnt[:, 0].astype(jnp.int32)
    padded = (counts + tm - 1) // tm * tm
    ends = jnp.cumsum(padded)
    starts = jnp.broadcast_to((ends - padded).astype(F32)[:, None], (N_EXPERTS, LANES))
    tr = min(2048, n)
    dest = pl.pallas_call(
        _dest_kernel,
        grid=(n // tr,),
        in_specs=[pl.BlockSpec((8, tr), lambda i: (0, i)), pl.BlockSpec((N_EXPERTS, LANES), lambda i: (0, 0))],
        out_specs=pl.BlockSpec((8, tr), lambda i: (0, i)),
        out_shape=jax.ShapeDtypeStruct((8, n), jnp.int32),
        compiler_params=_params(("parallel",)),
        name="route_dest",
    )(rt, starts)
    dest1, dest2 = dest[0], dest[1]
    tile_start = jnp.arange(n_tiles_max, dtype=jnp.int32) * tm
    tile_expert = jnp.sum((ends[None, :] <= tile_start[:, None]).astype(jnp.int32), axis=1)
    tile_expert = jnp.minimum(tile_expert, N_EXPERTS - 1)
    n_tiles = (ends[-1] // tm).astype(jnp.int32).reshape(1)
    valid = tile_start < ends[-1]
    prev_expert = jnp.concatenate([jnp.full((1,), -1, jnp.int32), tile_expert[:-1]])
    first = (valid & (tile_expert != prev_expert)).astype(jnp.int32)
    slot = (jnp.cumsum(first) - 1) % WEIGHT_SLOTS
    e_idx = jnp.arange(N_EXPERTS, dtype=jnp.int32)
    later = (padded > 0)[None, :] & (e_idx[None, :] > e_idx[:, None])
    next_run = jnp.min(jnp.where(later, e_idx[None, :], N_EXPERTS), axis=1)
    next_run = jnp.where(next_run == N_EXPERTS, -1, next_run).astype(jnp.int32)
    next_run2 = jnp.where(next_run >= 0, next_run[jnp.maximum(next_run, 0)], -1)
    plan = (tile_expert, first, slot.astype(jnp.int32), next_run[tile_expert], next_run2[tile_expert], n_tiles)
    return dest1, dest2, plan


WEIGHT_SLOTS = 3


def _moe_tiles_kernel(te_ref, first_ref, slot_ref, nxt_ref, nxt2_ref, nt_ref, xs_ref, wg_hbm, wu_hbm, wd_hbm,
                      ys_ref, wg_buf, wu_buf, wd_buf, sem):
    t = pl.program_id(0)

    def weight_copies(expert, slot):
        return (pltpu.make_async_copy(wg_hbm.at[expert], wg_buf.at[slot], sem.at[slot, 0]),
                pltpu.make_async_copy(wu_hbm.at[expert], wu_buf.at[slot], sem.at[slot, 1]),
                pltpu.make_async_copy(wd_hbm.at[expert], wd_buf.at[slot], sem.at[slot, 2]))

    @pl.when(t < nt_ref[0])
    def _():
        slot = slot_ref[t]

        @pl.when(t == 0)
        def _():
            for c in weight_copies(te_ref[0], 0):
                c.start()

            @pl.when(nxt_ref[0] >= 0)
            def _():
                for c in weight_copies(nxt_ref[0], 1):
                    c.start()

        @pl.when(first_ref[t] == 1)
        def _():
            for c in weight_copies(te_ref[t], slot):
                c.wait()

            @pl.when(nxt2_ref[t] >= 0)
            def _():
                for c in weight_copies(nxt2_ref[t], (slot + 2) % WEIGHT_SLOTS):
                    c.start()

        x = _unpack_bf16_pairs(xs_ref[...]).astype(BF16)
        hg = jnp.dot(x, wg_buf[slot], preferred_element_type=F32)
        hu = jnp.dot(x, wu_buf[slot], preferred_element_type=F32)
        hid = (hg * jax.nn.sigmoid(hg) * hu).astype(BF16)
        ys_ref[...] = _pack_bf16_pairs(jnp.dot(hid, wd_buf[slot], preferred_element_type=F32))


def _moe_tiles(xs, plan, wg, wu, wd, tm):
    r, dh = xs.shape
    ne, d, de = wg.shape
    row = lambda t, te, first, slot, nxt, nxt2, nt: (jnp.minimum(t, nt[0] - 1), 0)
    hbm = pl.BlockSpec(memory_space=pl.ANY)
    return pl.pallas_call(
        _moe_tiles_kernel,
        grid_spec=pltpu.PrefetchScalarGridSpec(
            num_scalar_prefetch=len(plan),
            grid=(r // tm,),
            in_specs=[pl.BlockSpec((tm, dh), row), hbm, hbm, hbm],
            out_specs=pl.BlockSpec((tm, dh), row),
            scratch_shapes=[
                pltpu.VMEM((WEIGHT_SLOTS, d, de), wg.dtype),
                pltpu.VMEM((WEIGHT_SLOTS, d, de), wu.dtype),
                pltpu.VMEM((WEIGHT_SLOTS, de, d), wd.dtype),
                pltpu.SemaphoreType.DMA((WEIGHT_SLOTS, 3)),
            ],
        ),
        out_shape=jax.ShapeDtypeStruct((r, dh), jnp.int32),
        compiler_params=_params(("arbitrary",)),
        name="moe_tiles",
    )(*plan, xs, wg, wu, wd)


def _combine_kernel(h_ref, y1_ref, y2_ref, route_ref, fg_ref, o_ref):
    lane = lax.broadcasted_iota(jnp.int32, route_ref.shape, 1)
    route = route_ref[...]
    w1 = jnp.sum(jnp.where(lane == ROUTE_W1, route, 0.0), axis=-1, keepdims=True)
    w2 = jnp.sum(jnp.where(lane == ROUTE_W2, route, 0.0), axis=-1, keepdims=True)
    y = h_ref[...] + w1 * _unpack_bf16_pairs(y1_ref[...]) + w2 * _unpack_bf16_pairs(y2_ref[...])
    o_ref[...] = y * lax.rsqrt(jnp.mean(y * y, axis=-1, keepdims=True) + EPS) * fg_ref[...]


def _combine(h, y1, y2, route, fg, tm):
    n, d = h.shape
    row = lambda i: (i, 0)
    return pl.pallas_call(
        _combine_kernel,
        grid=(n // tm,),
        in_specs=[
            pl.BlockSpec((tm, d), row),
            pl.BlockSpec((tm, d // 2), row),
            pl.BlockSpec((tm, d // 2), row),
            pl.BlockSpec((tm, LANES), row),
            pl.BlockSpec((1, d), lambda i: (0, 0)),
        ],
        out_specs=pl.BlockSpec((tm, d), row),
        out_shape=jax.ShapeDtypeStruct((n, d), F32),
        compiler_params=_params(("parallel",)),
        name="moe_combine",
    )(h, y1, y2, route, fg)


def _moe(a2p, route, rt, cnt, h, wg, wu, wd, fg, tm):
    n = h.shape[0]
    n_tiles_max = (2 * n) // tm + N_EXPERTS
    dest1, dest2, plan = _routing_tables(rt, cnt, tm, n_tiles_max)
    xs = _sc_scatter_rows(a2p, dest1, dest2, n_tiles_max * tm)
    ys = _moe_tiles(xs, plan, wg, wu, wd, tm)
    y1, y2 = _sc_gather_rows(ys, dest1, dest2)
    return _combine(h, y1, y2, route, fg, tm=min(TOKEN_TILE, n))


def _dup_heads(w, n_heads):
    d = w.shape[0]
    w = w.reshape(d, n_heads, 1, HEAD_DIM)
    return jnp.broadcast_to(w, (d, n_heads, 2, HEAD_DIM)).reshape(d, n_heads * 2 * HEAD_DIM)


def _encoder(x, proj_meta, wts):
    b, s, d = x.shape
    x2d = x.reshape(b * s, d)
    proj = _norm_proj(x2d, wts["g1"], wts["w_ext"], tm=TOKEN_TILE).reshape(b, s, -1)
    od0 = _diff_attn(proj, proj_meta, wts["lam_vecs"], wts["subln_g"], heads=(0,), bq=SKIP_BLOCK,
                     bk=min(SKIP_BLOCK, s), skip=True)
    od1 = _diff_attn(proj, proj_meta, wts["lam_vecs"], wts["subln_g"], heads=tuple(range(1, DA_HEADS)),
                     bq=DIFF_BQ, bk=min(DIFF_BK, s), skip=False)
    os_ = _win_attn(proj, proj_meta, wts["sink"])
    h, a2p, route, rt, cnt = _out_router(od0.reshape(b * s, -1), od1.reshape(b * s, -1), os_.reshape(b * s, -1), x2d,
                                     wts["wod"], wts["wos"],
                                     wts["g2"], wts["wr"], wts["br"], tm=TOKEN_TILE)
    y = _moe(a2p, route, rt, cnt, h, wts["wg"], wts["wu"], wts["wd"], wts["fg"], tm=MOE_TILE)
    return y.reshape(b, s, d)


def kernel(x_prompt, x_sample, meta, norm1_g, w_in, lam_q1, lam_k1, lam_q2, lam_k2, subln_g, sink, w_out,
           norm2_g, w_gr, b_gr, w_er, b_er, w_gate, w_up, w_down, final_g):
    d = x_prompt.shape[-1]
    w = w_in[0]
    c_kd = 2 * DA_HEADS * HEAD_DIM
    c_vd = 2 * c_kd
    c_qs = c_vd + DA_HEADS * 2 * HEAD_DIM
    c_ks = c_qs + SW_HEADS * HEAD_DIM
    c_vs = c_ks + SW_KV_HEADS * HEAD_DIM
    w_ext = jnp.concatenate(
        [w[:, :c_ks], _dup_heads(w[:, c_ks:c_vs], SW_KV_HEADS), _dup_heads(w[:, c_vs:], SW_KV_HEADS)],
        axis=1).astype(BF16)
    w_router = jnp.concatenate([w_er[0], w_gr[0]], axis=1)
    w_router = jnp.pad(w_router, ((0, 0), (0, LANES - w_router.shape[1])))
    wr_hi = w_router.astype(BF16)
    wr_lo = (w_router - wr_hi.astype(F32)).astype(BF16)
    br = jnp.pad(jnp.concatenate([b_er[0], b_gr[0]]), (0, LANES - N_EXPERTS - N_GROUPS)).reshape(1, LANES)
    wo = w_out[0].astype(BF16)
    half = DA_HEADS * 2 * HEAD_DIM
    wts = dict(
        g1=norm1_g[0].reshape(1, d), w_ext=w_ext,
        lam_vecs=(lam_q1[0].reshape(1, -1), lam_k1[0].reshape(1, -1),
                  lam_q2[0].reshape(1, -1), lam_k2[0].reshape(1, -1)),
        subln_g=subln_g[0].reshape(1, -1), sink=sink[0].reshape(1, -1),
        wod=wo[:half], wos=wo[half:], g2=norm2_g[0].reshape(1, d),
        wr=jnp.concatenate([wr_hi, wr_lo], axis=1), br=br,
        wg=w_gate[0].astype(BF16), wu=w_up[0].astype(BF16), wd=w_down[0].astype(BF16),
        fg=final_g.reshape(1, d),
    )
    proj_meta = _norm_proj(meta, wts["g1"], w_ext, tm=N_META)
    proj_meta = jnp.pad(proj_meta, ((0, LANES - N_META), (0, 0)))
    return _encoder(x_prompt, proj_meta, wts), _encoder(x_sample, proj_meta, wts)
```
